```python
import math
import jax
import jax.numpy as jnp
from jax import lax
import numpy as np


D_MODEL = 1024
BATCH = 4
SEQ = 4096
DEPTH = 4

N_A_LAYERS = DEPTH // 2
N_B_LAYERS = DEPTH - N_A_LAYERS
N_HEADS = 8
A_HEAD_DIM = 64
A_QK_WIDTH = N_HEADS * 2 * A_HEAD_DIM
A_V_WIDTH = N_HEADS * 2 * A_HEAD_DIM
B_GROUPS = ((128, 1), (512, 4), (2048, 16))
B_NUM_GROUPS = len(B_GROUPS)
B_HEAD_DIM = 64
B_QK_WIDTH = B_NUM_GROUPS * N_HEADS * B_HEAD_DIM
B_OUT_WIDTH = N_HEADS * B_HEAD_DIM
Q_BLOCK = 128
REL_BUCKETS = 32
REL_MAX_EXACT = 16
REL_MAX_DIST = 2048
MOE_GROUPS = 4
MOE_EXPERTS_PER_GROUP = 8
MOE_EXPERTS = MOE_GROUPS * MOE_EXPERTS_PER_GROUP
MOE_TOP_K = 2
MOE_HIDDEN = 512
MOE_BLOCK = 128
EPS = 1e-6

kernel_name = 'hybrid_yoco_diffattn_dilated_hmoe'


def rmsnorm(t, g):
    t32 = t.astype(jnp.float32)
    y = t32 * lax.rsqrt(jnp.mean(t32 * t32, axis=-1, keepdims=True) + EPS)
    return (y * g.astype(jnp.float32)).astype(t.dtype)


def modulate(h, shift, scale):
    return h * (1 + scale[:, None, :]) + shift[:, None, :]


def t5_bucket(dist):
    n = jnp.maximum(dist, 0)
    nf = jnp.maximum(n, 1).astype(jnp.float32)
    large = REL_MAX_EXACT + (jnp.log(nf / REL_MAX_EXACT) / math.log(REL_MAX_DIST / REL_MAX_EXACT)
                             * (REL_BUCKETS - REL_MAX_EXACT)).astype(jnp.int32)
    large = jnp.minimum(large, REL_BUCKETS - 1)
    return jnp.where(n < REL_MAX_EXACT, n, large)


def lambda_init(layer):
    return 0.8 - 0.6 * math.exp(-0.3 * layer)


def diff_attention(h, w_qkv, q_gain, k_gain, lam_p, subln_gain, w_o, rel_bias, lam_init):
    b, s, _ = h.shape
    q, k, v = jnp.split(h @ w_qkv, [A_QK_WIDTH, 2 * A_QK_WIDTH], axis=-1)
    q = rmsnorm(q.reshape(b, s, N_HEADS, 2, A_HEAD_DIM), q_gain)
    k = rmsnorm(k.reshape(b, s, N_HEADS, 2, A_HEAD_DIM), k_gain)
    v = v.reshape(b, s, N_HEADS, 2 * A_HEAD_DIM)
    lp = lam_p.astype(jnp.float32)
    lam = jnp.exp(jnp.sum(lp[0] * lp[1])) - jnp.exp(jnp.sum(lp[2] * lp[3])) + lam_init
    scale = A_HEAD_DIM ** -0.5
    k_pos = jnp.arange(s)

    def block(i):
        start = i * Q_BLOCK
        q_blk = lax.dynamic_slice_in_dim(q, start, Q_BLOCK, axis=1)
        logits = jnp.einsum('bqhcd,bkhcd->bchqk', q_blk, k,
                            preferred_element_type=jnp.float32) * scale
        dist = (start + jnp.arange(Q_BLOCK))[:, None] - k_pos[None, :]
        bias = jnp.transpose(rel_bias[t5_bucket(dist)], (2, 0, 1)).astype(jnp.float32)
        logits = jnp.where(dist >= 0, logits + bias, -jnp.inf)
        p = jax.nn.softmax(logits, axis=-1)
        attn = p[:, 0] - lam * p[:, 1]
        return jnp.einsum('bhqk,bkhe->bqhe', attn.astype(v.dtype), v)

    o = lax.map(block, jnp.arange(s // Q_BLOCK))
    o = jnp.moveaxis(o, 0, 1).reshape(b, s, N_HEADS, 2 * A_HEAD_DIM)
    o = rmsnorm(o, subln_gain) * (1 - lam_init)
    return o.reshape(b, s, A_V_WIDTH) @ w_o


def _to_subseq(t, dil):
    b, s = t.shape[:2]
    rest = t.shape[2:]
    length = s // dil
    t = jnp.swapaxes(t.reshape((b, length, dil) + rest), 1, 2).reshape((b * dil, length) + rest)
    lp = -(-length // Q_BLOCK) * Q_BLOCK
    return jnp.pad(t, [(0, 0), (0, lp - length)] + [(0, 0)] * len(rest))


def _from_subseq(t, b, s, dil):
    length = s // dil
    rest = t.shape[2:]
    t = t[:, :length].reshape((b, dil, length) + rest)
    return jnp.swapaxes(t, 1, 2).reshape((b, s) + rest)


def _key_blocks(t):
    bp, lp = t.shape[:2]
    rest = t.shape[2:]
    nb = lp // Q_BLOCK
    tp = jnp.pad(t, [(0, 0), (Q_BLOCK, 0)] + [(0, 0)] * len(rest))
    prev = tp[:, :lp].reshape((bp, nb, Q_BLOCK) + rest)
    cur = tp[:, Q_BLOCK:].reshape((bp, nb, Q_BLOCK) + rest)
    return jnp.concatenate([prev, cur], axis=2)


def _band_bias_mask(band, dil, nb, rel_bias):
    i = jnp.arange(Q_BLOCK)[:, None]
    j = jnp.arange(2 * Q_BLOCK)[None, :]
    rel = i + Q_BLOCK - j
    key_idx = jnp.arange(nb)[:, None, None] * Q_BLOCK + j[None] - Q_BLOCK
    mask = (rel >= 0)[None] & (rel <= band)[None] & (key_idx >= 0)
    bias = jnp.transpose(rel_bias[t5_bucket(dil * rel)], (2, 0, 1)).astype(jnp.float32)
    return bias, mask


def shared_dilated_kv(x, c_act, kv_norm, kv_ada_w, kv_ada_b, kv_w, kv_k_norm):
    b, s, _ = x.shape
    shift, scale = jnp.split(c_act @ kv_ada_w + kv_ada_b, 2, axis=-1)
    h = modulate(rmsnorm(x, kv_norm), shift, scale)
    k, v = jnp.split(h @ kv_w, 2, axis=-1)
    k = rmsnorm(k.reshape(b, s, B_NUM_GROUPS, N_HEADS, B_HEAD_DIM), kv_k_norm[:, None, :])
    v = v.reshape(b, s, B_NUM_GROUPS, N_HEADS, B_HEAD_DIM)
    blocks = []
    for g, (_, dil) in enumerate(B_GROUPS):
        blocks.append((_key_blocks(_to_subseq(k[:, :, g], dil)),
                       _key_blocks(_to_subseq(v[:, :, g], dil))))
    return blocks


def dilated_attention(h, w_q, q_gain, w_o, kv_blocks, rel_bias):
    b, s, _ = h.shape
    q = rmsnorm((h @ w_q).reshape(b, s, B_NUM_GROUPS, N_HEADS, B_HEAD_DIM), q_gain[:, None, :])
    scale = B_HEAD_DIM ** -0.5
    outs, lses = [], []
    for g, (window, dil) in enumerate(B_GROUPS):
        kb, vb = kv_blocks[g]
        bp, nb = kb.shape[:2]
        qb = _to_subseq(q[:, :, g], dil).reshape(bp, nb, Q_BLOCK, N_HEADS, B_HEAD_DIM)
        bias, mask = _band_bias_mask(window // dil, dil, nb, rel_bias)
        logits = jnp.einsum('bnqhd,bnkhd->bnhqk', qb, kb,
                            preferred_element_type=jnp.float32) * scale
        logits = jnp.where(mask[None, :, None], logits + bias[None, None], -jnp.inf)
        lse = jax.nn.logsumexp(logits, axis=-1)
        p = jnp.exp(logits - lse[..., None])
        o = jnp.einsum('bnhqk,bnkhd->bnqhd', p.astype(vb.dtype), vb)
        outs.append(_from_subseq(o.reshape(bp, nb * Q_BLOCK, N_HEADS, B_HEAD_DIM), b, s, dil))
        lses.append(_from_subseq(jnp.swapaxes(lse, 2, 3).reshape(bp, nb * Q_BLOCK, N_HEADS), b, s, dil))
    w = jax.nn.softmax(jnp.stack(lses, axis=2), axis=2)
    o = jnp.einsum('bsgh,bsghd->bshd', w.astype(outs[0].dtype), jnp.stack(outs, axis=2))
    return o.reshape(b, s, B_OUT_WIDTH) @ w_o


def hier_moe(h, w_group, b_group, w_expert, b_expert, w_gate, w_up, w_down):
    t = h.shape[0]
    g_logits = (h @ w_group).astype(jnp.float32) + b_group.astype(jnp.float32)
    g_idx = jnp.argmax(g_logits, axis=-1)
    g_w = jnp.take_along_axis(jax.nn.softmax(g_logits, axis=-1), g_idx[:, None], axis=1)
    e_logits = ((h @ w_expert).astype(jnp.float32) + b_expert.astype(jnp.float32)).reshape(
        t, MOE_GROUPS, MOE_EXPERTS_PER_GROUP)
    e_sel = jnp.take_along_axis(e_logits, g_idx[:, None, None], axis=1)[:, 0]
    top_v, top_i = lax.top_k(e_sel, MOE_TOP_K)
    gates = g_w * jax.nn.softmax(top_v, axis=-1)
    eid = (g_idx[:, None] * MOE_EXPERTS_PER_GROUP + top_i).reshape(-1)
    tok = jnp.repeat(jnp.arange(t), MOE_TOP_K)
    wts = gates.reshape(-1)
    n_assign = t * MOE_TOP_K
    order = jnp.argsort(eid)
    eid_s, tok_s, wts_s = eid[order], tok[order], wts[order]
    counts = jax.ops.segment_sum(jnp.ones_like(eid_s), eid_s, num_segments=MOE_EXPERTS)
    starts = jnp.cumsum(counts) - counts
    padded = (counts + MOE_BLOCK - 1) // MOE_BLOCK * MOE_BLOCK
    pad_end = jnp.cumsum(padded)
    pad_start = pad_end - padded
    dest = pad_start[eid_s] + jnp.arange(n_assign) - starts[eid_s]
    n_blocks = -(-n_assign // MOE_BLOCK) + MOE_EXPERTS
    rows = n_blocks * MOE_BLOCK
    row_tok = jnp.zeros((rows,), jnp.int32).at[dest].set(tok_s)
    row_w = jnp.zeros((rows,), jnp.float32).at[dest].set(wts_s)
    block_e = jnp.minimum(jnp.searchsorted(pad_end, jnp.arange(n_blocks) * MOE_BLOCK, side='right'),
                          MOE_EXPERTS - 1)
    xs = h[row_tok].reshape(n_blocks, MOE_BLOCK, h.shape[1])

    def expert_block(args):
        xb, e = args
        return (jax.nn.silu(xb @ w_gate[e]) * (xb @ w_up[e])) @ w_down[e]

    ys = lax.map(expert_block, (xs, block_e)).reshape(rows, h.shape[1])
    return jnp.zeros_like(h).at[row_tok].add(ys * row_w[:, None].astype(h.dtype))


def setup_inputs(seed: int = 0) -> dict:
    key = jax.random.key(seed)
    ks = jax.random.split(key, 28)
    D = D_MODEL

    def nrm(k, shape, scale):
        return jax.random.normal(k, shape, jnp.float32) * scale

    def gain(k, shape):
        return 1.0 + 0.02 * jax.random.normal(k, shape, jnp.float32)

    return {
        'x': nrm(ks[0], (BATCH, SEQ, D), 1.0),
        'c': nrm(ks[1], (BATCH, D), 1.0),
        'rel_bias': nrm(ks[2], (REL_BUCKETS, N_HEADS), 0.5),
        'ada_w': nrm(ks[3], (DEPTH, D, 6 * D), 0.02),
        'ada_b': nrm(ks[4], (DEPTH, 6 * D), 0.02),
        'norm_attn': gain(ks[5], (DEPTH, D)),
        'norm_ffn': gain(ks[6], (DEPTH, D)),
        'a_w_qkv': nrm(ks[7], (N_A_LAYERS, D, 2 * A_QK_WIDTH + A_V_WIDTH), D ** -0.5),
        'a_q_norm': gain(ks[8], (N_A_LAYERS, A_HEAD_DIM)),
        'a_k_norm': gain(ks[9], (N_A_LAYERS, A_HEAD_DIM)),
        'a_lambda': nrm(ks[10], (N_A_LAYERS, 4, A_HEAD_DIM), 0.1),
        'a_subln': gain(ks[11], (N_A_LAYERS, 2 * A_HEAD_DIM)),
        'a_w_o': nrm(ks[12], (N_A_LAYERS, A_V_WIDTH, D), A_V_WIDTH ** -0.5),
        'kv_norm': gain(ks[13], (D,)),
        'kv_ada_w': nrm(ks[14], (D, 2 * D), 0.02),
        'kv_ada_b': nrm(ks[15], (2 * D,), 0.02),
        'kv_w': nrm(ks[16], (D, 2 * B_QK_WIDTH), D ** -0.5),
        'kv_k_norm': gain(ks[17], (B_NUM_GROUPS, B_HEAD_DIM)),
        'b_w_q': nrm(ks[18], (N_B_LAYERS, D, B_QK_WIDTH), D ** -0.5),
        'b_q_norm': gain(ks[19], (N_B_LAYERS, B_NUM_GROUPS, B_HEAD_DIM)),
        'b_w_o': nrm(ks[20], (N_B_LAYERS, B_OUT_WIDTH, D), B_OUT_WIDTH ** -0.5),
        'moe_w_group': nrm(ks[21], (DEPTH, D, MOE_GROUPS), D ** -0.5),
        'moe_b_group': nrm(ks[22], (DEPTH, MOE_GROUPS), 0.01),
        'moe_w_expert': nrm(ks[23], (DEPTH, D, MOE_EXPERTS), D ** -0.5),
        'moe_b_expert': nrm(ks[24], (DEPTH, MOE_EXPERTS), 0.01),
        'moe_w_gate': nrm(ks[25], (DEPTH, MOE_EXPERTS, D, MOE_HIDDEN), D ** -0.5),
        'moe_w_up': nrm(ks[26], (DEPTH, MOE_EXPERTS, D, MOE_HIDDEN), D ** -0.5),
        'moe_w_down': nrm(ks[27], (DEPTH, MOE_EXPERTS, MOE_HIDDEN, D), MOE_HIDDEN ** -0.5),
    }


def reference(x, c, rel_bias, ada_w, ada_b, norm_attn, norm_ffn, a_w_qkv, a_q_norm, a_k_norm,
              a_lambda, a_subln, a_w_o, kv_norm, kv_ada_w, kv_ada_b, kv_w, kv_k_norm,
              b_w_q, b_q_norm, b_w_o, moe_w_group, moe_b_group, moe_w_expert, moe_b_expert,
              moe_w_gate, moe_w_up, moe_w_down):
    b, s, d = x.shape
    c_act = jax.nn.silu(c)
    kv_blocks = None
    for layer in range(DEPTH):
        mod = c_act @ ada_w[layer] + ada_b[layer]
        sh_a, sc_a, g_a, sh_f, sc_f, g_f = jnp.split(mod, 6, axis=-1)
        h = modulate(rmsnorm(x, norm_attn[layer]), sh_a, sc_a)
        if layer < N_A_LAYERS:
            y = diff_attention(h, a_w_qkv[layer], a_q_norm[layer], a_k_norm[layer], a_lambda[layer],
                               a_subln[layer], a_w_o[layer], rel_bias, lambda_init(layer))
        else:
            if layer == N_A_LAYERS:
                kv_blocks = shared_dilated_kv(x, c_act, kv_norm, kv_ada_w, kv_ada_b, kv_w, kv_k_norm)
            j = layer - N_A_LAYERS
            y = dilated_attention(h, b_w_q[j], b_q_norm[j], b_w_o[j], kv_blocks, rel_bias)
        x = x + g_a[:, None, :] * y
        h = modulate(rmsnorm(x, norm_ffn[layer]), sh_f, sc_f)
        y = hier_moe(h.reshape(b * s, d), moe_w_group[layer], moe_b_group[layer], moe_w_expert[layer],
                     moe_b_expert[layer], moe_w_gate[layer], moe_w_up[layer], moe_w_down[layer])
        x = x + g_f[:, None, :] * y.reshape(b, s, d)
    return x
```

```python
import functools
import math

import jax
import jax.numpy as jnp
import numpy as np
from jax import lax
from jax.experimental import pallas as pl
from jax.experimental.pallas import tpu as pltpu

F32 = jnp.float32
BF16 = jnp.bfloat16

D_MODEL = 1024
DEPTH = 4
N_A_LAYERS = 2
N_HEADS = 8
HEAD_DIM = 64
B_GROUPS = ((128, 1), (512, 4), (2048, 16))
N_GROUPS = len(B_GROUPS)
B_WIDTH = N_HEADS * HEAD_DIM
REL_BUCKETS = 32
REL_MAX_EXACT = 16
REL_MAX_DIST = 2048
MOE_GROUPS = 4
MOE_EPG = 8
MOE_EXPERTS = MOE_GROUPS * MOE_EPG
MOE_HIDDEN = 512
EPS = 1e-6
NEG_INF = float("-inf")

LANES = 128
MXU_DIM = 256
ROUTER_COLS = 128
EXPERT_COL0 = 8

TOK_TILE = 512
ATT_TILE = 512
BQ_TILE = 256
BQ_SUB = 128
MOE_BLK = 256
ROUTE_TILE = 1024
DISP_TILE = 512
COMB_TILE = 256
VMEM_LIMIT = 56 * 1024 * 1024


def _cparams(sem):
    return pltpu.CompilerParams(dimension_semantics=sem, vmem_limit_bytes=VMEM_LIMIT)


def _bucket_table(max_dist):
    n = np.arange(max_dist + 1)
    nf = np.maximum(n, 1).astype(np.float64)
    large = REL_MAX_EXACT + (np.log(nf / REL_MAX_EXACT) / math.log(REL_MAX_DIST / REL_MAX_EXACT)
                             * (REL_BUCKETS - REL_MAX_EXACT)).astype(np.int64)
    large = np.minimum(large, REL_BUCKETS - 1)
    return np.where(n < REL_MAX_EXACT, n, large).astype(np.int32)


def _attn_a_bias_tiles(rel_bias, seq, tile):
    table = _bucket_table(seq)
    last_start = int(np.argmax(table == REL_BUCKETS - 1))
    nd = 0
    while nd * tile - (tile - 1) < last_start and nd * tile < seq:
        nd += 1
    nd += 1
    r = np.arange(tile)[:, None]
    c = np.arange(tile)[None, :]
    tiles, masks = [], []
    for d in range(nd):
        dist = d * tile + r - c
        tiles.append(table[np.clip(dist, 0, seq)])
        masks.append(dist >= 0)
    idx = np.stack(tiles)
    mask = np.stack(masks)
    bias = jnp.transpose(rel_bias.astype(F32)[idx], (3, 0, 1, 2))
    return jnp.where(mask[None], bias, NEG_INF), nd


def _attn_b_bias_tiles(rel_bias, band, dil):
    table = _bucket_table(dil * 2 * BQ_SUB)
    i = np.arange(BQ_SUB)[:, None]
    j = np.arange(2 * BQ_SUB)[None, :]
    rel = i + BQ_SUB - j
    ok = (rel >= 0) & (rel <= band)
    idx = table[np.clip(dil * rel, 0, dil * 2 * BQ_SUB)]
    bias = jnp.transpose(rel_bias.astype(F32)[idx], (2, 0, 1))
    general = jnp.where(ok[None], bias, NEG_INF)
    first = jnp.where((ok & (j >= BQ_SUB))[None], bias, NEG_INF)
    return jnp.stack([first, general])


def _group_sum_matrix():
    r = np.arange(MXU_DIM)
    return jnp.asarray((r[:, None] // HEAD_DIM) == (r[None, :] // HEAD_DIM), BF16)


def _mod_kernel(c_ref, w_ref, b_ref, o_ref):
    c = c_ref[...]
    c_act = c * (1.0 / (1.0 + jnp.exp(-c)))
    o_ref[...] = jnp.dot(c_act, w_ref[...], preferred_element_type=F32,
                         precision=lax.Precision.HIGHEST) + b_ref[...]


def _modulation(c_pad, w, b, tn=1024):
    nl, d, n = w.shape
    return pl.pallas_call(
        _mod_kernel,
        grid=(nl, n // tn),
        in_specs=[pl.BlockSpec((8, d), lambda l, j: (0, 0)),
                  pl.BlockSpec((None, d, tn), lambda l, j: (l, 0, j)),
                  pl.BlockSpec((None, 1, tn), lambda l, j: (l, 0, j))],
        out_specs=pl.BlockSpec((None, 8, tn), lambda l, j: (l, 0, j)),
        out_shape=jax.ShapeDtypeStruct((nl, 8, n), F32),
        compiler_params=_cparams(("parallel", "parallel")),
        name="adaln_mod",
    )(c_pad, w, b.reshape(nl, 1, n))


def _rms(x, gain):
    return x * lax.rsqrt(jnp.mean(x * x, axis=-1, keepdims=True) + EPS) * gain


def _proj_kernel(x_ref, g_ref, sh_ref, sc_ref, w_ref, hg_ref, bd_ref, o_ref, *, n_out, n_norm):
    h = _rms(x_ref[...], g_ref[...]) * (1.0 + sc_ref[...]) + sh_ref[...]
    hb = h.astype(BF16)
    for c0 in range(0, n_out, MXU_DIM):
        a = jnp.dot(hb, w_ref[:, c0:c0 + MXU_DIM], preferred_element_type=F32)
        if c0 < n_norm:
            ms = jnp.dot((a * a).astype(BF16), bd_ref[...], preferred_element_type=F32) * (1.0 / HEAD_DIM)
            a = a * lax.rsqrt(ms + EPS) * hg_ref[:, c0:c0 + MXU_DIM]
        o_ref[:, c0:c0 + MXU_DIM] = a.astype(BF16)


def _norm_proj(x, gain, shift, scale, w_bf16, head_gain, n_norm, seq):
    t, d = x.shape
    n_out = w_bf16.shape[1]
    tm = min(TOK_TILE, seq)
    per_b = seq // tm
    kern = functools.partial(_proj_kernel, n_out=n_out, n_norm=n_norm)
    return pl.pallas_call(
        kern,
        grid=(t // tm,),
        in_specs=[pl.BlockSpec((tm, d), lambda i: (i, 0)),
                  pl.BlockSpec((1, d), lambda i: (0, 0)),
                  pl.BlockSpec((None, 1, d), lambda i: (i // per_b, 0, 0)),
                  pl.BlockSpec((None, 1, d), lambda i: (i // per_b, 0, 0)),
                  pl.BlockSpec((d, n_out), lambda i: (0, 0)),
                  pl.BlockSpec((1, n_out), lambda i: (0, 0)),
                  pl.BlockSpec((MXU_DIM, MXU_DIM), lambda i: (0, 0))],
        out_specs=pl.BlockSpec((tm, n_out), lambda i: (i, 0)),
        out_shape=jax.ShapeDtypeStruct((t, n_out), BF16),
        compiler_params=_cparams(("parallel",)),
        name="norm_proj",
    )(x, gain.reshape(1, d), shift, scale, w_bf16, head_gain.reshape(1, n_out), _group_sum_matrix())


def _attn_a_kernel(q_ref, k_ref, v_ref, bias_ref, lam_ref, g_ref, o_ref, m_sc, l_sc, acc_sc,
                   *, tile, nd, lam_init):
    qi = pl.program_id(2)
    q = q_ref[...]
    lane = lax.broadcasted_iota(jnp.int32, q.shape, 1)
    zero = jnp.zeros_like(q)
    qq = jnp.concatenate([jnp.where(lane < HEAD_DIM, q, zero),
                          jnp.where(lane >= HEAD_DIM, q, zero)], axis=0)
    m_sc[...] = jnp.full(m_sc.shape, NEG_INF, F32)
    l_sc[...] = jnp.zeros(l_sc.shape, F32)
    acc_sc[...] = jnp.zeros(acc_sc.shape, F32)

    def body(j, carry):
        start = pl.multiple_of(j * tile, tile)
        k = k_ref[pl.ds(start, tile), :]
        v = v_ref[pl.ds(start, tile), :]
        s = lax.dot_general(qq, k, (((1,), (1,)), ((), ())), preferred_element_type=F32)
        bias = bias_ref[jnp.minimum(qi - j, nd - 1)]
        for c in range(2):
            sc = s[c * tile:(c + 1) * tile] + bias
            m_prev = m_sc[c]
            m_new = jnp.maximum(m_prev, jnp.max(sc, axis=-1, keepdims=True))
            alpha = jnp.exp(m_prev - m_new)
            p = jnp.exp(sc - m_new)
            l_sc[c] = alpha * l_sc[c] + jnp.sum(p, axis=-1, keepdims=True)
            acc_sc[c] = alpha * acc_sc[c] + jnp.dot(p.astype(BF16), v, preferred_element_type=F32)
            m_sc[c] = m_new
        return carry

    lax.fori_loop(0, qi + 1, body, 0)

    lp = lam_ref[...]
    lam = (jnp.exp(jnp.sum(lp[0:1] * lp[1:2], axis=-1, keepdims=True))
           - jnp.exp(jnp.sum(lp[2:3] * lp[3:4], axis=-1, keepdims=True)) + lam_init)
    o = acc_sc[0] / l_sc[0] - lam * (acc_sc[1] / l_sc[1])
    o_ref[...] = (_rms(o, g_ref[...]) * (1.0 - lam_init)).astype(BF16)


def _attn_a(qkv, bias_tiles, nd, lam_p, subln, lam_init, batch, seq):
    tile = min(ATT_TILE, seq)
    nq = seq // tile
    qkv3 = qkv.reshape(batch, seq, 3 * N_HEADS * LANES)
    kern = functools.partial(_attn_a_kernel, tile=tile, nd=nd, lam_init=lam_init)
    out = pl.pallas_call(
        kern,
        grid=(N_HEADS, batch, nq),
        in_specs=[pl.BlockSpec((None, tile, LANES), lambda h, b, i: (b, i, h)),
                  pl.BlockSpec((None, seq, LANES), lambda h, b, i: (b, 0, N_HEADS + h)),
                  pl.BlockSpec((None, seq, LANES), lambda h, b, i: (b, 0, 2 * N_HEADS + h)),
                  pl.BlockSpec((None, nd, tile, tile), lambda h, b, i: (h, 0, 0, 0)),
                  pl.BlockSpec((4, HEAD_DIM), lambda h, b, i: (0, 0)),
                  pl.BlockSpec((1, LANES), lambda h, b, i: (0, 0))],
        out_specs=pl.BlockSpec((None, tile, LANES), lambda h, b, i: (b, i, h)),
        out_shape=jax.ShapeDtypeStruct((batch, seq, N_HEADS * LANES), BF16),
        scratch_shapes=[pltpu.VMEM((2, tile, 1), F32), pltpu.VMEM((2, tile, 1), F32),
                        pltpu.VMEM((2, tile, LANES), F32)],
        compiler_params=_cparams(("parallel", "parallel", "parallel")),
        name="attn_a",
    )(qkv3, qkv3, qkv3, bias_tiles, lam_p, subln.reshape(1, LANES))
    return out.reshape(batch * seq, N_HEADS * LANES)


def _attn_b_kernel(q_ref, k_ref, v_ref, kp_ref, vp_ref, bias_ref, o_ref, lse_ref, *, tq):
    n = pl.program_id(2)
    for sb in range(tq // BQ_SUB):
        rows = slice(sb * BQ_SUB, (sb + 1) * BQ_SUB)
        prev = slice((sb - 1) * BQ_SUB, sb * BQ_SUB)
        qs = q_ref[rows, :]
        if sb == 0:
            kk = jnp.concatenate([kp_ref[...], k_ref[rows, :]], axis=0)
            vv = jnp.concatenate([vp_ref[...], v_ref[rows, :]], axis=0)
            variant = jnp.where(n == 0, 0, 1)
        else:
            kk = jnp.concatenate([k_ref[prev, :], k_ref[rows, :]], axis=0)
            vv = jnp.concatenate([v_ref[prev, :], v_ref[rows, :]], axis=0)
            variant = 1
        lane = lax.broadcasted_iota(jnp.int32, (BQ_SUB, LANES), 1)
        low = lane < HEAD_DIM
        for hp in range(N_HEADS // 2):
            cols = slice(hp * LANES, (hp + 1) * LANES)
            qp = qs[:, cols]
            zero = jnp.zeros_like(qp)
            qq = jnp.concatenate([jnp.where(low, qp, zero), jnp.where(low, zero, qp)], axis=0)
            s = lax.dot_general(qq, kk[:, cols], (((1,), (1,)), ((), ())), preferred_element_type=F32)
            outs, lses = [], []
            for c in range(2):
                sc = s[c * BQ_SUB:(c + 1) * BQ_SUB] + bias_ref[variant, 2 * hp + c]
                m = jnp.max(sc, axis=-1, keepdims=True)
                p = jnp.exp(sc - m)
                l = jnp.sum(p, axis=-1, keepdims=True)
                outs.append(jnp.dot(p.astype(BF16), vv[:, cols], preferred_element_type=F32) / l)
                lses.append(m + jnp.log(l))
            o_ref[rows, cols] = jnp.where(low, outs[0], outs[1])
            lse_ref[rows, cols] = jnp.where(low, lses[0], lses[1])


def _attn_b_group(q, kv, bias_tiles, g, dil, batch, seq):
    length = seq // dil
    tq = min(BQ_TILE, length)
    nq = length // tq
    sub = tq // BQ_SUB
    q3 = q.reshape(batch, length, dil * N_GROUPS * B_WIDTH)
    kv3 = kv.reshape(batch, length, dil * 2 * N_GROUPS * B_WIDTH)
    kern = functools.partial(_attn_b_kernel, tq=tq)
    blk = lambda rows: (None, rows, B_WIDTH)
    out_shape = jax.ShapeDtypeStruct((batch, length, dil * B_WIDTH), F32)
    o, lse = pl.pallas_call(
        kern,
        grid=(batch, dil, nq),
        in_specs=[pl.BlockSpec(blk(tq), lambda b, r, n: (b, n, r * N_GROUPS + g)),
                  pl.BlockSpec(blk(tq), lambda b, r, n: (b, n, r * 2 * N_GROUPS + g)),
                  pl.BlockSpec(blk(tq), lambda b, r, n: (b, n, r * 2 * N_GROUPS + N_GROUPS + g)),
                  pl.BlockSpec(blk(BQ_SUB), lambda b, r, n: (b, jnp.maximum(n * sub - 1, 0), r * 2 * N_GROUPS + g)),
                  pl.BlockSpec(blk(BQ_SUB),
                               lambda b, r, n: (b, jnp.maximum(n * sub - 1, 0), r * 2 * N_GROUPS + N_GROUPS + g)),
                  pl.BlockSpec((2, N_HEADS, BQ_SUB, 2 * BQ_SUB), lambda b, r, n: (0, 0, 0, 0))],
        out_specs=[pl.BlockSpec(blk(tq), lambda b, r, n: (b, n, r)),
                   pl.BlockSpec(blk(tq), lambda b, r, n: (b, n, r))],
        out_shape=[out_shape, out_shape],
        compiler_params=_cparams(("parallel", "parallel", "parallel")),
        name=f"attn_b_g{g}",
    )(q3, kv3, kv3, kv3, kv3, bias_tiles)
    return o.reshape(batch * seq, B_WIDTH), lse.reshape(batch * seq, B_WIDTH)


def _out_tail(o_bf16, wo_ref, x_ref, ga_ref, gf_ref, shf_ref, scf_ref, wr_ref, br_ref, x1_ref, hf_ref, lg_ref):
    y = jnp.dot(o_bf16, wo_ref[...], preferred_element_type=F32)
    x1 = x_ref[...] + ga_ref[...] * y
    x1_ref[...] = x1
    hf = _rms(x1, gf_ref[...]) * (1.0 + scf_ref[...]) + shf_ref[...]
    hf_ref[...] = hf
    lg_ref[...] = jnp.dot(hf, wr_ref[...], preferred_element_type=F32,
                          precision=lax.Precision.HIGHEST) + br_ref[...]


def _out_a_kernel(o_ref, *rest):
    _out_tail(o_ref[...], *rest)


def _out_b_kernel(o0, o1, o2, l0, l1, l2, *rest):
    la, lb, lc = l0[...], l1[...], l2[...]
    m = jnp.maximum(jnp.maximum(la, lb), lc)
    ea, eb, ec = jnp.exp(la - m), jnp.exp(lb - m), jnp.exp(lc - m)
    z = ea + eb + ec
    o = (ea / z) * o0[...] + (eb / z) * o1[...] + (ec / z) * o2[...]
    _out_tail(o.astype(BF16), *rest)


def _out_proj(att_inputs, wo_bf16, x, gate_a, gain_f, shift_f, scale_f, w_router, b_router, seq):
    t, d = x.shape
    tm = min(TOK_TILE, seq)
    per_b = seq // tm
    row = lambda i: (i, 0)
    fixed = lambda i: (0, 0)
    per_batch = lambda i: (i // per_b, 0, 0)
    att_specs = [pl.BlockSpec((tm, a.shape[1]), row) for a in att_inputs]
    kern = _out_a_kernel if len(att_inputs) == 1 else _out_b_kernel
    return pl.pallas_call(
        kern,
        grid=(t // tm,),
        in_specs=att_specs + [pl.BlockSpec(wo_bf16.shape, fixed),
                              pl.BlockSpec((tm, d), row),
                              pl.BlockSpec((None, 1, d), per_batch),
                              pl.BlockSpec((1, d), fixed),
                              pl.BlockSpec((None, 1, d), per_batch),
                              pl.BlockSpec((None, 1, d), per_batch),
                              pl.BlockSpec((d, ROUTER_COLS), fixed),
                              pl.BlockSpec((1, ROUTER_COLS), fixed)],
        out_specs=[pl.BlockSpec((tm, d), row), pl.BlockSpec((tm, d), row),
                   pl.BlockSpec((tm, ROUTER_COLS), row)],
        out_shape=[jax.ShapeDtypeStruct((t, d), F32), jax.ShapeDtypeStruct((t, d), F32),
                   jax.ShapeDtypeStruct((t, ROUTER_COLS), F32)],
        compiler_params=_cparams(("parallel",)),
        name="out_proj",
    )(*att_inputs, wo_bf16, x, gate_a, gain_f.reshape(1, d), shift_f, scale_f, w_router, b_router)


def _route_kernel(lg_ref, eid_ref, gate_ref):
    lt = lg_ref[...].T
    best, g_idx = lt[0:1], jnp.zeros((1, lt.shape[1]), jnp.int32)
    for g in range(1, MOE_GROUPS):
        upd = lt[g:g + 1] > best
        best = jnp.where(upd, lt[g:g + 1], best)
        g_idx = jnp.where(upd, g, g_idx)
    denom = jnp.zeros_like(best)
    for g in range(MOE_GROUPS):
        denom = denom + jnp.exp(lt[g:g + 1] - best)
    g_w = 1.0 / denom
    e_sel = lt[EXPERT_COL0:EXPERT_COL0 + MOE_EPG]
    for g in range(1, MOE_GROUPS):
        r0 = EXPERT_COL0 + g * MOE_EPG
        e_sel = jnp.where(g_idx == g, lt[r0:r0 + MOE_EPG], e_sel)

    def first_max(vals):
        v, i = vals[0:1], jnp.zeros((1, vals.shape[1]), jnp.int32)
        for e in range(1, MOE_EPG):
            upd = vals[e:e + 1] > v
            v = jnp.where(upd, vals[e:e + 1], v)
            i = jnp.where(upd, e, i)
        return v, i

    v1, i1 = first_max(e_sel)
    row = lax.broadcasted_iota(jnp.int32, e_sel.shape, 0)
    v2, i2 = first_max(jnp.where(row == i1, NEG_INF, e_sel))
    e2 = jnp.exp(v2 - v1)
    w1 = 1.0 / (1.0 + e2)
    w2 = e2 / (1.0 + e2)
    erow = lax.broadcasted_iota(jnp.int32, eid_ref.shape, 0)
    eid_ref[...] = jnp.where(erow == 0, g_idx * MOE_EPG + i1, jnp.where(erow == 1, g_idx * MOE_EPG + i2, 0))
    grow = lax.broadcasted_iota(jnp.int32, lt.shape, 0)
    gates = jnp.where(grow == 0, g_w * w1, jnp.where(grow == 1, g_w * w2, 0.0))
    gate_ref[...] = gates.T


def _route(logits):
    t = logits.shape[0]
    tm = min(ROUTE_TILE, t)
    return pl.pallas_call(
        _route_kernel,
        grid=(t // tm,),
        in_specs=[pl.BlockSpec((tm, ROUTER_COLS), lambda i: (i, 0))],
        out_specs=[pl.BlockSpec((8, tm), lambda i: (0, i)), pl.BlockSpec((tm, ROUTER_COLS), lambda i: (i, 0))],
        out_shape=[jax.ShapeDtypeStruct((8, t), jnp.int32), jax.ShapeDtypeStruct((t, ROUTER_COLS), F32)],
        compiler_params=_cparams(("parallel",)),
        name="moe_route",
    )(logits)


def _dispatch_plan(eid, n_blocks):
    flat = eid.reshape(-1)
    onehot = (flat[:, None] == jnp.arange(MOE_EXPERTS, dtype=jnp.int32)[None, :]).astype(jnp.int32)
    csum = jnp.cumsum(onehot, axis=0)
    pos = jnp.sum(csum * onehot, axis=1) - 1
    counts = csum[-1]
    padded = (counts + MOE_BLK - 1) // MOE_BLK * MOE_BLK
    pad_end = jnp.cumsum(padded)
    pad_start = pad_end - padded
    dest = jnp.sum(pad_start[None, :] * onehot, axis=1) + pos
    block_e = jnp.minimum(jnp.searchsorted(pad_end, jnp.arange(n_blocks, dtype=jnp.int32) * MOE_BLK, side="right"),
                          MOE_EXPERTS - 1).astype(jnp.int32)
    n_used = (pad_end[-1] // MOE_BLK).astype(jnp.int32).reshape(1)
    return dest.astype(jnp.int32).reshape(eid.shape), block_e, n_used


def _row_copy(src_ref, src_row, dst_ref, dst_row, sem):
    return pltpu.make_async_copy(src_ref.at[pl.ds(src_row, 1), :], dst_ref.at[pl.ds(dst_row, 1), :], sem)


def _dispatch_kernel(dest_ref, h_ref, xs_in_ref, xs_ref, sem, *, tm):
    del xs_in_ref
    base = pl.program_id(0) * tm

    def start(i, carry):
        for k in range(2):
            _row_copy(h_ref, base + i, xs_ref, dest_ref[0, 0, k * tm + i], sem).start()
        return carry

    def wait(i, carry):
        for k in range(2):
            _row_copy(h_ref, base + i, xs_ref, dest_ref[0, 0, k * tm + i], sem).wait()
        return carry

    lax.fori_loop(0, tm, start, 0)
    lax.fori_loop(0, tm, wait, 0)


def _dispatch(h, dest, rows):
    t, d = h.shape
    tm = min(DISP_TILE, t)
    steps = t // tm
    dest_steps = dest.reshape(2, steps, tm).transpose(1, 0, 2).reshape(steps, 1, 2 * tm)
    kern = functools.partial(_dispatch_kernel, tm=tm)
    return pl.pallas_call(
        kern,
        grid=(steps,),
        in_specs=[pl.BlockSpec((1, 1, 2 * tm), lambda i: (i, 0, 0), memory_space=pltpu.SMEM),
                  pl.BlockSpec(memory_space=pl.ANY),
                  pl.BlockSpec(memory_space=pl.ANY)],
        out_specs=pl.BlockSpec(memory_space=pl.ANY),
        out_shape=jax.ShapeDtypeStruct((rows, d), F32),
        scratch_shapes=[pltpu.SemaphoreType.DMA(())],
        input_output_aliases={2: 0},
        compiler_params=_cparams(("arbitrary",)),
        name="moe_dispatch",
    )(dest_steps, h, jnp.zeros((rows, d), F32))


def _expert_kernel(be_ref, nu_ref, xs_ref, wg_ref, wu_ref, wd_ref, ys_ref):
    live = pl.program_id(0) < nu_ref[0]

    @pl.when(jnp.logical_not(live))
    def _():
        ys_ref[...] = jnp.zeros(ys_ref.shape, F32)

    @pl.when(live)
    def _():
        x = xs_ref[...].astype(BF16)
        g = jnp.dot(x, wg_ref[...].astype(BF16), preferred_element_type=F32)
        u = jnp.dot(x, wu_ref[...].astype(BF16), preferred_element_type=F32)
        a = (g * (1.0 / (1.0 + jnp.exp(-g))) * u).astype(BF16)
        ys_ref[...] = jnp.dot(a, wd_ref[...].astype(BF16), preferred_element_type=F32)


def _experts(xs, block_e, n_used, w_gate, w_up, w_down, layer):
    rows, d = xs.shape
    n_blocks = rows // MOE_BLK
    hid = w_gate.shape[-1]
    used = lambda i, be, nu: (jnp.minimum(i, nu[0] - 1), 0)
    expert = lambda i, be, nu: (layer, be[jnp.minimum(i, nu[0] - 1)], 0, 0)
    grid_spec = pltpu.PrefetchScalarGridSpec(
        num_scalar_prefetch=2,
        grid=(n_blocks,),
        in_specs=[pl.BlockSpec((MOE_BLK, d), used),
                  pl.BlockSpec((None, None, d, hid), expert),
                  pl.BlockSpec((None, None, d, hid), expert),
                  pl.BlockSpec((None, None, hid, d), expert)],
        out_specs=pl.BlockSpec((MOE_BLK, d), lambda i, be, nu: (i, 0)),
    )
    return pl.pallas_call(
        _expert_kernel,
        grid_spec=grid_spec,
        out_shape=jax.ShapeDtypeStruct((rows, d), F32),
        compiler_params=_cparams(("arbitrary",)),
        name="moe_experts",
    )(block_e, n_used, xs, w_gate, w_up, w_down)


def _combine_kernel(dest_ref, ys_ref, x_ref, gf_ref, gate_ref, o_ref, buf, sem, *, tm):
    def start(i, carry):
        for k in range(2):
            _row_copy(ys_ref, dest_ref[0, 0, k * tm + i], buf.at[k], i, sem).start()
        return carry

    def wait(i, carry):
        for k in range(2):
            _row_copy(ys_ref, dest_ref[0, 0, k * tm + i], buf.at[k], i, sem).wait()
        return carry

    lax.fori_loop(0, tm, start, 0)
    lax.fori_loop(0, tm, wait, 0)
    gates = gate_ref[...]
    y = gates[:, 0:1] * buf[0] + gates[:, 1:2] * buf[1]
    o_ref[...] = x_ref[...] + gf_ref[...] * y


def _combine(ys, dest, x, gate_f, gates, seq):
    t, d = x.shape
    tm = min(COMB_TILE, seq)
    steps = t // tm
    per_b = seq // tm
    dest_steps = dest.reshape(2, steps, tm).transpose(1, 0, 2).reshape(steps, 1, 2 * tm)
    kern = functools.partial(_combine_kernel, tm=tm)
    return pl.pallas_call(
        kern,
        grid=(steps,),
        in_specs=[pl.BlockSpec((1, 1, 2 * tm), lambda i: (i, 0, 0), memory_space=pltpu.SMEM),
                  pl.BlockSpec(memory_space=pl.ANY),
                  pl.BlockSpec((tm, d), lambda i: (i, 0)),
                  pl.BlockSpec((None, 1, d), lambda i: (i // per_b, 0, 0)),
                  pl.BlockSpec((tm, ROUTER_COLS), lambda i: (i, 0))],
        out_specs=pl.BlockSpec((tm, d), lambda i: (i, 0)),
        out_shape=jax.ShapeDtypeStruct((t, d), F32),
        scratch_shapes=[pltpu.VMEM((2, tm, d), F32), pltpu.SemaphoreType.DMA(())],
        compiler_params=_cparams(("arbitrary",)),
        name="moe_combine",
    )(dest_steps, ys, x, gate_f, gates)


def _moe(h, logits, x1, gate_f, w_gate, w_up, w_down, layer, seq):
    t = h.shape[0]
    n_blocks = 2 * t // MOE_BLK + MOE_EXPERTS
    eid, gates = _route(logits)
    dest, block_e, n_used = _dispatch_plan(eid[:2], n_blocks)
    xs = _dispatch(h, dest, n_blocks * MOE_BLK)
    ys = _experts(xs, block_e, n_used, w_gate, w_up, w_down, layer)
    return _combine(ys, dest, x1, gate_f, gates, seq)


def _router_weights(w_group, b_group, w_expert, b_expert):
    d = w_group.shape[0]
    w = jnp.zeros((d, ROUTER_COLS), F32)
    w = w.at[:, 0:MOE_GROUPS].set(w_group.astype(F32))
    w = w.at[:, EXPERT_COL0:EXPERT_COL0 + MOE_EXPERTS].set(w_expert.astype(F32))
    b = jnp.zeros((1, ROUTER_COLS), F32)
    b = b.at[0, 0:MOE_GROUPS].set(b_group.astype(F32))
    b = b.at[0, EXPERT_COL0:EXPERT_COL0 + MOE_EXPERTS].set(b_expert.astype(F32))
    return w, b


def _lambda_init(layer):
    return 0.8 - 0.6 * math.exp(-0.3 * layer)


def kernel(x, c, rel_bias, ada_w, ada_b, norm_attn, norm_ffn, a_w_qkv, a_q_norm, a_k_norm, a_lambda, a_subln, a_w_o, kv_norm, kv_ada_w, kv_ada_b, kv_w, kv_k_norm, b_w_q, b_q_norm, b_w_o, moe_w_group, moe_b_group, moe_w_expert, moe_b_expert, moe_w_gate, moe_w_up, moe_w_down):
    batch, seq, d = x.shape
    t = batch * seq
    scale = HEAD_DIM ** -0.5
    n_bw = N_GROUPS * B_WIDTH

    c_pad = jnp.zeros((8, d), F32).at[:batch].set(c.astype(F32))
    mod = _modulation(c_pad, ada_w, ada_b)[:, :batch]
    kv_mod = _modulation(c_pad, kv_ada_w[None], kv_ada_b[None])[0, :batch]

    def part(m, i):
        return m[:, i * d:(i + 1) * d].reshape(batch, 1, d)

    a_bias, nd = _attn_a_bias_tiles(rel_bias, seq, min(ATT_TILE, seq))
    xf = x.reshape(t, d).astype(F32)
    kv = None
    for layer in range(DEPTH):
        m = mod[layer]
        sh_a, sc_a, g_a, sh_f, sc_f, g_f = (part(m, i) for i in range(6))
        w_r, b_r = _router_weights(moe_w_group[layer], moe_b_group[layer], moe_w_expert[layer], moe_b_expert[layer])
        if layer < N_A_LAYERS:
            qk_gain = jnp.concatenate([jnp.tile(a_q_norm[layer].astype(F32) * scale, 2 * N_HEADS),
                                       jnp.tile(a_k_norm[layer].astype(F32), 2 * N_HEADS),
                                       jnp.ones((N_HEADS * LANES,), F32)])
            qkv = _norm_proj(xf, norm_attn[layer], sh_a, sc_a, a_w_qkv[layer].astype(BF16), qk_gain,
                             2 * N_HEADS * LANES, seq)
            o = _attn_a(qkv, a_bias, nd, a_lambda[layer].astype(F32), a_subln[layer].astype(F32),
                        _lambda_init(layer), batch, seq)
            att = [o]
            wo = a_w_o[layer].astype(BF16)
        else:
            j = layer - N_A_LAYERS
            if kv is None:
                k_gain = jnp.concatenate([jnp.tile(kv_k_norm.astype(F32), (1, N_HEADS)).reshape(-1),
                                          jnp.ones((n_bw,), F32)])
                kv = _norm_proj(xf, kv_norm, part(kv_mod, 0), part(kv_mod, 1), kv_w.astype(BF16), k_gain, n_bw, seq)
            q_gain = jnp.tile(b_q_norm[j].astype(F32) * scale, (1, N_HEADS)).reshape(-1)
            q = _norm_proj(xf, norm_attn[layer], sh_a, sc_a, b_w_q[j].astype(BF16), q_gain, n_bw, seq)
            outs, lses = [], []
            for g, (window, dil) in enumerate(B_GROUPS):
                o_g, lse_g = _attn_b_group(q, kv, _attn_b_bias_tiles(rel_bias, window // dil, dil), g, dil, batch, seq)
                outs.append(o_g)
                lses.append(lse_g)
            att = outs + lses
            wo = b_w_o[j].astype(BF16)
        x1, hf, logits = _out_proj(att, wo, xf, g_a, norm_ffn[layer], sh_f, sc_f, w_r, b_r, seq)
        xf = _moe(hf, logits, x1, g_f, moe_w_gate, moe_w_up, moe_w_down, layer, seq)
    return xf.reshape(batch, seq, d).astype(x.dtype)
```

```python
import functools
import math

import jax
import jax.numpy as jnp
import numpy as np
from jax import lax
from jax.experimental import pallas as pl
from jax.experimental.pallas import tpu as pltpu

F32 = jnp.float32
BF16 = jnp.bfloat16

D_MODEL = 1024
DEPTH = 4
N_A_LAYERS = 2
N_HEADS = 8
HEAD_DIM = 64
B_GROUPS = ((128, 1), (512, 4), (2048, 16))
N_GROUPS = len(B_GROUPS)
B_WIDTH = N_HEADS * HEAD_DIM
REL_BUCKETS = 32
REL_MAX_EXACT = 16
REL_MAX_DIST = 2048
MOE_GROUPS = 4
MOE_EPG = 8
MOE_EXPERTS = MOE_GROUPS * MOE_EPG
MOE_HIDDEN = 512
EPS = 1e-6
NEG_INF = float("-inf")

LANES = 128
MXU_DIM = 256
ROUTER_COLS = 128
EXPERT_COL0 = 8

TOK_TILE = 512
ATT_TILE = 512
BQ_TILE = 256
BQ_SUB = 128
MOE_BLK = 256
ROUTE_TILE = 1024
DISP_TILE = 512
COMB_TILE = 256
VMEM_LIMIT = 56 * 1024 * 1024


def _cparams(sem):
    return pltpu.CompilerParams(dimension_semantics=sem, vmem_limit_bytes=VMEM_LIMIT)


def _bucket_table(max_dist):
    n = np.arange(max_dist + 1)
    nf = np.maximum(n, 1).astype(np.float64)
    large = REL_MAX_EXACT + (np.log(nf / REL_MAX_EXACT) / math.log(REL_MAX_DIST / REL_MAX_EXACT)
                             * (REL_BUCKETS - REL_MAX_EXACT)).astype(np.int64)
    large = np.minimum(large, REL_BUCKETS - 1)
    return np.where(n < REL_MAX_EXACT, n, large).astype(np.int32)


def _toeplitz(v, n, m):
    length = v.shape[-1]
    assert length >= n + m - 1 and length - 1 >= m
    lead = v.shape[:-1]
    flat = jnp.tile(v, (1,) * len(lead) + (n,))[..., :n * (length - 1)]
    return flat.reshape(lead + (n, length - 1))[..., :m]


def _diag_values(rel_bias, dist, ok):
    table = _bucket_table(int(dist.max()))
    vals = rel_bias.astype(F32)[table[np.clip(dist, 0, None)]]
    vals = jnp.where(ok[..., None], vals, NEG_INF)
    return jnp.moveaxis(vals, -1, 0)


def _attn_a_bias_tiles(rel_bias, seq, tile):
    table = _bucket_table(seq)
    last_start = int(np.argmax(table == REL_BUCKETS - 1))
    nd = 0
    while nd * tile - (tile - 1) < last_start and nd * tile < seq:
        nd += 1
    nd += 1
    length = 2 * tile
    u = np.arange(length)
    u = np.where(u < tile, u, u - length)
    dist = np.arange(nd)[:, None] * tile - u[None, :]
    diag = _diag_values(rel_bias, dist, dist >= 0)
    return _toeplitz(diag, tile, tile), nd


def _attn_b_bias_tiles(rel_bias, band, dil):
    length = 3 * BQ_SUB
    u = np.arange(length)
    u = np.where(u < 2 * BQ_SUB, u, u - length)
    rel = BQ_SUB - u
    ok = (rel >= 0) & (rel <= band)
    general = _toeplitz(_diag_values(rel_bias, dil * rel, ok), BQ_SUB, 2 * BQ_SUB)
    no_prev = np.arange(2 * BQ_SUB)[None, None, :] >= BQ_SUB
    return jnp.stack([jnp.where(no_prev, general, NEG_INF), general])


def _group_sum_matrix():
    r = np.arange(MXU_DIM)
    return jnp.asarray((r[:, None] // HEAD_DIM) == (r[None, :] // HEAD_DIM), BF16)


def _mod_kernel(c_ref, w_ref, b_ref, o_ref):
    c = c_ref[...]
    c_act = c * (1.0 / (1.0 + jnp.exp(-c)))
    o_ref[...] = jnp.dot(c_act, w_ref[...], preferred_element_type=F32,
                         precision=lax.Precision.HIGHEST) + b_ref[...]


def _modulation(c_pad, w, b, tn=1024):
    nl, d, n = w.shape
    return pl.pallas_call(
        _mod_kernel,
        grid=(nl, n // tn),
        in_specs=[pl.BlockSpec((8, d), lambda l, j: (0, 0)),
                  pl.BlockSpec((None, d, tn), lambda l, j: (l, 0, j)),
                  pl.BlockSpec((None, 1, tn), lambda l, j: (l, 0, j))],
        out_specs=pl.BlockSpec((None, 8, tn), lambda l, j: (l, 0, j)),
        out_shape=jax.ShapeDtypeStruct((nl, 8, n), F32),
        compiler_params=_cparams(("parallel", "parallel")),
        name="adaln_mod",
    )(c_pad, w, b.reshape(nl, 1, n))


def _rms(x, gain):
    return x * lax.rsqrt(jnp.mean(x * x, axis=-1, keepdims=True) + EPS) * gain


def _proj_kernel(x_ref, g_ref, sh_ref, sc_ref, w_ref, hg_ref, bd_ref, o_ref, *, n_out, n_norm):
    h = _rms(x_ref[...], g_ref[...]) * (1.0 + sc_ref[...]) + sh_ref[...]
    hb = h.astype(BF16)
    for c0 in range(0, n_out, MXU_DIM):
        a = jnp.dot(hb, w_ref[:, c0:c0 + MXU_DIM], preferred_element_type=F32)
        if c0 < n_norm:
            ms = jnp.dot((a * a).astype(BF16), bd_ref[...], preferred_element_type=F32) * (1.0 / HEAD_DIM)
            a = a * lax.rsqrt(ms + EPS) * hg_ref[:, c0:c0 + MXU_DIM]
        o_ref[:, c0:c0 + MXU_DIM] = a.astype(BF16)


def _norm_proj(x, gain, shift, scale, w_bf16, head_gain, n_norm, seq):
    t, d = x.shape
    n_out = w_bf16.shape[1]
    tm = min(TOK_TILE, seq)
    per_b = seq // tm
    kern = functools.partial(_proj_kernel, n_out=n_out, n_norm=n_norm)
    return pl.pallas_call(
        kern,
        grid=(t // tm,),
        in_specs=[pl.BlockSpec((tm, d), lambda i: (i, 0)),
                  pl.BlockSpec((1, d), lambda i: (0, 0)),
                  pl.BlockSpec((None, 1, d), lambda i: (i // per_b, 0, 0)),
                  pl.BlockSpec((None, 1, d), lambda i: (i // per_b, 0, 0)),
                  pl.BlockSpec((d, n_out), lambda i: (0, 0)),
                  pl.BlockSpec((1, n_out), lambda i: (0, 0)),
                  pl.BlockSpec((MXU_DIM, MXU_DIM), lambda i: (0, 0))],
        out_specs=pl.BlockSpec((tm, n_out), lambda i: (i, 0)),
        out_shape=jax.ShapeDtypeStruct((t, n_out), BF16),
        compiler_params=_cparams(("parallel",)),
        name="norm_proj",
    )(x, gain.reshape(1, d), shift, scale, w_bf16, head_gain.reshape(1, n_out), _group_sum_matrix())


def _attn_a_kernel(q_ref, k_ref, v_ref, bias_ref, lam_ref, g_ref, o_ref, m_sc, l_sc, acc_sc,
                   *, tile, nd, lam_init):
    qi = pl.program_id(2)
    q = q_ref[...]
    lane = lax.broadcasted_iota(jnp.int32, q.shape, 1)
    zero = jnp.zeros_like(q)
    qq = jnp.concatenate([jnp.where(lane < HEAD_DIM, q, zero),
                          jnp.where(lane >= HEAD_DIM, q, zero)], axis=0)
    m_sc[...] = jnp.full(m_sc.shape, NEG_INF, F32)
    l_sc[...] = jnp.zeros(l_sc.shape, F32)
    acc_sc[...] = jnp.zeros(acc_sc.shape, F32)

    def body(j, carry):
        start = pl.multiple_of(j * tile, tile)
        k = k_ref[pl.ds(start, tile), :]
        v = v_ref[pl.ds(start, tile), :]
        s = lax.dot_general(qq, k, (((1,), (1,)), ((), ())), preferred_element_type=F32)
        bias = bias_ref[jnp.minimum(qi - j, nd - 1)]
        for c in range(2):
            sc = s[c * tile:(c + 1) * tile] + bias
            m_prev = m_sc[c]
            m_new = jnp.maximum(m_prev, jnp.max(sc, axis=-1, keepdims=True))
            alpha = jnp.exp(m_prev - m_new)
            p = jnp.exp(sc - m_new)
            l_sc[c] = alpha * l_sc[c] + jnp.sum(p, axis=-1, keepdims=True)
            acc_sc[c] = alpha * acc_sc[c] + jnp.dot(p.astype(BF16), v, preferred_element_type=F32)
            m_sc[c] = m_new
        return carry

    lax.fori_loop(0, qi + 1, body, 0)

    lp = lam_ref[...]
    lam = (jnp.exp(jnp.sum(lp[0:1] * lp[1:2], axis=-1, keepdims=True))
           - jnp.exp(jnp.sum(lp[2:3] * lp[3:4], axis=-1, keepdims=True)) + lam_init)
    o = acc_sc[0] / l_sc[0] - lam * (acc_sc[1] / l_sc[1])
    o_ref[...] = (_rms(o, g_ref[...]) * (1.0 - lam_init)).astype(BF16)


def _attn_a(qkv, bias_tiles, nd, lam_p, subln, lam_init, batch, seq):
    tile = min(ATT_TILE, seq)
    nq = seq // tile
    qkv3 = qkv.reshape(batch, seq, 3 * N_HEADS * LANES)
    kern = functools.partial(_attn_a_kernel, tile=tile, nd=nd, lam_init=lam_init)
    out = pl.pallas_call(
        kern,
        grid=(N_HEADS, batch, nq),
        in_specs=[pl.BlockSpec((None, tile, LANES), lambda h, b, i: (b, i, h)),
                  pl.BlockSpec((None, seq, LANES), lambda h, b, i: (b, 0, N_HEADS + h)),
                  pl.BlockSpec((None, seq, LANES), lambda h, b, i: (b, 0, 2 * N_HEADS + h)),
                  pl.BlockSpec((None, nd, tile, tile), lambda h, b, i: (h, 0, 0, 0)),
                  pl.BlockSpec((4, HEAD_DIM), lambda h, b, i: (0, 0)),
                  pl.BlockSpec((1, LANES), lambda h, b, i: (0, 0))],
        out_specs=pl.BlockSpec((None, tile, LANES), lambda h, b, i: (b, i, h)),
        out_shape=jax.ShapeDtypeStruct((batch, seq, N_HEADS * LANES), BF16),
        scratch_shapes=[pltpu.VMEM((2, tile, 1), F32), pltpu.VMEM((2, tile, 1), F32),
                        pltpu.VMEM((2, tile, LANES), F32)],
        compiler_params=_cparams(("parallel", "parallel", "parallel")),
        name="attn_a",
    )(qkv3, qkv3, qkv3, bias_tiles, lam_p, subln.reshape(1, LANES))
    return out.reshape(batch * seq, N_HEADS * LANES)


def _attn_b_kernel(q_ref, k_ref, v_ref, kp_ref, vp_ref, bias_ref, o_ref, lse_ref, *, tq):
    n = pl.program_id(2)
    for sb in range(tq // BQ_SUB):
        rows = slice(sb * BQ_SUB, (sb + 1) * BQ_SUB)
        prev = slice((sb - 1) * BQ_SUB, sb * BQ_SUB)
        qs = q_ref[rows, :]
        if sb == 0:
            kk = jnp.concatenate([kp_ref[...], k_ref[rows, :]], axis=0)
            vv = jnp.concatenate([vp_ref[...], v_ref[rows, :]], axis=0)
            variant = jnp.where(n == 0, 0, 1)
        else:
            kk = jnp.concatenate([k_ref[prev, :], k_ref[rows, :]], axis=0)
            vv = jnp.concatenate([v_ref[prev, :], v_ref[rows, :]], axis=0)
            variant = 1
        lane = lax.broadcasted_iota(jnp.int32, (BQ_SUB, LANES), 1)
        low = lane < HEAD_DIM
        for hp in range(N_HEADS // 2):
            cols = slice(hp * LANES, (hp + 1) * LANES)
            qp = qs[:, cols]
            zero = jnp.zeros_like(qp)
            qq = jnp.concatenate([jnp.where(low, qp, zero), jnp.where(low, zero, qp)], axis=0)
            s = lax.dot_general(qq, kk[:, cols], (((1,), (1,)), ((), ())), preferred_element_type=F32)
            outs, lses = [], []
            for c in range(2):
                sc = s[c * BQ_SUB:(c + 1) * BQ_SUB] + bias_ref[variant, 2 * hp + c]
                m = jnp.max(sc, axis=-1, keepdims=True)
                p = jnp.exp(sc - m)
                l = jnp.sum(p, axis=-1, keepdims=True)
                outs.append(jnp.dot(p.astype(BF16), vv[:, cols], preferred_element_type=F32) / l)
                lses.append(m + jnp.log(l))
            o_ref[rows, cols] = jnp.where(low, outs[0], outs[1])
            lse_ref[rows, cols] = jnp.where(low, lses[0], lses[1])


def _attn_b_group(q, kv, bias_tiles, g, dil, batch, seq):
    length = seq // dil
    tq = min(BQ_TILE, length)
    nq = length // tq
    sub = tq // BQ_SUB
    q3 = q.reshape(batch, length, dil * N_GROUPS * B_WIDTH)
    kv3 = kv.reshape(batch, length, dil * 2 * N_GROUPS * B_WIDTH)
    kern = functools.partial(_attn_b_kernel, tq=tq)
    blk = lambda rows: (None, rows, B_WIDTH)
    out_shape = jax.ShapeDtypeStruct((batch, length, dil * B_WIDTH), F32)
    o, lse = pl.pallas_call(
        kern,
        grid=(batch, dil, nq),
        in_specs=[pl.BlockSpec(blk(tq), lambda b, r, n: (b, n, r * N_GROUPS + g)),
                  pl.BlockSpec(blk(tq), lambda b, r, n: (b, n, r * 2 * N_GROUPS + g)),
                  pl.BlockSpec(blk(tq), lambda b, r, n: (b, n, r * 2 * N_GROUPS + N_GROUPS + g)),
                  pl.BlockSpec(blk(BQ_SUB), lambda b, r, n: (b, jnp.maximum(n * sub - 1, 0), r * 2 * N_GROUPS + g)),
                  pl.BlockSpec(blk(BQ_SUB),
                               lambda b, r, n: (b, jnp.maximum(n * sub - 1, 0), r * 2 * N_GROUPS + N_GROUPS + g)),
                  pl.BlockSpec((2, N_HEADS, BQ_SUB, 2 * BQ_SUB), lambda b, r, n: (0, 0, 0, 0))],
        out_specs=[pl.BlockSpec(blk(tq), lambda b, r, n: (b, n, r)),
                   pl.BlockSpec(blk(tq), lambda b, r, n: (b, n, r))],
        out_shape=[out_shape, out_shape],
        compiler_params=_cparams(("parallel", "parallel", "parallel")),
        name=f"attn_b_g{g}",
    )(q3, kv3, kv3, kv3, kv3, bias_tiles)
    return o.reshape(batch * seq, B_WIDTH), lse.reshape(batch * seq, B_WIDTH)


def _out_tail(o_bf16, wo_ref, x_ref, ga_ref, gf_ref, shf_ref, scf_ref, wr_ref, br_ref, x1_ref, hf_ref, lg_ref):
    y = jnp.dot(o_bf16, wo_ref[...], preferred_element_type=F32)
    x1 = x_ref[...] + ga_ref[...] * y
    x1_ref[...] = x1
    hf = _rms(x1, gf_ref[...]) * (1.0 + scf_ref[...]) + shf_ref[...]
    hf_ref[...] = hf
    lg_ref[...] = jnp.dot(hf, wr_ref[...], preferred_element_type=F32,
                          precision=lax.Precision.HIGHEST) + br_ref[...]


def _out_a_kernel(o_ref, *rest):
    _out_tail(o_ref[...], *rest)


def _out_b_kernel(o0, o1, o2, l0, l1, l2, *rest):
    la, lb, lc = l0[...], l1[...], l2[...]
    m = jnp.maximum(jnp.maximum(la, lb), lc)
    ea, eb, ec = jnp.exp(la - m), jnp.exp(lb - m), jnp.exp(lc - m)
    z = ea + eb + ec
    o = (ea / z) * o0[...] + (eb / z) * o1[...] + (ec / z) * o2[...]
    _out_tail(o.astype(BF16), *rest)


def _out_proj(att_inputs, wo_bf16, x, gate_a, gain_f, shift_f, scale_f, w_router, b_router, seq):
    t, d = x.shape
    tm = min(TOK_TILE, seq)
    per_b = seq // tm
    row = lambda i: (i, 0)
    fixed = lambda i: (0, 0)
    per_batch = lambda i: (i // per_b, 0, 0)
    att_specs = [pl.BlockSpec((tm, a.shape[1]), row) for a in att_inputs]
    kern = _out_a_kernel if len(att_inputs) == 1 else _out_b_kernel
    return pl.pallas_call(
        kern,
        grid=(t // tm,),
        in_specs=att_specs + [pl.BlockSpec(wo_bf16.shape, fixed),
                              pl.BlockSpec((tm, d), row),
                              pl.BlockSpec((None, 1, d), per_batch),
                              pl.BlockSpec((1, d), fixed),
                              pl.BlockSpec((None, 1, d), per_batch),
                              pl.BlockSpec((None, 1, d), per_batch),
                              pl.BlockSpec((d, ROUTER_COLS), fixed),
                              pl.BlockSpec((1, ROUTER_COLS), fixed)],
        out_specs=[pl.BlockSpec((tm, d), row), pl.BlockSpec((tm, d), row),
                   pl.BlockSpec((tm, ROUTER_COLS), row)],
        out_shape=[jax.ShapeDtypeStruct((t, d), F32), jax.ShapeDtypeStruct((t, d), F32),
                   jax.ShapeDtypeStruct((t, ROUTER_COLS), F32)],
        compiler_params=_cparams(("parallel",)),
        name="out_proj",
    )(*att_inputs, wo_bf16, x, gate_a, gain_f.reshape(1, d), shift_f, scale_f, w_router, b_router)


def _route_kernel(lg_ref, tri_ref, eid_ref, gate_ref, cnt_ref, cnt_sc):
    lt = lg_ref[...].T
    best, g_idx = lt[0:1], jnp.zeros((1, lt.shape[1]), jnp.int32)
    for g in range(1, MOE_GROUPS):
        upd = lt[g:g + 1] > best
        best = jnp.where(upd, lt[g:g + 1], best)
        g_idx = jnp.where(upd, g, g_idx)
    denom = jnp.zeros_like(best)
    for g in range(MOE_GROUPS):
        denom = denom + jnp.exp(lt[g:g + 1] - best)
    g_w = 1.0 / denom
    e_sel = lt[EXPERT_COL0:EXPERT_COL0 + MOE_EPG]
    for g in range(1, MOE_GROUPS):
        r0 = EXPERT_COL0 + g * MOE_EPG
        e_sel = jnp.where(g_idx == g, lt[r0:r0 + MOE_EPG], e_sel)

    def first_max(vals):
        v, i = vals[0:1], jnp.zeros((1, vals.shape[1]), jnp.int32)
        for e in range(1, MOE_EPG):
            upd = vals[e:e + 1] > v
            v = jnp.where(upd, vals[e:e + 1], v)
            i = jnp.where(upd, e, i)
        return v, i

    v1, i1 = first_max(e_sel)
    row = lax.broadcasted_iota(jnp.int32, e_sel.shape, 0)
    v2, i2 = first_max(jnp.where(row == i1, NEG_INF, e_sel))
    e2 = jnp.exp(v2 - v1)
    w1 = 1.0 / (1.0 + e2)
    w2 = e2 / (1.0 + e2)
    e1 = g_idx * MOE_EPG + i1
    e2 = g_idx * MOE_EPG + i2

    @pl.when(pl.program_id(0) == 0)
    def _():
        cnt_sc[...] = jnp.zeros(cnt_sc.shape, F32)

    erows = lax.broadcasted_iota(jnp.int32, (MOE_EXPERTS, lt.shape[1]), 0)
    carry = cnt_sc[...]
    ranks = []
    for e_k in (e1, e2):
        oh = (erows == e_k).astype(F32)
        pre = jnp.dot(oh.astype(BF16), tri_ref[...], preferred_element_type=F32)
        ranks.append(jnp.sum(oh * (pre - 1.0 + carry), axis=0, keepdims=True))
        carry = carry + pre[:, lt.shape[1] - 1:]
    cnt_sc[...] = carry
    cnt_ref[...] = jnp.broadcast_to(carry, cnt_ref.shape).astype(jnp.int32)

    erow = lax.broadcasted_iota(jnp.int32, eid_ref.shape, 0)
    eid_ref[...] = jnp.where(erow == 0, e1, jnp.where(erow == 1, e2, jnp.where(
        erow == 2, ranks[0].astype(jnp.int32), jnp.where(erow == 3, ranks[1].astype(jnp.int32), 0))))
    grow = lax.broadcasted_iota(jnp.int32, lt.shape, 0)
    gates = jnp.where(grow == 0, g_w * w1, jnp.where(grow == 1, g_w * w2, 0.0))
    gate_ref[...] = gates.T


def _route(logits):
    t = logits.shape[0]
    tm = min(ROUTE_TILE, t)
    r = np.arange(tm)
    tri = jnp.asarray(r[:, None] <= r[None, :], BF16)
    return pl.pallas_call(
        _route_kernel,
        grid=(t // tm,),
        in_specs=[pl.BlockSpec((tm, ROUTER_COLS), lambda i: (i, 0)),
                  pl.BlockSpec((tm, tm), lambda i: (0, 0))],
        out_specs=[pl.BlockSpec((8, tm), lambda i: (0, i)), pl.BlockSpec((tm, ROUTER_COLS), lambda i: (i, 0)),
                   pl.BlockSpec((MOE_EXPERTS, LANES), lambda i: (0, 0))],
        out_shape=[jax.ShapeDtypeStruct((8, t), jnp.int32), jax.ShapeDtypeStruct((t, ROUTER_COLS), F32),
                   jax.ShapeDtypeStruct((MOE_EXPERTS, LANES), jnp.int32)],
        scratch_shapes=[pltpu.VMEM((MOE_EXPERTS, 1), F32)],
        compiler_params=_cparams(("arbitrary",)),
        name="moe_route",
    )(logits, tri)


def _dispatch_plan(route, counts, n_blocks):
    eid, rank = route[0:2], route[2:4]
    padded = (counts + MOE_BLK - 1) // MOE_BLK * MOE_BLK
    pad_end = jnp.cumsum(padded)
    pad_start = pad_end - padded
    onehot = eid[..., None] == jnp.arange(MOE_EXPERTS, dtype=jnp.int32)
    dest = jnp.sum(jnp.where(onehot, pad_start, 0), axis=-1) + rank
    block_e = jnp.minimum(jnp.searchsorted(pad_end, jnp.arange(n_blocks, dtype=jnp.int32) * MOE_BLK, side="right"),
                          MOE_EXPERTS - 1).astype(jnp.int32)
    n_used = (pad_end[-1] // MOE_BLK).astype(jnp.int32).reshape(1)
    return dest.astype(jnp.int32), block_e, n_used


def _row_copy(src_ref, src_row, dst_ref, dst_row, sem):
    return pltpu.make_async_copy(src_ref.at[pl.ds(src_row, 1), :], dst_ref.at[pl.ds(dst_row, 1), :], sem)


def _dispatch_kernel(dest_ref, h_ref, xs_in_ref, xs_ref, sem, *, tm):
    del xs_in_ref

    def start(i, carry):
        for k in range(2):
            _row_copy(h_ref, i, xs_ref, dest_ref[0, 0, k * tm + i], sem).start()
        return carry

    def wait(i, carry):
        for k in range(2):
            _row_copy(h_ref, i, xs_ref, dest_ref[0, 0, k * tm + i], sem).wait()
        return carry

    lax.fori_loop(0, tm, start, 0)
    lax.fori_loop(0, tm, wait, 0)


def _dispatch(h, dest, rows):
    t, d = h.shape
    tm = min(DISP_TILE, t)
    steps = t // tm
    dest_steps = dest.reshape(2, steps, tm).transpose(1, 0, 2).reshape(steps, 1, 2 * tm)
    kern = functools.partial(_dispatch_kernel, tm=tm)
    return pl.pallas_call(
        kern,
        grid=(steps,),
        in_specs=[pl.BlockSpec((1, 1, 2 * tm), lambda i: (i, 0, 0), memory_space=pltpu.SMEM),
                  pl.BlockSpec((tm, d), lambda i: (i, 0)),
                  pl.BlockSpec(memory_space=pl.ANY)],
        out_specs=pl.BlockSpec(memory_space=pl.ANY),
        out_shape=jax.ShapeDtypeStruct((rows, d), F32),
        scratch_shapes=[pltpu.SemaphoreType.DMA(())],
        input_output_aliases={2: 0},
        compiler_params=_cparams(("arbitrary",)),
        name="moe_dispatch",
    )(dest_steps, h, jnp.zeros((rows, d), F32))


def _expert_kernel(be_ref, nu_ref, xs_ref, wg_ref, wu_ref, wd_ref, ys_ref):
    live = pl.program_id(0) < nu_ref[0]

    @pl.when(jnp.logical_not(live))
    def _():
        ys_ref[...] = jnp.zeros(ys_ref.shape, F32)

    @pl.when(live)
    def _():
        x = xs_ref[...].astype(BF16)
        g = jnp.dot(x, wg_ref[...].astype(BF16), preferred_element_type=F32)
        u = jnp.dot(x, wu_ref[...].astype(BF16), preferred_element_type=F32)
        a = (g * (1.0 / (1.0 + jnp.exp(-g))) * u).astype(BF16)
        ys_ref[...] = jnp.dot(a, wd_ref[...].astype(BF16), preferred_element_type=F32)


def _experts(xs, block_e, n_used, w_gate, w_up, w_down, layer):
    rows, d = xs.shape
    n_blocks = rows // MOE_BLK
    hid = w_gate.shape[-1]
    used = lambda i, be, nu: (jnp.minimum(i, nu[0] - 1), 0)
    expert = lambda i, be, nu: (layer, be[jnp.minimum(i, nu[0] - 1)], 0, 0)
    grid_spec = pltpu.PrefetchScalarGridSpec(
        num_scalar_prefetch=2,
        grid=(n_blocks,),
        in_specs=[pl.BlockSpec((MOE_BLK, d), used),
                  pl.BlockSpec((None, None, d, hid), expert),
                  pl.BlockSpec((None, None, d, hid), expert),
                  pl.BlockSpec((None, None, hid, d), expert)],
        out_specs=pl.BlockSpec((MOE_BLK, d), lambda i, be, nu: (i, 0)),
    )
    return pl.pallas_call(
        _expert_kernel,
        grid_spec=grid_spec,
        out_shape=jax.ShapeDtypeStruct((rows, d), F32),
        compiler_params=_cparams(("arbitrary",)),
        name="moe_experts",
    )(block_e, n_used, xs, w_gate, w_up, w_down)


def _combine_kernel(dest_ref, ys_ref, x_ref, gf_ref, gate_ref, o_ref, buf, sem, *, tm):
    def start(i, carry):
        for k in range(2):
            _row_copy(ys_ref, dest_ref[0, 0, k * tm + i], buf.at[k], i, sem).start()
        return carry

    def wait(i, carry):
        for k in range(2):
            _row_copy(ys_ref, dest_ref[0, 0, k * tm + i], buf.at[k], i, sem).wait()
        return carry

    lax.fori_loop(0, tm, start, 0)
    lax.fori_loop(0, tm, wait, 0)
    gates = gate_ref[...]
    y = gates[:, 0:1] * buf[0] + gates[:, 1:2] * buf[1]
    o_ref[...] = x_ref[...] + gf_ref[...] * y


def _combine(ys, dest, x, gate_f, gates, seq):
    t, d = x.shape
    tm = min(COMB_TILE, seq)
    steps = t // tm
    per_b = seq // tm
    dest_steps = dest.reshape(2, steps, tm).transpose(1, 0, 2).reshape(steps, 1, 2 * tm)
    kern = functools.partial(_combine_kernel, tm=tm)
    return pl.pallas_call(
        kern,
        grid=(steps,),
        in_specs=[pl.BlockSpec((1, 1, 2 * tm), lambda i: (i, 0, 0), memory_space=pltpu.SMEM),
                  pl.BlockSpec(memory_space=pl.ANY),
                  pl.BlockSpec((tm, d), lambda i: (i, 0)),
                  pl.BlockSpec((None, 1, d), lambda i: (i // per_b, 0, 0)),
                  pl.BlockSpec((tm, ROUTER_COLS), lambda i: (i, 0))],
        out_specs=pl.BlockSpec((tm, d), lambda i: (i, 0)),
        out_shape=jax.ShapeDtypeStruct((t, d), F32),
        scratch_shapes=[pltpu.VMEM((2, tm, d), F32), pltpu.SemaphoreType.DMA(())],
        compiler_params=_cparams(("arbitrary",)),
        name="moe_combine",
    )(dest_steps, ys, x, gate_f, gates)


def _moe(h, logits, x1, gate_f, w_gate, w_up, w_down, layer, seq):
    t = h.shape[0]
    n_blocks = 2 * t // MOE_BLK + MOE_EXPERTS
    route, gates, counts = _route(logits)
    dest, block_e, n_used = _dispatch_plan(route, counts[:, 0], n_blocks)
    xs = _dispatch(h, dest, n_blocks * MOE_BLK)
    ys = _experts(xs, block_e, n_used, w_gate, w_up, w_down, layer)
    return _combine(ys, dest, x1, gate_f, gates, seq)


def _router_weights(w_group, b_group, w_expert, b_expert):
    d = w_group.shape[0]
    w = jnp.zeros((d, ROUTER_COLS), F32)
    w = w.at[:, 0:MOE_GROUPS].set(w_group.astype(F32))
    w = w.at[:, EXPERT_COL0:EXPERT_COL0 + MOE_EXPERTS].set(w_expert.astype(F32))
    b = jnp.zeros((1, ROUTER_COLS), F32)
    b = b.at[0, 0:MOE_GROUPS].set(b_group.astype(F32))
    b = b.at[0, EXPERT_COL0:EXPERT_COL0 + MOE_EXPERTS].set(b_expert.astype(F32))
    return w, b


def _lambda_init(layer):
    return 0.8 - 0.6 * math.exp(-0.3 * layer)


def kernel(x, c, rel_bias, ada_w, ada_b, norm_attn, norm_ffn, a_w_qkv, a_q_norm, a_k_norm, a_lambda, a_subln, a_w_o, kv_norm, kv_ada_w, kv_ada_b, kv_w, kv_k_norm, b_w_q, b_q_norm, b_w_o, moe_w_group, moe_b_group, moe_w_expert, moe_b_expert, moe_w_gate, moe_w_up, moe_w_down):
    batch, seq, d = x.shape
    t = batch * seq
    scale = HEAD_DIM ** -0.5
    n_bw = N_GROUPS * B_WIDTH

    c_pad = jnp.zeros((8, d), F32).at[:batch].set(c.astype(F32))
    mod = _modulation(c_pad, ada_w, ada_b)[:, :batch]
    kv_mod = _modulation(c_pad, kv_ada_w[None], kv_ada_b[None])[0, :batch]

    def part(m, i):
        return m[:, i * d:(i + 1) * d].reshape(batch, 1, d)

    a_bias, nd = _attn_a_bias_tiles(rel_bias, seq, min(ATT_TILE, seq))
    b_bias = [_attn_b_bias_tiles(rel_bias, window // dil, dil) for window, dil in B_GROUPS]
    xf = x.reshape(t, d).astype(F32)
    kv = None
    for layer in range(DEPTH):
        m = mod[layer]
        sh_a, sc_a, g_a, sh_f, sc_f, g_f = (part(m, i) for i in range(6))
        w_r, b_r = _router_weights(moe_w_group[layer], moe_b_group[layer], moe_w_expert[layer], moe_b_expert[layer])
        if layer < N_A_LAYERS:
            qk_gain = jnp.concatenate([jnp.tile(a_q_norm[layer].astype(F32) * scale, 2 * N_HEADS),
                                       jnp.tile(a_k_norm[layer].astype(F32), 2 * N_HEADS),
                                       jnp.ones((N_HEADS * LANES,), F32)])
            qkv = _norm_proj(xf, norm_attn[layer], sh_a, sc_a, a_w_qkv[layer].astype(BF16), qk_gain,
                             2 * N_HEADS * LANES, seq)
            o = _attn_a(qkv, a_bias, nd, a_lambda[layer].astype(F32), a_subln[layer].astype(F32),
                        _lambda_init(layer), batch, seq)
            att = [o]
            wo = a_w_o[layer].astype(BF16)
        else:
            j = layer - N_A_LAYERS
            if kv is None:
                k_gain = jnp.concatenate([jnp.tile(kv_k_norm.astype(F32), (1, N_HEADS)).reshape(-1),
                                          jnp.ones((n_bw,), F32)])
                kv = _norm_proj(xf, kv_norm, part(kv_mod, 0), part(kv_mod, 1), kv_w.astype(BF16), k_gain, n_bw, seq)
            q_gain = jnp.tile(b_q_norm[j].astype(F32) * scale, (1, N_HEADS)).reshape(-1)
            q = _norm_proj(xf, norm_attn[layer], sh_a, sc_a, b_w_q[j].astype(BF16), q_gain, n_bw, seq)
            outs, lses = [], []
            for g, (window, dil) in enumerate(B_GROUPS):
                o_g, lse_g = _attn_b_group(q, kv, b_bias[g], g, dil, batch, seq)
                outs.append(o_g)
                lses.append(lse_g)
            att = outs + lses
            wo = b_w_o[j].astype(BF16)
        x1, hf, logits = _out_proj(att, wo, xf, g_a, norm_ffn[layer], sh_f, sc_f, w_r, b_r, seq)
        xf = _moe(hf, logits, x1, g_f, moe_w_gate, moe_w_up, moe_w_down, layer, seq)
    return xf.reshape(batch, seq, d).astype(x.dtype)
```

```python
import functools
import math

import jax
import jax.numpy as jnp
import numpy as np
from jax import lax
from jax.experimental import pallas as pl
from jax.experimental.pallas import tpu as pltpu

F32 = jnp.float32
BF16 = jnp.bfloat16

D_MODEL = 1024
DEPTH = 4
N_A_LAYERS = 2
N_HEADS = 8
HEAD_DIM = 64
B_GROUPS = ((128, 1), (512, 4), (2048, 16))
N_GROUPS = len(B_GROUPS)
B_WIDTH = N_HEADS * HEAD_DIM
REL_BUCKETS = 32
REL_MAX_EXACT = 16
REL_MAX_DIST = 2048
MOE_GROUPS = 4
MOE_EPG = 8
MOE_EXPERTS = MOE_GROUPS * MOE_EPG
MOE_HIDDEN = 512
EPS = 1e-6
NEG_INF = float("-inf")
LOG2_E = math.log2(math.e)

LANES = 128
MXU_DIM = 256
ROUTER_COLS = 128
EXPERT_COL0 = 8

TOK_TILE = 512
ATT_TILE = 512
BQ_TILE = 256
BQ_SUB = 128
MOE_BLK = 256
ROUTE_TILE = 1024
DISP_TILE = 512
COMB_TILE = 256
VMEM_LIMIT = 56 * 1024 * 1024


def _cparams(sem):
    return pltpu.CompilerParams(dimension_semantics=sem, vmem_limit_bytes=VMEM_LIMIT)


def _bucket_table(max_dist):
    n = np.arange(max_dist + 1)
    nf = np.maximum(n, 1).astype(np.float64)
    large = REL_MAX_EXACT + (np.log(nf / REL_MAX_EXACT) / math.log(REL_MAX_DIST / REL_MAX_EXACT)
                             * (REL_BUCKETS - REL_MAX_EXACT)).astype(np.int64)
    large = np.minimum(large, REL_BUCKETS - 1)
    return np.where(n < REL_MAX_EXACT, n, large).astype(np.int32)


def _toeplitz(v, n, m):
    length = v.shape[-1]
    assert length >= n + m - 1 and length - 1 >= m
    lead = v.shape[:-1]
    flat = jnp.tile(v, (1,) * len(lead) + (n,))[..., :n * (length - 1)]
    return flat.reshape(lead + (n, length - 1))[..., :m]


def _diag_values(rel_bias, dist, ok):
    table = _bucket_table(int(dist.max()))
    vals = rel_bias.astype(F32)[table[np.clip(dist, 0, None)]]
    vals = jnp.where(ok[..., None], vals, NEG_INF)
    return jnp.moveaxis(vals, -1, 0)


def _attn_a_bias_tiles(rel_bias, seq, tile):
    table = _bucket_table(seq)
    last_start = int(np.argmax(table == REL_BUCKETS - 1))
    nd = 0
    while nd * tile - (tile - 1) < last_start and nd * tile < seq:
        nd += 1
    nd += 1
    length = 2 * tile
    u = np.arange(length)
    u = np.where(u < tile, u, u - length)
    dist = np.arange(nd)[:, None] * tile - u[None, :]
    diag = _diag_values(rel_bias, dist, dist >= 0) * LOG2_E
    return _toeplitz(diag, tile, tile), nd


def _attn_b_bias_tiles(rel_bias, band, dil):
    length = 3 * BQ_SUB
    u = np.arange(length)
    u = np.where(u < 2 * BQ_SUB, u, u - length)
    rel = BQ_SUB - u
    ok = (rel >= 0) & (rel <= band)
    general = _toeplitz(_diag_values(rel_bias, dil * rel, ok), BQ_SUB, 2 * BQ_SUB)
    no_prev = np.arange(2 * BQ_SUB)[None, None, :] >= BQ_SUB
    return jnp.stack([jnp.where(no_prev, general, NEG_INF), general])


def _group_sum_matrix():
    r = np.arange(MXU_DIM)
    return jnp.asarray((r[:, None] // HEAD_DIM) == (r[None, :] // HEAD_DIM), BF16)


def _mod_kernel(c_ref, w_ref, b_ref, o_ref):
    c = c_ref[...]
    c_act = c * (1.0 / (1.0 + jnp.exp(-c)))
    o_ref[...] = jnp.dot(c_act, w_ref[...], preferred_element_type=F32,
                         precision=lax.Precision.HIGHEST) + b_ref[...]


def _modulation(c_pad, w, b, tn=1024):
    nl, d, n = w.shape
    return pl.pallas_call(
        _mod_kernel,
        grid=(nl, n // tn),
        in_specs=[pl.BlockSpec((8, d), lambda l, j: (0, 0)),
                  pl.BlockSpec((None, d, tn), lambda l, j: (l, 0, j)),
                  pl.BlockSpec((None, 1, tn), lambda l, j: (l, 0, j))],
        out_specs=pl.BlockSpec((None, 8, tn), lambda l, j: (l, 0, j)),
        out_shape=jax.ShapeDtypeStruct((nl, 8, n), F32),
        compiler_params=_cparams(("parallel", "parallel")),
        name="adaln_mod",
    )(c_pad, w, b.reshape(nl, 1, n))


def _rms(x, gain):
    return x * lax.rsqrt(jnp.mean(x * x, axis=-1, keepdims=True) + EPS) * gain


def _proj_kernel(x_ref, g_ref, sh_ref, sc_ref, w_ref, hg_ref, bd_ref, o_ref, *, n_out, n_norm):
    h = _rms(x_ref[...], g_ref[...]) * (1.0 + sc_ref[...]) + sh_ref[...]
    hb = h.astype(BF16)
    for c0 in range(0, n_out, MXU_DIM):
        a = jnp.dot(hb, w_ref[:, c0:c0 + MXU_DIM], preferred_element_type=F32)
        if c0 < n_norm:
            ms = jnp.dot((a * a).astype(BF16), bd_ref[...], preferred_element_type=F32) * (1.0 / HEAD_DIM)
            a = a * lax.rsqrt(ms + EPS) * hg_ref[:, c0:c0 + MXU_DIM]
        o_ref[:, c0:c0 + MXU_DIM] = a.astype(BF16)


def _norm_proj(x, gain, shift, scale, w_bf16, head_gain, n_norm, seq):
    t, d = x.shape
    n_out = w_bf16.shape[1]
    tm = min(TOK_TILE, seq)
    per_b = seq // tm
    kern = functools.partial(_proj_kernel, n_out=n_out, n_norm=n_norm)
    return pl.pallas_call(
        kern,
        grid=(t // tm,),
        in_specs=[pl.BlockSpec((tm, d), lambda i: (i, 0)),
                  pl.BlockSpec((1, d), lambda i: (0, 0)),
                  pl.BlockSpec((None, 1, d), lambda i: (i // per_b, 0, 0)),
                  pl.BlockSpec((None, 1, d), lambda i: (i // per_b, 0, 0)),
                  pl.BlockSpec((d, n_out), lambda i: (0, 0)),
                  pl.BlockSpec((1, n_out), lambda i: (0, 0)),
                  pl.BlockSpec((MXU_DIM, MXU_DIM), lambda i: (0, 0))],
        out_specs=pl.BlockSpec((tm, n_out), lambda i: (i, 0)),
        out_shape=jax.ShapeDtypeStruct((t, n_out), BF16),
        compiler_params=_cparams(("parallel",)),
        name="norm_proj",
    )(x, gain.reshape(1, d), shift, scale, w_bf16, head_gain.reshape(1, n_out), _group_sum_matrix())


def _attn_a_kernel(q_ref, k_ref, v_ref, bias_ref, lam_ref, g_ref, o_ref, m_sc, acc_sc, *, tile, nd, lam_init):
    qi = pl.program_id(2)
    q = q_ref[...]
    lane = lax.broadcasted_iota(jnp.int32, q.shape, 1)
    zero = jnp.zeros_like(q)
    qq = jnp.concatenate([jnp.where(lane < HEAD_DIM, q, zero),
                          jnp.where(lane >= HEAD_DIM, q, zero)], axis=0)
    m_sc[...] = jnp.full(m_sc.shape, NEG_INF, F32)
    acc_sc[...] = jnp.zeros(acc_sc.shape, F32)
    ones = jnp.ones((tile, LANES), BF16)

    def body(j, carry):
        start = pl.multiple_of(j * tile, tile)
        k = k_ref[pl.ds(start, tile), :]
        v_ones = jnp.concatenate([v_ref[pl.ds(start, tile), :], ones], axis=1)
        s = lax.dot_general(qq, k, (((1,), (1,)), ((), ())), preferred_element_type=F32)
        bias = bias_ref[jnp.minimum(qi - j, nd - 1)]
        for c in range(2):
            sc = s[c * tile:(c + 1) * tile] + bias
            m_prev = m_sc[c]
            m_new = jnp.maximum(m_prev, jnp.max(sc, axis=-1, keepdims=True))
            alpha = jnp.exp2(m_prev - m_new)
            p = jnp.exp2(sc - pltpu.repeat(m_new, tile // LANES, axis=1))
            acc_sc[c] = (pltpu.repeat(alpha, 2, axis=1) * acc_sc[c]
                         + jnp.dot(p.astype(BF16), v_ones, preferred_element_type=F32))
            m_sc[c] = m_new
        return carry

    lax.fori_loop(0, qi + 1, body, 0)

    lp = lam_ref[...]
    lam = (jnp.exp(jnp.sum(lp[0:1] * lp[1:2], axis=-1, keepdims=True))
           - jnp.exp(jnp.sum(lp[2:3] * lp[3:4], axis=-1, keepdims=True)) + lam_init)
    a0, a1 = acc_sc[0], acc_sc[1]
    o = a0[:, :LANES] / a0[:, LANES:] - lam * (a1[:, :LANES] / a1[:, LANES:])
    o_ref[...] = (_rms(o, g_ref[...]) * (1.0 - lam_init)).astype(BF16)


def _attn_a(qkv, bias_tiles, nd, lam_p, subln, lam_init, batch, seq):
    tile = min(ATT_TILE, seq)
    nq = seq // tile
    qkv3 = qkv.reshape(batch, seq, 3 * N_HEADS * LANES)
    kern = functools.partial(_attn_a_kernel, tile=tile, nd=nd, lam_init=lam_init)
    out = pl.pallas_call(
        kern,
        grid=(N_HEADS, batch, nq),
        in_specs=[pl.BlockSpec((None, tile, LANES), lambda h, b, i: (b, i, h)),
                  pl.BlockSpec((None, seq, LANES), lambda h, b, i: (b, 0, N_HEADS + h)),
                  pl.BlockSpec((None, seq, LANES), lambda h, b, i: (b, 0, 2 * N_HEADS + h)),
                  pl.BlockSpec((None, nd, tile, tile), lambda h, b, i: (h, 0, 0, 0)),
                  pl.BlockSpec((4, HEAD_DIM), lambda h, b, i: (0, 0)),
                  pl.BlockSpec((1, LANES), lambda h, b, i: (0, 0))],
        out_specs=pl.BlockSpec((None, tile, LANES), lambda h, b, i: (b, i, h)),
        out_shape=jax.ShapeDtypeStruct((batch, seq, N_HEADS * LANES), BF16),
        scratch_shapes=[pltpu.VMEM((2, tile, LANES), F32), pltpu.VMEM((2, tile, 2 * LANES), F32)],
        compiler_params=_cparams(("parallel", "parallel", "parallel")),
        name="attn_a",
    )(qkv3, qkv3, qkv3, bias_tiles, lam_p, subln.reshape(1, LANES))
    return out.reshape(batch * seq, N_HEADS * LANES)


def _attn_b_kernel(q_ref, k_ref, v_ref, kp_ref, vp_ref, bias_ref, o_ref, lse_ref, *, tq):
    n = pl.program_id(2)
    for sb in range(tq // BQ_SUB):
        rows = slice(sb * BQ_SUB, (sb + 1) * BQ_SUB)
        prev = slice((sb - 1) * BQ_SUB, sb * BQ_SUB)
        qs = q_ref[rows, :]
        if sb == 0:
            kk = jnp.concatenate([kp_ref[...], k_ref[rows, :]], axis=0)
            vv = jnp.concatenate([vp_ref[...], v_ref[rows, :]], axis=0)
            variant = jnp.where(n == 0, 0, 1)
        else:
            kk = jnp.concatenate([k_ref[prev, :], k_ref[rows, :]], axis=0)
            vv = jnp.concatenate([v_ref[prev, :], v_ref[rows, :]], axis=0)
            variant = 1
        lane = lax.broadcasted_iota(jnp.int32, (BQ_SUB, LANES), 1)
        low = lane < HEAD_DIM
        for hp in range(N_HEADS // 2):
            cols = slice(hp * LANES, (hp + 1) * LANES)
            qp = qs[:, cols]
            zero = jnp.zeros_like(qp)
            qq = jnp.concatenate([jnp.where(low, qp, zero), jnp.where(low, zero, qp)], axis=0)
            s = lax.dot_general(qq, kk[:, cols], (((1,), (1,)), ((), ())), preferred_element_type=F32)
            outs, lses = [], []
            for c in range(2):
                sc = s[c * BQ_SUB:(c + 1) * BQ_SUB] + bias_ref[variant, 2 * hp + c]
                m = jnp.max(sc, axis=-1, keepdims=True)
                p = jnp.exp(sc - m)
                l = jnp.sum(p, axis=-1, keepdims=True)
                outs.append(jnp.dot(p.astype(BF16), vv[:, cols], preferred_element_type=F32) / l)
                lses.append(m + jnp.log(l))
            o_ref[rows, cols] = jnp.where(low, outs[0], outs[1])
            lse_ref[rows, cols] = jnp.where(low, lses[0], lses[1])


def _attn_b_group(q, kv, bias_tiles, g, dil, batch, seq):
    length = seq // dil
    tq = min(BQ_TILE, length)
    nq = length // tq
    sub = tq // BQ_SUB
    q3 = q.reshape(batch, length, dil * N_GROUPS * B_WIDTH)
    kv3 = kv.reshape(batch, length, dil * 2 * N_GROUPS * B_WIDTH)
    kern = functools.partial(_attn_b_kernel, tq=tq)
    blk = lambda rows: (None, rows, B_WIDTH)
    out_shape = jax.ShapeDtypeStruct((batch, length, dil * B_WIDTH), F32)
    o, lse = pl.pallas_call(
        kern,
        grid=(batch, dil, nq),
        in_specs=[pl.BlockSpec(blk(tq), lambda b, r, n: (b, n, r * N_GROUPS + g)),
                  pl.BlockSpec(blk(tq), lambda b, r, n: (b, n, r * 2 * N_GROUPS + g)),
                  pl.BlockSpec(blk(tq), lambda b, r, n: (b, n, r * 2 * N_GROUPS + N_GROUPS + g)),
                  pl.BlockSpec(blk(BQ_SUB), lambda b, r, n: (b, jnp.maximum(n * sub - 1, 0), r * 2 * N_GROUPS + g)),
                  pl.BlockSpec(blk(BQ_SUB),
                               lambda b, r, n: (b, jnp.maximum(n * sub - 1, 0), r * 2 * N_GROUPS + N_GROUPS + g)),
                  pl.BlockSpec((2, N_HEADS, BQ_SUB, 2 * BQ_SUB), lambda b, r, n: (0, 0, 0, 0))],
        out_specs=[pl.BlockSpec(blk(tq), lambda b, r, n: (b, n, r)),
                   pl.BlockSpec(blk(tq), lambda b, r, n: (b, n, r))],
        out_shape=[out_shape, out_shape],
        compiler_params=_cparams(("parallel", "parallel", "parallel")),
        name=f"attn_b_g{g}",
    )(q3, kv3, kv3, kv3, kv3, bias_tiles)
    return o.reshape(batch * seq, B_WIDTH), lse.reshape(batch * seq, B_WIDTH)


def _split_bf16(a):
    hi = a.astype(BF16)
    return hi, (a - hi.astype(F32)).astype(BF16)


def _out_tail(o_bf16, wo_ref, x_ref, ga_ref, gf_ref, shf_ref, scf_ref, wrh_ref, wrl_ref, br_ref,
              x1_ref, hf_ref, lg_ref):
    y = jnp.dot(o_bf16, wo_ref[...], preferred_element_type=F32)
    x1 = x_ref[...] + ga_ref[...] * y
    x1_ref[...] = x1
    hf = _rms(x1, gf_ref[...]) * (1.0 + scf_ref[...]) + shf_ref[...]
    hf_ref[...] = hf
    hi, lo = _split_bf16(hf)
    lg_ref[...] = (jnp.dot(hi, wrh_ref[...], preferred_element_type=F32)
                   + jnp.dot(lo, wrh_ref[...], preferred_element_type=F32)
                   + jnp.dot(hi, wrl_ref[...], preferred_element_type=F32) + br_ref[...])


def _out_a_kernel(o_ref, *rest):
    _out_tail(o_ref[...], *rest)


def _out_b_kernel(o0, o1, o2, l0, l1, l2, *rest):
    la, lb, lc = l0[...], l1[...], l2[...]
    m = jnp.maximum(jnp.maximum(la, lb), lc)
    ea, eb, ec = jnp.exp(la - m), jnp.exp(lb - m), jnp.exp(lc - m)
    z = ea + eb + ec
    o = (ea / z) * o0[...] + (eb / z) * o1[...] + (ec / z) * o2[...]
    _out_tail(o.astype(BF16), *rest)


def _out_proj(att_inputs, wo_bf16, x, gate_a, gain_f, shift_f, scale_f, w_router, b_router, seq):
    t, d = x.shape
    tm = min(TOK_TILE, seq)
    per_b = seq // tm
    row = lambda i: (i, 0)
    fixed = lambda i: (0, 0)
    per_batch = lambda i: (i // per_b, 0, 0)
    att_specs = [pl.BlockSpec((tm, a.shape[1]), row) for a in att_inputs]
    kern = _out_a_kernel if len(att_inputs) == 1 else _out_b_kernel
    return pl.pallas_call(
        kern,
        grid=(t // tm,),
        in_specs=att_specs + [pl.BlockSpec(wo_bf16.shape, fixed),
                              pl.BlockSpec((tm, d), row),
                              pl.BlockSpec((None, 1, d), per_batch),
                              pl.BlockSpec((1, d), fixed),
                              pl.BlockSpec((None, 1, d), per_batch),
                              pl.BlockSpec((None, 1, d), per_batch),
                              pl.BlockSpec((d, ROUTER_COLS), fixed),
                              pl.BlockSpec((d, ROUTER_COLS), fixed),
                              pl.BlockSpec((1, ROUTER_COLS), fixed)],
        out_specs=[pl.BlockSpec((tm, d), row), pl.BlockSpec((tm, d), row),
                   pl.BlockSpec((tm, ROUTER_COLS), row)],
        out_shape=[jax.ShapeDtypeStruct((t, d), F32), jax.ShapeDtypeStruct((t, d), F32),
                   jax.ShapeDtypeStruct((t, ROUTER_COLS), F32)],
        compiler_params=_cparams(("parallel",)),
        name="out_proj",
    )(*att_inputs, wo_bf16, x, gate_a, gain_f.reshape(1, d), shift_f, scale_f, *_split_bf16(w_router), b_router)


def _route_kernel(lg_ref, tri_ref, eid_ref, gate_ref, cnt_ref, cnt_sc):
    lt = lg_ref[...].T
    best, g_idx = lt[0:1], jnp.zeros((1, lt.shape[1]), jnp.int32)
    for g in range(1, MOE_GROUPS):
        upd = lt[g:g + 1] > best
        best = jnp.where(upd, lt[g:g + 1], best)
        g_idx = jnp.where(upd, g, g_idx)
    denom = jnp.zeros_like(best)
    for g in range(MOE_GROUPS):
        denom = denom + jnp.exp(lt[g:g + 1] - best)
    g_w = 1.0 / denom
    e_sel = lt[EXPERT_COL0:EXPERT_COL0 + MOE_EPG]
    for g in range(1, MOE_GROUPS):
        r0 = EXPERT_COL0 + g * MOE_EPG
        e_sel = jnp.where(g_idx == g, lt[r0:r0 + MOE_EPG], e_sel)

    def first_max(vals):
        v, i = vals[0:1], jnp.zeros((1, vals.shape[1]), jnp.int32)
        for e in range(1, MOE_EPG):
            upd = vals[e:e + 1] > v
            v = jnp.where(upd, vals[e:e + 1], v)
            i = jnp.where(upd, e, i)
        return v, i

    v1, i1 = first_max(e_sel)
    row = lax.broadcasted_iota(jnp.int32, e_sel.shape, 0)
    v2, i2 = first_max(jnp.where(row == i1, NEG_INF, e_sel))
    e2 = jnp.exp(v2 - v1)
    w1 = 1.0 / (1.0 + e2)
    w2 = e2 / (1.0 + e2)
    e1 = g_idx * MOE_EPG + i1
    e2 = g_idx * MOE_EPG + i2

    @pl.when(pl.program_id(0) == 0)
    def _():
        cnt_sc[...] = jnp.zeros(cnt_sc.shape, F32)

    erows = lax.broadcasted_iota(jnp.int32, (MOE_EXPERTS, lt.shape[1]), 0)
    carry = cnt_sc[...]
    ranks = []
    for e_k in (e1, e2):
        oh = (erows == e_k).astype(F32)
        pre = jnp.dot(oh.astype(BF16), tri_ref[...], preferred_element_type=F32)
        ranks.append(jnp.sum(oh * (pre - 1.0 + carry), axis=0, keepdims=True))
        carry = carry + pre[:, lt.shape[1] - 1:]
    cnt_sc[...] = carry
    cnt_ref[...] = jnp.broadcast_to(carry, cnt_ref.shape).astype(jnp.int32)

    erow = lax.broadcasted_iota(jnp.int32, eid_ref.shape, 0)
    eid_ref[...] = jnp.where(erow == 0, e1, jnp.where(erow == 1, e2, jnp.where(
        erow == 2, ranks[0].astype(jnp.int32), jnp.where(erow == 3, ranks[1].astype(jnp.int32), 0))))
    grow = lax.broadcasted_iota(jnp.int32, lt.shape, 0)
    gates = jnp.where(grow == 0, g_w * w1, jnp.where(grow == 1, g_w * w2, 0.0))
    gate_ref[...] = gates.T


def _route(logits):
    t = logits.shape[0]
    tm = min(ROUTE_TILE, t)
    r = np.arange(tm)
    tri = jnp.asarray(r[:, None] <= r[None, :], BF16)
    return pl.pallas_call(
        _route_kernel,
        grid=(t // tm,),
        in_specs=[pl.BlockSpec((tm, ROUTER_COLS), lambda i: (i, 0)),
                  pl.BlockSpec((tm, tm), lambda i: (0, 0))],
        out_specs=[pl.BlockSpec((8, tm), lambda i: (0, i)), pl.BlockSpec((tm, ROUTER_COLS), lambda i: (i, 0)),
                   pl.BlockSpec((MOE_EXPERTS, LANES), lambda i: (0, 0))],
        out_shape=[jax.ShapeDtypeStruct((8, t), jnp.int32), jax.ShapeDtypeStruct((t, ROUTER_COLS), F32),
                   jax.ShapeDtypeStruct((MOE_EXPERTS, LANES), jnp.int32)],
        scratch_shapes=[pltpu.VMEM((MOE_EXPERTS, 1), F32)],
        compiler_params=_cparams(("arbitrary",)),
        name="moe_route",
    )(logits, tri)


def _dispatch_plan(route, counts, n_blocks):
    eid, rank = route[0:2], route[2:4]
    padded = (counts + MOE_BLK - 1) // MOE_BLK * MOE_BLK
    pad_end = jnp.cumsum(padded)
    pad_start = pad_end - padded
    onehot = eid[..., None] == jnp.arange(MOE_EXPERTS, dtype=jnp.int32)
    dest = jnp.sum(jnp.where(onehot, pad_start, 0), axis=-1) + rank
    block_start = jnp.arange(n_blocks, dtype=jnp.int32) * MOE_BLK
    block_e = jnp.minimum(jnp.sum(pad_end[None, :] <= block_start[:, None], axis=1), MOE_EXPERTS - 1).astype(jnp.int32)
    n_used = (pad_end[-1] // MOE_BLK).astype(jnp.int32).reshape(1)
    return dest.astype(jnp.int32), block_e, n_used


def _row_copy(src_ref, src_row, dst_ref, dst_row, sem):
    return pltpu.make_async_copy(src_ref.at[pl.ds(src_row, 1), :], dst_ref.at[pl.ds(dst_row, 1), :], sem)


def _dispatch_kernel(dest_ref, h_ref, xs_in_ref, xs_ref, sem, *, tm):
    del xs_in_ref

    def start(i, carry):
        for k in range(2):
            _row_copy(h_ref, i, xs_ref, dest_ref[0, 0, k * tm + i], sem).start(priority=k)
        return carry

    def wait(i, carry):
        for k in range(2):
            _row_copy(h_ref, i, xs_ref, dest_ref[0, 0, k * tm + i], sem).wait()
        return carry

    lax.fori_loop(0, tm, start, 0)
    lax.fori_loop(0, tm, wait, 0)


def _dispatch(h, dest, rows):
    t, d = h.shape
    tm = min(DISP_TILE, t)
    steps = t // tm
    dest_steps = dest.reshape(2, steps, tm).transpose(1, 0, 2).reshape(steps, 1, 2 * tm)
    kern = functools.partial(_dispatch_kernel, tm=tm)
    return pl.pallas_call(
        kern,
        grid=(steps,),
        in_specs=[pl.BlockSpec((1, 1, 2 * tm), lambda i: (i, 0, 0), memory_space=pltpu.SMEM),
                  pl.BlockSpec((tm, d), lambda i: (i, 0)),
                  pl.BlockSpec(memory_space=pl.ANY)],
        out_specs=pl.BlockSpec(memory_space=pl.ANY),
        out_shape=jax.ShapeDtypeStruct((rows, d), F32),
        scratch_shapes=[pltpu.SemaphoreType.DMA(())],
        input_output_aliases={2: 0},
        compiler_params=_cparams(("arbitrary",)),
        name="moe_dispatch",
    )(dest_steps, h, jnp.zeros((rows, d), F32))


def _expert_kernel(be_ref, nu_ref, xs_ref, wg_ref, wu_ref, wd_ref, ys_ref):
    live = pl.program_id(0) < nu_ref[0]

    @pl.when(jnp.logical_not(live))
    def _():
        ys_ref[...] = jnp.zeros(ys_ref.shape, F32)

    @pl.when(live)
    def _():
        x = xs_ref[...].astype(BF16)
        g = jnp.dot(x, wg_ref[...].astype(BF16), preferred_element_type=F32)
        u = jnp.dot(x, wu_ref[...].astype(BF16), preferred_element_type=F32)
        a = (g * (1.0 / (1.0 + jnp.exp(-g))) * u).astype(BF16)
        ys_ref[...] = jnp.dot(a, wd_ref[...].astype(BF16), preferred_element_type=F32)


def _experts(xs, block_e, n_used, w_gate, w_up, w_down, layer):
    rows, d = xs.shape
    n_blocks = rows // MOE_BLK
    hid = w_gate.shape[-1]
    used = lambda i, be, nu: (jnp.minimum(i, nu[0] - 1), 0)
    expert = lambda i, be, nu: (layer, be[jnp.minimum(i, nu[0] - 1)], 0, 0)
    grid_spec = pltpu.PrefetchScalarGridSpec(
        num_scalar_prefetch=2,
        grid=(n_blocks,),
        in_specs=[pl.BlockSpec((MOE_BLK, d), used),
                  pl.BlockSpec((None, None, d, hid), expert),
                  pl.BlockSpec((None, None, d, hid), expert),
                  pl.BlockSpec((None, None, hid, d), expert)],
        out_specs=pl.BlockSpec((MOE_BLK, d), lambda i, be, nu: (i, 0)),
    )
    return pl.pallas_call(
        _expert_kernel,
        grid_spec=grid_spec,
        out_shape=jax.ShapeDtypeStruct((rows, d), F32),
        compiler_params=_cparams(("arbitrary",)),
        name="moe_experts",
    )(block_e, n_used, xs, w_gate, w_up, w_down)


def _combine_kernel(dest_ref, ys_ref, x_ref, gf_ref, gate_ref, o_ref, buf, sem, *, tm):
    def start(i, carry):
        for k in range(2):
            _row_copy(ys_ref, dest_ref[0, 0, k * tm + i], buf.at[k], i, sem).start(priority=k)
        return carry

    def wait(i, carry):
        for k in range(2):
            _row_copy(ys_ref, dest_ref[0, 0, k * tm + i], buf.at[k], i, sem).wait()
        return carry

    lax.fori_loop(0, tm, start, 0)
    lax.fori_loop(0, tm, wait, 0)
    gates = gate_ref[...]
    y = gates[:, 0:1] * buf[0] + gates[:, 1:2] * buf[1]
    o_ref[...] = x_ref[...] + gf_ref[...] * y


def _combine(ys, dest, x, gate_f, gates, seq):
    t, d = x.shape
    tm = min(COMB_TILE, seq)
    steps = t // tm
    per_b = seq // tm
    dest_steps = dest.reshape(2, steps, tm).transpose(1, 0, 2).reshape(steps, 1, 2 * tm)
    kern = functools.partial(_combine_kernel, tm=tm)
    return pl.pallas_call(
        kern,
        grid=(steps,),
        in_specs=[pl.BlockSpec((1, 1, 2 * tm), lambda i: (i, 0, 0), memory_space=pltpu.SMEM),
                  pl.BlockSpec(memory_space=pl.ANY),
                  pl.BlockSpec((tm, d), lambda i: (i, 0)),
                  pl.BlockSpec((None, 1, d), lambda i: (i // per_b, 0, 0)),
                  pl.BlockSpec((tm, ROUTER_COLS), lambda i: (i, 0))],
        out_specs=pl.BlockSpec((tm, d), lambda i: (i, 0)),
        out_shape=jax.ShapeDtypeStruct((t, d), F32),
        scratch_shapes=[pltpu.VMEM((2, tm, d), F32), pltpu.SemaphoreType.DMA(())],
        compiler_params=_cparams(("arbitrary",)),
        name="moe_combine",
    )(dest_steps, ys, x, gate_f, gates)


def _moe(h, logits, x1, gate_f, w_gate, w_up, w_down, layer, seq):
    t = h.shape[0]
    n_blocks = 2 * t // MOE_BLK + MOE_EXPERTS
    route, gates, counts = _route(logits)
    dest, block_e, n_used = _dispatch_plan(route, counts[:, 0], n_blocks)
    xs = _dispatch(h, dest, n_blocks * MOE_BLK)
    ys = _experts(xs, block_e, n_used, w_gate, w_up, w_down, layer)
    return _combine(ys, dest, x1, gate_f, gates, seq)


def _router_weights(w_group, b_group, w_expert, b_expert):
    d = w_group.shape[0]
    w = jnp.zeros((d, ROUTER_COLS), F32)
    w = w.at[:, 0:MOE_GROUPS].set(w_group.astype(F32))
    w = w.at[:, EXPERT_COL0:EXPERT_COL0 + MOE_EXPERTS].set(w_expert.astype(F32))
    b = jnp.zeros((1, ROUTER_COLS), F32)
    b = b.at[0, 0:MOE_GROUPS].set(b_group.astype(F32))
    b = b.at[0, EXPERT_COL0:EXPERT_COL0 + MOE_EXPERTS].set(b_expert.astype(F32))
    return w, b


def _lambda_init(layer):
    return 0.8 - 0.6 * math.exp(-0.3 * layer)


def kernel(x, c, rel_bias, ada_w, ada_b, norm_attn, norm_ffn, a_w_qkv, a_q_norm, a_k_norm, a_lambda, a_subln, a_w_o, kv_norm, kv_ada_w, kv_ada_b, kv_w, kv_k_norm, b_w_q, b_q_norm, b_w_o, moe_w_group, moe_b_group, moe_w_expert, moe_b_expert, moe_w_gate, moe_w_up, moe_w_down):
    batch, seq, d = x.shape
    t = batch * seq
    scale = HEAD_DIM ** -0.5
    n_bw = N_GROUPS * B_WIDTH

    c_pad = jnp.zeros((8, d), F32).at[:batch].set(c.astype(F32))
    mod = _modulation(c_pad, ada_w, ada_b)[:, :batch]
    kv_mod = _modulation(c_pad, kv_ada_w[None], kv_ada_b[None])[0, :batch]

    def part(m, i):
        return m[:, i * d:(i + 1) * d].reshape(batch, 1, d)

    a_bias, nd = _attn_a_bias_tiles(rel_bias, seq, min(ATT_TILE, seq))
    b_bias = [_attn_b_bias_tiles(rel_bias, window // dil, dil) for window, dil in B_GROUPS]
    xf = x.reshape(t, d).astype(F32)
    kv = None
    for layer in range(DEPTH):
        m = mod[layer]
        sh_a, sc_a, g_a, sh_f, sc_f, g_f = (part(m, i) for i in range(6))
        w_r, b_r = _router_weights(moe_w_group[layer], moe_b_group[layer], moe_w_expert[layer], moe_b_expert[layer])
        if layer < N_A_LAYERS:
            qk_gain = jnp.concatenate([jnp.tile(a_q_norm[layer].astype(F32) * (scale * LOG2_E), 2 * N_HEADS),
                                       jnp.tile(a_k_norm[layer].astype(F32), 2 * N_HEADS),
                                       jnp.ones((N_HEADS * LANES,), F32)])
            qkv = _norm_proj(xf, norm_attn[layer], sh_a, sc_a, a_w_qkv[layer].astype(BF16), qk_gain,
                             2 * N_HEADS * LANES, seq)
            o = _attn_a(qkv, a_bias, nd, a_lambda[layer].astype(F32), a_subln[layer].astype(F32),
                        _lambda_init(layer), batch, seq)
            att = [o]
            wo = a_w_o[layer].astype(BF16)
        else:
            j = layer - N_A_LAYERS
            if kv is None:
                k_gain = jnp.concatenate([jnp.tile(kv_k_norm.astype(F32), (1, N_HEADS)).reshape(-1),
                                          jnp.ones((n_bw,), F32)])
                kv = _norm_proj(xf, kv_norm, part(kv_mod, 0), part(kv_mod, 1), kv_w.astype(BF16), k_gain, n_bw, seq)
            q_gain = jnp.tile(b_q_norm[j].astype(F32) * scale, (1, N_HEADS)).reshape(-1)
            q = _norm_proj(xf, norm_attn[layer], sh_a, sc_a, b_w_q[j].astype(BF16), q_gain, n_bw, seq)
            outs, lses = [], []
            for g, (window, dil) in enumerate(B_GROUPS):
                o_g, lse_g = _attn_b_group(q, kv, b_bias[g], g, dil, batch, seq)
                outs.append(o_g)
                lses.append(lse_g)
            att = outs + lses
            wo = b_w_o[j].astype(BF16)
        x1, hf, logits = _out_proj(att, wo, xf, g_a, norm_ffn[layer], sh_f, sc_f, w_r, b_r, seq)
        xf = _moe(hf, logits, x1, g_f, moe_w_gate, moe_w_up, moe_w_down, layer, seq)
    return xf.reshape(batch, seq, d).astype(x.dtype)
```

```python
import functools
import math

import jax
import jax.numpy as jnp
import numpy as np
from jax import lax
from jax.experimental import pallas as pl
from jax.experimental.pallas import tpu as pltpu

F32 = jnp.float32
BF16 = jnp.bfloat16

D_MODEL = 1024
DEPTH = 4
N_A_LAYERS = 2
N_HEADS = 8
HEAD_DIM = 64
B_GROUPS = ((128, 1), (512, 4), (2048, 16))
N_GROUPS = len(B_GROUPS)
B_WIDTH = N_HEADS * HEAD_DIM
REL_BUCKETS = 32
REL_MAX_EXACT = 16
REL_MAX_DIST = 2048
MOE_GROUPS = 4
MOE_EPG = 8
MOE_EXPERTS = MOE_GROUPS * MOE_EPG
MOE_HIDDEN = 512
EPS = 1e-6
NEG_INF = float("-inf")
LOG2_E = math.log2(math.e)

LANES = 128
MXU_DIM = 256
ROUTER_COLS = 128
EXPERT_COL0 = 8

TOK_TILE = 512
ATT_TILE = 512
BQ_TILE = 256
BQ_SUB = 128
MOE_BLK = 256
MOE_TILE = 512
SEG_ALIGN = 8
SEG_CHUNK = 8
VMEM_LIMIT = 56 * 1024 * 1024


def _cparams(sem):
    return pltpu.CompilerParams(dimension_semantics=sem, vmem_limit_bytes=VMEM_LIMIT)


def _bucket_table(max_dist):
    n = np.arange(max_dist + 1)
    nf = np.maximum(n, 1).astype(np.float64)
    large = REL_MAX_EXACT + (np.log(nf / REL_MAX_EXACT) / math.log(REL_MAX_DIST / REL_MAX_EXACT)
                             * (REL_BUCKETS - REL_MAX_EXACT)).astype(np.int64)
    large = np.minimum(large, REL_BUCKETS - 1)
    return np.where(n < REL_MAX_EXACT, n, large).astype(np.int32)


def _toeplitz(v, n, m):
    length = v.shape[-1]
    assert length >= n + m - 1 and length - 1 >= m
    lead = v.shape[:-1]
    flat = jnp.tile(v, (1,) * len(lead) + (n,))[..., :n * (length - 1)]
    return flat.reshape(lead + (n, length - 1))[..., :m]


def _diag_values(rel_bias, dist, ok):
    table = _bucket_table(int(dist.max()))
    vals = rel_bias.astype(F32)[table[np.clip(dist, 0, None)]]
    vals = jnp.where(ok[..., None], vals, NEG_INF)
    return jnp.moveaxis(vals, -1, 0)


def _attn_a_bias_tiles(rel_bias, seq, tile):
    table = _bucket_table(seq)
    last_start = int(np.argmax(table == REL_BUCKETS - 1))
    nd = 0
    while nd * tile - (tile - 1) < last_start and nd * tile < seq:
        nd += 1
    nd += 1
    length = 2 * tile
    u = np.arange(length)
    u = np.where(u < tile, u, u - length)
    dist = np.arange(nd)[:, None] * tile - u[None, :]
    diag = _diag_values(rel_bias, dist, dist >= 0) * LOG2_E
    return _toeplitz(diag, tile, tile), nd


def _attn_b_bias_tiles(rel_bias, band, dil):
    length = 3 * BQ_SUB
    u = np.arange(length)
    u = np.where(u < 2 * BQ_SUB, u, u - length)
    rel = BQ_SUB - u
    ok = (rel >= 0) & (rel <= band)
    general = _toeplitz(_diag_values(rel_bias, dil * rel, ok), BQ_SUB, 2 * BQ_SUB)
    no_prev = np.arange(2 * BQ_SUB)[None, None, :] >= BQ_SUB
    return jnp.stack([jnp.where(no_prev, general, NEG_INF), general])


def _group_sum_matrix():
    r = np.arange(MXU_DIM)
    return jnp.asarray((r[:, None] // HEAD_DIM) == (r[None, :] // HEAD_DIM), BF16)


def _mod_kernel(c_ref, w_ref, b_ref, o_ref):
    c = c_ref[...]
    c_act = c * (1.0 / (1.0 + jnp.exp(-c)))
    o_ref[...] = jnp.dot(c_act, w_ref[...], preferred_element_type=F32,
                         precision=lax.Precision.HIGHEST) + b_ref[...]


def _modulation(c_pad, w, b, tn=1024):
    nl, d, n = w.shape
    return pl.pallas_call(
        _mod_kernel,
        grid=(nl, n // tn),
        in_specs=[pl.BlockSpec((8, d), lambda l, j: (0, 0)),
                  pl.BlockSpec((None, d, tn), lambda l, j: (l, 0, j)),
                  pl.BlockSpec((None, 1, tn), lambda l, j: (l, 0, j))],
        out_specs=pl.BlockSpec((None, 8, tn), lambda l, j: (l, 0, j)),
        out_shape=jax.ShapeDtypeStruct((nl, 8, n), F32),
        compiler_params=_cparams(("parallel", "parallel")),
        name="adaln_mod",
    )(c_pad, w, b.reshape(nl, 1, n))


def _rms(x, gain):
    return x * lax.rsqrt(jnp.mean(x * x, axis=-1, keepdims=True) + EPS) * gain


def _proj_kernel(x_ref, g_ref, sh_ref, sc_ref, w_ref, hg_ref, bd_ref, o_ref, *, n_out, n_norm):
    h = _rms(x_ref[...], g_ref[...]) * (1.0 + sc_ref[...]) + sh_ref[...]
    hb = h.astype(BF16)
    for c0 in range(0, n_out, MXU_DIM):
        a = jnp.dot(hb, w_ref[:, c0:c0 + MXU_DIM], preferred_element_type=F32)
        if c0 < n_norm:
            ms = jnp.dot((a * a).astype(BF16), bd_ref[...], preferred_element_type=F32) * (1.0 / HEAD_DIM)
            a = a * lax.rsqrt(ms + EPS) * hg_ref[:, c0:c0 + MXU_DIM]
        o_ref[:, c0:c0 + MXU_DIM] = a.astype(BF16)


def _norm_proj(x, gain, shift, scale, w_bf16, head_gain, n_norm, seq):
    t, d = x.shape
    n_out = w_bf16.shape[1]
    tm = min(TOK_TILE, seq)
    per_b = seq // tm
    kern = functools.partial(_proj_kernel, n_out=n_out, n_norm=n_norm)
    return pl.pallas_call(
        kern,
        grid=(t // tm,),
        in_specs=[pl.BlockSpec((tm, d), lambda i: (i, 0)),
                  pl.BlockSpec((1, d), lambda i: (0, 0)),
                  pl.BlockSpec((None, 1, d), lambda i: (i // per_b, 0, 0)),
                  pl.BlockSpec((None, 1, d), lambda i: (i // per_b, 0, 0)),
                  pl.BlockSpec((d, n_out), lambda i: (0, 0)),
                  pl.BlockSpec((1, n_out), lambda i: (0, 0)),
                  pl.BlockSpec((MXU_DIM, MXU_DIM), lambda i: (0, 0))],
        out_specs=pl.BlockSpec((tm, n_out), lambda i: (i, 0)),
        out_shape=jax.ShapeDtypeStruct((t, n_out), BF16),
        compiler_params=_cparams(("parallel",)),
        name="norm_proj",
    )(x, gain.reshape(1, d), shift, scale, w_bf16, head_gain.reshape(1, n_out), _group_sum_matrix())


def _lane_tile(a, reps):
    return jnp.concatenate([a] * reps, axis=1)


def _attn_a_kernel(q_ref, k_ref, v_ref, bias_ref, lam_ref, g_ref, o_ref, m_sc, acc_sc, *, tile, nd, lam_init):
    qi = pl.program_id(2)
    q = q_ref[...]
    lane = lax.broadcasted_iota(jnp.int32, q.shape, 1)
    zero = jnp.zeros_like(q)
    qq = jnp.concatenate([jnp.where(lane < HEAD_DIM, q, zero),
                          jnp.where(lane >= HEAD_DIM, q, zero)], axis=0)
    m_sc[...] = jnp.full(m_sc.shape, NEG_INF, F32)
    acc_sc[...] = jnp.zeros(acc_sc.shape, F32)
    ones = jnp.ones((tile, LANES), BF16)

    def body(j, carry):
        start = pl.multiple_of(j * tile, tile)
        k = k_ref[pl.ds(start, tile), :]
        v_ones = jnp.concatenate([v_ref[pl.ds(start, tile), :], ones], axis=1)
        s = lax.dot_general(qq, k, (((1,), (1,)), ((), ())), preferred_element_type=F32)
        bias = bias_ref[jnp.minimum(qi - j, nd - 1)]
        for c in range(2):
            sc = s[c * tile:(c + 1) * tile] + bias
            m_prev = m_sc[c]
            m_new = jnp.maximum(m_prev, jnp.max(sc, axis=-1, keepdims=True))
            alpha = jnp.exp2(m_prev - m_new)
            p = jnp.exp2(sc - _lane_tile(m_new, tile // LANES))
            acc_sc[c] = (_lane_tile(alpha, 2) * acc_sc[c]
                         + jnp.dot(p.astype(BF16), v_ones, preferred_element_type=F32))
            m_sc[c] = m_new
        return carry

    lax.fori_loop(0, qi + 1, body, 0)

    lp = lam_ref[...]
    lam = (jnp.exp(jnp.sum(lp[0:1] * lp[1:2], axis=-1, keepdims=True))
           - jnp.exp(jnp.sum(lp[2:3] * lp[3:4], axis=-1, keepdims=True)) + lam_init)
    a0, a1 = acc_sc[0], acc_sc[1]
    o = a0[:, :LANES] / a0[:, LANES:] - lam * (a1[:, :LANES] / a1[:, LANES:])
    o_ref[...] = (_rms(o, g_ref[...]) * (1.0 - lam_init)).astype(BF16)


def _attn_a(qkv, bias_tiles, nd, lam_p, subln, lam_init, batch, seq):
    tile = min(ATT_TILE, seq)
    nq = seq // tile
    qkv3 = qkv.reshape(batch, seq, 3 * N_HEADS * LANES)
    kern = functools.partial(_attn_a_kernel, tile=tile, nd=nd, lam_init=lam_init)
    out = pl.pallas_call(
        kern,
        grid=(N_HEADS, batch, nq),
        in_specs=[pl.BlockSpec((None, tile, LANES), lambda h, b, i: (b, i, h)),
                  pl.BlockSpec((None, seq, LANES), lambda h, b, i: (b, 0, N_HEADS + h)),
                  pl.BlockSpec((None, seq, LANES), lambda h, b, i: (b, 0, 2 * N_HEADS + h)),
                  pl.BlockSpec((None, nd, tile, tile), lambda h, b, i: (h, 0, 0, 0)),
                  pl.BlockSpec((4, HEAD_DIM), lambda h, b, i: (0, 0)),
                  pl.BlockSpec((1, LANES), lambda h, b, i: (0, 0))],
        out_specs=pl.BlockSpec((None, tile, LANES), lambda h, b, i: (b, i, h)),
        out_shape=jax.ShapeDtypeStruct((batch, seq, N_HEADS * LANES), BF16),
        scratch_shapes=[pltpu.VMEM((2, tile, LANES), F32), pltpu.VMEM((2, tile, 2 * LANES), F32)],
        compiler_params=_cparams(("parallel", "parallel", "parallel")),
        name="attn_a",
    )(qkv3, qkv3, qkv3, bias_tiles, lam_p, subln.reshape(1, LANES))
    return out.reshape(batch * seq, N_HEADS * LANES)


def _attn_b_kernel(q_ref, k_ref, v_ref, kp_ref, vp_ref, bias_ref, o_ref, lse_ref, *, tq):
    n = pl.program_id(2)
    for sb in range(tq // BQ_SUB):
        rows = slice(sb * BQ_SUB, (sb + 1) * BQ_SUB)
        prev = slice((sb - 1) * BQ_SUB, sb * BQ_SUB)
        qs = q_ref[rows, :]
        if sb == 0:
            kk = jnp.concatenate([kp_ref[...], k_ref[rows, :]], axis=0)
            vv = jnp.concatenate([vp_ref[...], v_ref[rows, :]], axis=0)
            variant = jnp.where(n == 0, 0, 1)
        else:
            kk = jnp.concatenate([k_ref[prev, :], k_ref[rows, :]], axis=0)
            vv = jnp.concatenate([v_ref[prev, :], v_ref[rows, :]], axis=0)
            variant = 1
        lane = lax.broadcasted_iota(jnp.int32, (BQ_SUB, LANES), 1)
        low = lane < HEAD_DIM
        for hp in range(N_HEADS // 2):
            cols = slice(hp * LANES, (hp + 1) * LANES)
            qp = qs[:, cols]
            zero = jnp.zeros_like(qp)
            qq = jnp.concatenate([jnp.where(low, qp, zero), jnp.where(low, zero, qp)], axis=0)
            s = lax.dot_general(qq, kk[:, cols], (((1,), (1,)), ((), ())), preferred_element_type=F32)
            outs, lses = [], []
            for c in range(2):
                sc = s[c * BQ_SUB:(c + 1) * BQ_SUB] + bias_ref[variant, 2 * hp + c]
                m = jnp.max(sc, axis=-1, keepdims=True)
                p = jnp.exp(sc - m)
                l = jnp.sum(p, axis=-1, keepdims=True)
                outs.append(jnp.dot(p.astype(BF16), vv[:, cols], preferred_element_type=F32) / l)
                lses.append(m + jnp.log(l))
            o_ref[rows, cols] = jnp.where(low, outs[0], outs[1])
            lse_ref[rows, cols] = jnp.where(low, lses[0], lses[1])


def _attn_b_group(q, kv, bias_tiles, g, dil, batch, seq):
    length = seq // dil
    tq = min(BQ_TILE, length)
    nq = length // tq
    sub = tq // BQ_SUB
    q3 = q.reshape(batch, length, dil * N_GROUPS * B_WIDTH)
    kv3 = kv.reshape(batch, length, dil * 2 * N_GROUPS * B_WIDTH)
    kern = functools.partial(_attn_b_kernel, tq=tq)
    blk = lambda rows: (None, rows, B_WIDTH)
    out_shape = jax.ShapeDtypeStruct((batch, length, dil * B_WIDTH), F32)
    o, lse = pl.pallas_call(
        kern,
        grid=(batch, dil, nq),
        in_specs=[pl.BlockSpec(blk(tq), lambda b, r, n: (b, n, r * N_GROUPS + g)),
                  pl.BlockSpec(blk(tq), lambda b, r, n: (b, n, r * 2 * N_GROUPS + g)),
                  pl.BlockSpec(blk(tq), lambda b, r, n: (b, n, r * 2 * N_GROUPS + N_GROUPS + g)),
                  pl.BlockSpec(blk(BQ_SUB), lambda b, r, n: (b, jnp.maximum(n * sub - 1, 0), r * 2 * N_GROUPS + g)),
                  pl.BlockSpec(blk(BQ_SUB),
                               lambda b, r, n: (b, jnp.maximum(n * sub - 1, 0), r * 2 * N_GROUPS + N_GROUPS + g)),
                  pl.BlockSpec((2, N_HEADS, BQ_SUB, 2 * BQ_SUB), lambda b, r, n: (0, 0, 0, 0))],
        out_specs=[pl.BlockSpec(blk(tq), lambda b, r, n: (b, n, r)),
                   pl.BlockSpec(blk(tq), lambda b, r, n: (b, n, r))],
        out_shape=[out_shape, out_shape],
        compiler_params=_cparams(("parallel", "parallel", "parallel")),
        name=f"attn_b_g{g}",
    )(q3, kv3, kv3, kv3, kv3, bias_tiles)
    return o.reshape(batch * seq, B_WIDTH), lse.reshape(batch * seq, B_WIDTH)


def _split_bf16(a):
    hi = a.astype(BF16)
    return hi, (a - hi.astype(F32)).astype(BF16)


def _out_tail(o_bf16, wo_ref, x_ref, ga_ref, gf_ref, shf_ref, scf_ref, wrh_ref, wrl_ref, br_ref,
              x1_ref, hf_ref, lg_ref):
    y = jnp.dot(o_bf16, wo_ref[...], preferred_element_type=F32)
    x1 = x_ref[...] + ga_ref[...] * y
    x1_ref[...] = x1
    hf = _rms(x1, gf_ref[...]) * (1.0 + scf_ref[...]) + shf_ref[...]
    hf_ref[...] = hf
    hi, lo = _split_bf16(hf)
    lg_ref[...] = (jnp.dot(hi, wrh_ref[...], preferred_element_type=F32)
                   + jnp.dot(lo, wrh_ref[...], preferred_element_type=F32)
                   + jnp.dot(hi, wrl_ref[...], preferred_element_type=F32) + br_ref[...])


def _out_a_kernel(o_ref, *rest):
    _out_tail(o_ref[...], *rest)


def _out_b_kernel(o0, o1, o2, l0, l1, l2, *rest):
    la, lb, lc = l0[...], l1[...], l2[...]
    m = jnp.maximum(jnp.maximum(la, lb), lc)
    ea, eb, ec = jnp.exp(la - m), jnp.exp(lb - m), jnp.exp(lc - m)
    z = ea + eb + ec
    o = (ea / z) * o0[...] + (eb / z) * o1[...] + (ec / z) * o2[...]
    _out_tail(o.astype(BF16), *rest)


def _out_proj(att_inputs, wo_bf16, x, gate_a, gain_f, shift_f, scale_f, w_router, b_router, seq):
    t, d = x.shape
    tm = min(TOK_TILE, seq)
    per_b = seq // tm
    row = lambda i: (i, 0)
    fixed = lambda i: (0, 0)
    per_batch = lambda i: (i // per_b, 0, 0)
    att_specs = [pl.BlockSpec((tm, a.shape[1]), row) for a in att_inputs]
    kern = _out_a_kernel if len(att_inputs) == 1 else _out_b_kernel
    return pl.pallas_call(
        kern,
        grid=(t // tm,),
        in_specs=att_specs + [pl.BlockSpec(wo_bf16.shape, fixed),
                              pl.BlockSpec((tm, d), row),
                              pl.BlockSpec((None, 1, d), per_batch),
                              pl.BlockSpec((1, d), fixed),
                              pl.BlockSpec((None, 1, d), per_batch),
                              pl.BlockSpec((None, 1, d), per_batch),
                              pl.BlockSpec((d, ROUTER_COLS), fixed),
                              pl.BlockSpec((d, ROUTER_COLS), fixed),
                              pl.BlockSpec((1, ROUTER_COLS), fixed)],
        out_specs=[pl.BlockSpec((tm, d), row), pl.BlockSpec((tm, d), row),
                   pl.BlockSpec((tm, ROUTER_COLS), row)],
        out_shape=[jax.ShapeDtypeStruct((t, d), F32), jax.ShapeDtypeStruct((t, d), F32),
                   jax.ShapeDtypeStruct((t, ROUTER_COLS), F32)],
        compiler_params=_cparams(("parallel",)),
        name="out_proj",
    )(*att_inputs, wo_bf16, x, gate_a, gain_f.reshape(1, d), shift_f, scale_f, *_split_bf16(w_router), b_router)


def _route_kernel(lg_ref, tri_ref, eid_ref, gate_ref, seg_ref, cnt_ref, cnt_sc):
    lt = lg_ref[...].T
    best, g_idx = lt[0:1], jnp.zeros((1, lt.shape[1]), jnp.int32)
    for g in range(1, MOE_GROUPS):
        upd = lt[g:g + 1] > best
        best = jnp.where(upd, lt[g:g + 1], best)
        g_idx = jnp.where(upd, g, g_idx)
    denom = jnp.zeros_like(best)
    for g in range(MOE_GROUPS):
        denom = denom + jnp.exp(lt[g:g + 1] - best)
    g_w = 1.0 / denom
    e_sel = lt[EXPERT_COL0:EXPERT_COL0 + MOE_EPG]
    for g in range(1, MOE_GROUPS):
        r0 = EXPERT_COL0 + g * MOE_EPG
        e_sel = jnp.where(g_idx == g, lt[r0:r0 + MOE_EPG], e_sel)

    def first_max(vals):
        v, i = vals[0:1], jnp.zeros((1, vals.shape[1]), jnp.int32)
        for e in range(1, MOE_EPG):
            upd = vals[e:e + 1] > v
            v = jnp.where(upd, vals[e:e + 1], v)
            i = jnp.where(upd, e, i)
        return v, i

    v1, i1 = first_max(e_sel)
    row = lax.broadcasted_iota(jnp.int32, e_sel.shape, 0)
    v2, i2 = first_max(jnp.where(row == i1, NEG_INF, e_sel))
    e2 = jnp.exp(v2 - v1)
    w1 = 1.0 / (1.0 + e2)
    w2 = e2 / (1.0 + e2)
    e1 = g_idx * MOE_EPG + i1
    e2 = g_idx * MOE_EPG + i2

    @pl.when(pl.program_id(0) == 0)
    def _():
        cnt_sc[...] = jnp.zeros(cnt_sc.shape, F32)

    tm = lt.shape[1]
    erows = lax.broadcasted_iota(jnp.int32, (MOE_EXPERTS, tm), 0)
    ohs, pres = [], []
    for e_k in (e1, e2):
        oh = (erows == e_k).astype(F32)
        ohs.append(oh)
        pres.append(jnp.dot(oh.astype(BF16), tri_ref[...], preferred_element_type=F32))
    n0 = pres[0][:, tm - 1:]
    n_tot = n0 + pres[1][:, tm - 1:]
    chunks = jnp.floor((n_tot + (SEG_ALIGN - 1.0)) * (1.0 / SEG_ALIGN))
    er = lax.broadcasted_iota(jnp.int32, (MOE_EXPERTS, MOE_EXPERTS), 0)
    ec = lax.broadcasted_iota(jnp.int32, (MOE_EXPERTS, MOE_EXPERTS), 1)
    before = jnp.dot((ec < er).astype(BF16), jnp.broadcast_to(chunks, (MOE_EXPERTS, LANES)).astype(BF16),
                     preferred_element_type=F32)
    seg_start = before[:, 0:1] * SEG_ALIGN
    slot0 = jnp.sum(ohs[0] * (seg_start + pres[0] - 1.0), axis=0, keepdims=True)
    slot1 = jnp.sum(ohs[1] * (seg_start + n0 + pres[1] - 1.0), axis=0, keepdims=True)
    seg_off = cnt_sc[...]
    cnt_sc[...] = seg_off + chunks * SEG_ALIGN
    cnt_ref[...] = jnp.broadcast_to(cnt_sc[...], cnt_ref.shape).astype(jnp.int32)
    scol = lax.broadcasted_iota(jnp.int32, seg_ref.shape, 1)
    seg_ref[...] = jnp.where(scol == 0, chunks, jnp.where(scol == 1, seg_start, jnp.where(
        scol == 2, seg_off, 0.0))).astype(jnp.int32)

    erow = lax.broadcasted_iota(jnp.int32, eid_ref.shape, 0)
    eid_ref[...] = jnp.where(erow == 0, e1, jnp.where(erow == 1, e2, jnp.where(
        erow == 2, slot0.astype(jnp.int32), jnp.where(erow == 3, slot1.astype(jnp.int32), 0))))
    grow = lax.broadcasted_iota(jnp.int32, lt.shape, 0)
    gates = jnp.where(grow == 0, g_w * w1, jnp.where(grow == 1, g_w * w2, 0.0))
    gate_ref[...] = gates.T


def _route(logits):
    t = logits.shape[0]
    tm = min(MOE_TILE, t)
    n_tiles = t // tm
    r = np.arange(tm)
    tri = jnp.asarray(r[:, None] <= r[None, :], BF16)
    return pl.pallas_call(
        _route_kernel,
        grid=(n_tiles,),
        in_specs=[pl.BlockSpec((tm, ROUTER_COLS), lambda i: (i, 0)),
                  pl.BlockSpec((tm, tm), lambda i: (0, 0))],
        out_specs=[pl.BlockSpec((8, tm), lambda i: (0, i)), pl.BlockSpec((tm, ROUTER_COLS), lambda i: (i, 0)),
                   pl.BlockSpec((MOE_EXPERTS, LANES), lambda i: (i, 0)),
                   pl.BlockSpec((MOE_EXPERTS, LANES), lambda i: (0, 0))],
        out_shape=[jax.ShapeDtypeStruct((8, t), jnp.int32), jax.ShapeDtypeStruct((t, ROUTER_COLS), F32),
                   jax.ShapeDtypeStruct((n_tiles * MOE_EXPERTS, LANES), jnp.int32),
                   jax.ShapeDtypeStruct((MOE_EXPERTS, LANES), jnp.int32)],
        scratch_shapes=[pltpu.VMEM((MOE_EXPERTS, 1), F32)],
        compiler_params=_cparams(("arbitrary",)),
        name="moe_route",
    )(logits, tri)


def _moe_rows(t):
    n_tiles = t // min(MOE_TILE, t)
    rows = 2 * t + n_tiles * MOE_EXPERTS * (SEG_ALIGN - 1) + MOE_EXPERTS * (MOE_BLK - 1)
    return -(-rows // MOE_BLK) * MOE_BLK


def _dispatch_plan(route, seg, totals, n_blocks):
    n_tiles = seg.shape[0] // MOE_EXPERTS
    tm = route.shape[1] // n_tiles
    padded = (totals + MOE_BLK - 1) // MOE_BLK * MOE_BLK
    pad_end = jnp.cumsum(padded)
    pad_start = pad_end - padded
    seg = seg.reshape(n_tiles, MOE_EXPERTS, LANES)
    table = jnp.concatenate([pad_start[None, :] + seg[:, :, 2], seg[:, :, 0], seg[:, :, 1],
                             jnp.zeros((n_tiles, LANES - 3 * MOE_EXPERTS), jnp.int32)], axis=1)
    slots = route[2:4].reshape(2, n_tiles, tm).transpose(1, 0, 2).reshape(n_tiles, 1, 2 * tm)
    block_start = jnp.arange(n_blocks, dtype=jnp.int32) * MOE_BLK
    block_e = jnp.minimum(jnp.sum(pad_end[None, :] <= block_start[:, None], axis=1), MOE_EXPERTS - 1).astype(jnp.int32)
    n_used = (pad_end[-1] // MOE_BLK).astype(jnp.int32).reshape(1)
    return table.reshape(n_tiles, 1, LANES), slots, block_e, n_used


def _segment_copies(table_ref, hbm_ref, sorted_ref, sem, to_hbm, wait):
    for e in range(MOE_EXPERTS):
        dst0 = table_ref[0, 0, e]
        n_chunks = table_ref[0, 0, MOE_EXPERTS + e]
        src0 = table_ref[0, 0, 2 * MOE_EXPERTS + e]

        def chunk(c, carry):
            local = sorted_ref.at[pl.ds(pl.multiple_of(src0 + c * SEG_CHUNK, SEG_CHUNK), SEG_CHUNK), :]
            remote = hbm_ref.at[pl.ds(pl.multiple_of(dst0 + c * SEG_CHUNK, SEG_CHUNK), SEG_CHUNK), :]
            copy = pltpu.make_async_copy(local, remote, sem) if to_hbm else pltpu.make_async_copy(remote, local, sem)
            if wait:
                copy.wait()
            else:
                copy.start()
            return carry

        lax.fori_loop(0, n_chunks, chunk, 0)


def _dispatch_kernel(table_ref, slot_ref, h_ref, xs_in_ref, xs_ref, sorted_sc, sem, *, tm):
    del xs_in_ref
    for e in range(MOE_EXPERTS):
        n_chunks = table_ref[0, 0, MOE_EXPERTS + e]
        src0 = table_ref[0, 0, 2 * MOE_EXPERTS + e]

        @pl.when(n_chunks > 0)
        def _():
            last = pl.multiple_of(src0 + (n_chunks - 1) * SEG_CHUNK, SEG_CHUNK)
            sorted_sc[pl.ds(last, SEG_CHUNK), :] = jnp.zeros((SEG_CHUNK, sorted_sc.shape[1]), F32)

    def place(i, carry):
        for k in range(2):
            sorted_sc[pl.ds(slot_ref[0, 0, k * tm + i], 1), :] = h_ref[pl.ds(i, 1), :]
        return carry

    lax.fori_loop(0, tm, place, 0, unroll=8)
    _segment_copies(table_ref, xs_ref, sorted_sc, sem, to_hbm=True, wait=False)
    _segment_copies(table_ref, xs_ref, sorted_sc, sem, to_hbm=True, wait=True)


def _sorted_rows(tm):
    return -(-(2 * tm + MOE_EXPERTS * (SEG_ALIGN - 1)) // SEG_CHUNK) * SEG_CHUNK


def _dispatch(h, table, slots, rows):
    t, d = h.shape
    tm = min(MOE_TILE, t)
    kern = functools.partial(_dispatch_kernel, tm=tm)
    return pl.pallas_call(
        kern,
        grid=(t // tm,),
        in_specs=[pl.BlockSpec((1, 1, LANES), lambda i: (i, 0, 0), memory_space=pltpu.SMEM),
                  pl.BlockSpec((1, 1, 2 * tm), lambda i: (i, 0, 0), memory_space=pltpu.SMEM),
                  pl.BlockSpec((tm, d), lambda i: (i, 0)),
                  pl.BlockSpec(memory_space=pl.ANY)],
        out_specs=pl.BlockSpec(memory_space=pl.ANY),
        out_shape=jax.ShapeDtypeStruct((rows, d), F32),
        scratch_shapes=[pltpu.VMEM((_sorted_rows(tm), d), F32), pltpu.SemaphoreType.DMA(())],
        input_output_aliases={3: 0},
        compiler_params=_cparams(("arbitrary",)),
        name="moe_dispatch",
    )(table, slots, h, jnp.zeros((rows, d), F32))


def _expert_kernel(be_ref, nu_ref, xs_ref, wg_ref, wu_ref, wd_ref, ys_ref):
    live = pl.program_id(0) < nu_ref[0]

    @pl.when(jnp.logical_not(live))
    def _():
        ys_ref[...] = jnp.zeros(ys_ref.shape, F32)

    @pl.when(live)
    def _():
        x = xs_ref[...].astype(BF16)
        g = jnp.dot(x, wg_ref[...].astype(BF16), preferred_element_type=F32)
        u = jnp.dot(x, wu_ref[...].astype(BF16), preferred_element_type=F32)
        a = (g * (1.0 / (1.0 + jnp.exp(-g))) * u).astype(BF16)
        ys_ref[...] = jnp.dot(a, wd_ref[...].astype(BF16), preferred_element_type=F32)


def _experts(xs, block_e, n_used, w_gate, w_up, w_down, layer):
    rows, d = xs.shape
    n_blocks = rows // MOE_BLK
    hid = w_gate.shape[-1]
    used = lambda i, be, nu: (jnp.minimum(i, nu[0] - 1), 0)
    expert = lambda i, be, nu: (layer, be[jnp.minimum(i, nu[0] - 1)], 0, 0)
    grid_spec = pltpu.PrefetchScalarGridSpec(
        num_scalar_prefetch=2,
        grid=(n_blocks,),
        in_specs=[pl.BlockSpec((MOE_BLK, d), used),
                  pl.BlockSpec((None, None, d, hid), expert),
                  pl.BlockSpec((None, None, d, hid), expert),
                  pl.BlockSpec((None, None, hid, d), expert)],
        out_specs=pl.BlockSpec((MOE_BLK, d), lambda i, be, nu: (i, 0)),
    )
    return pl.pallas_call(
        _expert_kernel,
        grid_spec=grid_spec,
        out_shape=jax.ShapeDtypeStruct((rows, d), F32),
        compiler_params=_cparams(("arbitrary",)),
        name="moe_experts",
    )(block_e, n_used, xs, w_gate, w_up, w_down)


def _combine_kernel(table_ref, slot_ref, ys_ref, x_ref, gf_ref, gate_ref, o_ref, sorted_sc, buf, sem, *, tm):
    _segment_copies(table_ref, ys_ref, sorted_sc, sem, to_hbm=False, wait=False)
    _segment_copies(table_ref, ys_ref, sorted_sc, sem, to_hbm=False, wait=True)

    def pick(i, carry):
        for k in range(2):
            buf[k, pl.ds(i, 1), :] = sorted_sc[pl.ds(slot_ref[0, 0, k * tm + i], 1), :]
        return carry

    lax.fori_loop(0, tm, pick, 0, unroll=8)
    gates = gate_ref[...]
    y = gates[:, 0:1] * buf[0] + gates[:, 1:2] * buf[1]
    o_ref[...] = x_ref[...] + gf_ref[...] * y


def _combine(ys, table, slots, x, gate_f, gates, seq):
    t, d = x.shape
    tm = min(MOE_TILE, t)
    per_b = seq // tm if seq >= tm else 1
    kern = functools.partial(_combine_kernel, tm=tm)
    return pl.pallas_call(
        kern,
        grid=(t // tm,),
        in_specs=[pl.BlockSpec((1, 1, LANES), lambda i: (i, 0, 0), memory_space=pltpu.SMEM),
                  pl.BlockSpec((1, 1, 2 * tm), lambda i: (i, 0, 0), memory_space=pltpu.SMEM),
                  pl.BlockSpec(memory_space=pl.ANY),
                  pl.BlockSpec((tm, d), lambda i: (i, 0)),
                  pl.BlockSpec((None, 1, d), lambda i: (i // per_b, 0, 0)),
                  pl.BlockSpec((tm, ROUTER_COLS), lambda i: (i, 0))],
        out_specs=pl.BlockSpec((tm, d), lambda i: (i, 0)),
        out_shape=jax.ShapeDtypeStruct((t, d), F32),
        scratch_shapes=[pltpu.VMEM((_sorted_rows(tm), d), F32), pltpu.VMEM((2, tm, d), F32),
                        pltpu.SemaphoreType.DMA(())],
        compiler_params=_cparams(("arbitrary",)),
        name="moe_combine",
    )(table, slots, ys, x, gate_f, gates)


def _moe(h, logits, x1, gate_f, w_gate, w_up, w_down, layer, seq):
    rows = _moe_rows(h.shape[0])
    route, gates, seg, totals = _route(logits)
    table, slots, block_e, n_used = _dispatch_plan(route, seg, totals[:, 0], rows // MOE_BLK)
    xs = _dispatch(h, table, slots, rows)
    ys = _experts(xs, block_e, n_used, w_gate, w_up, w_down, layer)
    return _combine(ys, table, slots, x1, gate_f, gates, seq)


def _router_weights(w_group, b_group, w_expert, b_expert):
    d = w_group.shape[0]
    w = jnp.zeros((d, ROUTER_COLS), F32)
    w = w.at[:, 0:MOE_GROUPS].set(w_group.astype(F32))
    w = w.at[:, EXPERT_COL0:EXPERT_COL0 + MOE_EXPERTS].set(w_expert.astype(F32))
    b = jnp.zeros((1, ROUTER_COLS), F32)
    b = b.at[0, 0:MOE_GROUPS].set(b_group.astype(F32))
    b = b.at[0, EXPERT_COL0:EXPERT_COL0 + MOE_EXPERTS].set(b_expert.astype(F32))
    return w, b


def _lambda_init(layer):
    return 0.8 - 0.6 * math.exp(-0.3 * layer)


def kernel(x, c, rel_bias, ada_w, ada_b, norm_attn, norm_ffn, a_w_qkv, a_q_norm, a_k_norm, a_lambda, a_subln, a_w_o, kv_norm, kv_ada_w, kv_ada_b, kv_w, kv_k_norm, b_w_q, b_q_norm, b_w_o, moe_w_group, moe_b_group, moe_w_expert, moe_b_expert, moe_w_gate, moe_w_up, moe_w_down):
    batch, seq, d = x.shape
    t = batch * seq
    scale = HEAD_DIM ** -0.5
    n_bw = N_GROUPS * B_WIDTH

    c_pad = jnp.zeros((8, d), F32).at[:batch].set(c.astype(F32))
    mod = _modulation(c_pad, ada_w, ada_b)[:, :batch]
    kv_mod = _modulation(c_pad, kv_ada_w[None], kv_ada_b[None])[0, :batch]

    def part(m, i):
        return m[:, i * d:(i + 1) * d].reshape(batch, 1, d)

    a_bias, nd = _attn_a_bias_tiles(rel_bias, seq, min(ATT_TILE, seq))
    b_bias = [_attn_b_bias_tiles(rel_bias, window // dil, dil) for window, dil in B_GROUPS]
    xf = x.reshape(t, d).astype(F32)
    kv = None
    for layer in range(DEPTH):
        m = mod[layer]
        sh_a, sc_a, g_a, sh_f, sc_f, g_f = (part(m, i) for i in range(6))
        w_r, b_r = _router_weights(moe_w_group[layer], moe_b_group[layer], moe_w_expert[layer], moe_b_expert[layer])
        if layer < N_A_LAYERS:
            qk_gain = jnp.concatenate([jnp.tile(a_q_norm[layer].astype(F32) * (scale * LOG2_E), 2 * N_HEADS),
                                       jnp.tile(a_k_norm[layer].astype(F32), 2 * N_HEADS),
                                       jnp.ones((N_HEADS * LANES,), F32)])
            qkv = _norm_proj(xf, norm_attn[layer], sh_a, sc_a, a_w_qkv[layer].astype(BF16), qk_gain,
                             2 * N_HEADS * LANES, seq)
            o = _attn_a(qkv, a_bias, nd, a_lambda[layer].astype(F32), a_subln[layer].astype(F32),
                        _lambda_init(layer), batch, seq)
            att = [o]
            wo = a_w_o[layer].astype(BF16)
        else:
            j = layer - N_A_LAYERS
            if kv is None:
                k_gain = jnp.concatenate([jnp.tile(kv_k_norm.astype(F32), (1, N_HEADS)).reshape(-1),
                                          jnp.ones((n_bw,), F32)])
                kv = _norm_proj(xf, kv_norm, part(kv_mod, 0), part(kv_mod, 1), kv_w.astype(BF16), k_gain, n_bw, seq)
            q_gain = jnp.tile(b_q_norm[j].astype(F32) * scale, (1, N_HEADS)).reshape(-1)
            q = _norm_proj(xf, norm_attn[layer], sh_a, sc_a, b_w_q[j].astype(BF16), q_gain, n_bw, seq)
            outs, lses = [], []
            for g, (window, dil) in enumerate(B_GROUPS):
                o_g, lse_g = _attn_b_group(q, kv, b_bias[g], g, dil, batch, seq)
                outs.append(o_g)
                lses.append(lse_g)
            att = outs + lses
            wo = b_w_o[j].astype(BF16)
        x1, hf, logits = _out_proj(att, wo, xf, g_a, norm_ffn[layer], sh_f, sc_f, w_r, b_r, seq)
        xf = _moe(hf, logits, x1, g_f, moe_w_gate, moe_w_up, moe_w_down, layer, seq)
    return xf.reshape(batch, seq, d).astype(x.dtype)
```

```python
import functools
import math

import jax
import jax.numpy as jnp
import numpy as np
from jax import lax
from jax.experimental import pallas as pl
from jax.experimental.pallas import tpu as pltpu

F32 = jnp.float32
BF16 = jnp.bfloat16

D_MODEL = 1024
DEPTH = 4
N_A_LAYERS = 2
N_HEADS = 8
HEAD_DIM = 64
B_GROUPS = ((128, 1), (512, 4), (2048, 16))
N_GROUPS = len(B_GROUPS)
B_WIDTH = N_HEADS * HEAD_DIM
REL_BUCKETS = 32
REL_MAX_EXACT = 16
REL_MAX_DIST = 2048
MOE_GROUPS = 4
MOE_EPG = 8
MOE_EXPERTS = MOE_GROUPS * MOE_EPG
MOE_HIDDEN = 512
EPS = 1e-6
NEG_INF = float("-inf")
LOG2_E = math.log2(math.e)

LANES = 128
MXU_DIM = 256
ROUTER_COLS = 128
EXPERT_COL0 = 8

TOK_TILE = 512
ATT_TILE = 512
B_TOKENS = 1024
BQ_SUB = 128
MOE_BLK = 256
MOE_TILE = 512
SEG_ALIGN = 8
SEG_CHUNK = 8
VMEM_LIMIT = 56 * 1024 * 1024


def _cparams(sem):
    return pltpu.CompilerParams(dimension_semantics=sem, vmem_limit_bytes=VMEM_LIMIT)


def _bucket_table(max_dist):
    n = np.arange(max_dist + 1)
    nf = np.maximum(n, 1).astype(np.float64)
    large = REL_MAX_EXACT + (np.log(nf / REL_MAX_EXACT) / math.log(REL_MAX_DIST / REL_MAX_EXACT)
                             * (REL_BUCKETS - REL_MAX_EXACT)).astype(np.int64)
    large = np.minimum(large, REL_BUCKETS - 1)
    return np.where(n < REL_MAX_EXACT, n, large).astype(np.int32)


def _toeplitz(v, n, m):
    length = v.shape[-1]
    assert length >= n + m - 1 and length - 1 >= m
    lead = v.shape[:-1]
    flat = jnp.tile(v, (1,) * len(lead) + (n,))[..., :n * (length - 1)]
    return flat.reshape(lead + (n, length - 1))[..., :m]


def _diag_values(rel_bias, dist, ok):
    table = _bucket_table(int(dist.max()))
    vals = rel_bias.astype(F32)[table[np.clip(dist, 0, None)]]
    vals = jnp.where(ok[..., None], vals, NEG_INF)
    return jnp.moveaxis(vals, -1, 0)


def _attn_a_bias_diags(rel_bias, seq, tile):
    table = _bucket_table(seq)
    last_start = int(np.argmax(table == REL_BUCKETS - 1))
    nd = 0
    while nd * tile - (tile - 1) < last_start and nd * tile < seq:
        nd += 1
    nd += 1
    length = 2 * tile
    u = np.arange(length)
    u = np.where(u < tile, u, u - length)
    dist = np.arange(nd)[:, None] * tile - u[None, :]
    return _diag_values(rel_bias, dist, dist >= 0) * LOG2_E, nd


def _attn_b_bias_tiles(rel_bias, band, dil):
    length = 3 * BQ_SUB
    u = np.arange(length)
    u = np.where(u < 2 * BQ_SUB, u, u - length)
    rel = BQ_SUB - u
    ok = (rel >= 0) & (rel <= band)
    general = _toeplitz(_diag_values(rel_bias, dil * rel, ok), BQ_SUB, 2 * BQ_SUB)
    no_prev = np.arange(2 * BQ_SUB)[None, None, :] >= BQ_SUB
    return jnp.stack([jnp.where(no_prev, general, NEG_INF), general])


def _group_sum_matrix():
    r = np.arange(MXU_DIM)
    return jnp.asarray((r[:, None] // HEAD_DIM) == (r[None, :] // HEAD_DIM), BF16)


def _mod_kernel(c_ref, w_ref, b_ref, o_ref):
    c = c_ref[...]
    c_act = c * (1.0 / (1.0 + jnp.exp(-c)))
    o_ref[...] = jnp.dot(c_act, w_ref[...], preferred_element_type=F32,
                         precision=lax.Precision.HIGHEST) + b_ref[...]


def _modulation(c_pad, w, b, tn=1024):
    nl, d, n = w.shape
    return pl.pallas_call(
        _mod_kernel,
        grid=(nl, n // tn),
        in_specs=[pl.BlockSpec((8, d), lambda l, j: (0, 0)),
                  pl.BlockSpec((None, d, tn), lambda l, j: (l, 0, j)),
                  pl.BlockSpec((None, 1, tn), lambda l, j: (l, 0, j))],
        out_specs=pl.BlockSpec((None, 8, tn), lambda l, j: (l, 0, j)),
        out_shape=jax.ShapeDtypeStruct((nl, 8, n), F32),
        compiler_params=_cparams(("parallel", "parallel")),
        name="adaln_mod",
    )(c_pad, w, b.reshape(nl, 1, n))


def _rms(x, gain):
    return x * lax.rsqrt(jnp.mean(x * x, axis=-1, keepdims=True) + EPS) * gain


def _proj_kernel(x_ref, g_ref, sh_ref, sc_ref, w_ref, hg_ref, bd_ref, o_ref, *, n_out, n_norm):
    h = _rms(x_ref[...], g_ref[...]) * (1.0 + sc_ref[...]) + sh_ref[...]
    hb = h.astype(BF16)
    for c0 in range(0, n_out, MXU_DIM):
        a = jnp.dot(hb, w_ref[:, c0:c0 + MXU_DIM], preferred_element_type=F32)
        if c0 < n_norm:
            ms = jnp.dot((a * a).astype(BF16), bd_ref[...], preferred_element_type=F32) * (1.0 / HEAD_DIM)
            a = a * lax.rsqrt(ms + EPS) * hg_ref[:, c0:c0 + MXU_DIM]
        o_ref[:, c0:c0 + MXU_DIM] = a.astype(o_ref.dtype)


def _norm_proj(x, gain, shift, scale, w_bf16, head_gain, n_norm, seq, out_dtype):
    t, d = x.shape
    n_out = w_bf16.shape[1]
    tm = min(TOK_TILE, seq)
    per_b = seq // tm
    kern = functools.partial(_proj_kernel, n_out=n_out, n_norm=n_norm)
    return pl.pallas_call(
        kern,
        grid=(t // tm,),
        in_specs=[pl.BlockSpec((tm, d), lambda i: (i, 0)),
                  pl.BlockSpec((1, d), lambda i: (0, 0)),
                  pl.BlockSpec((None, 1, d), lambda i: (i // per_b, 0, 0)),
                  pl.BlockSpec((None, 1, d), lambda i: (i // per_b, 0, 0)),
                  pl.BlockSpec((d, n_out), lambda i: (0, 0)),
                  pl.BlockSpec((1, n_out), lambda i: (0, 0)),
                  pl.BlockSpec((MXU_DIM, MXU_DIM), lambda i: (0, 0))],
        out_specs=pl.BlockSpec((tm, n_out), lambda i: (i, 0)),
        out_shape=jax.ShapeDtypeStruct((t, n_out), out_dtype),
        compiler_params=_cparams(("parallel",)),
        name="norm_proj",
    )(x, gain.reshape(1, d), shift, scale, w_bf16, head_gain.reshape(1, n_out), _group_sum_matrix())


def _lane_tile(a, reps):
    return jnp.concatenate([a] * reps, axis=1)


def _attn_a_kernel(q_ref, k_ref, v_ref, diag_ref, lam_ref, g_ref, o_ref, bias_ref, m_sc, acc_sc,
                   *, tile, nd, lam_init):
    qi = pl.program_id(2)

    @pl.when((pl.program_id(1) == 0) & (qi == 0))
    def _():
        for d in range(nd):
            row = jnp.broadcast_to(diag_ref[d:d + 1, :], (tile, 2 * tile))
            bias_ref[d] = pltpu.roll(row, 0, 1, stride=1, stride_axis=0)[:, :tile]

    q = q_ref[...]
    lane = lax.broadcasted_iota(jnp.int32, q.shape, 1)
    zero = jnp.zeros_like(q)
    qq = jnp.concatenate([jnp.where(lane < HEAD_DIM, q, zero),
                          jnp.where(lane >= HEAD_DIM, q, zero)], axis=0)
    m_sc[...] = jnp.full(m_sc.shape, NEG_INF, F32)
    acc_sc[...] = jnp.zeros(acc_sc.shape, F32)
    ones = jnp.ones((tile, LANES), BF16)

    def body(j, carry):
        start = pl.multiple_of(j * tile, tile)
        k = k_ref[pl.ds(start, tile), :]
        v_ones = jnp.concatenate([v_ref[pl.ds(start, tile), :], ones], axis=1)
        s = lax.dot_general(qq, k, (((1,), (1,)), ((), ())), preferred_element_type=F32)
        bias = bias_ref[jnp.minimum(qi - j, nd - 1)]
        for c in range(2):
            sc = s[c * tile:(c + 1) * tile] + bias
            m_prev = m_sc[c]
            m_new = jnp.maximum(m_prev, jnp.max(sc, axis=-1, keepdims=True))
            alpha = jnp.exp2(m_prev - m_new)
            p = jnp.exp2(sc - _lane_tile(m_new, tile // LANES))
            acc_sc[c] = (_lane_tile(alpha, 2) * acc_sc[c]
                         + jnp.dot(p.astype(BF16), v_ones, preferred_element_type=F32))
            m_sc[c] = m_new
        return carry

    lax.fori_loop(0, qi + 1, body, 0)

    lp = lam_ref[...]
    lam = (jnp.exp(jnp.sum(lp[0:1] * lp[1:2], axis=-1, keepdims=True))
           - jnp.exp(jnp.sum(lp[2:3] * lp[3:4], axis=-1, keepdims=True)) + lam_init)
    a0, a1 = acc_sc[0], acc_sc[1]
    o = a0[:, :LANES] / a0[:, LANES:] - lam * (a1[:, :LANES] / a1[:, LANES:])
    o_ref[...] = (_rms(o, g_ref[...]) * (1.0 - lam_init)).astype(BF16)


def _attn_a(qkv, bias_diags, nd, lam_p, subln, lam_init, batch, seq):
    tile = min(ATT_TILE, seq)
    nq = seq // tile
    qkv3 = qkv.reshape(batch, seq, 3 * N_HEADS * LANES)
    kern = functools.partial(_attn_a_kernel, tile=tile, nd=nd, lam_init=lam_init)
    out = pl.pallas_call(
        kern,
        grid=(N_HEADS, batch, nq),
        in_specs=[pl.BlockSpec((None, tile, LANES), lambda h, b, i: (b, i, h)),
                  pl.BlockSpec((None, seq, LANES), lambda h, b, i: (b, 0, N_HEADS + h)),
                  pl.BlockSpec((None, seq, LANES), lambda h, b, i: (b, 0, 2 * N_HEADS + h)),
                  pl.BlockSpec((None, nd, 2 * tile), lambda h, b, i: (h, 0, 0)),
                  pl.BlockSpec((4, HEAD_DIM), lambda h, b, i: (0, 0)),
                  pl.BlockSpec((1, LANES), lambda h, b, i: (0, 0))],
        out_specs=pl.BlockSpec((None, tile, LANES), lambda h, b, i: (b, i, h)),
        out_shape=jax.ShapeDtypeStruct((batch, seq, N_HEADS * LANES), BF16),
        scratch_shapes=[pltpu.VMEM((nd, tile, tile), F32), pltpu.VMEM((2, tile, LANES), F32),
                        pltpu.VMEM((2, tile, 2 * LANES), F32)],
        compiler_params=_cparams(("arbitrary", "arbitrary", "arbitrary")),
        name="attn_a",
    )(qkv3, qkv3, qkv3, bias_diags, lam_p, subln.reshape(1, LANES))
    return out.reshape(batch * seq, N_HEADS * LANES)


def _attn_b_kernel(q_ref, k_ref, v_ref, kp_ref, vp_ref, bias_ref, o_ref, lse_ref, *, dil, n_sub):
    first_tile = pl.program_id(1) == 0
    span = BQ_SUB * dil
    lane = lax.broadcasted_iota(jnp.int32, (BQ_SUB, LANES), 1)
    low = lane < HEAD_DIM

    def residue(ref, base):
        if dil == 1:
            return ref[pl.ds(base, BQ_SUB), :]
        return ref[pl.ds(base, BQ_SUB, stride=dil), :]

    for sb in range(n_sub):
        for r in range(dil):
            base = sb * span + r
            qp = residue(q_ref, base).astype(BF16)
            if sb == 0:
                k_prev, v_prev = residue(kp_ref, r), residue(vp_ref, r)
                variant = jnp.where(first_tile, 0, 1)
            else:
                k_prev, v_prev = residue(k_ref, base - span), residue(v_ref, base - span)
                variant = 1
            kk = jnp.concatenate([k_prev, residue(k_ref, base)], axis=0).astype(BF16)
            vv = jnp.concatenate([v_prev, residue(v_ref, base)], axis=0).astype(BF16)
            zero = jnp.zeros_like(qp)
            qq = jnp.concatenate([jnp.where(low, qp, zero), jnp.where(low, zero, qp)], axis=0)
            s = lax.dot_general(qq, kk, (((1,), (1,)), ((), ())), preferred_element_type=F32)
            outs, lses = [], []
            for c in range(2):
                sc = s[c * BQ_SUB:(c + 1) * BQ_SUB] + bias_ref[variant, c]
                m = jnp.max(sc, axis=-1, keepdims=True)
                p = jnp.exp(sc - m)
                l = jnp.sum(p, axis=-1, keepdims=True)
                outs.append(jnp.dot(p.astype(BF16), vv, preferred_element_type=F32) / l)
                lses.append(m + jnp.log(l))
            o_val = jnp.where(low, outs[0], outs[1])
            lse_val = jnp.where(low, lses[0], lses[1])
            if dil == 1:
                o_ref[pl.ds(base, BQ_SUB), :] = o_val
                lse_ref[pl.ds(base, BQ_SUB), :] = lse_val
            else:
                o_ref[pl.ds(base, BQ_SUB, stride=dil), :] = o_val
                lse_ref[pl.ds(base, BQ_SUB, stride=dil), :] = lse_val


def _attn_b_group(q, kv, bias_tiles, g, dil, batch, seq):
    span = BQ_SUB * dil
    n_sub = max(1, min(B_TOKENS, seq) // span)
    tile = n_sub * span
    pairs = N_HEADS // 2
    q3 = q.reshape(batch, seq, N_GROUPS * B_WIDTH)
    kv3 = kv.reshape(batch, seq, 2 * N_GROUPS * B_WIDTH)
    kern = functools.partial(_attn_b_kernel, dil=dil, n_sub=n_sub)
    k_col = lambda hp: g * pairs + hp
    v_col = lambda hp: (N_GROUPS + g) * pairs + hp
    prev = lambda n: jnp.maximum(n * n_sub - 1, 0)
    out_shape = jax.ShapeDtypeStruct((batch, seq, B_WIDTH), F32)
    o, lse = pl.pallas_call(
        kern,
        grid=(batch, seq // tile, pairs),
        in_specs=[pl.BlockSpec((None, tile, LANES), lambda b, n, hp: (b, n, k_col(hp))),
                  pl.BlockSpec((None, tile, LANES), lambda b, n, hp: (b, n, k_col(hp))),
                  pl.BlockSpec((None, tile, LANES), lambda b, n, hp: (b, n, v_col(hp))),
                  pl.BlockSpec((None, span, LANES), lambda b, n, hp: (b, prev(n), k_col(hp))),
                  pl.BlockSpec((None, span, LANES), lambda b, n, hp: (b, prev(n), v_col(hp))),
                  pl.BlockSpec((2, 2, BQ_SUB, 2 * BQ_SUB), lambda b, n, hp: (0, hp, 0, 0))],
        out_specs=[pl.BlockSpec((None, tile, LANES), lambda b, n, hp: (b, n, hp)),
                   pl.BlockSpec((None, tile, LANES), lambda b, n, hp: (b, n, hp))],
        out_shape=[out_shape, out_shape],
        compiler_params=_cparams(("parallel", "parallel", "parallel")),
        name=f"attn_b_g{g}",
    )(q3, kv3, kv3, kv3, kv3, bias_tiles)
    return o.reshape(batch * seq, B_WIDTH), lse.reshape(batch * seq, B_WIDTH)


def _split_bf16(a):
    hi = a.astype(BF16)
    return hi, (a - hi.astype(F32)).astype(BF16)


def _out_tail(o_bf16, wo_ref, x_ref, ga_ref, gf_ref, shf_ref, scf_ref, wrh_ref, wrl_ref, br_ref,
              x1_ref, hf_ref, lg_ref):
    y = jnp.dot(o_bf16, wo_ref[...], preferred_element_type=F32)
    x1 = x_ref[...] + ga_ref[...] * y
    x1_ref[...] = x1
    hf = _rms(x1, gf_ref[...]) * (1.0 + scf_ref[...]) + shf_ref[...]
    hf_ref[...] = hf
    hi, lo = _split_bf16(hf)
    lg_ref[...] = (jnp.dot(hi, wrh_ref[...], preferred_element_type=F32)
                   + jnp.dot(lo, wrh_ref[...], preferred_element_type=F32)
                   + jnp.dot(hi, wrl_ref[...], preferred_element_type=F32) + br_ref[...])


def _out_a_kernel(o_ref, *rest):
    _out_tail(o_ref[...], *rest)


def _out_b_kernel(o0, o1, o2, l0, l1, l2, *rest):
    la, lb, lc = l0[...], l1[...], l2[...]
    m = jnp.maximum(jnp.maximum(la, lb), lc)
    ea, eb, ec = jnp.exp(la - m), jnp.exp(lb - m), jnp.exp(lc - m)
    z = ea + eb + ec
    o = (ea / z) * o0[...] + (eb / z) * o1[...] + (ec / z) * o2[...]
    _out_tail(o.astype(BF16), *rest)


def _out_proj(att_inputs, wo_bf16, x, gate_a, gain_f, shift_f, scale_f, w_router, b_router, seq):
    t, d = x.shape
    tm = min(TOK_TILE, seq)
    per_b = seq // tm
    row = lambda i: (i, 0)
    fixed = lambda i: (0, 0)
    per_batch = lambda i: (i // per_b, 0, 0)
    att_specs = [pl.BlockSpec((tm, a.shape[1]), row) for a in att_inputs]
    kern = _out_a_kernel if len(att_inputs) == 1 else _out_b_kernel
    return pl.pallas_call(
        kern,
        grid=(t // tm,),
        in_specs=att_specs + [pl.BlockSpec(wo_bf16.shape, fixed),
                              pl.BlockSpec((tm, d), row),
                              pl.BlockSpec((None, 1, d), per_batch),
                              pl.BlockSpec((1, d), fixed),
                              pl.BlockSpec((None, 1, d), per_batch),
                              pl.BlockSpec((None, 1, d), per_batch),
                              pl.BlockSpec((d, ROUTER_COLS), fixed),
                              pl.BlockSpec((d, ROUTER_COLS), fixed),
                              pl.BlockSpec((1, ROUTER_COLS), fixed)],
        out_specs=[pl.BlockSpec((tm, d), row), pl.BlockSpec((tm, d), row),
                   pl.BlockSpec((tm, ROUTER_COLS), row)],
        out_shape=[jax.ShapeDtypeStruct((t, d), F32), jax.ShapeDtypeStruct((t, d), F32),
                   jax.ShapeDtypeStruct((t, ROUTER_COLS), F32)],
        compiler_params=_cparams(("parallel",)),
        name="out_proj",
    )(*att_inputs, wo_bf16, x, gate_a, gain_f.reshape(1, d), shift_f, scale_f, *_split_bf16(w_router), b_router)


def _route_kernel(lg_ref, tri_ref, eid_ref, gate_ref, seg_ref, cnt_ref, cnt_sc):
    lt = lg_ref[...].T
    best, g_idx = lt[0:1], jnp.zeros((1, lt.shape[1]), jnp.int32)
    for g in range(1, MOE_GROUPS):
        upd = lt[g:g + 1] > best
        best = jnp.where(upd, lt[g:g + 1], best)
        g_idx = jnp.where(upd, g, g_idx)
    denom = jnp.zeros_like(best)
    for g in range(MOE_GROUPS):
        denom = denom + jnp.exp(lt[g:g + 1] - best)
    g_w = 1.0 / denom
    e_sel = lt[EXPERT_COL0:EXPERT_COL0 + MOE_EPG]
    for g in range(1, MOE_GROUPS):
        r0 = EXPERT_COL0 + g * MOE_EPG
        e_sel = jnp.where(g_idx == g, lt[r0:r0 + MOE_EPG], e_sel)

    def first_max(vals):
        v, i = vals[0:1], jnp.zeros((1, vals.shape[1]), jnp.int32)
        for e in range(1, MOE_EPG):
            upd = vals[e:e + 1] > v
            v = jnp.where(upd, vals[e:e + 1], v)
            i = jnp.where(upd, e, i)
        return v, i

    v1, i1 = first_max(e_sel)
    row = lax.broadcasted_iota(jnp.int32, e_sel.shape, 0)
    v2, i2 = first_max(jnp.where(row == i1, NEG_INF, e_sel))
    e2 = jnp.exp(v2 - v1)
    w1 = 1.0 / (1.0 + e2)
    w2 = e2 / (1.0 + e2)
    e1 = g_idx * MOE_EPG + i1
    e2 = g_idx * MOE_EPG + i2

    @pl.when(pl.program_id(0) == 0)
    def _():
        cnt_sc[...] = jnp.zeros(cnt_sc.shape, F32)

    tm = lt.shape[1]
    erows = lax.broadcasted_iota(jnp.int32, (MOE_EXPERTS, tm), 0)
    ohs, pres = [], []
    for e_k in (e1, e2):
        oh = (erows == e_k).astype(F32)
        ohs.append(oh)
        pres.append(jnp.dot(oh.astype(BF16), tri_ref[...], preferred_element_type=F32))
    n0 = pres[0][:, tm - 1:]
    n_tot = n0 + pres[1][:, tm - 1:]
    chunks = jnp.floor((n_tot + (SEG_ALIGN - 1.0)) * (1.0 / SEG_ALIGN))
    er = lax.broadcasted_iota(jnp.int32, (MOE_EXPERTS, MOE_EXPERTS), 0)
    ec = lax.broadcasted_iota(jnp.int32, (MOE_EXPERTS, MOE_EXPERTS), 1)
    before = jnp.dot((ec < er).astype(BF16), jnp.broadcast_to(chunks, (MOE_EXPERTS, LANES)).astype(BF16),
                     preferred_element_type=F32)
    seg_start = before[:, 0:1] * SEG_ALIGN
    slot0 = jnp.sum(ohs[0] * (seg_start + pres[0] - 1.0), axis=0, keepdims=True)
    slot1 = jnp.sum(ohs[1] * (seg_start + n0 + pres[1] - 1.0), axis=0, keepdims=True)
    seg_off = cnt_sc[...]
    cnt_sc[...] = seg_off + chunks * SEG_ALIGN
    cnt_ref[...] = jnp.broadcast_to(cnt_sc[...], cnt_ref.shape).astype(jnp.int32)
    scol = lax.broadcasted_iota(jnp.int32, seg_ref.shape, 1)
    seg_ref[...] = jnp.where(scol == 0, chunks, jnp.where(scol == 1, seg_start, jnp.where(
        scol == 2, seg_off, 0.0))).astype(jnp.int32)

    erow = lax.broadcasted_iota(jnp.int32, eid_ref.shape, 0)
    eid_ref[...] = jnp.where(erow == 0, e1, jnp.where(erow == 1, e2, jnp.where(
        erow == 2, slot0.astype(jnp.int32), jnp.where(erow == 3, slot1.astype(jnp.int32), 0))))
    grow = lax.broadcasted_iota(jnp.int32, lt.shape, 0)
    gates = jnp.where(grow == 0, g_w * w1, jnp.where(grow == 1, g_w * w2, 0.0))
    gate_ref[...] = gates.T


def _route(logits):
    t = logits.shape[0]
    tm = min(MOE_TILE, t)
    n_tiles = t // tm
    r = np.arange(tm)
    tri = jnp.asarray(r[:, None] <= r[None, :], BF16)
    return pl.pallas_call(
        _route_kernel,
        grid=(n_tiles,),
        in_specs=[pl.BlockSpec((tm, ROUTER_COLS), lambda i: (i, 0)),
                  pl.BlockSpec((tm, tm), lambda i: (0, 0))],
        out_specs=[pl.BlockSpec((8, tm), lambda i: (0, i)), pl.BlockSpec((tm, ROUTER_COLS), lambda i: (i, 0)),
                   pl.BlockSpec((MOE_EXPERTS, LANES), lambda i: (i, 0)),
                   pl.BlockSpec((MOE_EXPERTS, LANES), lambda i: (0, 0))],
        out_shape=[jax.ShapeDtypeStruct((8, t), jnp.int32), jax.ShapeDtypeStruct((t, ROUTER_COLS), F32),
                   jax.ShapeDtypeStruct((n_tiles * MOE_EXPERTS, LANES), jnp.int32),
                   jax.ShapeDtypeStruct((MOE_EXPERTS, LANES), jnp.int32)],
        scratch_shapes=[pltpu.VMEM((MOE_EXPERTS, 1), F32)],
        compiler_params=_cparams(("arbitrary",)),
        name="moe_route",
    )(logits, tri)


def _moe_rows(t):
    n_tiles = t // min(MOE_TILE, t)
    rows = 2 * t + n_tiles * MOE_EXPERTS * (SEG_ALIGN - 1) + MOE_EXPERTS * (MOE_BLK - 1)
    return -(-rows // MOE_BLK) * MOE_BLK


def _dispatch_plan(route, seg, totals, n_blocks):
    n_tiles = seg.shape[0] // MOE_EXPERTS
    tm = route.shape[1] // n_tiles
    padded = (totals + MOE_BLK - 1) // MOE_BLK * MOE_BLK
    pad_end = jnp.cumsum(padded)
    pad_start = pad_end - padded
    seg = seg.reshape(n_tiles, MOE_EXPERTS, LANES)
    table = jnp.concatenate([pad_start[None, :] + seg[:, :, 2], seg[:, :, 0], seg[:, :, 1],
                             jnp.zeros((n_tiles, LANES - 3 * MOE_EXPERTS), jnp.int32)], axis=1)
    slots = route[2:4].reshape(2, n_tiles, tm).transpose(1, 0, 2).reshape(n_tiles, 1, 2 * tm)
    block_start = jnp.arange(n_blocks, dtype=jnp.int32) * MOE_BLK
    block_e = jnp.minimum(jnp.sum(pad_end[None, :] <= block_start[:, None], axis=1), MOE_EXPERTS - 1).astype(jnp.int32)
    n_used = (pad_end[-1] // MOE_BLK).astype(jnp.int32).reshape(1)
    return table.reshape(n_tiles, 1, LANES), slots, block_e, n_used


def _segment_copies(table_ref, hbm_ref, sorted_ref, sem, to_hbm, wait):
    for e in range(MOE_EXPERTS):
        dst0 = table_ref[0, 0, e]
        n_chunks = table_ref[0, 0, MOE_EXPERTS + e]
        src0 = table_ref[0, 0, 2 * MOE_EXPERTS + e]

        def chunk(c, carry):
            local = sorted_ref.at[pl.ds(pl.multiple_of(src0 + c * SEG_CHUNK, SEG_CHUNK), SEG_CHUNK), :]
            remote = hbm_ref.at[pl.ds(pl.multiple_of(dst0 + c * SEG_CHUNK, SEG_CHUNK), SEG_CHUNK), :]
            copy = pltpu.make_async_copy(local, remote, sem) if to_hbm else pltpu.make_async_copy(remote, local, sem)
            if wait:
                copy.wait()
            else:
                copy.start()
            return carry

        lax.fori_loop(0, n_chunks, chunk, 0)


def _dispatch_kernel(table_ref, slot_ref, h_ref, xs_in_ref, xs_ref, sorted_sc, sem, *, tm):
    del xs_in_ref
    for e in range(MOE_EXPERTS):
        n_chunks = table_ref[0, 0, MOE_EXPERTS + e]
        src0 = table_ref[0, 0, 2 * MOE_EXPERTS + e]

        @pl.when(n_chunks > 0)
        def _():
            last = pl.multiple_of(src0 + (n_chunks - 1) * SEG_CHUNK, SEG_CHUNK)
            sorted_sc[pl.ds(last, SEG_CHUNK), :] = jnp.zeros((SEG_CHUNK, sorted_sc.shape[1]), F32)

    def place(i, carry):
        for k in range(2):
            sorted_sc[pl.ds(slot_ref[0, 0, k * tm + i], 1), :] = h_ref[pl.ds(i, 1), :]
        return carry

    lax.fori_loop(0, tm, place, 0, unroll=8)
    _segment_copies(table_ref, xs_ref, sorted_sc, sem, to_hbm=True, wait=False)
    _segment_copies(table_ref, xs_ref, sorted_sc, sem, to_hbm=True, wait=True)


def _sorted_rows(tm):
    return -(-(2 * tm + MOE_EXPERTS * (SEG_ALIGN - 1)) // SEG_CHUNK) * SEG_CHUNK


def _dispatch(h, table, slots, rows):
    t, d = h.shape
    tm = min(MOE_TILE, t)
    kern = functools.partial(_dispatch_kernel, tm=tm)
    return pl.pallas_call(
        kern,
        grid=(t // tm,),
        in_specs=[pl.BlockSpec((1, 1, LANES), lambda i: (i, 0, 0), memory_space=pltpu.SMEM),
                  pl.BlockSpec((1, 1, 2 * tm), lambda i: (i, 0, 0), memory_space=pltpu.SMEM),
                  pl.BlockSpec((tm, d), lambda i: (i, 0)),
                  pl.BlockSpec(memory_space=pl.ANY)],
        out_specs=pl.BlockSpec(memory_space=pl.ANY),
        out_shape=jax.ShapeDtypeStruct((rows, d), F32),
        scratch_shapes=[pltpu.VMEM((_sorted_rows(tm), d), F32), pltpu.SemaphoreType.DMA(())],
        input_output_aliases={3: 0},
        compiler_params=_cparams(("arbitrary",)),
        name="moe_dispatch",
    )(table, slots, h, jnp.zeros((rows, d), F32))


def _expert_kernel(be_ref, nu_ref, xs_ref, wg_ref, wu_ref, wd_ref, ys_ref):
    live = pl.program_id(0) < nu_ref[0]

    @pl.when(jnp.logical_not(live))
    def _():
        ys_ref[...] = jnp.zeros(ys_ref.shape, F32)

    @pl.when(live)
    def _():
        x = xs_ref[...].astype(BF16)
        g = jnp.dot(x, wg_ref[...].astype(BF16), preferred_element_type=F32)
        u = jnp.dot(x, wu_ref[...].astype(BF16), preferred_element_type=F32)
        a = (g * (1.0 / (1.0 + jnp.exp(-g))) * u).astype(BF16)
        ys_ref[...] = jnp.dot(a, wd_ref[...].astype(BF16), preferred_element_type=F32)


def _experts(xs, block_e, n_used, w_gate, w_up, w_down, layer):
    rows, d = xs.shape
    n_blocks = rows // MOE_BLK
    hid = w_gate.shape[-1]
    used = lambda i, be, nu: (jnp.minimum(i, nu[0] - 1), 0)
    expert = lambda i, be, nu: (layer, be[jnp.minimum(i, nu[0] - 1)], 0, 0)
    grid_spec = pltpu.PrefetchScalarGridSpec(
        num_scalar_prefetch=2,
        grid=(n_blocks,),
        in_specs=[pl.BlockSpec((MOE_BLK, d), used),
                  pl.BlockSpec((None, None, d, hid), expert),
                  pl.BlockSpec((None, None, d, hid), expert),
                  pl.BlockSpec((None, None, hid, d), expert)],
        out_specs=pl.BlockSpec((MOE_BLK, d), lambda i, be, nu: (i, 0)),
    )
    return pl.pallas_call(
        _expert_kernel,
        grid_spec=grid_spec,
        out_shape=jax.ShapeDtypeStruct((rows, d), F32),
        compiler_params=_cparams(("arbitrary",)),
        name="moe_experts",
    )(block_e, n_used, xs, w_gate, w_up, w_down)


def _combine_kernel(table_ref, slot_ref, ys_ref, x_ref, gf_ref, gate_ref, o_ref, sorted_sc, buf, sem, *, tm):
    _segment_copies(table_ref, ys_ref, sorted_sc, sem, to_hbm=False, wait=False)
    _segment_copies(table_ref, ys_ref, sorted_sc, sem, to_hbm=False, wait=True)

    def pick(i, carry):
        for k in range(2):
            buf[k, pl.ds(i, 1), :] = sorted_sc[pl.ds(slot_ref[0, 0, k * tm + i], 1), :]
        return carry

    lax.fori_loop(0, tm, pick, 0, unroll=8)
    gates = gate_ref[...]
    y = gates[:, 0:1] * buf[0] + gates[:, 1:2] * buf[1]
    o_ref[...] = x_ref[...] + gf_ref[...] * y


def _combine(ys, table, slots, x, gate_f, gates, seq):
    t, d = x.shape
    tm = min(MOE_TILE, t)
    per_b = seq // tm if seq >= tm else 1
    kern = functools.partial(_combine_kernel, tm=tm)
    return pl.pallas_call(
        kern,
        grid=(t // tm,),
        in_specs=[pl.BlockSpec((1, 1, LANES), lambda i: (i, 0, 0), memory_space=pltpu.SMEM),
                  pl.BlockSpec((1, 1, 2 * tm), lambda i: (i, 0, 0), memory_space=pltpu.SMEM),
                  pl.BlockSpec(memory_space=pl.ANY),
                  pl.BlockSpec((tm, d), lambda i: (i, 0)),
                  pl.BlockSpec((None, 1, d), lambda i: (i // per_b, 0, 0)),
                  pl.BlockSpec((tm, ROUTER_COLS), lambda i: (i, 0))],
        out_specs=pl.BlockSpec((tm, d), lambda i: (i, 0)),
        out_shape=jax.ShapeDtypeStruct((t, d), F32),
        scratch_shapes=[pltpu.VMEM((_sorted_rows(tm), d), F32), pltpu.VMEM((2, tm, d), F32),
                        pltpu.SemaphoreType.DMA(())],
        compiler_params=_cparams(("arbitrary",)),
        name="moe_combine",
    )(table, slots, ys, x, gate_f, gates)


def _moe(h, logits, x1, gate_f, w_gate, w_up, w_down, layer, seq):
    rows = _moe_rows(h.shape[0])
    route, gates, seg, totals = _route(logits)
    table, slots, block_e, n_used = _dispatch_plan(route, seg, totals[:, 0], rows // MOE_BLK)
    xs = _dispatch(h, table, slots, rows)
    ys = _experts(xs, block_e, n_used, w_gate, w_up, w_down, layer)
    return _combine(ys, table, slots, x1, gate_f, gates, seq)


def _router_weights(w_group, b_group, w_expert, b_expert):
    d = w_group.shape[0]
    w = jnp.zeros((d, ROUTER_COLS), F32)
    w = w.at[:, 0:MOE_GROUPS].set(w_group.astype(F32))
    w = w.at[:, EXPERT_COL0:EXPERT_COL0 + MOE_EXPERTS].set(w_expert.astype(F32))
    b = jnp.zeros((1, ROUTER_COLS), F32)
    b = b.at[0, 0:MOE_GROUPS].set(b_group.astype(F32))
    b = b.at[0, EXPERT_COL0:EXPERT_COL0 + MOE_EXPERTS].set(b_expert.astype(F32))
    return w, b


def _lambda_init(layer):
    return 0.8 - 0.6 * math.exp(-0.3 * layer)


def kernel(x, c, rel_bias, ada_w, ada_b, norm_attn, norm_ffn, a_w_qkv, a_q_norm, a_k_norm, a_lambda, a_subln, a_w_o, kv_norm, kv_ada_w, kv_ada_b, kv_w, kv_k_norm, b_w_q, b_q_norm, b_w_o, moe_w_group, moe_b_group, moe_w_expert, moe_b_expert, moe_w_gate, moe_w_up, moe_w_down):
    batch, seq, d = x.shape
    t = batch * seq
    scale = HEAD_DIM ** -0.5
    n_bw = N_GROUPS * B_WIDTH

    c_pad = jnp.zeros((8, d), F32).at[:batch].set(c.astype(F32))
    mod = _modulation(c_pad, ada_w, ada_b)[:, :batch]
    kv_mod = _modulation(c_pad, kv_ada_w[None], kv_ada_b[None])[0, :batch]

    def part(m, i):
        return m[:, i * d:(i + 1) * d].reshape(batch, 1, d)

    a_bias, nd = _attn_a_bias_diags(rel_bias, seq, min(ATT_TILE, seq))
    b_bias = [_attn_b_bias_tiles(rel_bias, window // dil, dil) for window, dil in B_GROUPS]
    xf = x.reshape(t, d).astype(F32)
    kv = None
    for layer in range(DEPTH):
        m = mod[layer]
        sh_a, sc_a, g_a, sh_f, sc_f, g_f = (part(m, i) for i in range(6))
        w_r, b_r = _router_weights(moe_w_group[layer], moe_b_group[layer], moe_w_expert[layer], moe_b_expert[layer])
        if layer < N_A_LAYERS:
            qk_gain = jnp.concatenate([jnp.tile(a_q_norm[layer].astype(F32) * (scale * LOG2_E), 2 * N_HEADS),
                                       jnp.tile(a_k_norm[layer].astype(F32), 2 * N_HEADS),
                                       jnp.ones((N_HEADS * LANES,), F32)])
            qkv = _norm_proj(xf, norm_attn[layer], sh_a, sc_a, a_w_qkv[layer].astype(BF16), qk_gain,
                             2 * N_HEADS * LANES, seq, BF16)
            o = _attn_a(qkv, a_bias, nd, a_lambda[layer].astype(F32), a_subln[layer].astype(F32),
                        _lambda_init(layer), batch, seq)
            att = [o]
            wo = a_w_o[layer].astype(BF16)
        else:
            j = layer - N_A_LAYERS
            if kv is None:
                k_gain = jnp.concatenate([jnp.tile(kv_k_norm.astype(F32), (1, N_HEADS)).reshape(-1),
                                          jnp.ones((n_bw,), F32)])
                kv = _norm_proj(xf, kv_norm, part(kv_mod, 0), part(kv_mod, 1), kv_w.astype(BF16), k_gain, n_bw, seq, F32)
            q_gain = jnp.tile(b_q_norm[j].astype(F32) * scale, (1, N_HEADS)).reshape(-1)
            q = _norm_proj(xf, norm_attn[layer], sh_a, sc_a, b_w_q[j].astype(BF16), q_gain, n_bw, seq, F32)
            outs, lses = [], []
            for g, (window, dil) in enumerate(B_GROUPS):
                o_g, lse_g = _attn_b_group(q, kv, b_bias[g], g, dil, batch, seq)
                outs.append(o_g)
                lses.append(lse_g)
            att = outs + lses
            wo = b_w_o[j].astype(BF16)
        x1, hf, logits = _out_proj(att, wo, xf, g_a, norm_ffn[layer], sh_f, sc_f, w_r, b_r, seq)
        xf = _moe(hf, logits, x1, g_f, moe_w_gate, moe_w_up, moe_w_down, layer, seq)
    return xf.reshape(batch, seq, d).astype(x.dtype)
```

```python
import functools
import math

import jax
import jax.numpy as jnp
import numpy as np
from jax import lax
from jax.experimental import pallas as pl
from jax.experimental.pallas import tpu as pltpu

F32 = jnp.float32
BF16 = jnp.bfloat16

D_MODEL = 1024
DEPTH = 4
N_A_LAYERS = 2
N_HEADS = 8
HEAD_DIM = 64
B_GROUPS = ((128, 1), (512, 4), (2048, 16))
N_GROUPS = len(B_GROUPS)
B_WIDTH = N_HEADS * HEAD_DIM
REL_BUCKETS = 32
REL_MAX_EXACT = 16
REL_MAX_DIST = 2048
MOE_GROUPS = 4
MOE_EPG = 8
MOE_EXPERTS = MOE_GROUPS * MOE_EPG
MOE_HIDDEN = 512
EPS = 1e-6
NEG_INF = float("-inf")
LOG2_E = math.log2(math.e)

LANES = 128
MXU_DIM = 256
ROUTER_COLS = 128
EXPERT_COL0 = 8

TOK_TILE = 512
ATT_TILE = 512
B_TOKENS = 1024
BQ_SUB = 128
MOE_BLK = 256
MOE_TILE = 512
SEG_ALIGN = 8
SEG_CHUNK = 8
TABLE_W = 256
VMEM_LIMIT = 56 * 1024 * 1024


def _cparams(sem):
    return pltpu.CompilerParams(dimension_semantics=sem, vmem_limit_bytes=VMEM_LIMIT)


def _bucket_table(max_dist):
    n = np.arange(max_dist + 1)
    nf = np.maximum(n, 1).astype(np.float64)
    large = REL_MAX_EXACT + (np.log(nf / REL_MAX_EXACT) / math.log(REL_MAX_DIST / REL_MAX_EXACT)
                             * (REL_BUCKETS - REL_MAX_EXACT)).astype(np.int64)
    large = np.minimum(large, REL_BUCKETS - 1)
    return np.where(n < REL_MAX_EXACT, n, large).astype(np.int32)


def _toeplitz(v, n, m):
    length = v.shape[-1]
    assert length >= n + m - 1 and length - 1 >= m
    lead = v.shape[:-1]
    flat = jnp.tile(v, (1,) * len(lead) + (n,))[..., :n * (length - 1)]
    return flat.reshape(lead + (n, length - 1))[..., :m]


def _diag_values(rel_bias, dist, ok):
    table = _bucket_table(int(dist.max()))
    vals = rel_bias.astype(F32)[table[np.clip(dist, 0, None)]]
    vals = jnp.where(ok[..., None], vals, NEG_INF)
    return jnp.moveaxis(vals, -1, 0)


def _attn_a_bias_diags(rel_bias, seq, tile):
    table = _bucket_table(seq)
    last_start = int(np.argmax(table == REL_BUCKETS - 1))
    nd = 0
    while nd * tile - (tile - 1) < last_start and nd * tile < seq:
        nd += 1
    nd += 1
    length = 2 * tile
    u = np.arange(length)
    u = np.where(u < tile, u, u - length)
    dist = np.arange(nd)[:, None] * tile - u[None, :]
    return _diag_values(rel_bias, dist, dist >= 0) * LOG2_E, nd


def _attn_b_bias_tiles(rel_bias, band, dil):
    length = 3 * BQ_SUB
    u = np.arange(length)
    u = np.where(u < 2 * BQ_SUB, u, u - length)
    rel = BQ_SUB - u
    ok = (rel >= 0) & (rel <= band)
    general = _toeplitz(_diag_values(rel_bias, dil * rel, ok), BQ_SUB, 2 * BQ_SUB)
    no_prev = np.arange(2 * BQ_SUB)[None, None, :] >= BQ_SUB
    return jnp.stack([jnp.where(no_prev, general, NEG_INF), general])


def _group_sum_matrix():
    r = np.arange(MXU_DIM)
    return jnp.asarray((r[:, None] // HEAD_DIM) == (r[None, :] // HEAD_DIM), BF16)


def _mod_kernel(c_ref, w_ref, b_ref, o_ref):
    c = c_ref[...]
    c_act = c * (1.0 / (1.0 + jnp.exp(-c)))
    o_ref[...] = jnp.dot(c_act, w_ref[...], preferred_element_type=F32,
                         precision=lax.Precision.HIGHEST) + b_ref[...]


def _modulation(c_pad, w, b, tn=1024):
    nl, d, n = w.shape
    return pl.pallas_call(
        _mod_kernel,
        grid=(nl, n // tn),
        in_specs=[pl.BlockSpec((8, d), lambda l, j: (0, 0)),
                  pl.BlockSpec((None, d, tn), lambda l, j: (l, 0, j)),
                  pl.BlockSpec((None, 1, tn), lambda l, j: (l, 0, j))],
        out_specs=pl.BlockSpec((None, 8, tn), lambda l, j: (l, 0, j)),
        out_shape=jax.ShapeDtypeStruct((nl, 8, n), F32),
        compiler_params=_cparams(("parallel", "parallel")),
        name="adaln_mod",
    )(c_pad, w, b.reshape(nl, 1, n))


def _rms(x, gain):
    return x * lax.rsqrt(jnp.mean(x * x, axis=-1, keepdims=True) + EPS) * gain


def _proj_kernel(x_ref, g_ref, sh_ref, sc_ref, w_ref, hg_ref, bd_ref, o_ref, *, n_out, n_norm):
    h = _rms(x_ref[...], g_ref[...]) * (1.0 + sc_ref[...]) + sh_ref[...]
    hb = h.astype(BF16)
    wide = 2 * MXU_DIM
    for c0 in range(0, n_out, wide):
        a = jnp.dot(hb, w_ref[:, c0:c0 + wide], preferred_element_type=F32)
        if c0 < n_norm:
            sq = (a * a).astype(BF16)
            ms = jnp.concatenate([jnp.dot(sq[:, :MXU_DIM], bd_ref[...], preferred_element_type=F32),
                                  jnp.dot(sq[:, MXU_DIM:], bd_ref[...], preferred_element_type=F32)], axis=1)
            a = a * lax.rsqrt(ms * (1.0 / HEAD_DIM) + EPS) * hg_ref[:, c0:c0 + wide]
        o_ref[:, c0:c0 + wide] = a.astype(o_ref.dtype)


def _norm_proj(x, gain, shift, scale, w_bf16, head_gain, n_norm, seq, out_dtype):
    t, d = x.shape
    n_out = w_bf16.shape[1]
    tm = min(TOK_TILE, seq)
    per_b = seq // tm
    kern = functools.partial(_proj_kernel, n_out=n_out, n_norm=n_norm)
    return pl.pallas_call(
        kern,
        grid=(t // tm,),
        in_specs=[pl.BlockSpec((tm, d), lambda i: (i, 0)),
                  pl.BlockSpec((1, d), lambda i: (0, 0)),
                  pl.BlockSpec((None, 1, d), lambda i: (i // per_b, 0, 0)),
                  pl.BlockSpec((None, 1, d), lambda i: (i // per_b, 0, 0)),
                  pl.BlockSpec((d, n_out), lambda i: (0, 0)),
                  pl.BlockSpec((1, n_out), lambda i: (0, 0)),
                  pl.BlockSpec((MXU_DIM, MXU_DIM), lambda i: (0, 0))],
        out_specs=pl.BlockSpec((tm, n_out), lambda i: (i, 0)),
        out_shape=jax.ShapeDtypeStruct((t, n_out), out_dtype),
        compiler_params=_cparams(("parallel",)),
        name="norm_proj",
    )(x, gain.reshape(1, d), shift, scale, w_bf16, head_gain.reshape(1, n_out), _group_sum_matrix())


def _lane_tile(a, reps):
    return jnp.concatenate([a] * reps, axis=1)


def _attn_a_kernel(q_ref, k_ref, v_ref, diag_ref, lam_ref, g_ref, o_ref, bias_ref, m_sc, acc_sc,
                   *, tile, nd, lam_init):
    qi = pl.program_id(2)

    @pl.when((pl.program_id(1) == 0) & (qi == 0))
    def _():
        for d in range(nd):
            row = jnp.broadcast_to(diag_ref[d:d + 1, :], (tile, 2 * tile))
            bias_ref[d] = pltpu.roll(row, 0, 1, stride=1, stride_axis=0)[:, :tile]

    q = q_ref[...]
    lane = lax.broadcasted_iota(jnp.int32, q.shape, 1)
    zero = jnp.zeros_like(q)
    qq = jnp.concatenate([jnp.where(lane < HEAD_DIM, q, zero),
                          jnp.where(lane >= HEAD_DIM, q, zero)], axis=0)
    m_sc[...] = jnp.full(m_sc.shape, NEG_INF, F32)
    acc_sc[...] = jnp.zeros(acc_sc.shape, F32)
    ones = jnp.ones((tile, LANES), BF16)

    def body(j, carry):
        start = pl.multiple_of(j * tile, tile)
        k = k_ref[pl.ds(start, tile), :]
        v_ones = jnp.concatenate([v_ref[pl.ds(start, tile), :], ones], axis=1)
        s = lax.dot_general(qq, k, (((1,), (1,)), ((), ())), preferred_element_type=F32)
        bias = bias_ref[jnp.minimum(qi - j, nd - 1)]
        for c in range(2):
            sc = s[c * tile:(c + 1) * tile] + bias
            m_prev = m_sc[c]
            m_new = jnp.maximum(m_prev, jnp.max(sc, axis=-1, keepdims=True))
            alpha = jnp.exp2(m_prev - m_new)
            p = jnp.exp2(sc - _lane_tile(m_new, tile // LANES))
            acc_sc[c] = (_lane_tile(alpha, 2) * acc_sc[c]
                         + jnp.dot(p.astype(BF16), v_ones, preferred_element_type=F32))
            m_sc[c] = m_new
        return carry

    lax.fori_loop(0, qi + 1, body, 0)

    lp = lam_ref[...]
    lam = (jnp.exp(jnp.sum(lp[0:1] * lp[1:2], axis=-1, keepdims=True))
           - jnp.exp(jnp.sum(lp[2:3] * lp[3:4], axis=-1, keepdims=True)) + lam_init)
    a0, a1 = acc_sc[0], acc_sc[1]
    o = a0[:, :LANES] / a0[:, LANES:] - lam * (a1[:, :LANES] / a1[:, LANES:])
    o_ref[...] = (_rms(o, g_ref[...]) * (1.0 - lam_init)).astype(BF16)


def _attn_a(qkv, bias_diags, nd, lam_p, subln, lam_init, batch, seq):
    tile = min(ATT_TILE, seq)
    nq = seq // tile
    qkv3 = qkv.reshape(batch, seq, 3 * N_HEADS * LANES)
    kern = functools.partial(_attn_a_kernel, tile=tile, nd=nd, lam_init=lam_init)
    out = pl.pallas_call(
        kern,
        grid=(N_HEADS, batch, nq),
        in_specs=[pl.BlockSpec((None, tile, LANES), lambda h, b, i: (b, i, h)),
                  pl.BlockSpec((None, seq, LANES), lambda h, b, i: (b, 0, N_HEADS + h)),
                  pl.BlockSpec((None, seq, LANES), lambda h, b, i: (b, 0, 2 * N_HEADS + h)),
                  pl.BlockSpec((None, nd, 2 * tile), lambda h, b, i: (h, 0, 0)),
                  pl.BlockSpec((4, HEAD_DIM), lambda h, b, i: (0, 0)),
                  pl.BlockSpec((1, LANES), lambda h, b, i: (0, 0))],
        out_specs=pl.BlockSpec((None, tile, LANES), lambda h, b, i: (b, i, h)),
        out_shape=jax.ShapeDtypeStruct((batch, seq, N_HEADS * LANES), BF16),
        scratch_shapes=[pltpu.VMEM((nd, tile, tile), F32), pltpu.VMEM((2, tile, LANES), F32),
                        pltpu.VMEM((2, tile, 2 * LANES), F32)],
        compiler_params=_cparams(("arbitrary", "arbitrary", "arbitrary")),
        name="attn_a",
    )(qkv3, qkv3, qkv3, bias_diags, lam_p, subln.reshape(1, LANES))
    return out.reshape(batch * seq, N_HEADS * LANES)


def _attn_b_kernel(q_ref, k_ref, v_ref, kp_ref, vp_ref, bias_ref, o_ref, lse_ref, *, dil, n_sub):
    first_tile = pl.program_id(1) == 0
    span = BQ_SUB * dil
    lane = lax.broadcasted_iota(jnp.int32, (BQ_SUB, LANES), 1)
    low = lane < HEAD_DIM

    def residue(ref, base):
        if dil == 1:
            return ref[pl.ds(base, BQ_SUB), :]
        return ref[pl.ds(base, BQ_SUB, stride=dil), :]

    for sb in range(n_sub):
        for r in range(dil):
            base = sb * span + r
            qp = residue(q_ref, base).astype(BF16)
            if sb == 0:
                k_prev, v_prev = residue(kp_ref, r), residue(vp_ref, r)
                variant = jnp.where(first_tile, 0, 1)
            else:
                k_prev, v_prev = residue(k_ref, base - span), residue(v_ref, base - span)
                variant = 1
            kk = jnp.concatenate([k_prev, residue(k_ref, base)], axis=0).astype(BF16)
            vv = jnp.concatenate([v_prev, residue(v_ref, base)], axis=0).astype(BF16)
            zero = jnp.zeros_like(qp)
            qq = jnp.concatenate([jnp.where(low, qp, zero), jnp.where(low, zero, qp)], axis=0)
            s = lax.dot_general(qq, kk, (((1,), (1,)), ((), ())), preferred_element_type=F32)
            outs, lses = [], []
            for c in range(2):
                sc = s[c * BQ_SUB:(c + 1) * BQ_SUB] + bias_ref[variant, c]
                m = jnp.max(sc, axis=-1, keepdims=True)
                p = jnp.exp(sc - m)
                l = jnp.sum(p, axis=-1, keepdims=True)
                outs.append(jnp.dot(p.astype(BF16), vv, preferred_element_type=F32) / l)
                lses.append(m + jnp.log(l))
            o_val = jnp.where(low, outs[0], outs[1])
            lse_val = jnp.where(low, lses[0], lses[1])
            if dil == 1:
                o_ref[pl.ds(base, BQ_SUB), :] = o_val
                lse_ref[pl.ds(base, BQ_SUB), :] = lse_val
            else:
                o_ref[pl.ds(base, BQ_SUB, stride=dil), :] = o_val
                lse_ref[pl.ds(base, BQ_SUB, stride=dil), :] = lse_val


def _attn_b_group(q, kv, bias_tiles, g, dil, batch, seq):
    span = BQ_SUB * dil
    n_sub = max(1, min(B_TOKENS, seq) // span)
    tile = n_sub * span
    pairs = N_HEADS // 2
    q3 = q.reshape(batch, seq, N_GROUPS * B_WIDTH)
    kv3 = kv.reshape(batch, seq, 2 * N_GROUPS * B_WIDTH)
    kern = functools.partial(_attn_b_kernel, dil=dil, n_sub=n_sub)
    k_col = lambda hp: g * pairs + hp
    v_col = lambda hp: (N_GROUPS + g) * pairs + hp
    prev = lambda n: jnp.maximum(n * n_sub - 1, 0)
    out_shape = jax.ShapeDtypeStruct((batch, seq, B_WIDTH), F32)
    o, lse = pl.pallas_call(
        kern,
        grid=(batch, seq // tile, pairs),
        in_specs=[pl.BlockSpec((None, tile, LANES), lambda b, n, hp: (b, n, k_col(hp))),
                  pl.BlockSpec((None, tile, LANES), lambda b, n, hp: (b, n, k_col(hp))),
                  pl.BlockSpec((None, tile, LANES), lambda b, n, hp: (b, n, v_col(hp))),
                  pl.BlockSpec((None, span, LANES), lambda b, n, hp: (b, prev(n), k_col(hp))),
                  pl.BlockSpec((None, span, LANES), lambda b, n, hp: (b, prev(n), v_col(hp))),
                  pl.BlockSpec((2, 2, BQ_SUB, 2 * BQ_SUB), lambda b, n, hp: (0, hp, 0, 0))],
        out_specs=[pl.BlockSpec((None, tile, LANES), lambda b, n, hp: (b, n, hp)),
                   pl.BlockSpec((None, tile, LANES), lambda b, n, hp: (b, n, hp))],
        out_shape=[out_shape, out_shape],
        compiler_params=_cparams(("parallel", "parallel", "parallel")),
        name=f"attn_b_g{g}",
    )(q3, kv3, kv3, kv3, kv3, bias_tiles)
    return o.reshape(batch * seq, B_WIDTH), lse.reshape(batch * seq, B_WIDTH)


def _split_bf16(a):
    hi = a.astype(BF16)
    return hi, (a - hi.astype(F32)).astype(BF16)


def _out_tail(o_bf16, wo_ref, x_ref, ga_ref, gf_ref, shf_ref, scf_ref, wr_ref, br_ref, x1_ref, hf_ref, lg_ref):
    y = jnp.dot(o_bf16, wo_ref[...], preferred_element_type=F32)
    x1 = x_ref[...] + ga_ref[...] * y
    x1_ref[...] = x1
    hf = _rms(x1, gf_ref[...]) * (1.0 + scf_ref[...]) + shf_ref[...]
    hf_ref[...] = hf
    hi, lo = _split_bf16(hf)
    both = jnp.dot(hi, wr_ref[...], preferred_element_type=F32)
    lg_ref[...] = (both[:, :ROUTER_COLS] + jnp.dot(lo, wr_ref[:, :ROUTER_COLS], preferred_element_type=F32)
                   + both[:, ROUTER_COLS:] + br_ref[...])


def _out_a_kernel(o_ref, *rest):
    _out_tail(o_ref[...], *rest)


def _out_b_kernel(o0, o1, o2, l0, l1, l2, *rest):
    la, lb, lc = l0[...], l1[...], l2[...]
    m = jnp.maximum(jnp.maximum(la, lb), lc)
    ea, eb, ec = jnp.exp(la - m), jnp.exp(lb - m), jnp.exp(lc - m)
    z = ea + eb + ec
    o = (ea / z) * o0[...] + (eb / z) * o1[...] + (ec / z) * o2[...]
    _out_tail(o.astype(BF16), *rest)


def _out_proj(att_inputs, wo_bf16, x, gate_a, gain_f, shift_f, scale_f, w_router, b_router, seq):
    t, d = x.shape
    tm = min(TOK_TILE, seq)
    per_b = seq // tm
    row = lambda i: (i, 0)
    fixed = lambda i: (0, 0)
    per_batch = lambda i: (i // per_b, 0, 0)
    att_specs = [pl.BlockSpec((tm, a.shape[1]), row) for a in att_inputs]
    kern = _out_a_kernel if len(att_inputs) == 1 else _out_b_kernel
    return pl.pallas_call(
        kern,
        grid=(t // tm,),
        in_specs=att_specs + [pl.BlockSpec(wo_bf16.shape, fixed),
                              pl.BlockSpec((tm, d), row),
                              pl.BlockSpec((None, 1, d), per_batch),
                              pl.BlockSpec((1, d), fixed),
                              pl.BlockSpec((None, 1, d), per_batch),
                              pl.BlockSpec((None, 1, d), per_batch),
                              pl.BlockSpec((d, 2 * ROUTER_COLS), fixed),
                              pl.BlockSpec((1, ROUTER_COLS), fixed)],
        out_specs=[pl.BlockSpec((tm, d), row), pl.BlockSpec((tm, d), row),
                   pl.BlockSpec((tm, ROUTER_COLS), row)],
        out_shape=[jax.ShapeDtypeStruct((t, d), F32), jax.ShapeDtypeStruct((t, d), F32),
                   jax.ShapeDtypeStruct((t, ROUTER_COLS), F32)],
        compiler_params=_cparams(("parallel",)),
        name="out_proj",
    )(*att_inputs, wo_bf16, x, gate_a, gain_f.reshape(1, d), shift_f, scale_f,
      jnp.concatenate(_split_bf16(w_router), axis=1), b_router)


def _route_kernel(lg_ref, tri_ref, eid_ref, gate_ref, seg_ref, cnt_ref, cnt_sc):
    lt = lg_ref[...].T
    best, g_idx = lt[0:1], jnp.zeros((1, lt.shape[1]), jnp.int32)
    for g in range(1, MOE_GROUPS):
        upd = lt[g:g + 1] > best
        best = jnp.where(upd, lt[g:g + 1], best)
        g_idx = jnp.where(upd, g, g_idx)
    denom = jnp.zeros_like(best)
    for g in range(MOE_GROUPS):
        denom = denom + jnp.exp(lt[g:g + 1] - best)
    g_w = 1.0 / denom
    e_sel = lt[EXPERT_COL0:EXPERT_COL0 + MOE_EPG]
    for g in range(1, MOE_GROUPS):
        r0 = EXPERT_COL0 + g * MOE_EPG
        e_sel = jnp.where(g_idx == g, lt[r0:r0 + MOE_EPG], e_sel)

    def first_max(vals):
        v, i = vals[0:1], jnp.zeros((1, vals.shape[1]), jnp.int32)
        for e in range(1, MOE_EPG):
            upd = vals[e:e + 1] > v
            v = jnp.where(upd, vals[e:e + 1], v)
            i = jnp.where(upd, e, i)
        return v, i

    v1, i1 = first_max(e_sel)
    row = lax.broadcasted_iota(jnp.int32, e_sel.shape, 0)
    v2, i2 = first_max(jnp.where(row == i1, NEG_INF, e_sel))
    e2 = jnp.exp(v2 - v1)
    w1 = 1.0 / (1.0 + e2)
    w2 = e2 / (1.0 + e2)
    e1 = g_idx * MOE_EPG + i1
    e2 = g_idx * MOE_EPG + i2

    @pl.when(pl.program_id(0) == 0)
    def _():
        cnt_sc[...] = jnp.zeros(cnt_sc.shape, F32)

    tm = lt.shape[1]
    erows = lax.broadcasted_iota(jnp.int32, (MOE_EXPERTS, tm), 0)
    ohs, pres = [], []
    for e_k in (e1, e2):
        oh = (erows == e_k).astype(F32)
        ohs.append(oh)
        pres.append(jnp.dot(oh.astype(BF16), tri_ref[...], preferred_element_type=F32))
    n0 = pres[0][:, tm - 1:]
    n_tot = n0 + pres[1][:, tm - 1:]
    chunks = jnp.floor((n_tot + (SEG_ALIGN - 1.0)) * (1.0 / SEG_ALIGN))
    er = lax.broadcasted_iota(jnp.int32, (MOE_EXPERTS, MOE_EXPERTS), 0)
    ec = lax.broadcasted_iota(jnp.int32, (MOE_EXPERTS, MOE_EXPERTS), 1)
    before = jnp.dot((ec < er).astype(BF16), jnp.broadcast_to(chunks, (MOE_EXPERTS, LANES)).astype(BF16),
                     preferred_element_type=F32)
    seg_start = before[:, 0:1] * SEG_ALIGN
    slot0 = jnp.sum(ohs[0] * (seg_start + pres[0] - 1.0), axis=0, keepdims=True)
    slot1 = jnp.sum(ohs[1] * (seg_start + n0 + pres[1] - 1.0), axis=0, keepdims=True)
    seg_off = cnt_sc[...]
    cnt_sc[...] = seg_off + chunks * SEG_ALIGN
    cnt_ref[...] = jnp.broadcast_to(cnt_sc[...], cnt_ref.shape).astype(jnp.int32)
    scol = lax.broadcasted_iota(jnp.int32, seg_ref.shape, 1)
    seg_ref[...] = jnp.where(scol == 0, chunks, jnp.where(scol == 1, seg_start, jnp.where(
        scol == 2, seg_off, 0.0))).astype(jnp.int32)

    erow = lax.broadcasted_iota(jnp.int32, eid_ref.shape, 0)
    eid_ref[...] = jnp.where(erow == 0, e1, jnp.where(erow == 1, e2, jnp.where(
        erow == 2, slot0.astype(jnp.int32), jnp.where(erow == 3, slot1.astype(jnp.int32), 0))))
    grow = lax.broadcasted_iota(jnp.int32, lt.shape, 0)
    gates = jnp.where(grow == 0, g_w * w1, jnp.where(grow == 1, g_w * w2, 0.0))
    gate_ref[...] = gates.T


def _route(logits):
    t = logits.shape[0]
    tm = min(MOE_TILE, t)
    n_tiles = t // tm
    r = np.arange(tm)
    tri = jnp.asarray(r[:, None] <= r[None, :], BF16)
    return pl.pallas_call(
        _route_kernel,
        grid=(n_tiles,),
        in_specs=[pl.BlockSpec((tm, ROUTER_COLS), lambda i: (i, 0)),
                  pl.BlockSpec((tm, tm), lambda i: (0, 0))],
        out_specs=[pl.BlockSpec((8, tm), lambda i: (0, i)), pl.BlockSpec((tm, ROUTER_COLS), lambda i: (i, 0)),
                   pl.BlockSpec((MOE_EXPERTS, LANES), lambda i: (i, 0)),
                   pl.BlockSpec((MOE_EXPERTS, LANES), lambda i: (0, 0))],
        out_shape=[jax.ShapeDtypeStruct((8, t), jnp.int32), jax.ShapeDtypeStruct((t, ROUTER_COLS), F32),
                   jax.ShapeDtypeStruct((n_tiles * MOE_EXPERTS, LANES), jnp.int32),
                   jax.ShapeDtypeStruct((MOE_EXPERTS, LANES), jnp.int32)],
        scratch_shapes=[pltpu.VMEM((MOE_EXPERTS, 1), F32)],
        compiler_params=_cparams(("arbitrary",)),
        name="moe_route",
    )(logits, tri)


def _moe_rows(t):
    n_tiles = t // min(MOE_TILE, t)
    rows = 2 * t + n_tiles * MOE_EXPERTS * (SEG_ALIGN - 1) + MOE_EXPERTS * (MOE_BLK - 1)
    return -(-rows // MOE_BLK) * MOE_BLK


def _chunk_rows(first, count, row0, n_slots):
    c = jnp.arange(n_slots, dtype=jnp.int32)
    run = jnp.sum((first + count)[..., None, :] <= c[:, None], axis=-1)
    hit = run[..., None] == jnp.arange(first.shape[-1], dtype=jnp.int32)
    base = jnp.sum(jnp.where(hit, (row0 - SEG_CHUNK * first)[..., None, :], 0), axis=-1)
    return jnp.where(run < first.shape[-1], base + SEG_CHUNK * c, 0)


def _dispatch_plan(route, seg, totals, n_blocks):
    n_tiles = seg.shape[0] // MOE_EXPERTS
    tm = route.shape[1] // n_tiles
    n_slots = _sorted_rows(tm) // SEG_CHUNK
    padded = (totals + MOE_BLK - 1) // MOE_BLK * MOE_BLK
    pad_end = jnp.cumsum(padded)
    pad_start = pad_end - padded
    n_used = pad_end[-1] // MOE_BLK
    seg = seg.reshape(n_tiles, MOE_EXPERTS, LANES)
    chunks, first, seg_off = seg[:, :, 0], seg[:, :, 1] // SEG_CHUNK, seg[:, :, 2]
    dst = _chunk_rows(first, chunks, pad_start[None, :] + seg_off, n_slots)
    fill = (padded - totals) // SEG_CHUNK
    fill_first = jnp.cumsum(fill) - fill
    fill_dst = _chunk_rows(fill_first, fill, pad_start + totals, MOE_EXPERTS * (MOE_BLK // SEG_CHUNK))
    tail = jnp.stack([jnp.broadcast_to(n_used, (n_tiles,)), jnp.broadcast_to(jnp.sum(fill), (n_tiles,)),
                      jnp.sum(chunks, axis=1)], axis=1)
    table = jnp.concatenate([dst, jnp.zeros((n_tiles, TABLE_W - 3 - n_slots), jnp.int32), tail], axis=1)
    slots = route[2:4].reshape(2, n_tiles, tm).transpose(1, 0, 2).reshape(n_tiles, 1, 2 * tm)
    block_start = jnp.arange(n_blocks, dtype=jnp.int32) * MOE_BLK
    block_e = jnp.minimum(jnp.sum(pad_end[None, :] <= block_start[:, None], axis=1), MOE_EXPERTS - 1).astype(jnp.int32)
    return (table.astype(jnp.int32).reshape(n_tiles, 1, TABLE_W), slots, fill_dst.astype(jnp.int32).reshape(1, 1, -1),
            block_e, n_used.astype(jnp.int32).reshape(1))


def _chunk_copies(table_ref, hbm_ref, sorted_ref, sem, to_hbm, wait):
    def chunk(c, carry):
        local = sorted_ref.at[pl.ds(pl.multiple_of(c * SEG_CHUNK, SEG_CHUNK), SEG_CHUNK), :]
        remote = hbm_ref.at[pl.ds(pl.multiple_of(table_ref[0, 0, c], SEG_CHUNK), SEG_CHUNK), :]
        copy = pltpu.make_async_copy(local, remote, sem) if to_hbm else pltpu.make_async_copy(remote, local, sem)
        if wait:
            copy.wait()
        else:
            copy.start()
        return carry

    lax.fori_loop(0, table_ref[0, 0, TABLE_W - 1], chunk, 0)


def _zero_fill(table_ref, fill_ref, xs_ref, zero_sc, sem, n_blocks, wait):
    def finish(copy):
        if wait:
            copy.wait()
        else:
            copy.start()

    def chunk(c, carry):
        row = pl.multiple_of(fill_ref[0, 0, c], SEG_CHUNK)
        finish(pltpu.make_async_copy(zero_sc.at[pl.ds(0, SEG_CHUNK), :], xs_ref.at[pl.ds(row, SEG_CHUNK), :], sem))
        return carry

    def block(b, carry):
        row = pl.multiple_of(b * MOE_BLK, MOE_BLK)
        finish(pltpu.make_async_copy(zero_sc, xs_ref.at[pl.ds(row, MOE_BLK), :], sem))
        return carry

    lax.fori_loop(0, table_ref[0, 0, TABLE_W - 2], chunk, 0)
    lax.fori_loop(table_ref[0, 0, TABLE_W - 3], n_blocks, block, 0)


def _dispatch_kernel(table_ref, slot_ref, fill_ref, h_ref, xs_ref, sorted_sc, zero_sc, sem, zero_sem,
                     *, tm, n_blocks):
    first_step = pl.program_id(0) == 0

    @pl.when(first_step)
    def _():
        sorted_sc[...] = jnp.zeros(sorted_sc.shape, F32)
        zero_sc[...] = jnp.zeros(zero_sc.shape, F32)
        _zero_fill(table_ref, fill_ref, xs_ref, zero_sc, zero_sem, n_blocks, wait=False)

    def place(i, carry):
        for k in range(2):
            sorted_sc[pl.ds(slot_ref[0, 0, k * tm + i], 1), :] = h_ref[pl.ds(i, 1), :]
        return carry

    lax.fori_loop(0, tm, place, 0, unroll=8)
    _chunk_copies(table_ref, xs_ref, sorted_sc, sem, to_hbm=True, wait=False)
    _chunk_copies(table_ref, xs_ref, sorted_sc, sem, to_hbm=True, wait=True)

    @pl.when(first_step)
    def _():
        _zero_fill(table_ref, fill_ref, xs_ref, zero_sc, zero_sem, n_blocks, wait=True)


def _sorted_rows(tm):
    return -(-(2 * tm + MOE_EXPERTS * (SEG_ALIGN - 1)) // SEG_CHUNK) * SEG_CHUNK


def _dispatch(h, table, slots, fill, rows):
    t, d = h.shape
    tm = min(MOE_TILE, t)
    kern = functools.partial(_dispatch_kernel, tm=tm, n_blocks=rows // MOE_BLK)
    return pl.pallas_call(
        kern,
        grid=(t // tm,),
        in_specs=[pl.BlockSpec((1, 1, TABLE_W), lambda i: (i, 0, 0), memory_space=pltpu.SMEM),
                  pl.BlockSpec((1, 1, 2 * tm), lambda i: (i, 0, 0), memory_space=pltpu.SMEM),
                  pl.BlockSpec(fill.shape, lambda i: (0, 0, 0), memory_space=pltpu.SMEM),
                  pl.BlockSpec((tm, d), lambda i: (i, 0))],
        out_specs=pl.BlockSpec(memory_space=pl.ANY),
        out_shape=jax.ShapeDtypeStruct((rows, d), F32),
        scratch_shapes=[pltpu.VMEM((_sorted_rows(tm), d), F32), pltpu.VMEM((MOE_BLK, d), F32),
                        pltpu.SemaphoreType.DMA(()), pltpu.SemaphoreType.DMA(())],
        compiler_params=_cparams(("arbitrary",)),
        name="moe_dispatch",
    )(table, slots, fill, h)


def _expert_kernel(be_ref, nu_ref, xs_ref, wg_ref, wu_ref, wd_ref, ys_ref, wg_sc, wu_sc, wd_sc):
    i = pl.program_id(0)
    live = i < nu_ref[0]

    @pl.when(jnp.logical_not(live))
    def _():
        ys_ref[...] = jnp.zeros(ys_ref.shape, F32)

    @pl.when(live)
    def _():
        @pl.when((i == 0) | (be_ref[i] != be_ref[jnp.maximum(i - 1, 0)]))
        def _():
            wg_sc[...] = wg_ref[...].astype(BF16)
            wu_sc[...] = wu_ref[...].astype(BF16)
            wd_sc[...] = wd_ref[...].astype(BF16)

        x = xs_ref[...].astype(BF16)
        g = jnp.dot(x, wg_sc[...], preferred_element_type=F32)
        u = jnp.dot(x, wu_sc[...], preferred_element_type=F32)
        a = (g * (1.0 / (1.0 + jnp.exp(-g))) * u).astype(BF16)
        ys_ref[...] = jnp.dot(a, wd_sc[...], preferred_element_type=F32)


def _experts(xs, block_e, n_used, w_gate, w_up, w_down, layer):
    rows, d = xs.shape
    n_blocks = rows // MOE_BLK
    hid = w_gate.shape[-1]
    used = lambda i, be, nu: (jnp.minimum(i, nu[0] - 1), 0)
    expert = lambda i, be, nu: (layer, be[jnp.minimum(i, nu[0] - 1)], 0, 0)
    grid_spec = pltpu.PrefetchScalarGridSpec(
        num_scalar_prefetch=2,
        grid=(n_blocks,),
        in_specs=[pl.BlockSpec((MOE_BLK, d), used),
                  pl.BlockSpec((None, None, d, hid), expert),
                  pl.BlockSpec((None, None, d, hid), expert),
                  pl.BlockSpec((None, None, hid, d), expert)],
        out_specs=pl.BlockSpec((MOE_BLK, d), lambda i, be, nu: (i, 0)),
        scratch_shapes=[pltpu.VMEM((d, hid), BF16), pltpu.VMEM((d, hid), BF16), pltpu.VMEM((hid, d), BF16)],
    )
    return pl.pallas_call(
        _expert_kernel,
        grid_spec=grid_spec,
        out_shape=jax.ShapeDtypeStruct((rows, d), F32),
        compiler_params=_cparams(("arbitrary",)),
        name="moe_experts",
    )(block_e, n_used, xs, w_gate, w_up, w_down)


def _combine_kernel(table_ref, slot_ref, ys_ref, x_ref, gf_ref, gate_ref, o_ref, sorted_sc, buf, sem, *, tm):
    _chunk_copies(table_ref, ys_ref, sorted_sc, sem, to_hbm=False, wait=False)
    _chunk_copies(table_ref, ys_ref, sorted_sc, sem, to_hbm=False, wait=True)

    def pick(i, carry):
        for k in range(2):
            buf[k, pl.ds(i, 1), :] = sorted_sc[pl.ds(slot_ref[0, 0, k * tm + i], 1), :]
        return carry

    lax.fori_loop(0, tm, pick, 0, unroll=8)
    gates = gate_ref[...]
    y = gates[:, 0:1] * buf[0] + gates[:, 1:2] * buf[1]
    o_ref[...] = x_ref[...] + gf_ref[...] * y


def _combine(ys, table, slots, x, gate_f, gates, seq):
    t, d = x.shape
    tm = min(MOE_TILE, t)
    per_b = seq // tm if seq >= tm else 1
    kern = functools.partial(_combine_kernel, tm=tm)
    return pl.pallas_call(
        kern,
        grid=(t // tm,),
        in_specs=[pl.BlockSpec((1, 1, TABLE_W), lambda i: (i, 0, 0), memory_space=pltpu.SMEM),
                  pl.BlockSpec((1, 1, 2 * tm), lambda i: (i, 0, 0), memory_space=pltpu.SMEM),
                  pl.BlockSpec(memory_space=pl.ANY),
                  pl.BlockSpec((tm, d), lambda i: (i, 0)),
                  pl.BlockSpec((None, 1, d), lambda i: (i // per_b, 0, 0)),
                  pl.BlockSpec((tm, ROUTER_COLS), lambda i: (i, 0))],
        out_specs=pl.BlockSpec((tm, d), lambda i: (i, 0)),
        out_shape=jax.ShapeDtypeStruct((t, d), F32),
        scratch_shapes=[pltpu.VMEM((_sorted_rows(tm), d), F32), pltpu.VMEM((2, tm, d), F32),
                        pltpu.SemaphoreType.DMA(())],
        compiler_params=_cparams(("arbitrary",)),
        name="moe_combine",
    )(table, slots, ys, x, gate_f, gates)


def _moe(h, logits, x1, gate_f, w_gate, w_up, w_down, layer, seq):
    rows = _moe_rows(h.shape[0])
    route, gates, seg, totals = _route(logits)
    table, slots, fill, block_e, n_used = _dispatch_plan(route, seg, totals[:, 0], rows // MOE_BLK)
    xs = _dispatch(h, table, slots, fill, rows)
    ys = _experts(xs, block_e, n_used, w_gate, w_up, w_down, layer)
    return _combine(ys, table, slots, x1, gate_f, gates, seq)


def _router_weights(w_group, b_group, w_expert, b_expert):
    d = w_group.shape[0]
    w = jnp.zeros((d, ROUTER_COLS), F32)
    w = w.at[:, 0:MOE_GROUPS].set(w_group.astype(F32))
    w = w.at[:, EXPERT_COL0:EXPERT_COL0 + MOE_EXPERTS].set(w_expert.astype(F32))
    b = jnp.zeros((1, ROUTER_COLS), F32)
    b = b.at[0, 0:MOE_GROUPS].set(b_group.astype(F32))
    b = b.at[0, EXPERT_COL0:EXPERT_COL0 + MOE_EXPERTS].set(b_expert.astype(F32))
    return w, b


def _lambda_init(layer):
    return 0.8 - 0.6 * math.exp(-0.3 * layer)


def kernel(x, c, rel_bias, ada_w, ada_b, norm_attn, norm_ffn, a_w_qkv, a_q_norm, a_k_norm, a_lambda, a_subln, a_w_o, kv_norm, kv_ada_w, kv_ada_b, kv_w, kv_k_norm, b_w_q, b_q_norm, b_w_o, moe_w_group, moe_b_group, moe_w_expert, moe_b_expert, moe_w_gate, moe_w_up, moe_w_down):
    batch, seq, d = x.shape
    t = batch * seq
    scale = HEAD_DIM ** -0.5
    n_bw = N_GROUPS * B_WIDTH

    c_pad = jnp.zeros((8, d), F32).at[:batch].set(c.astype(F32))
    mod = _modulation(c_pad, ada_w, ada_b)[:, :batch]
    kv_mod = _modulation(c_pad, kv_ada_w[None], kv_ada_b[None])[0, :batch]

    def part(m, i):
        return m[:, i * d:(i + 1) * d].reshape(batch, 1, d)

    a_bias, nd = _attn_a_bias_diags(rel_bias, seq, min(ATT_TILE, seq))
    b_bias = [_attn_b_bias_tiles(rel_bias, window // dil, dil) for window, dil in B_GROUPS]
    xf = x.reshape(t, d).astype(F32)
    kv = None
    for layer in range(DEPTH):
        m = mod[layer]
        sh_a, sc_a, g_a, sh_f, sc_f, g_f = (part(m, i) for i in range(6))
        w_r, b_r = _router_weights(moe_w_group[layer], moe_b_group[layer], moe_w_expert[layer], moe_b_expert[layer])
        if layer < N_A_LAYERS:
            qk_gain = jnp.concatenate([jnp.tile(a_q_norm[layer].astype(F32) * (scale * LOG2_E), 2 * N_HEADS),
                                       jnp.tile(a_k_norm[layer].astype(F32), 2 * N_HEADS),
                                       jnp.ones((N_HEADS * LANES,), F32)])
            qkv = _norm_proj(xf, norm_attn[layer], sh_a, sc_a, a_w_qkv[layer].astype(BF16), qk_gain,
                             2 * N_HEADS * LANES, seq, BF16)
            o = _attn_a(qkv, a_bias, nd, a_lambda[layer].astype(F32), a_subln[layer].astype(F32),
                        _lambda_init(layer), batch, seq)
            att = [o]
            wo = a_w_o[layer].astype(BF16)
        else:
            j = layer - N_A_LAYERS
            if kv is None:
                k_gain = jnp.concatenate([jnp.tile(kv_k_norm.astype(F32), (1, N_HEADS)).reshape(-1),
                                          jnp.ones((n_bw,), F32)])
                kv = _norm_proj(xf, kv_norm, part(kv_mod, 0), part(kv_mod, 1), kv_w.astype(BF16), k_gain, n_bw, seq, F32)
            q_gain = jnp.tile(b_q_norm[j].astype(F32) * scale, (1, N_HEADS)).reshape(-1)
            q = _norm_proj(xf, norm_attn[layer], sh_a, sc_a, b_w_q[j].astype(BF16), q_gain, n_bw, seq, F32)
            outs, lses = [], []
            for g, (window, dil) in enumerate(B_GROUPS):
                o_g, lse_g = _attn_b_group(q, kv, b_bias[g], g, dil, batch, seq)
                outs.append(o_g)
                lses.append(lse_g)
            att = outs + lses
            wo = b_w_o[j].astype(BF16)
        x1, hf, logits = _out_proj(att, wo, xf, g_a, norm_ffn[layer], sh_f, sc_f, w_r, b_r, seq)
        xf = _moe(hf, logits, x1, g_f, moe_w_gate, moe_w_up, moe_w_down, layer, seq)
    return xf.reshape(batch, seq, d).astype(x.dtype)
```

```python
import functools
import math

import jax
import jax.numpy as jnp
import numpy as np
from jax import lax
from jax.experimental import pallas as pl
from jax.experimental.pallas import tpu as pltpu

F32 = jnp.float32
BF16 = jnp.bfloat16

D_MODEL = 1024
DEPTH = 4
N_A_LAYERS = 2
N_HEADS = 8
HEAD_DIM = 64
B_GROUPS = ((128, 1), (512, 4), (2048, 16))
N_GROUPS = len(B_GROUPS)
B_WIDTH = N_HEADS * HEAD_DIM
REL_BUCKETS = 32
REL_MAX_EXACT = 16
REL_MAX_DIST = 2048
MOE_GROUPS = 4
MOE_EPG = 8
MOE_EXPERTS = MOE_GROUPS * MOE_EPG
MOE_HIDDEN = 512
EPS = 1e-6
NEG_INF = float("-inf")
LOG2_E = math.log2(math.e)

LANES = 128
MXU_DIM = 256
ROUTER_COLS = 128
EXPERT_COL0 = 8

TOK_TILE = 512
ATT_TILE = 512
B_TOKENS = 2048
BQ_SUB = 128
MOE_BLK = 256
MOE_TILE = 512
SEG_ALIGN = 8
SEG_CHUNK = 8
TABLE_W = 256
VMEM_LIMIT = 56 * 1024 * 1024


def _cparams(sem):
    return pltpu.CompilerParams(dimension_semantics=sem, vmem_limit_bytes=VMEM_LIMIT)


def _bucket_table(max_dist):
    n = np.arange(max_dist + 1)
    nf = np.maximum(n, 1).astype(np.float64)
    large = REL_MAX_EXACT + (np.log(nf / REL_MAX_EXACT) / math.log(REL_MAX_DIST / REL_MAX_EXACT)
                             * (REL_BUCKETS - REL_MAX_EXACT)).astype(np.int64)
    large = np.minimum(large, REL_BUCKETS - 1)
    return np.where(n < REL_MAX_EXACT, n, large).astype(np.int32)


def _toeplitz(v, n, m):
    length = v.shape[-1]
    assert length >= n + m - 1 and length - 1 >= m
    lead = v.shape[:-1]
    flat = jnp.tile(v, (1,) * len(lead) + (n,))[..., :n * (length - 1)]
    return flat.reshape(lead + (n, length - 1))[..., :m]


def _diag_values(rel_bias, dist, ok):
    table = _bucket_table(int(dist.max()))
    vals = rel_bias.astype(F32)[table[np.clip(dist, 0, None)]]
    vals = jnp.where(ok[..., None], vals, NEG_INF)
    return jnp.moveaxis(vals, -1, 0)


def _attn_a_bias_diags(rel_bias, seq, tile):
    table = _bucket_table(seq)
    last_start = int(np.argmax(table == REL_BUCKETS - 1))
    nd = 0
    while nd * tile - (tile - 1) < last_start and nd * tile < seq:
        nd += 1
    nd += 1
    length = 2 * tile
    u = np.arange(length)
    u = np.where(u < tile, u, u - length)
    dist = np.arange(nd)[:, None] * tile - u[None, :]
    return _diag_values(rel_bias, dist, dist >= 0) * LOG2_E, nd


def _attn_b_bias_tiles(rel_bias, band, dil):
    length = 3 * BQ_SUB
    u = np.arange(length)
    u = np.where(u < 2 * BQ_SUB, u, u - length)
    rel = BQ_SUB - u
    ok = (rel >= 0) & (rel <= band)
    general = _toeplitz(_diag_values(rel_bias, dil * rel, ok) * LOG2_E, BQ_SUB, 2 * BQ_SUB)
    no_prev = np.arange(2 * BQ_SUB)[None, None, :] >= BQ_SUB
    return jnp.stack([jnp.where(no_prev, general, NEG_INF), general])


def _group_sum_matrix():
    r = np.arange(MXU_DIM)
    return jnp.asarray((r[:, None] // HEAD_DIM) == (r[None, :] // HEAD_DIM), BF16)


def _mod_kernel(c_ref, w_ref, b_ref, o_ref):
    c = c_ref[...]
    c_act = c * (1.0 / (1.0 + jnp.exp(-c)))
    o_ref[...] = jnp.dot(c_act, w_ref[...], preferred_element_type=F32,
                         precision=lax.Precision.HIGHEST) + b_ref[...]


def _modulation(c_pad, w, b, tn=1024):
    nl, d, n = w.shape
    return pl.pallas_call(
        _mod_kernel,
        grid=(nl, n // tn),
        in_specs=[pl.BlockSpec((8, d), lambda l, j: (0, 0)),
                  pl.BlockSpec((None, d, tn), lambda l, j: (l, 0, j)),
                  pl.BlockSpec((None, 1, tn), lambda l, j: (l, 0, j))],
        out_specs=pl.BlockSpec((None, 8, tn), lambda l, j: (l, 0, j)),
        out_shape=jax.ShapeDtypeStruct((nl, 8, n), F32),
        compiler_params=_cparams(("parallel", "parallel")),
        name="adaln_mod",
    )(c_pad, w, b.reshape(nl, 1, n))


def _rms(x, gain):
    return x * lax.rsqrt(jnp.mean(x * x, axis=-1, keepdims=True) + EPS) * gain


def _proj_kernel(x_ref, g_ref, sh_ref, sc_ref, w_ref, hg_ref, bd_ref, o_ref, *, n_out, n_norm):
    h = _rms(x_ref[...], g_ref[...]) * (1.0 + sc_ref[...]) + sh_ref[...]
    hb = h.astype(BF16)
    wide = 2 * MXU_DIM
    for c0 in range(0, n_out, wide):
        a = jnp.dot(hb, w_ref[:, c0:c0 + wide], preferred_element_type=F32)
        if c0 < n_norm:
            sq = (a * a).astype(BF16)
            ms = jnp.concatenate([jnp.dot(sq[:, :MXU_DIM], bd_ref[...], preferred_element_type=F32),
                                  jnp.dot(sq[:, MXU_DIM:], bd_ref[...], preferred_element_type=F32)], axis=1)
            a = a * lax.rsqrt(ms * (1.0 / HEAD_DIM) + EPS) * hg_ref[:, c0:c0 + wide]
        o_ref[:, c0:c0 + wide] = a.astype(o_ref.dtype)


def _norm_proj(x, gain, shift, scale, w_bf16, head_gain, n_norm, seq, out_dtype):
    t, d = x.shape
    n_out = w_bf16.shape[1]
    tm = min(TOK_TILE, seq)
    per_b = seq // tm
    kern = functools.partial(_proj_kernel, n_out=n_out, n_norm=n_norm)
    return pl.pallas_call(
        kern,
        grid=(t // tm,),
        in_specs=[pl.BlockSpec((tm, d), lambda i: (i, 0)),
                  pl.BlockSpec((1, d), lambda i: (0, 0)),
                  pl.BlockSpec((None, 1, d), lambda i: (i // per_b, 0, 0)),
                  pl.BlockSpec((None, 1, d), lambda i: (i // per_b, 0, 0)),
                  pl.BlockSpec((d, n_out), lambda i: (0, 0)),
                  pl.BlockSpec((1, n_out), lambda i: (0, 0)),
                  pl.BlockSpec((MXU_DIM, MXU_DIM), lambda i: (0, 0))],
        out_specs=pl.BlockSpec((tm, n_out), lambda i: (i, 0)),
        out_shape=jax.ShapeDtypeStruct((t, n_out), out_dtype),
        compiler_params=_cparams(("parallel",)),
        name="norm_proj",
    )(x, gain.reshape(1, d), shift, scale, w_bf16, head_gain.reshape(1, n_out), _group_sum_matrix())


def _lane_tile(a, reps):
    return jnp.concatenate([a] * reps, axis=1)


def _attn_a_kernel(q_ref, k_ref, v_ref, diag_ref, lam_ref, g_ref, o_ref, bias_ref, m_sc, acc_sc,
                   *, tile, nd, lam_init):
    qi = pl.program_id(2)

    @pl.when((pl.program_id(1) == 0) & (qi == 0))
    def _():
        for d in range(nd):
            row = jnp.broadcast_to(diag_ref[d:d + 1, :], (tile, 2 * tile))
            bias_ref[d] = pltpu.roll(row, 0, 1, stride=1, stride_axis=0)[:, :tile]

    q = q_ref[...]
    lane = lax.broadcasted_iota(jnp.int32, q.shape, 1)
    zero = jnp.zeros_like(q)
    qq = jnp.concatenate([jnp.where(lane < HEAD_DIM, q, zero),
                          jnp.where(lane >= HEAD_DIM, q, zero)], axis=0)
    m_sc[...] = jnp.full(m_sc.shape, NEG_INF, F32)
    acc_sc[...] = jnp.zeros(acc_sc.shape, F32)
    ones = jnp.ones((tile, LANES), BF16)

    def kv_tile(j):
        start = pl.multiple_of(j * tile, tile)
        k = k_ref[pl.ds(start, tile), :]
        v_ones = jnp.concatenate([v_ref[pl.ds(start, tile), :], ones], axis=1)
        s = lax.dot_general(qq, k, (((1,), (1,)), ((), ())), preferred_element_type=F32)
        bias = bias_ref[jnp.minimum(qi - j, nd - 1)]
        for c in range(2):
            sc = s[c * tile:(c + 1) * tile] + bias
            m_prev = m_sc[c]
            m_new = jnp.maximum(m_prev, jnp.max(sc, axis=-1, keepdims=True))
            alpha = jnp.exp2(m_prev - m_new)
            p = jnp.exp2(sc - _lane_tile(m_new, tile // LANES))
            acc_sc[c] = (_lane_tile(alpha, 2) * acc_sc[c]
                         + jnp.dot(p.astype(BF16), v_ones, preferred_element_type=F32))
            m_sc[c] = m_new

    def kv_pair(i, carry):
        kv_tile(2 * i)
        kv_tile(2 * i + 1)
        return carry

    n_kv = qi + 1
    lax.fori_loop(0, n_kv // 2, kv_pair, 0)

    @pl.when(n_kv % 2 == 1)
    def _():
        kv_tile(qi)

    lp = lam_ref[...]
    lam = (jnp.exp(jnp.sum(lp[0:1] * lp[1:2], axis=-1, keepdims=True))
           - jnp.exp(jnp.sum(lp[2:3] * lp[3:4], axis=-1, keepdims=True)) + lam_init)
    a0, a1 = acc_sc[0], acc_sc[1]
    o = a0[:, :LANES] / a0[:, LANES:] - lam * (a1[:, :LANES] / a1[:, LANES:])
    o_ref[...] = (_rms(o, g_ref[...]) * (1.0 - lam_init)).astype(BF16)


def _attn_a(qkv, bias_diags, nd, lam_p, subln, lam_init, batch, seq):
    tile = min(ATT_TILE, seq)
    nq = seq // tile
    qkv3 = qkv.reshape(batch, seq, 3 * N_HEADS * LANES)
    kern = functools.partial(_attn_a_kernel, tile=tile, nd=nd, lam_init=lam_init)
    out = pl.pallas_call(
        kern,
        grid=(N_HEADS, batch, nq),
        in_specs=[pl.BlockSpec((None, tile, LANES), lambda h, b, i: (b, i, h)),
                  pl.BlockSpec((None, seq, LANES), lambda h, b, i: (b, 0, N_HEADS + h)),
                  pl.BlockSpec((None, seq, LANES), lambda h, b, i: (b, 0, 2 * N_HEADS + h)),
                  pl.BlockSpec((None, nd, 2 * tile), lambda h, b, i: (h, 0, 0)),
                  pl.BlockSpec((4, HEAD_DIM), lambda h, b, i: (0, 0)),
                  pl.BlockSpec((1, LANES), lambda h, b, i: (0, 0))],
        out_specs=pl.BlockSpec((None, tile, LANES), lambda h, b, i: (b, i, h)),
        out_shape=jax.ShapeDtypeStruct((batch, seq, N_HEADS * LANES), BF16),
        scratch_shapes=[pltpu.VMEM((nd, tile, tile), F32), pltpu.VMEM((2, tile, LANES), F32),
                        pltpu.VMEM((2, tile, 2 * LANES), F32)],
        compiler_params=_cparams(("arbitrary", "arbitrary", "arbitrary")),
        name="attn_a",
    )(qkv3, qkv3, qkv3, bias_diags, lam_p, subln.reshape(1, LANES))
    return out.reshape(batch * seq, N_HEADS * LANES)


def _attn_b_kernel(*refs, tile):
    ins, o_ref, scratch = refs[:6 * N_GROUPS], refs[6 * N_GROUPS], refs[6 * N_GROUPS + 1:]
    first_tile = pl.program_id(1) == 0
    lane = lax.broadcasted_iota(jnp.int32, (BQ_SUB, LANES), 1)
    low = lane < HEAD_DIM
    ones = jnp.ones((2 * BQ_SUB, LANES), BF16)

    for g, (_, dil) in enumerate(B_GROUPS):
        q_ref, k_ref, v_ref, kp_ref, vp_ref, bias_ref = ins[6 * g:6 * g + 6]
        og_ref, lg_ref = scratch[2 * g:2 * g + 2]
        span = BQ_SUB * dil

        def residue(base, dil=dil):
            return pl.ds(base, BQ_SUB) if dil == 1 else pl.ds(base, BQ_SUB, stride=dil)

        for sb in range(tile // span):
            for r in range(dil):
                base = sb * span + r
                rows = residue(base)
                qp = q_ref[rows, :].astype(BF16)
                if sb == 0:
                    k_prev, v_prev = kp_ref[residue(r), :], vp_ref[residue(r), :]
                    variant = jnp.where(first_tile, 0, 1)
                else:
                    k_prev, v_prev = k_ref[residue(base - span), :], v_ref[residue(base - span), :]
                    variant = 1
                kk = jnp.concatenate([k_prev, k_ref[rows, :]], axis=0).astype(BF16)
                vv = jnp.concatenate([v_prev, v_ref[rows, :]], axis=0).astype(BF16)
                v_ones = jnp.concatenate([vv, ones], axis=1)
                zero = jnp.zeros_like(qp)
                qq = jnp.concatenate([jnp.where(low, qp, zero), jnp.where(low, zero, qp)], axis=0)
                s = lax.dot_general(qq, kk, (((1,), (1,)), ((), ())), preferred_element_type=F32)
                outs, lses = [], []
                for c in range(2):
                    sc = s[c * BQ_SUB:(c + 1) * BQ_SUB] + bias_ref[variant, c]
                    m = jnp.max(sc, axis=-1, keepdims=True)
                    p = jnp.exp2(sc - m)
                    acc = jnp.dot(p.astype(BF16), v_ones, preferred_element_type=F32)
                    outs.append(acc[:, :LANES] / acc[:, LANES:])
                    lses.append(m + jnp.log2(acc[:, LANES:]))
                og_ref[rows, :] = jnp.where(low, outs[0], outs[1])
                lg_ref[rows, :] = jnp.where(low, lses[0], lses[1])

    l0, l1, l2 = scratch[1][...], scratch[3][...], scratch[5][...]
    m = jnp.maximum(jnp.maximum(l0, l1), l2)
    e0, e1, e2 = jnp.exp2(l0 - m), jnp.exp2(l1 - m), jnp.exp2(l2 - m)
    z = e0 + e1 + e2
    o_ref[...] = ((e0 / z) * scratch[0][...] + (e1 / z) * scratch[2][...] + (e2 / z) * scratch[4][...]).astype(BF16)


def _attn_b(q, kv, bias_tiles, batch, seq):
    tile = min(B_TOKENS, seq)
    pairs = N_HEADS // 2
    q3 = q.reshape(batch, seq, N_GROUPS * B_WIDTH)
    kv3 = kv.reshape(batch, seq, 2 * N_GROUPS * B_WIDTH)
    operands, in_specs = [], []
    for g, (_, dil) in enumerate(B_GROUPS):
        span = BQ_SUB * dil
        assert tile % span == 0
        k_col = lambda hp, g=g: g * pairs + hp
        v_col = lambda hp, g=g: (N_GROUPS + g) * pairs + hp
        prev = lambda n, per=tile // span: jnp.maximum(n * per - 1, 0)
        operands += [q3, kv3, kv3, kv3, kv3, bias_tiles[g]]
        in_specs += [pl.BlockSpec((None, tile, LANES), lambda b, n, hp, c=k_col: (b, n, c(hp))),
                     pl.BlockSpec((None, tile, LANES), lambda b, n, hp, c=k_col: (b, n, c(hp))),
                     pl.BlockSpec((None, tile, LANES), lambda b, n, hp, c=v_col: (b, n, c(hp))),
                     pl.BlockSpec((None, span, LANES), lambda b, n, hp, c=k_col, p=prev: (b, p(n), c(hp))),
                     pl.BlockSpec((None, span, LANES), lambda b, n, hp, c=v_col, p=prev: (b, p(n), c(hp))),
                     pl.BlockSpec((2, 2, BQ_SUB, 2 * BQ_SUB), lambda b, n, hp: (0, hp, 0, 0))]
    out = pl.pallas_call(
        functools.partial(_attn_b_kernel, tile=tile),
        grid=(batch, seq // tile, pairs),
        in_specs=in_specs,
        out_specs=pl.BlockSpec((None, tile, LANES), lambda b, n, hp: (b, n, hp)),
        out_shape=jax.ShapeDtypeStruct((batch, seq, B_WIDTH), BF16),
        scratch_shapes=[pltpu.VMEM((tile, LANES), F32)] * (2 * N_GROUPS),
        compiler_params=_cparams(("parallel", "parallel", "parallel")),
        name="attn_b",
    )(*operands)
    return out.reshape(batch * seq, B_WIDTH)


def _split_bf16(a):
    hi = a.astype(BF16)
    return hi, (a - hi.astype(F32)).astype(BF16)


def _out_tail(o_bf16, wo_ref, x_ref, ga_ref, gf_ref, shf_ref, scf_ref, wr_ref, br_ref, x1_ref, hf_ref, lg_ref):
    y = jnp.dot(o_bf16, wo_ref[...], preferred_element_type=F32)
    x1 = x_ref[...] + ga_ref[...] * y
    x1_ref[...] = x1
    hf = _rms(x1, gf_ref[...]) * (1.0 + scf_ref[...]) + shf_ref[...]
    hf_ref[...] = hf
    hi, lo = _split_bf16(hf)
    both = jnp.dot(hi, wr_ref[...], preferred_element_type=F32)
    lg_ref[...] = (both[:, :ROUTER_COLS] + jnp.dot(lo, wr_ref[:, :ROUTER_COLS], preferred_element_type=F32)
                   + both[:, ROUTER_COLS:] + br_ref[...])


def _out_kernel(o_ref, *rest):
    _out_tail(o_ref[...], *rest)


def _out_proj(att, wo_bf16, x, gate_a, gain_f, shift_f, scale_f, w_router, b_router, seq):
    t, d = x.shape
    tm = min(TOK_TILE, seq)
    per_b = seq // tm
    row = lambda i: (i, 0)
    fixed = lambda i: (0, 0)
    per_batch = lambda i: (i // per_b, 0, 0)
    return pl.pallas_call(
        _out_kernel,
        grid=(t // tm,),
        in_specs=[pl.BlockSpec((tm, att.shape[1]), row),
                  pl.BlockSpec(wo_bf16.shape, fixed),
                  pl.BlockSpec((tm, d), row),
                  pl.BlockSpec((None, 1, d), per_batch),
                  pl.BlockSpec((1, d), fixed),
                  pl.BlockSpec((None, 1, d), per_batch),
                  pl.BlockSpec((None, 1, d), per_batch),
                  pl.BlockSpec((d, 2 * ROUTER_COLS), fixed),
                  pl.BlockSpec((1, ROUTER_COLS), fixed)],
        out_specs=[pl.BlockSpec((tm, d), row), pl.BlockSpec((tm, d), row),
                   pl.BlockSpec((tm, ROUTER_COLS), row)],
        out_shape=[jax.ShapeDtypeStruct((t, d), F32), jax.ShapeDtypeStruct((t, d), F32),
                   jax.ShapeDtypeStruct((t, ROUTER_COLS), F32)],
        compiler_params=_cparams(("parallel",)),
        name="out_proj",
    )(att, wo_bf16, x, gate_a, gain_f.reshape(1, d), shift_f, scale_f,
      jnp.concatenate(_split_bf16(w_router), axis=1), b_router)


def _route_kernel(lg_ref, tri_ref, eid_ref, gate_ref, seg_ref, cnt_ref, cnt_sc):
    lt = lg_ref[...].T
    best, g_idx = lt[0:1], jnp.zeros((1, lt.shape[1]), jnp.int32)
    for g in range(1, MOE_GROUPS):
        upd = lt[g:g + 1] > best
        best = jnp.where(upd, lt[g:g + 1], best)
        g_idx = jnp.where(upd, g, g_idx)
    denom = jnp.zeros_like(best)
    for g in range(MOE_GROUPS):
        denom = denom + jnp.exp(lt[g:g + 1] - best)
    g_w = 1.0 / denom
    e_sel = lt[EXPERT_COL0:EXPERT_COL0 + MOE_EPG]
    for g in range(1, MOE_GROUPS):
        r0 = EXPERT_COL0 + g * MOE_EPG
        e_sel = jnp.where(g_idx == g, lt[r0:r0 + MOE_EPG], e_sel)

    def first_max(vals):
        v, i = vals[0:1], jnp.zeros((1, vals.shape[1]), jnp.int32)
        for e in range(1, MOE_EPG):
            upd = vals[e:e + 1] > v
            v = jnp.where(upd, vals[e:e + 1], v)
            i = jnp.where(upd, e, i)
        return v, i

    v1, i1 = first_max(e_sel)
    row = lax.broadcasted_iota(jnp.int32, e_sel.shape, 0)
    v2, i2 = first_max(jnp.where(row == i1, NEG_INF, e_sel))
    e2 = jnp.exp(v2 - v1)
    w1 = 1.0 / (1.0 + e2)
    w2 = e2 / (1.0 + e2)
    e1 = g_idx * MOE_EPG + i1
    e2 = g_idx * MOE_EPG + i2

    @pl.when(pl.program_id(0) == 0)
    def _():
        cnt_sc[...] = jnp.zeros(cnt_sc.shape, F32)

    tm = lt.shape[1]
    erows = lax.broadcasted_iota(jnp.int32, (MOE_EXPERTS, tm), 0)
    ohs, pres = [], []
    for e_k in (e1, e2):
        oh = (erows == e_k).astype(F32)
        ohs.append(oh)
        pres.append(jnp.dot(oh.astype(BF16), tri_ref[...], preferred_element_type=F32))
    n0 = pres[0][:, tm - 1:]
    n_tot = n0 + pres[1][:, tm - 1:]
    chunks = jnp.floor((n_tot + (SEG_ALIGN - 1.0)) * (1.0 / SEG_ALIGN))
    er = lax.broadcasted_iota(jnp.int32, (MOE_EXPERTS, MOE_EXPERTS), 0)
    ec = lax.broadcasted_iota(jnp.int32, (MOE_EXPERTS, MOE_EXPERTS), 1)
    before = jnp.dot((ec < er).astype(BF16), jnp.broadcast_to(chunks, (MOE_EXPERTS, LANES)).astype(BF16),
                     preferred_element_type=F32)
    seg_start = before[:, 0:1] * SEG_ALIGN
    slot0 = jnp.sum(ohs[0] * (seg_start + pres[0] - 1.0), axis=0, keepdims=True)
    slot1 = jnp.sum(ohs[1] * (seg_start + n0 + pres[1] - 1.0), axis=0, keepdims=True)
    seg_off = cnt_sc[...]
    cnt_sc[...] = seg_off + chunks * SEG_ALIGN
    cnt_ref[...] = jnp.broadcast_to(cnt_sc[...], cnt_ref.shape).astype(jnp.int32)
    scol = lax.broadcasted_iota(jnp.int32, seg_ref.shape, 1)
    seg_ref[...] = jnp.where(scol == 0, chunks, jnp.where(scol == 1, seg_start, jnp.where(
        scol == 2, seg_off, 0.0))).astype(jnp.int32)

    erow = lax.broadcasted_iota(jnp.int32, eid_ref.shape, 0)
    eid_ref[...] = jnp.where(erow == 0, e1, jnp.where(erow == 1, e2, jnp.where(
        erow == 2, slot0.astype(jnp.int32), jnp.where(erow == 3, slot1.astype(jnp.int32), 0))))
    grow = lax.broadcasted_iota(jnp.int32, lt.shape, 0)
    gates = jnp.where(grow == 0, g_w * w1, jnp.where(grow == 1, g_w * w2, 0.0))
    gate_ref[...] = gates.T


def _route(logits):
    t = logits.shape[0]
    tm = min(MOE_TILE, t)
    n_tiles = t // tm
    r = np.arange(tm)
    tri = jnp.asarray(r[:, None] <= r[None, :], BF16)
    return pl.pallas_call(
        _route_kernel,
        grid=(n_tiles,),
        in_specs=[pl.BlockSpec((tm, ROUTER_COLS), lambda i: (i, 0)),
                  pl.BlockSpec((tm, tm), lambda i: (0, 0))],
        out_specs=[pl.BlockSpec((8, tm), lambda i: (0, i)), pl.BlockSpec((tm, ROUTER_COLS), lambda i: (i, 0)),
                   pl.BlockSpec((MOE_EXPERTS, LANES), lambda i: (i, 0)),
                   pl.BlockSpec((MOE_EXPERTS, LANES), lambda i: (0, 0))],
        out_shape=[jax.ShapeDtypeStruct((8, t), jnp.int32), jax.ShapeDtypeStruct((t, ROUTER_COLS), F32),
                   jax.ShapeDtypeStruct((n_tiles * MOE_EXPERTS, LANES), jnp.int32),
                   jax.ShapeDtypeStruct((MOE_EXPERTS, LANES), jnp.int32)],
        scratch_shapes=[pltpu.VMEM((MOE_EXPERTS, 1), F32)],
        compiler_params=_cparams(("arbitrary",)),
        name="moe_route",
    )(logits, tri)


def _moe_rows(t):
    n_tiles = t // min(MOE_TILE, t)
    rows = 2 * t + n_tiles * MOE_EXPERTS * (SEG_ALIGN - 1) + MOE_EXPERTS * (MOE_BLK - 1)
    return -(-rows // MOE_BLK) * MOE_BLK


def _chunk_rows(first, count, row0, n_slots):
    c = jnp.arange(n_slots, dtype=jnp.int32)
    run = jnp.sum((first + count)[..., None, :] <= c[:, None], axis=-1)
    hit = run[..., None] == jnp.arange(first.shape[-1], dtype=jnp.int32)
    base = jnp.sum(jnp.where(hit, (row0 - SEG_CHUNK * first)[..., None, :], 0), axis=-1)
    return jnp.where(run < first.shape[-1], base + SEG_CHUNK * c, 0)


def _dispatch_plan(route, seg, totals, n_blocks):
    n_tiles = seg.shape[0] // MOE_EXPERTS
    tm = route.shape[1] // n_tiles
    n_slots = _sorted_rows(tm) // SEG_CHUNK
    padded = (totals + MOE_BLK - 1) // MOE_BLK * MOE_BLK
    pad_end = jnp.cumsum(padded)
    pad_start = pad_end - padded
    n_used = pad_end[-1] // MOE_BLK
    seg = seg.reshape(n_tiles, MOE_EXPERTS, LANES)
    chunks, first, seg_off = seg[:, :, 0], seg[:, :, 1] // SEG_CHUNK, seg[:, :, 2]
    dst = _chunk_rows(first, chunks, pad_start[None, :] + seg_off, n_slots)
    fill = (padded - totals) // SEG_CHUNK
    fill_first = jnp.cumsum(fill) - fill
    fill_dst = _chunk_rows(fill_first, fill, pad_start + totals, MOE_EXPERTS * (MOE_BLK // SEG_CHUNK))
    tail = jnp.stack([jnp.broadcast_to(n_used, (n_tiles,)), jnp.broadcast_to(jnp.sum(fill), (n_tiles,)),
                      jnp.sum(chunks, axis=1)], axis=1)
    table = jnp.concatenate([dst, jnp.zeros((n_tiles, TABLE_W - 3 - n_slots), jnp.int32), tail], axis=1)
    slots = route[2:4].reshape(2, n_tiles, tm).transpose(1, 0, 2).reshape(n_tiles, 1, 2 * tm)
    block_start = jnp.arange(n_blocks, dtype=jnp.int32) * MOE_BLK
    block_e = jnp.minimum(jnp.sum(pad_end[None, :] <= block_start[:, None], axis=1), MOE_EXPERTS - 1).astype(jnp.int32)
    return (table.astype(jnp.int32).reshape(n_tiles, 1, TABLE_W), slots, fill_dst.astype(jnp.int32).reshape(1, 1, -1),
            block_e, n_used.astype(jnp.int32).reshape(1))


def _chunk_copies(table_ref, hbm_ref, sorted_ref, sem, to_hbm, wait):
    def chunk(c, carry):
        local = sorted_ref.at[pl.ds(pl.multiple_of(c * SEG_CHUNK, SEG_CHUNK), SEG_CHUNK), :]
        remote = hbm_ref.at[pl.ds(pl.multiple_of(table_ref[0, 0, c], SEG_CHUNK), SEG_CHUNK), :]
        copy = pltpu.make_async_copy(local, remote, sem) if to_hbm else pltpu.make_async_copy(remote, local, sem)
        if wait:
            copy.wait()
        else:
            copy.start()
        return carry

    lax.fori_loop(0, table_ref[0, 0, TABLE_W - 1], chunk, 0)


def _zero_fill(table_ref, fill_ref, xs_ref, zero_sc, sem, n_blocks, wait):
    def finish(copy):
        if wait:
            copy.wait()
        else:
            copy.start()

    def chunk(c, carry):
        row = pl.multiple_of(fill_ref[0, 0, c], SEG_CHUNK)
        finish(pltpu.make_async_copy(zero_sc.at[pl.ds(0, SEG_CHUNK), :], xs_ref.at[pl.ds(row, SEG_CHUNK), :], sem))
        return carry

    def block(b, carry):
        row = pl.multiple_of(b * MOE_BLK, MOE_BLK)
        finish(pltpu.make_async_copy(zero_sc, xs_ref.at[pl.ds(row, MOE_BLK), :], sem))
        return carry

    lax.fori_loop(0, table_ref[0, 0, TABLE_W - 2], chunk, 0)
    lax.fori_loop(table_ref[0, 0, TABLE_W - 3], n_blocks, block, 0)


def _dispatch_kernel(table_ref, slot_ref, fill_ref, h_ref, xs_ref, sorted_sc, zero_sc, sem, zero_sem,
                     *, tm, n_blocks):
    first_step = pl.program_id(0) == 0

    @pl.when(first_step)
    def _():
        sorted_sc[...] = jnp.zeros(sorted_sc.shape, F32)
        zero_sc[...] = jnp.zeros(zero_sc.shape, F32)
        _zero_fill(table_ref, fill_ref, xs_ref, zero_sc, zero_sem, n_blocks, wait=False)

    def place(i, carry):
        for k in range(2):
            sorted_sc[pl.ds(slot_ref[0, 0, k * tm + i], 1), :] = h_ref[pl.ds(i, 1), :]
        return carry

    lax.fori_loop(0, tm, place, 0, unroll=8)
    _chunk_copies(table_ref, xs_ref, sorted_sc, sem, to_hbm=True, wait=False)
    _chunk_copies(table_ref, xs_ref, sorted_sc, sem, to_hbm=True, wait=True)

    @pl.when(first_step)
    def _():
        _zero_fill(table_ref, fill_ref, xs_ref, zero_sc, zero_sem, n_blocks, wait=True)


def _sorted_rows(tm):
    return -(-(2 * tm + MOE_EXPERTS * (SEG_ALIGN - 1)) // SEG_CHUNK) * SEG_CHUNK


def _dispatch(h, table, slots, fill, rows):
    t, d = h.shape
    tm = min(MOE_TILE, t)
    kern = functools.partial(_dispatch_kernel, tm=tm, n_blocks=rows // MOE_BLK)
    return pl.pallas_call(
        kern,
        grid=(t // tm,),
        in_specs=[pl.BlockSpec((1, 1, TABLE_W), lambda i: (i, 0, 0), memory_space=pltpu.SMEM),
                  pl.BlockSpec((1, 1, 2 * tm), lambda i: (i, 0, 0), memory_space=pltpu.SMEM),
                  pl.BlockSpec(fill.shape, lambda i: (0, 0, 0), memory_space=pltpu.SMEM),
                  pl.BlockSpec((tm, d), lambda i: (i, 0))],
        out_specs=pl.BlockSpec(memory_space=pl.ANY),
        out_shape=jax.ShapeDtypeStruct((rows, d), F32),
        scratch_shapes=[pltpu.VMEM((_sorted_rows(tm), d), F32), pltpu.VMEM((MOE_BLK, d), F32),
                        pltpu.SemaphoreType.DMA(()), pltpu.SemaphoreType.DMA(())],
        compiler_params=_cparams(("arbitrary",)),
        name="moe_dispatch",
    )(table, slots, fill, h)


def _expert_kernel(be_ref, nu_ref, xs_ref, wg_ref, wu_ref, wd_ref, ys_ref, wg_sc, wu_sc, wd_sc):
    i = pl.program_id(0)
    live = i < nu_ref[0]

    @pl.when(jnp.logical_not(live))
    def _():
        ys_ref[...] = jnp.zeros(ys_ref.shape, F32)

    @pl.when(live)
    def _():
        @pl.when((i == 0) | (be_ref[i] != be_ref[jnp.maximum(i - 1, 0)]))
        def _():
            wg_sc[...] = wg_ref[...].astype(BF16)
            wu_sc[...] = wu_ref[...].astype(BF16)
            wd_sc[...] = wd_ref[...].astype(BF16)

        x = xs_ref[...].astype(BF16)
        g = jnp.dot(x, wg_sc[...], preferred_element_type=F32)
        u = jnp.dot(x, wu_sc[...], preferred_element_type=F32)
        a = (g * (1.0 / (1.0 + jnp.exp(-g))) * u).astype(BF16)
        ys_ref[...] = jnp.dot(a, wd_sc[...], preferred_element_type=F32)


def _experts(xs, block_e, n_used, w_gate, w_up, w_down, layer):
    rows, d = xs.shape
    n_blocks = rows // MOE_BLK
    hid = w_gate.shape[-1]
    used = lambda i, be, nu: (jnp.minimum(i, nu[0] - 1), 0)
    expert = lambda i, be, nu: (layer, be[jnp.minimum(i, nu[0] - 1)], 0, 0)
    grid_spec = pltpu.PrefetchScalarGridSpec(
        num_scalar_prefetch=2,
        grid=(n_blocks,),
        in_specs=[pl.BlockSpec((MOE_BLK, d), used),
                  pl.BlockSpec((None, None, d, hid), expert),
                  pl.BlockSpec((None, None, d, hid), expert),
                  pl.BlockSpec((None, None, hid, d), expert)],
        out_specs=pl.BlockSpec((MOE_BLK, d), lambda i, be, nu: (i, 0)),
        scratch_shapes=[pltpu.VMEM((d, hid), BF16), pltpu.VMEM((d, hid), BF16), pltpu.VMEM((hid, d), BF16)],
    )
    return pl.pallas_call(
        _expert_kernel,
        grid_spec=grid_spec,
        out_shape=jax.ShapeDtypeStruct((rows, d), F32),
        compiler_params=_cparams(("arbitrary",)),
        name="moe_experts",
    )(block_e, n_used, xs, w_gate, w_up, w_down)


def _combine_kernel(table_ref, slot_ref, ys_ref, x_ref, gf_ref, gate_ref, o_ref, sorted_sc, buf, sem, *, tm):
    _chunk_copies(table_ref, ys_ref, sorted_sc, sem, to_hbm=False, wait=False)
    _chunk_copies(table_ref, ys_ref, sorted_sc, sem, to_hbm=False, wait=True)

    def pick(i, carry):
        for k in range(2):
            buf[k, pl.ds(i, 1), :] = sorted_sc[pl.ds(slot_ref[0, 0, k * tm + i], 1), :]
        return carry

    lax.fori_loop(0, tm, pick, 0, unroll=8)
    gates = gate_ref[...]
    y = gates[:, 0:1] * buf[0] + gates[:, 1:2] * buf[1]
    o_ref[...] = x_ref[...] + gf_ref[...] * y


def _combine(ys, table, slots, x, gate_f, gates, seq):
    t, d = x.shape
    tm = min(MOE_TILE, t)
    per_b = seq // tm if seq >= tm else 1
    kern = functools.partial(_combine_kernel, tm=tm)
    return pl.pallas_call(
        kern,
        grid=(t // tm,),
        in_specs=[pl.BlockSpec((1, 1, TABLE_W), lambda i: (i, 0, 0), memory_space=pltpu.SMEM),
                  pl.BlockSpec((1, 1, 2 * tm), lambda i: (i, 0, 0), memory_space=pltpu.SMEM),
                  pl.BlockSpec(memory_space=pl.ANY),
                  pl.BlockSpec((tm, d), lambda i: (i, 0)),
                  pl.BlockSpec((None, 1, d), lambda i: (i // per_b, 0, 0)),
                  pl.BlockSpec((tm, ROUTER_COLS), lambda i: (i, 0))],
        out_specs=pl.BlockSpec((tm, d), lambda i: (i, 0)),
        out_shape=jax.ShapeDtypeStruct((t, d), F32),
        scratch_shapes=[pltpu.VMEM((_sorted_rows(tm), d), F32), pltpu.VMEM((2, tm, d), F32),
                        pltpu.SemaphoreType.DMA(())],
        compiler_params=_cparams(("arbitrary",)),
        name="moe_combine",
    )(table, slots, ys, x, gate_f, gates)


def _moe(h, logits, x1, gate_f, w_gate, w_up, w_down, layer, seq):
    rows = _moe_rows(h.shape[0])
    route, gates, seg, totals = _route(logits)
    table, slots, fill, block_e, n_used = _dispatch_plan(route, seg, totals[:, 0], rows // MOE_BLK)
    xs = _dispatch(h, table, slots, fill, rows)
    ys = _experts(xs, block_e, n_used, w_gate, w_up, w_down, layer)
    return _combine(ys, table, slots, x1, gate_f, gates, seq)


def _router_weights(w_group, b_group, w_expert, b_expert):
    d = w_group.shape[0]
    w = jnp.zeros((d, ROUTER_COLS), F32)
    w = w.at[:, 0:MOE_GROUPS].set(w_group.astype(F32))
    w = w.at[:, EXPERT_COL0:EXPERT_COL0 + MOE_EXPERTS].set(w_expert.astype(F32))
    b = jnp.zeros((1, ROUTER_COLS), F32)
    b = b.at[0, 0:MOE_GROUPS].set(b_group.astype(F32))
    b = b.at[0, EXPERT_COL0:EXPERT_COL0 + MOE_EXPERTS].set(b_expert.astype(F32))
    return w, b


def _lambda_init(layer):
    return 0.8 - 0.6 * math.exp(-0.3 * layer)


def kernel(x, c, rel_bias, ada_w, ada_b, norm_attn, norm_ffn, a_w_qkv, a_q_norm, a_k_norm, a_lambda, a_subln, a_w_o, kv_norm, kv_ada_w, kv_ada_b, kv_w, kv_k_norm, b_w_q, b_q_norm, b_w_o, moe_w_group, moe_b_group, moe_w_expert, moe_b_expert, moe_w_gate, moe_w_up, moe_w_down):
    batch, seq, d = x.shape
    t = batch * seq
    scale = HEAD_DIM ** -0.5
    n_bw = N_GROUPS * B_WIDTH

    c_pad = jnp.zeros((8, d), F32).at[:batch].set(c.astype(F32))
    mod = _modulation(c_pad, ada_w, ada_b)[:, :batch]
    kv_mod = _modulation(c_pad, kv_ada_w[None], kv_ada_b[None])[0, :batch]

    def part(m, i):
        return m[:, i * d:(i + 1) * d].reshape(batch, 1, d)

    a_bias, nd = _attn_a_bias_diags(rel_bias, seq, min(ATT_TILE, seq))
    b_bias = [_attn_b_bias_tiles(rel_bias, window // dil, dil) for window, dil in B_GROUPS]
    xf = x.reshape(t, d).astype(F32)
    kv = None
    for layer in range(DEPTH):
        m = mod[layer]
        sh_a, sc_a, g_a, sh_f, sc_f, g_f = (part(m, i) for i in range(6))
        w_r, b_r = _router_weights(moe_w_group[layer], moe_b_group[layer], moe_w_expert[layer], moe_b_expert[layer])
        if layer < N_A_LAYERS:
            qk_gain = jnp.concatenate([jnp.tile(a_q_norm[layer].astype(F32) * (scale * LOG2_E), 2 * N_HEADS),
                                       jnp.tile(a_k_norm[layer].astype(F32), 2 * N_HEADS),
                                       jnp.ones((N_HEADS * LANES,), F32)])
            qkv = _norm_proj(xf, norm_attn[layer], sh_a, sc_a, a_w_qkv[layer].astype(BF16), qk_gain,
                             2 * N_HEADS * LANES, seq, BF16)
            att = _attn_a(qkv, a_bias, nd, a_lambda[layer].astype(F32), a_subln[layer].astype(F32),
                          _lambda_init(layer), batch, seq)
            wo = a_w_o[layer].astype(BF16)
        else:
            j = layer - N_A_LAYERS
            if kv is None:
                k_gain = jnp.concatenate([jnp.tile(kv_k_norm.astype(F32), (1, N_HEADS)).reshape(-1),
                                          jnp.ones((n_bw,), F32)])
                kv = _norm_proj(xf, kv_norm, part(kv_mod, 0), part(kv_mod, 1), kv_w.astype(BF16), k_gain, n_bw, seq, F32)
            q_gain = jnp.tile(b_q_norm[j].astype(F32) * (scale * LOG2_E), (1, N_HEADS)).reshape(-1)
            q = _norm_proj(xf, norm_attn[layer], sh_a, sc_a, b_w_q[j].astype(BF16), q_gain, n_bw, seq, F32)
            att = _attn_b(q, kv, b_bias, batch, seq)
            wo = b_w_o[j].astype(BF16)
        x1, hf, logits = _out_proj(att, wo, xf, g_a, norm_ffn[layer], sh_f, sc_f, w_r, b_r, seq)
        xf = _moe(hf, logits, x1, g_f, moe_w_gate, moe_w_up, moe_w_down, layer, seq)
    return xf.reshape(batch, seq, d).astype(x.dtype)
```

```python
import functools
import math

import jax
import jax.numpy as jnp
import numpy as np
from jax import lax
from jax.experimental import pallas as pl
from jax.experimental.pallas import tpu as pltpu

F32 = jnp.float32
BF16 = jnp.bfloat16

D_MODEL = 1024
DEPTH = 4
N_A_LAYERS = 2
N_HEADS = 8
HEAD_DIM = 64
B_GROUPS = ((128, 1), (512, 4), (2048, 16))
N_GROUPS = len(B_GROUPS)
B_WIDTH = N_HEADS * HEAD_DIM
REL_BUCKETS = 32
REL_MAX_EXACT = 16
REL_MAX_DIST = 2048
MOE_GROUPS = 4
MOE_EPG = 8
MOE_EXPERTS = MOE_GROUPS * MOE_EPG
MOE_HIDDEN = 512
EPS = 1e-6
NEG_INF = float("-inf")
LOG2_E = math.log2(math.e)

LANES = 128
MXU_DIM = 256
ROUTER_COLS = 128
EXPERT_COL0 = 8

TOK_TILE = 512
ATT_TILE = 512
B_TOKENS = 2048
BQ_SUB = 128
MOE_BLK = 256
MOE_TILE = 512
SEG_ALIGN = 8
SEG_CHUNK = 8
TABLE_W = 256
VMEM_LIMIT = 56 * 1024 * 1024


def _cparams(sem):
    return pltpu.CompilerParams(dimension_semantics=sem, vmem_limit_bytes=VMEM_LIMIT)


def _bucket_table(max_dist):
    n = np.arange(max_dist + 1)
    nf = np.maximum(n, 1).astype(np.float64)
    large = REL_MAX_EXACT + (np.log(nf / REL_MAX_EXACT) / math.log(REL_MAX_DIST / REL_MAX_EXACT)
                             * (REL_BUCKETS - REL_MAX_EXACT)).astype(np.int64)
    large = np.minimum(large, REL_BUCKETS - 1)
    return np.where(n < REL_MAX_EXACT, n, large).astype(np.int32)


def _toeplitz(v, n, m):
    length = v.shape[-1]
    assert length >= n + m - 1 and length - 1 >= m
    lead = v.shape[:-1]
    flat = jnp.tile(v, (1,) * len(lead) + (n,))[..., :n * (length - 1)]
    return flat.reshape(lead + (n, length - 1))[..., :m]


def _diag_values(rel_bias, dist, ok):
    table = _bucket_table(int(dist.max()))
    vals = rel_bias.astype(F32)[table[np.clip(dist, 0, None)]]
    vals = jnp.where(ok[..., None], vals, NEG_INF)
    return jnp.moveaxis(vals, -1, 0)


def _attn_a_bias_diags(rel_bias, seq, tile):
    table = _bucket_table(seq)
    last_start = int(np.argmax(table == REL_BUCKETS - 1))
    nd = 0
    while nd * tile - (tile - 1) < last_start and nd * tile < seq:
        nd += 1
    nd += 1
    length = 2 * tile
    u = np.arange(length)
    u = np.where(u < tile, u, u - length)
    dist = np.arange(nd)[:, None] * tile - u[None, :]
    return _diag_values(rel_bias, dist, dist >= 0) * LOG2_E, nd


def _attn_b_bias_tiles(rel_bias, band, dil):
    length = 3 * BQ_SUB
    u = np.arange(length)
    u = np.where(u < 2 * BQ_SUB, u, u - length)
    rel = BQ_SUB - u
    ok = (rel >= 0) & (rel <= band)
    general = _toeplitz(_diag_values(rel_bias, dil * rel, ok) * LOG2_E, BQ_SUB, 2 * BQ_SUB)
    no_prev = np.arange(2 * BQ_SUB)[None, None, :] >= BQ_SUB
    return jnp.stack([jnp.where(no_prev, general, NEG_INF), general])


def _group_sum_matrix():
    r = np.arange(MXU_DIM)
    return jnp.asarray((r[:, None] // HEAD_DIM) == (r[None, :] // HEAD_DIM), BF16)


def _mod_kernel(c_ref, w_ref, b_ref, o_ref):
    c = c_ref[...]
    c_act = c * (1.0 / (1.0 + jnp.exp(-c)))
    o_ref[...] = jnp.dot(c_act, w_ref[...], preferred_element_type=F32,
                         precision=lax.Precision.HIGHEST) + b_ref[...]


def _modulation(c_pad, w, b, tn=1024):
    nl, d, n = w.shape
    return pl.pallas_call(
        _mod_kernel,
        grid=(nl, n // tn),
        in_specs=[pl.BlockSpec((8, d), lambda l, j: (0, 0)),
                  pl.BlockSpec((None, d, tn), lambda l, j: (l, 0, j)),
                  pl.BlockSpec((None, 1, tn), lambda l, j: (l, 0, j))],
        out_specs=pl.BlockSpec((None, 8, tn), lambda l, j: (l, 0, j)),
        out_shape=jax.ShapeDtypeStruct((nl, 8, n), F32),
        compiler_params=_cparams(("parallel", "parallel")),
        name="adaln_mod",
    )(c_pad, w, b.reshape(nl, 1, n))


def _rms(x, gain):
    return x * lax.rsqrt(jnp.mean(x * x, axis=-1, keepdims=True) + EPS) * gain


def _proj_kernel(x_ref, g_ref, sh_ref, sc_ref, w_ref, hg_ref, bd_ref, o_ref, *, n_out, n_norm):
    h = _rms(x_ref[...], g_ref[...]) * (1.0 + sc_ref[...]) + sh_ref[...]
    hb = h.astype(BF16)
    wide = 2 * MXU_DIM
    for c0 in range(0, n_out, wide):
        a = jnp.dot(hb, w_ref[:, c0:c0 + wide], preferred_element_type=F32)
        if c0 < n_norm:
            sq = (a * a).astype(BF16)
            ms = jnp.concatenate([jnp.dot(sq[:, :MXU_DIM], bd_ref[...], preferred_element_type=F32),
                                  jnp.dot(sq[:, MXU_DIM:], bd_ref[...], preferred_element_type=F32)], axis=1)
            a = a * lax.rsqrt(ms * (1.0 / HEAD_DIM) + EPS) * hg_ref[:, c0:c0 + wide]
        o_ref[:, c0:c0 + wide] = a.astype(o_ref.dtype)


def _norm_proj(x, gain, shift, scale, w_bf16, head_gain, n_norm, seq, out_dtype):
    t, d = x.shape
    n_out = w_bf16.shape[1]
    tm = min(TOK_TILE, seq)
    per_b = seq // tm
    kern = functools.partial(_proj_kernel, n_out=n_out, n_norm=n_norm)
    return pl.pallas_call(
        kern,
        grid=(t // tm,),
        in_specs=[pl.BlockSpec((tm, d), lambda i: (i, 0)),
                  pl.BlockSpec((1, d), lambda i: (0, 0)),
                  pl.BlockSpec((None, 1, d), lambda i: (i // per_b, 0, 0)),
                  pl.BlockSpec((None, 1, d), lambda i: (i // per_b, 0, 0)),
                  pl.BlockSpec((d, n_out), lambda i: (0, 0)),
                  pl.BlockSpec((1, n_out), lambda i: (0, 0)),
                  pl.BlockSpec((MXU_DIM, MXU_DIM), lambda i: (0, 0))],
        out_specs=pl.BlockSpec((tm, n_out), lambda i: (i, 0)),
        out_shape=jax.ShapeDtypeStruct((t, n_out), out_dtype),
        compiler_params=_cparams(("parallel",)),
        name="norm_proj",
    )(x, gain.reshape(1, d), shift, scale, w_bf16, head_gain.reshape(1, n_out), _group_sum_matrix())


def _lane_tile(a, reps):
    return jnp.concatenate([a] * reps, axis=1)


def _attn_a_kernel(q_ref, k_ref, v_ref, diag_ref, lam_ref, g_ref, o_ref, bias_ref, m_sc, acc_sc,
                   *, tile, nd, lam_init):
    qi = pl.program_id(2)

    @pl.when((pl.program_id(1) == 0) & (qi == 0))
    def _():
        for d in range(nd):
            row = jnp.broadcast_to(diag_ref[d:d + 1, :], (tile, 2 * tile))
            bias_ref[d] = pltpu.roll(row, 0, 1, stride=1, stride_axis=0)[:, :tile]

    q = q_ref[...]
    lane = lax.broadcasted_iota(jnp.int32, q.shape, 1)
    zero = jnp.zeros_like(q)
    qq = jnp.concatenate([jnp.where(lane < HEAD_DIM, q, zero),
                          jnp.where(lane >= HEAD_DIM, q, zero)], axis=0)
    m_sc[...] = jnp.full(m_sc.shape, NEG_INF, F32)
    acc_sc[...] = jnp.zeros(acc_sc.shape, F32)
    ones = jnp.ones((tile, LANES), BF16)

    def kv_tile(j):
        start = pl.multiple_of(j * tile, tile)
        k = k_ref[pl.ds(start, tile), :]
        v_ones = jnp.concatenate([v_ref[pl.ds(start, tile), :], ones], axis=1)
        s = lax.dot_general(qq, k, (((1,), (1,)), ((), ())), preferred_element_type=F32)
        bias = bias_ref[jnp.minimum(qi - j, nd - 1)]
        for c in range(2):
            sc = s[c * tile:(c + 1) * tile] + bias
            m_prev = m_sc[c]
            m_new = jnp.maximum(m_prev, jnp.max(sc, axis=-1, keepdims=True))
            alpha = jnp.exp2(m_prev - m_new)
            p = jnp.exp2(sc - _lane_tile(m_new, tile // LANES))
            acc_sc[c] = (_lane_tile(alpha, 2) * acc_sc[c]
                         + jnp.dot(p.astype(BF16), v_ones, preferred_element_type=F32))
            m_sc[c] = m_new

    def kv_pair(i, carry):
        kv_tile(2 * i)
        kv_tile(2 * i + 1)
        return carry

    n_kv = qi + 1
    lax.fori_loop(0, n_kv // 2, kv_pair, 0)

    @pl.when(n_kv % 2 == 1)
    def _():
        kv_tile(qi)

    lp = lam_ref[...]
    lam = (jnp.exp(jnp.sum(lp[0:1] * lp[1:2], axis=-1, keepdims=True))
           - jnp.exp(jnp.sum(lp[2:3] * lp[3:4], axis=-1, keepdims=True)) + lam_init)
    a0, a1 = acc_sc[0], acc_sc[1]
    o = a0[:, :LANES] / a0[:, LANES:] - lam * (a1[:, :LANES] / a1[:, LANES:])
    o_ref[...] = (_rms(o, g_ref[...]) * (1.0 - lam_init)).astype(BF16)


def _attn_a(qkv, bias_diags, nd, lam_p, subln, lam_init, batch, seq):
    tile = min(ATT_TILE, seq)
    nq = seq // tile
    qkv3 = qkv.reshape(batch, seq, 3 * N_HEADS * LANES)
    kern = functools.partial(_attn_a_kernel, tile=tile, nd=nd, lam_init=lam_init)
    out = pl.pallas_call(
        kern,
        grid=(N_HEADS, batch, nq),
        in_specs=[pl.BlockSpec((None, tile, LANES), lambda h, b, i: (b, i, h)),
                  pl.BlockSpec((None, seq, LANES), lambda h, b, i: (b, 0, N_HEADS + h)),
                  pl.BlockSpec((None, seq, LANES), lambda h, b, i: (b, 0, 2 * N_HEADS + h)),
                  pl.BlockSpec((None, nd, 2 * tile), lambda h, b, i: (h, 0, 0)),
                  pl.BlockSpec((4, HEAD_DIM), lambda h, b, i: (0, 0)),
                  pl.BlockSpec((1, LANES), lambda h, b, i: (0, 0))],
        out_specs=pl.BlockSpec((None, tile, LANES), lambda h, b, i: (b, i, h)),
        out_shape=jax.ShapeDtypeStruct((batch, seq, N_HEADS * LANES), BF16),
        scratch_shapes=[pltpu.VMEM((nd, tile, tile), F32), pltpu.VMEM((2, tile, LANES), F32),
                        pltpu.VMEM((2, tile, 2 * LANES), F32)],
        compiler_params=_cparams(("arbitrary", "arbitrary", "arbitrary")),
        name="attn_a",
    )(qkv3, qkv3, qkv3, bias_diags, lam_p, subln.reshape(1, LANES))
    return out.reshape(batch * seq, N_HEADS * LANES)


def _attn_b_kernel(*refs, tile):
    ins, o_ref, scratch = refs[:6 * N_GROUPS], refs[6 * N_GROUPS], refs[6 * N_GROUPS + 1:]
    first_tile = pl.program_id(1) == 0
    lane = lax.broadcasted_iota(jnp.int32, (BQ_SUB, LANES), 1)
    low = lane < HEAD_DIM
    ones = jnp.ones((2 * BQ_SUB, LANES), BF16)

    for g, (_, dil) in enumerate(B_GROUPS):
        q_ref, k_ref, v_ref, kp_ref, vp_ref, bias_ref = ins[6 * g:6 * g + 6]
        og_ref, lg_ref = scratch[2 * g:2 * g + 2]
        span = BQ_SUB * dil

        def residue(base, dil=dil):
            return pl.ds(base, BQ_SUB) if dil == 1 else pl.ds(base, BQ_SUB, stride=dil)

        for sb in range(tile // span):
            for r in range(dil):
                base = sb * span + r
                rows = residue(base)
                qp = q_ref[rows, :].astype(BF16)
                if sb == 0:
                    k_prev, v_prev = kp_ref[residue(r), :], vp_ref[residue(r), :]
                    variant = jnp.where(first_tile, 0, 1)
                else:
                    k_prev, v_prev = k_ref[residue(base - span), :], v_ref[residue(base - span), :]
                    variant = 1
                kk = jnp.concatenate([k_prev, k_ref[rows, :]], axis=0).astype(BF16)
                vv = jnp.concatenate([v_prev, v_ref[rows, :]], axis=0).astype(BF16)
                v_ones = jnp.concatenate([vv, ones], axis=1)
                zero = jnp.zeros_like(qp)
                qq = jnp.concatenate([jnp.where(low, qp, zero), jnp.where(low, zero, qp)], axis=0)
                s = lax.dot_general(qq, kk, (((1,), (1,)), ((), ())), preferred_element_type=F32)
                outs, lses = [], []
                for c in range(2):
                    sc = s[c * BQ_SUB:(c + 1) * BQ_SUB] + bias_ref[variant, c]
                    m = jnp.max(sc, axis=-1, keepdims=True)
                    p = jnp.exp2(sc - m)
                    acc = jnp.dot(p.astype(BF16), v_ones, preferred_element_type=F32)
                    outs.append(acc[:, :LANES] / acc[:, LANES:])
                    lses.append(m + jnp.log2(acc[:, LANES:]))
                og_ref[rows, :] = jnp.where(low, outs[0], outs[1])
                lg_ref[rows, :] = jnp.where(low, lses[0], lses[1])

    l0, l1, l2 = scratch[1][...], scratch[3][...], scratch[5][...]
    m = jnp.maximum(jnp.maximum(l0, l1), l2)
    e0, e1, e2 = jnp.exp2(l0 - m), jnp.exp2(l1 - m), jnp.exp2(l2 - m)
    z = e0 + e1 + e2
    o_ref[...] = ((e0 / z) * scratch[0][...] + (e1 / z) * scratch[2][...] + (e2 / z) * scratch[4][...]).astype(BF16)


def _attn_b(q, kv, bias_tiles, batch, seq):
    tile = min(B_TOKENS, seq)
    pairs = N_HEADS // 2
    q3 = q.reshape(batch, seq, N_GROUPS * B_WIDTH)
    kv3 = kv.reshape(batch, seq, 2 * N_GROUPS * B_WIDTH)
    operands, in_specs = [], []
    for g, (_, dil) in enumerate(B_GROUPS):
        span = BQ_SUB * dil
        assert tile % span == 0
        k_col = lambda hp, g=g: g * pairs + hp
        v_col = lambda hp, g=g: (N_GROUPS + g) * pairs + hp
        prev = lambda n, per=tile // span: jnp.maximum(n * per - 1, 0)
        operands += [q3, kv3, kv3, kv3, kv3, bias_tiles[g]]
        in_specs += [pl.BlockSpec((None, tile, LANES), lambda b, n, hp, c=k_col: (b, n, c(hp))),
                     pl.BlockSpec((None, tile, LANES), lambda b, n, hp, c=k_col: (b, n, c(hp))),
                     pl.BlockSpec((None, tile, LANES), lambda b, n, hp, c=v_col: (b, n, c(hp))),
                     pl.BlockSpec((None, span, LANES), lambda b, n, hp, c=k_col, p=prev: (b, p(n), c(hp))),
                     pl.BlockSpec((None, span, LANES), lambda b, n, hp, c=v_col, p=prev: (b, p(n), c(hp))),
                     pl.BlockSpec((2, 2, BQ_SUB, 2 * BQ_SUB), lambda b, n, hp: (0, hp, 0, 0))]
    out = pl.pallas_call(
        functools.partial(_attn_b_kernel, tile=tile),
        grid=(batch, seq // tile, pairs),
        in_specs=in_specs,
        out_specs=pl.BlockSpec((None, tile, LANES), lambda b, n, hp: (b, n, hp)),
        out_shape=jax.ShapeDtypeStruct((batch, seq, B_WIDTH), BF16),
        scratch_shapes=[pltpu.VMEM((tile, LANES), F32)] * (2 * N_GROUPS),
        compiler_params=_cparams(("parallel", "parallel", "parallel")),
        name="attn_b",
    )(*operands)
    return out.reshape(batch * seq, B_WIDTH)


def _split_bf16(a):
    hi = a.astype(BF16)
    return hi, (a - hi.astype(F32)).astype(BF16)


def _out_tail(o_bf16, wo_ref, x_ref, ga_ref, gf_ref, shf_ref, scf_ref, wr_ref, br_ref, x1_ref, hf_ref, lg_ref):
    y = jnp.dot(o_bf16, wo_ref[...], preferred_element_type=F32)
    x1 = x_ref[...] + ga_ref[...] * y
    x1_ref[...] = x1
    hf = _rms(x1, gf_ref[...]) * (1.0 + scf_ref[...]) + shf_ref[...]
    hf_ref[...] = hf
    hi, lo = _split_bf16(hf)
    both = jnp.dot(hi, wr_ref[...], preferred_element_type=F32)
    lg_ref[...] = (both[:, :ROUTER_COLS] + jnp.dot(lo, wr_ref[:, :ROUTER_COLS], preferred_element_type=F32)
                   + both[:, ROUTER_COLS:] + br_ref[...])


def _out_kernel(o_ref, *rest):
    _out_tail(o_ref[...], *rest)


def _out_proj(att, wo_bf16, x, gate_a, gain_f, shift_f, scale_f, w_router, b_router, seq):
    t, d = x.shape
    tm = min(TOK_TILE, seq)
    per_b = seq // tm
    row = lambda i: (i, 0)
    fixed = lambda i: (0, 0)
    per_batch = lambda i: (i // per_b, 0, 0)
    return pl.pallas_call(
        _out_kernel,
        grid=(t // tm,),
        in_specs=[pl.BlockSpec((tm, att.shape[1]), row),
                  pl.BlockSpec(wo_bf16.shape, fixed),
                  pl.BlockSpec((tm, d), row),
                  pl.BlockSpec((None, 1, d), per_batch),
                  pl.BlockSpec((1, d), fixed),
                  pl.BlockSpec((None, 1, d), per_batch),
                  pl.BlockSpec((None, 1, d), per_batch),
                  pl.BlockSpec((d, 2 * ROUTER_COLS), fixed),
                  pl.BlockSpec((1, ROUTER_COLS), fixed)],
        out_specs=[pl.BlockSpec((tm, d), row), pl.BlockSpec((tm, d), row),
                   pl.BlockSpec((tm, ROUTER_COLS), row)],
        out_shape=[jax.ShapeDtypeStruct((t, d), F32), jax.ShapeDtypeStruct((t, d), F32),
                   jax.ShapeDtypeStruct((t, ROUTER_COLS), F32)],
        compiler_params=_cparams(("parallel",)),
        name="out_proj",
    )(att, wo_bf16, x, gate_a, gain_f.reshape(1, d), shift_f, scale_f,
      jnp.concatenate(_split_bf16(w_router), axis=1), b_router)


def _route_kernel(lg_ref, tri_ref, eid_ref, gate_ref, seg_ref, cnt_ref, cnt_sc):
    lt = lg_ref[...].T
    best, g_idx = lt[0:1], jnp.zeros((1, lt.shape[1]), jnp.int32)
    for g in range(1, MOE_GROUPS):
        upd = lt[g:g + 1] > best
        best = jnp.where(upd, lt[g:g + 1], best)
        g_idx = jnp.where(upd, g, g_idx)
    denom = jnp.zeros_like(best)
    for g in range(MOE_GROUPS):
        denom = denom + jnp.exp(lt[g:g + 1] - best)
    g_w = 1.0 / denom
    e_sel = lt[EXPERT_COL0:EXPERT_COL0 + MOE_EPG]
    for g in range(1, MOE_GROUPS):
        r0 = EXPERT_COL0 + g * MOE_EPG
        e_sel = jnp.where(g_idx == g, lt[r0:r0 + MOE_EPG], e_sel)

    def first_max(vals):
        v, i = vals[0:1], jnp.zeros((1, vals.shape[1]), jnp.int32)
        for e in range(1, MOE_EPG):
            upd = vals[e:e + 1] > v
            v = jnp.where(upd, vals[e:e + 1], v)
            i = jnp.where(upd, e, i)
        return v, i

    v1, i1 = first_max(e_sel)
    row = lax.broadcasted_iota(jnp.int32, e_sel.shape, 0)
    v2, i2 = first_max(jnp.where(row == i1, NEG_INF, e_sel))
    e2 = jnp.exp(v2 - v1)
    w1 = 1.0 / (1.0 + e2)
    w2 = e2 / (1.0 + e2)
    e1 = g_idx * MOE_EPG + i1
    e2 = g_idx * MOE_EPG + i2

    @pl.when(pl.program_id(0) == 0)
    def _():
        cnt_sc[...] = jnp.zeros(cnt_sc.shape, F32)

    tm = lt.shape[1]
    erows = lax.broadcasted_iota(jnp.int32, (MOE_EXPERTS, tm), 0)
    ohs, pres = [], []
    for e_k in (e1, e2):
        oh = (erows == e_k).astype(F32)
        ohs.append(oh)
        pres.append(jnp.dot(oh.astype(BF16), tri_ref[...], preferred_element_type=F32))
    n0 = pres[0][:, tm - 1:]
    n_tot = n0 + pres[1][:, tm - 1:]
    chunks = jnp.floor((n_tot + (SEG_ALIGN - 1.0)) * (1.0 / SEG_ALIGN))
    er = lax.broadcasted_iota(jnp.int32, (MOE_EXPERTS, MOE_EXPERTS), 0)
    ec = lax.broadcasted_iota(jnp.int32, (MOE_EXPERTS, MOE_EXPERTS), 1)
    before = jnp.dot((ec < er).astype(BF16), jnp.broadcast_to(chunks, (MOE_EXPERTS, LANES)).astype(BF16),
                     preferred_element_type=F32)
    seg_start = before[:, 0:1] * SEG_ALIGN
    slot0 = jnp.sum(ohs[0] * (seg_start + pres[0] - 1.0), axis=0, keepdims=True)
    slot1 = jnp.sum(ohs[1] * (seg_start + n0 + pres[1] - 1.0), axis=0, keepdims=True)
    seg_off = cnt_sc[...]
    cnt_sc[...] = seg_off + chunks * SEG_ALIGN
    cnt_ref[...] = jnp.broadcast_to(cnt_sc[...], cnt_ref.shape).astype(jnp.int32)
    scol = lax.broadcasted_iota(jnp.int32, seg_ref.shape, 1)
    seg_ref[...] = jnp.where(scol == 0, chunks, jnp.where(scol == 1, seg_start, jnp.where(
        scol == 2, seg_off, 0.0))).astype(jnp.int32)

    erow = lax.broadcasted_iota(jnp.int32, eid_ref.shape, 0)
    eid_ref[...] = jnp.where(erow == 0, e1, jnp.where(erow == 1, e2, jnp.where(
        erow == 2, slot0.astype(jnp.int32), jnp.where(erow == 3, slot1.astype(jnp.int32), 0))))
    grow = lax.broadcasted_iota(jnp.int32, lt.shape, 0)
    gates = jnp.where(grow == 0, g_w * w1, jnp.where(grow == 1, g_w * w2, 0.0))
    gate_ref[...] = gates.T


def _route(logits):
    t = logits.shape[0]
    tm = min(MOE_TILE, t)
    n_tiles = t // tm
    r = np.arange(tm)
    tri = jnp.asarray(r[:, None] <= r[None, :], BF16)
    return pl.pallas_call(
        _route_kernel,
        grid=(n_tiles,),
        in_specs=[pl.BlockSpec((tm, ROUTER_COLS), lambda i: (i, 0)),
                  pl.BlockSpec((tm, tm), lambda i: (0, 0))],
        out_specs=[pl.BlockSpec((8, tm), lambda i: (0, i)), pl.BlockSpec((tm, ROUTER_COLS), lambda i: (i, 0)),
                   pl.BlockSpec((MOE_EXPERTS, LANES), lambda i: (i, 0)),
                   pl.BlockSpec((MOE_EXPERTS, LANES), lambda i: (0, 0))],
        out_shape=[jax.ShapeDtypeStruct((8, t), jnp.int32), jax.ShapeDtypeStruct((t, ROUTER_COLS), F32),
                   jax.ShapeDtypeStruct((n_tiles * MOE_EXPERTS, LANES), jnp.int32),
                   jax.ShapeDtypeStruct((MOE_EXPERTS, LANES), jnp.int32)],
        scratch_shapes=[pltpu.VMEM((MOE_EXPERTS, 1), F32)],
        compiler_params=_cparams(("arbitrary",)),
        name="moe_route",
    )(logits, tri)


def _moe_rows(t):
    n_tiles = t // min(MOE_TILE, t)
    rows = 2 * t + n_tiles * MOE_EXPERTS * (SEG_ALIGN - 1) + MOE_EXPERTS * (MOE_BLK - 1)
    return -(-rows // MOE_BLK) * MOE_BLK


def _chunk_rows(first, count, row0, n_slots):
    c = jnp.arange(n_slots, dtype=jnp.int32)
    run = jnp.sum((first + count)[..., None, :] <= c[:, None], axis=-1)
    hit = run[..., None] == jnp.arange(first.shape[-1], dtype=jnp.int32)
    base = jnp.sum(jnp.where(hit, (row0 - SEG_CHUNK * first)[..., None, :], 0), axis=-1)
    return jnp.where(run < first.shape[-1], base + SEG_CHUNK * c, 0)


def _dispatch_plan(route, seg, totals, n_blocks):
    n_tiles = seg.shape[0] // MOE_EXPERTS
    tm = route.shape[1] // n_tiles
    n_slots = _sorted_rows(tm) // SEG_CHUNK
    padded = (totals + MOE_BLK - 1) // MOE_BLK * MOE_BLK
    pad_end = jnp.cumsum(padded)
    pad_start = pad_end - padded
    n_used = pad_end[-1] // MOE_BLK
    seg = seg.reshape(n_tiles, MOE_EXPERTS, LANES)
    chunks, first, seg_off = seg[:, :, 0], seg[:, :, 1] // SEG_CHUNK, seg[:, :, 2]
    dst = _chunk_rows(first, chunks, pad_start[None, :] + seg_off, n_slots)
    fill = (padded - totals) // SEG_CHUNK
    fill_first = jnp.cumsum(fill) - fill
    fill_dst = _chunk_rows(fill_first, fill, pad_start + totals, MOE_EXPERTS * (MOE_BLK // SEG_CHUNK))
    tail = jnp.stack([jnp.broadcast_to(n_used, (n_tiles,)), jnp.broadcast_to(jnp.sum(fill), (n_tiles,)),
                      jnp.sum(chunks, axis=1)], axis=1)
    table = jnp.concatenate([dst, jnp.zeros((n_tiles, TABLE_W - 3 - n_slots), jnp.int32), tail], axis=1)
    slots = route[2:4].reshape(2, n_tiles, tm).transpose(1, 0, 2).reshape(n_tiles, 1, 2 * tm)
    block_start = jnp.arange(n_blocks, dtype=jnp.int32) * MOE_BLK
    block_e = jnp.minimum(jnp.sum(pad_end[None, :] <= block_start[:, None], axis=1), MOE_EXPERTS - 1).astype(jnp.int32)
    return (table.astype(jnp.int32).reshape(n_tiles, 1, TABLE_W), slots, fill_dst.astype(jnp.int32).reshape(1, 1, -1),
            block_e, n_used.astype(jnp.int32).reshape(1))


def _chunk_copy(hbm_ref, hbm_row, sorted_ref, chunk, sem, to_hbm):
    local = sorted_ref.at[pl.ds(pl.multiple_of(chunk * SEG_CHUNK, SEG_CHUNK), SEG_CHUNK), :]
    remote = hbm_ref.at[pl.ds(pl.multiple_of(hbm_row, SEG_CHUNK), SEG_CHUNK), :]
    return pltpu.make_async_copy(local, remote, sem) if to_hbm else pltpu.make_async_copy(remote, local, sem)


def _start_chunks(table_ref, hbm_ref, sorted_ref, sem, to_hbm):
    def chunk(c, carry):
        _chunk_copy(hbm_ref, table_ref[0, 0, c], sorted_ref, c, sem, to_hbm).start()
        return carry

    lax.fori_loop(0, table_ref[0, 0, TABLE_W - 1], chunk, 0)


def _wait_chunks(count, hbm_ref, sorted_ref, sem, to_hbm):
    def chunk(c, carry):
        _chunk_copy(hbm_ref, 0, sorted_ref, 0, sem, to_hbm).wait()
        return carry

    lax.fori_loop(0, count, chunk, 0)


def _zero_fill(table_ref, fill_ref, xs_ref, zero_sc, sem, n_blocks, wait):
    def finish(copy):
        if wait:
            copy.wait()
        else:
            copy.start()

    def chunk(c, carry):
        row = pl.multiple_of(fill_ref[0, 0, c], SEG_CHUNK)
        finish(pltpu.make_async_copy(zero_sc.at[pl.ds(0, SEG_CHUNK), :], xs_ref.at[pl.ds(row, SEG_CHUNK), :], sem))
        return carry

    def block(b, carry):
        row = pl.multiple_of(b * MOE_BLK, MOE_BLK)
        finish(pltpu.make_async_copy(zero_sc, xs_ref.at[pl.ds(row, MOE_BLK), :], sem))
        return carry

    lax.fori_loop(0, table_ref[0, 0, TABLE_W - 2], chunk, 0)
    lax.fori_loop(table_ref[0, 0, TABLE_W - 3], n_blocks, block, 0)


def _dispatch_kernel(table_ref, slot_ref, fill_ref, h_ref, xs_ref, sorted_sc, zero_sc, pending_sc, sem, zero_sem,
                     *, tm, n_blocks):
    step = pl.program_id(0)
    first_step = step == 0
    parity = step % 2
    mine, other = sorted_sc.at[parity], sorted_sc.at[1 - parity]

    @pl.when(first_step)
    def _():
        sorted_sc[...] = jnp.zeros(sorted_sc.shape, F32)
        zero_sc[...] = jnp.zeros(zero_sc.shape, F32)
        pending_sc[0] = 0
        _zero_fill(table_ref, fill_ref, xs_ref, zero_sc, zero_sem, n_blocks, wait=False)

    def place(i, carry):
        for k in range(2):
            mine[pl.ds(slot_ref[0, 0, k * tm + i], 1), :] = h_ref[pl.ds(i, 1), :]
        return carry

    lax.fori_loop(0, tm, place, 0, unroll=8)
    _start_chunks(table_ref, xs_ref, mine, sem.at[parity], to_hbm=True)
    _wait_chunks(pending_sc[0], xs_ref, other, sem.at[1 - parity], to_hbm=True)
    pending_sc[0] = table_ref[0, 0, TABLE_W - 1]

    @pl.when(step == pl.num_programs(0) - 1)
    def _():
        _wait_chunks(pending_sc[0], xs_ref, mine, sem.at[parity], to_hbm=True)

    @pl.when(first_step)
    def _():
        _zero_fill(table_ref, fill_ref, xs_ref, zero_sc, zero_sem, n_blocks, wait=True)


def _sorted_rows(tm):
    return -(-(2 * tm + MOE_EXPERTS * (SEG_ALIGN - 1)) // SEG_CHUNK) * SEG_CHUNK


def _dispatch(h, table, slots, fill, rows):
    t, d = h.shape
    tm = min(MOE_TILE, t)
    kern = functools.partial(_dispatch_kernel, tm=tm, n_blocks=rows // MOE_BLK)
    return pl.pallas_call(
        kern,
        grid=(t // tm,),
        in_specs=[pl.BlockSpec((1, 1, TABLE_W), lambda i: (i, 0, 0), memory_space=pltpu.SMEM),
                  pl.BlockSpec((1, 1, 2 * tm), lambda i: (i, 0, 0), memory_space=pltpu.SMEM),
                  pl.BlockSpec(fill.shape, lambda i: (0, 0, 0), memory_space=pltpu.SMEM),
                  pl.BlockSpec((tm, d), lambda i: (i, 0))],
        out_specs=pl.BlockSpec(memory_space=pl.ANY),
        out_shape=jax.ShapeDtypeStruct((rows, d), F32),
        scratch_shapes=[pltpu.VMEM((2, _sorted_rows(tm), d), F32), pltpu.VMEM((MOE_BLK, d), F32),
                        pltpu.SMEM((1,), jnp.int32), pltpu.SemaphoreType.DMA((2,)), pltpu.SemaphoreType.DMA(())],
        compiler_params=_cparams(("arbitrary",)),
        name="moe_dispatch",
    )(table, slots, fill, h)


def _expert_kernel(be_ref, nu_ref, xs_ref, wg_ref, wu_ref, wd_ref, ys_ref, wg_sc, wu_sc, wd_sc):
    i = pl.program_id(0)
    live = i < nu_ref[0]

    @pl.when(jnp.logical_not(live))
    def _():
        ys_ref[...] = jnp.zeros(ys_ref.shape, F32)

    @pl.when(live)
    def _():
        @pl.when((i == 0) | (be_ref[i] != be_ref[jnp.maximum(i - 1, 0)]))
        def _():
            wg_sc[...] = wg_ref[...].astype(BF16)
            wu_sc[...] = wu_ref[...].astype(BF16)
            wd_sc[...] = wd_ref[...].astype(BF16)

        x = xs_ref[...].astype(BF16)
        g = jnp.dot(x, wg_sc[...], preferred_element_type=F32)
        u = jnp.dot(x, wu_sc[...], preferred_element_type=F32)
        a = (g * (1.0 / (1.0 + jnp.exp(-g))) * u).astype(BF16)
        ys_ref[...] = jnp.dot(a, wd_sc[...], preferred_element_type=F32)


def _experts(xs, block_e, n_used, w_gate, w_up, w_down, layer):
    rows, d = xs.shape
    n_blocks = rows // MOE_BLK
    hid = w_gate.shape[-1]
    used = lambda i, be, nu: (jnp.minimum(i, nu[0] - 1), 0)
    expert = lambda i, be, nu: (layer, be[jnp.minimum(i, nu[0] - 1)], 0, 0)
    grid_spec = pltpu.PrefetchScalarGridSpec(
        num_scalar_prefetch=2,
        grid=(n_blocks,),
        in_specs=[pl.BlockSpec((MOE_BLK, d), used),
                  pl.BlockSpec((None, None, d, hid), expert),
                  pl.BlockSpec((None, None, d, hid), expert),
                  pl.BlockSpec((None, None, hid, d), expert)],
        out_specs=pl.BlockSpec((MOE_BLK, d), lambda i, be, nu: (i, 0)),
        scratch_shapes=[pltpu.VMEM((d, hid), BF16), pltpu.VMEM((d, hid), BF16), pltpu.VMEM((hid, d), BF16)],
    )
    return pl.pallas_call(
        _expert_kernel,
        grid_spec=grid_spec,
        out_shape=jax.ShapeDtypeStruct((rows, d), F32),
        compiler_params=_cparams(("arbitrary",)),
        name="moe_experts",
    )(block_e, n_used, xs, w_gate, w_up, w_down)


def _combine_kernel(table_ref, next_table_ref, slot_ref, ys_ref, x_ref, gf_ref, gate_ref, o_ref,
                    sorted_sc, buf, sem, *, tm):
    step = pl.program_id(0)
    parity = step % 2
    mine, other = sorted_sc.at[parity], sorted_sc.at[1 - parity]

    @pl.when(step == 0)
    def _():
        _start_chunks(table_ref, ys_ref, mine, sem.at[parity], to_hbm=False)

    @pl.when(step + 1 < pl.num_programs(0))
    def _():
        _start_chunks(next_table_ref, ys_ref, other, sem.at[1 - parity], to_hbm=False)

    _wait_chunks(table_ref[0, 0, TABLE_W - 1], ys_ref, mine, sem.at[parity], to_hbm=False)

    def pick(i, carry):
        for k in range(2):
            buf[k, pl.ds(i, 1), :] = mine[pl.ds(slot_ref[0, 0, k * tm + i], 1), :]
        return carry

    lax.fori_loop(0, tm, pick, 0, unroll=8)
    gates = gate_ref[...]
    y = gates[:, 0:1] * buf[0] + gates[:, 1:2] * buf[1]
    o_ref[...] = x_ref[...] + gf_ref[...] * y


def _combine(ys, table, slots, x, gate_f, gates, seq):
    t, d = x.shape
    tm = min(MOE_TILE, t)
    per_b = seq // tm if seq >= tm else 1
    kern = functools.partial(_combine_kernel, tm=tm)
    last = t // tm - 1
    return pl.pallas_call(
        kern,
        grid=(t // tm,),
        in_specs=[pl.BlockSpec((1, 1, TABLE_W), lambda i: (i, 0, 0), memory_space=pltpu.SMEM),
                  pl.BlockSpec((1, 1, TABLE_W), lambda i: (jnp.minimum(i + 1, last), 0, 0), memory_space=pltpu.SMEM),
                  pl.BlockSpec((1, 1, 2 * tm), lambda i: (i, 0, 0), memory_space=pltpu.SMEM),
                  pl.BlockSpec(memory_space=pl.ANY),
                  pl.BlockSpec((tm, d), lambda i: (i, 0)),
                  pl.BlockSpec((None, 1, d), lambda i: (i // per_b, 0, 0)),
                  pl.BlockSpec((tm, ROUTER_COLS), lambda i: (i, 0))],
        out_specs=pl.BlockSpec((tm, d), lambda i: (i, 0)),
        out_shape=jax.ShapeDtypeStruct((t, d), F32),
        scratch_shapes=[pltpu.VMEM((2, _sorted_rows(tm), d), F32), pltpu.VMEM((2, tm, d), F32),
                        pltpu.SemaphoreType.DMA((2,))],
        compiler_params=_cparams(("arbitrary",)),
        name="moe_combine",
    )(table, table, slots, ys, x, gate_f, gates)


def _moe(h, logits, x1, gate_f, w_gate, w_up, w_down, layer, seq):
    rows = _moe_rows(h.shape[0])
    route, gates, seg, totals = _route(logits)
    table, slots, fill, block_e, n_used = _dispatch_plan(route, seg, totals[:, 0], rows // MOE_BLK)
    xs = _dispatch(h, table, slots, fill, rows)
    ys = _experts(xs, block_e, n_used, w_gate, w_up, w_down, layer)
    return _combine(ys, table, slots, x1, gate_f, gates, seq)


def _router_weights(w_group, b_group, w_expert, b_expert):
    d = w_group.shape[0]
    w = jnp.zeros((d, ROUTER_COLS), F32)
    w = w.at[:, 0:MOE_GROUPS].set(w_group.astype(F32))
    w = w.at[:, EXPERT_COL0:EXPERT_COL0 + MOE_EXPERTS].set(w_expert.astype(F32))
    b = jnp.zeros((1, ROUTER_COLS), F32)
    b = b.at[0, 0:MOE_GROUPS].set(b_group.astype(F32))
    b = b.at[0, EXPERT_COL0:EXPERT_COL0 + MOE_EXPERTS].set(b_expert.astype(F32))
    return w, b


def _lambda_init(layer):
    return 0.8 - 0.6 * math.exp(-0.3 * layer)


def kernel(x, c, rel_bias, ada_w, ada_b, norm_attn, norm_ffn, a_w_qkv, a_q_norm, a_k_norm, a_lambda, a_subln, a_w_o, kv_norm, kv_ada_w, kv_ada_b, kv_w, kv_k_norm, b_w_q, b_q_norm, b_w_o, moe_w_group, moe_b_group, moe_w_expert, moe_b_expert, moe_w_gate, moe_w_up, moe_w_down):
    batch, seq, d = x.shape
    t = batch * seq
    scale = HEAD_DIM ** -0.5
    n_bw = N_GROUPS * B_WIDTH

    c_pad = jnp.zeros((8, d), F32).at[:batch].set(c.astype(F32))
    mod = _modulation(c_pad, ada_w, ada_b)[:, :batch]
    kv_mod = _modulation(c_pad, kv_ada_w[None], kv_ada_b[None])[0, :batch]

    def part(m, i):
        return m[:, i * d:(i + 1) * d].reshape(batch, 1, d)

    a_bias, nd = _attn_a_bias_diags(rel_bias, seq, min(ATT_TILE, seq))
    b_bias = [_attn_b_bias_tiles(rel_bias, window // dil, dil) for window, dil in B_GROUPS]
    xf = x.reshape(t, d).astype(F32)
    kv = None
    for layer in range(DEPTH):
        m = mod[layer]
        sh_a, sc_a, g_a, sh_f, sc_f, g_f = (part(m, i) for i in range(6))
        w_r, b_r = _router_weights(moe_w_group[layer], moe_b_group[layer], moe_w_expert[layer], moe_b_expert[layer])
        if layer < N_A_LAYERS:
            qk_gain = jnp.concatenate([jnp.tile(a_q_norm[layer].astype(F32) * (scale * LOG2_E), 2 * N_HEADS),
                                       jnp.tile(a_k_norm[layer].astype(F32), 2 * N_HEADS),
                                       jnp.ones((N_HEADS * LANES,), F32)])
            qkv = _norm_proj(xf, norm_attn[layer], sh_a, sc_a, a_w_qkv[layer].astype(BF16), qk_gain,
                             2 * N_HEADS * LANES, seq, BF16)
            att = _attn_a(qkv, a_bias, nd, a_lambda[layer].astype(F32), a_subln[layer].astype(F32),
                          _lambda_init(layer), batch, seq)
            wo = a_w_o[layer].astype(BF16)
        else:
            j = layer - N_A_LAYERS
            if kv is None:
                k_gain = jnp.concatenate([jnp.tile(kv_k_norm.astype(F32), (1, N_HEADS)).reshape(-1),
                                          jnp.ones((n_bw,), F32)])
                kv = _norm_proj(xf, kv_norm, part(kv_mod, 0), part(kv_mod, 1), kv_w.astype(BF16), k_gain, n_bw, seq, F32)
            q_gain = jnp.tile(b_q_norm[j].astype(F32) * (scale * LOG2_E), (1, N_HEADS)).reshape(-1)
            q = _norm_proj(xf, norm_attn[layer], sh_a, sc_a, b_w_q[j].astype(BF16), q_gain, n_bw, seq, F32)
            att = _attn_b(q, kv, b_bias, batch, seq)
            wo = b_w_o[j].astype(BF16)
        x1, hf, logits = _out_proj(att, wo, xf, g_a, norm_ffn[layer], sh_f, sc_f, w_r, b_r, seq)
        xf = _moe(hf, logits, x1, g_f, moe_w_gate, moe_w_up, moe_w_down, layer, seq)
    return xf.reshape(batch, seq, d).astype(x.dtype)
```

```python
import functools
import math

import jax
import jax.numpy as jnp
import numpy as np
from jax import lax
from jax.experimental import pallas as pl
from jax.experimental.pallas import tpu as pltpu

F32 = jnp.float32
BF16 = jnp.bfloat16

D_MODEL = 1024
DEPTH = 4
N_A_LAYERS = 2
N_HEADS = 8
HEAD_DIM = 64
B_GROUPS = ((128, 1), (512, 4), (2048, 16))
N_GROUPS = len(B_GROUPS)
B_WIDTH = N_HEADS * HEAD_DIM
REL_BUCKETS = 32
REL_MAX_EXACT = 16
REL_MAX_DIST = 2048
MOE_GROUPS = 4
MOE_EPG = 8
MOE_EXPERTS = MOE_GROUPS * MOE_EPG
MOE_HIDDEN = 512
EPS = 1e-6
NEG_INF = float("-inf")
LOG2_E = math.log2(math.e)

LANES = 128
MXU_DIM = 256
ROUTER_COLS = 128
EXPERT_COL0 = 8

TOK_TILE = 512
ATT_TILE = 512
B_TOKENS = 2048
BQ_SUB = 128
MOE_BLK = 256
MOE_TILE = 512
SEG_ALIGN = 8
SEG_CHUNK = 8
TABLE_W = 256
VMEM_LIMIT = 56 * 1024 * 1024


def _cparams(sem):
    return pltpu.CompilerParams(dimension_semantics=sem, vmem_limit_bytes=VMEM_LIMIT)


def _bucket_table(max_dist):
    n = np.arange(max_dist + 1)
    nf = np.maximum(n, 1).astype(np.float64)
    large = REL_MAX_EXACT + (np.log(nf / REL_MAX_EXACT) / math.log(REL_MAX_DIST / REL_MAX_EXACT)
                             * (REL_BUCKETS - REL_MAX_EXACT)).astype(np.int64)
    large = np.minimum(large, REL_BUCKETS - 1)
    return np.where(n < REL_MAX_EXACT, n, large).astype(np.int32)


def _toeplitz(v, n, m):
    length = v.shape[-1]
    assert length >= n + m - 1 and length - 1 >= m
    lead = v.shape[:-1]
    flat = jnp.tile(v, (1,) * len(lead) + (n,))[..., :n * (length - 1)]
    return flat.reshape(lead + (n, length - 1))[..., :m]


def _diag_values(rel_bias, dist, ok):
    table = _bucket_table(int(dist.max()))
    vals = rel_bias.astype(F32)[table[np.clip(dist, 0, None)]]
    vals = jnp.where(ok[..., None], vals, NEG_INF)
    return jnp.moveaxis(vals, -1, 0)


def _attn_a_bias_diags(rel_bias, seq, tile):
    table = _bucket_table(seq)
    last_start = int(np.argmax(table == REL_BUCKETS - 1))
    nd = 0
    while nd * tile - (tile - 1) < last_start and nd * tile < seq:
        nd += 1
    nd += 1
    length = 2 * tile
    u = np.arange(length)
    u = np.where(u < tile, u, u - length)
    dist = np.arange(nd)[:, None] * tile - u[None, :]
    return _diag_values(rel_bias, dist, dist >= 0) * LOG2_E, nd


def _attn_b_bias_tiles(rel_bias, band, dil):
    length = 3 * BQ_SUB
    u = np.arange(length)
    u = np.where(u < 2 * BQ_SUB, u, u - length)
    rel = BQ_SUB - u
    ok = (rel >= 0) & (rel <= band)
    general = _toeplitz(_diag_values(rel_bias, dil * rel, ok) * LOG2_E, BQ_SUB, 2 * BQ_SUB)
    no_prev = np.arange(2 * BQ_SUB)[None, None, :] >= BQ_SUB
    return jnp.stack([jnp.where(no_prev, general, NEG_INF), general])


def _group_sum_matrix():
    r = np.arange(MXU_DIM)
    return jnp.asarray((r[:, None] // HEAD_DIM) == (r[None, :] // HEAD_DIM), BF16)


def _mod_kernel(c_ref, w_ref, b_ref, o_ref):
    c = c_ref[...]
    c_act = c * (1.0 / (1.0 + jnp.exp(-c)))
    o_ref[...] = jnp.dot(c_act, w_ref[...], preferred_element_type=F32,
                         precision=lax.Precision.HIGHEST) + b_ref[...]


def _modulation(c_pad, w, b, tn=1024):
    nl, d, n = w.shape
    return pl.pallas_call(
        _mod_kernel,
        grid=(nl, n // tn),
        in_specs=[pl.BlockSpec((8, d), lambda l, j: (0, 0)),
                  pl.BlockSpec((None, d, tn), lambda l, j: (l, 0, j)),
                  pl.BlockSpec((None, 1, tn), lambda l, j: (l, 0, j))],
        out_specs=pl.BlockSpec((None, 8, tn), lambda l, j: (l, 0, j)),
        out_shape=jax.ShapeDtypeStruct((nl, 8, n), F32),
        compiler_params=_cparams(("parallel", "parallel")),
        name="adaln_mod",
    )(c_pad, w, b.reshape(nl, 1, n))


def _rms(x, gain):
    return x * lax.rsqrt(jnp.mean(x * x, axis=-1, keepdims=True) + EPS) * gain


def _proj_kernel(x_ref, g_ref, sh_ref, sc_ref, w_ref, hg_ref, bd_ref, o_ref, *, n_out, n_norm):
    h = _rms(x_ref[...], g_ref[...]) * (1.0 + sc_ref[...]) + sh_ref[...]
    hb = h.astype(BF16)
    wide = 2 * MXU_DIM
    for c0 in range(0, n_out, wide):
        a = jnp.dot(hb, w_ref[:, c0:c0 + wide], preferred_element_type=F32)
        if c0 < n_norm:
            sq = (a * a).astype(BF16)
            ms = jnp.concatenate([jnp.dot(sq[:, :MXU_DIM], bd_ref[...], preferred_element_type=F32),
                                  jnp.dot(sq[:, MXU_DIM:], bd_ref[...], preferred_element_type=F32)], axis=1)
            a = a * lax.rsqrt(ms * (1.0 / HEAD_DIM) + EPS) * hg_ref[:, c0:c0 + wide]
        o_ref[:, c0:c0 + wide] = a.astype(o_ref.dtype)


def _norm_proj(x, gain, shift, scale, w_bf16, head_gain, n_norm, seq, out_dtype):
    t, d = x.shape
    n_out = w_bf16.shape[1]
    tm = min(TOK_TILE, seq)
    per_b = seq // tm
    kern = functools.partial(_proj_kernel, n_out=n_out, n_norm=n_norm)
    return pl.pallas_call(
        kern,
        grid=(t // tm,),
        in_specs=[pl.BlockSpec((tm, d), lambda i: (i, 0)),
                  pl.BlockSpec((1, d), lambda i: (0, 0)),
                  pl.BlockSpec((None, 1, d), lambda i: (i // per_b, 0, 0)),
                  pl.BlockSpec((None, 1, d), lambda i: (i // per_b, 0, 0)),
                  pl.BlockSpec((d, n_out), lambda i: (0, 0)),
                  pl.BlockSpec((1, n_out), lambda i: (0, 0)),
                  pl.BlockSpec((MXU_DIM, MXU_DIM), lambda i: (0, 0))],
        out_specs=pl.BlockSpec((tm, n_out), lambda i: (i, 0)),
        out_shape=jax.ShapeDtypeStruct((t, n_out), out_dtype),
        compiler_params=_cparams(("parallel",)),
        name="norm_proj",
    )(x, gain.reshape(1, d), shift, scale, w_bf16, head_gain.reshape(1, n_out), _group_sum_matrix())


def _lane_tile(a, reps):
    return jnp.concatenate([a] * reps, axis=1)


def _attn_a_kernel(q_ref, k_ref, v_ref, diag_ref, lam_ref, g_ref, o_ref, bias_ref, m_sc, acc_sc,
                   *, tile, nd, lam_init):
    qi = pl.program_id(2)

    @pl.when((pl.program_id(1) == 0) & (qi == 0))
    def _():
        for d in range(nd):
            row = jnp.broadcast_to(diag_ref[d:d + 1, :], (tile, 2 * tile))
            bias_ref[d] = pltpu.roll(row, 0, 1, stride=1, stride_axis=0)[:, :tile]

    q = q_ref[...]
    lane = lax.broadcasted_iota(jnp.int32, q.shape, 1)
    zero = jnp.zeros_like(q)
    qq = jnp.concatenate([jnp.where(lane < HEAD_DIM, q, zero),
                          jnp.where(lane >= HEAD_DIM, q, zero)], axis=0)
    m_sc[...] = jnp.full(m_sc.shape, NEG_INF, F32)
    acc_sc[...] = jnp.zeros(acc_sc.shape, F32)
    ones = jnp.ones((tile, LANES), BF16)

    def kv_tile(j):
        start = pl.multiple_of(j * tile, tile)
        k = k_ref[pl.ds(start, tile), :]
        v_ones = jnp.concatenate([v_ref[pl.ds(start, tile), :], ones], axis=1)
        s = lax.dot_general(qq, k, (((1,), (1,)), ((), ())), preferred_element_type=F32)
        bias = bias_ref[jnp.minimum(qi - j, nd - 1)]
        for c in range(2):
            sc = s[c * tile:(c + 1) * tile] + bias
            m_prev = m_sc[c]
            m_new = jnp.maximum(m_prev, jnp.max(sc, axis=-1, keepdims=True))
            alpha = jnp.exp2(m_prev - m_new)
            p = jnp.exp2(sc - _lane_tile(m_new, tile // LANES))
            acc_sc[c] = (_lane_tile(alpha, 2) * acc_sc[c]
                         + jnp.dot(p.astype(BF16), v_ones, preferred_element_type=F32))
            m_sc[c] = m_new

    def kv_pair(i, carry):
        kv_tile(2 * i)
        kv_tile(2 * i + 1)
        return carry

    n_kv = qi + 1
    lax.fori_loop(0, n_kv // 2, kv_pair, 0)

    @pl.when(n_kv % 2 == 1)
    def _():
        kv_tile(qi)

    lp = lam_ref[...]
    lam = (jnp.exp(jnp.sum(lp[0:1] * lp[1:2], axis=-1, keepdims=True))
           - jnp.exp(jnp.sum(lp[2:3] * lp[3:4], axis=-1, keepdims=True)) + lam_init)
    a0, a1 = acc_sc[0], acc_sc[1]
    o = a0[:, :LANES] / a0[:, LANES:] - lam * (a1[:, :LANES] / a1[:, LANES:])
    o_ref[...] = (_rms(o, g_ref[...]) * (1.0 - lam_init)).astype(BF16)


def _attn_a(qkv, bias_diags, nd, lam_p, subln, lam_init, batch, seq):
    tile = min(ATT_TILE, seq)
    nq = seq // tile
    qkv3 = qkv.reshape(batch, seq, 3 * N_HEADS * LANES)
    kern = functools.partial(_attn_a_kernel, tile=tile, nd=nd, lam_init=lam_init)
    out = pl.pallas_call(
        kern,
        grid=(N_HEADS, batch, nq),
        in_specs=[pl.BlockSpec((None, tile, LANES), lambda h, b, i: (b, i, h)),
                  pl.BlockSpec((None, seq, LANES), lambda h, b, i: (b, 0, N_HEADS + h)),
                  pl.BlockSpec((None, seq, LANES), lambda h, b, i: (b, 0, 2 * N_HEADS + h)),
                  pl.BlockSpec((None, nd, 2 * tile), lambda h, b, i: (h, 0, 0)),
                  pl.BlockSpec((4, HEAD_DIM), lambda h, b, i: (0, 0)),
                  pl.BlockSpec((1, LANES), lambda h, b, i: (0, 0))],
        out_specs=pl.BlockSpec((None, tile, LANES), lambda h, b, i: (b, i, h)),
        out_shape=jax.ShapeDtypeStruct((batch, seq, N_HEADS * LANES), BF16),
        scratch_shapes=[pltpu.VMEM((nd, tile, tile), F32), pltpu.VMEM((2, tile, LANES), F32),
                        pltpu.VMEM((2, tile, 2 * LANES), F32)],
        compiler_params=_cparams(("arbitrary", "arbitrary", "arbitrary")),
        name="attn_a",
    )(qkv3, qkv3, qkv3, bias_diags, lam_p, subln.reshape(1, LANES))
    return out.reshape(batch * seq, N_HEADS * LANES)


def _attn_b_kernel(*refs, tile):
    ins, o_ref, scratch = refs[:6 * N_GROUPS], refs[6 * N_GROUPS], refs[6 * N_GROUPS + 1:]
    first_tile = pl.program_id(1) == 0
    lane = lax.broadcasted_iota(jnp.int32, (BQ_SUB, LANES), 1)
    low = lane < HEAD_DIM
    ones = jnp.ones((2 * BQ_SUB, LANES), BF16)

    for g, (_, dil) in enumerate(B_GROUPS):
        q_ref, k_ref, v_ref, kp_ref, vp_ref, bias_ref = ins[6 * g:6 * g + 6]
        og_ref, lg_ref = scratch[2 * g:2 * g + 2]
        span = BQ_SUB * dil

        def residue(base, dil=dil):
            return pl.ds(base, BQ_SUB) if dil == 1 else pl.ds(base, BQ_SUB, stride=dil)

        for sb in range(tile // span):
            for r in range(dil):
                base = sb * span + r
                rows = residue(base)
                qp = q_ref[rows, :].astype(BF16)
                if sb == 0:
                    k_prev, v_prev = kp_ref[residue(r), :], vp_ref[residue(r), :]
                    variant = jnp.where(first_tile, 0, 1)
                else:
                    k_prev, v_prev = k_ref[residue(base - span), :], v_ref[residue(base - span), :]
                    variant = 1
                kk = jnp.concatenate([k_prev, k_ref[rows, :]], axis=0).astype(BF16)
                vv = jnp.concatenate([v_prev, v_ref[rows, :]], axis=0).astype(BF16)
                v_ones = jnp.concatenate([vv, ones], axis=1)
                zero = jnp.zeros_like(qp)
                qq = jnp.concatenate([jnp.where(low, qp, zero), jnp.where(low, zero, qp)], axis=0)
                s = lax.dot_general(qq, kk, (((1,), (1,)), ((), ())), preferred_element_type=F32)
                outs, lses = [], []
                for c in range(2):
                    sc = s[c * BQ_SUB:(c + 1) * BQ_SUB] + bias_ref[variant, c]
                    m = jnp.max(sc, axis=-1, keepdims=True)
                    p = jnp.exp2(sc - m)
                    acc = jnp.dot(p.astype(BF16), v_ones, preferred_element_type=F32)
                    outs.append(acc[:, :LANES] / acc[:, LANES:])
                    lses.append(m + jnp.log2(acc[:, LANES:]))
                og_ref[rows, :] = jnp.where(low, outs[0], outs[1])
                lg_ref[rows, :] = jnp.where(low, lses[0], lses[1])

    l0, l1, l2 = scratch[1][...], scratch[3][...], scratch[5][...]
    m = jnp.maximum(jnp.maximum(l0, l1), l2)
    e0, e1, e2 = jnp.exp2(l0 - m), jnp.exp2(l1 - m), jnp.exp2(l2 - m)
    z = e0 + e1 + e2
    o_ref[...] = ((e0 / z) * scratch[0][...] + (e1 / z) * scratch[2][...] + (e2 / z) * scratch[4][...]).astype(BF16)


def _attn_b(q, kv, bias_tiles, batch, seq):
    tile = min(B_TOKENS, seq)
    pairs = N_HEADS // 2
    q3 = q.reshape(batch, seq, N_GROUPS * B_WIDTH)
    kv3 = kv.reshape(batch, seq, 2 * N_GROUPS * B_WIDTH)
    operands, in_specs = [], []
    for g, (_, dil) in enumerate(B_GROUPS):
        span = BQ_SUB * dil
        assert tile % span == 0
        k_col = lambda hp, g=g: g * pairs + hp
        v_col = lambda hp, g=g: (N_GROUPS + g) * pairs + hp
        prev = lambda n, per=tile // span: jnp.maximum(n * per - 1, 0)
        operands += [q3, kv3, kv3, kv3, kv3, bias_tiles[g]]
        in_specs += [pl.BlockSpec((None, tile, LANES), lambda b, n, hp, c=k_col: (b, n, c(hp))),
                     pl.BlockSpec((None, tile, LANES), lambda b, n, hp, c=k_col: (b, n, c(hp))),
                     pl.BlockSpec((None, tile, LANES), lambda b, n, hp, c=v_col: (b, n, c(hp))),
                     pl.BlockSpec((None, span, LANES), lambda b, n, hp, c=k_col, p=prev: (b, p(n), c(hp))),
                     pl.BlockSpec((None, span, LANES), lambda b, n, hp, c=v_col, p=prev: (b, p(n), c(hp))),
                     pl.BlockSpec((2, 2, BQ_SUB, 2 * BQ_SUB), lambda b, n, hp: (0, hp, 0, 0))]
    out = pl.pallas_call(
        functools.partial(_attn_b_kernel, tile=tile),
        grid=(batch, seq // tile, pairs),
        in_specs=in_specs,
        out_specs=pl.BlockSpec((None, tile, LANES), lambda b, n, hp: (b, n, hp)),
        out_shape=jax.ShapeDtypeStruct((batch, seq, B_WIDTH), BF16),
        scratch_shapes=[pltpu.VMEM((tile, LANES), F32)] * (2 * N_GROUPS),
        compiler_params=_cparams(("parallel", "parallel", "parallel")),
        name="attn_b",
    )(*operands)
    return out.reshape(batch * seq, B_WIDTH)


def _split_bf16(a):
    hi = a.astype(BF16)
    return hi, (a - hi.astype(F32)).astype(BF16)


def _out_tail(o_bf16, wo_ref, x_ref, ga_ref, gf_ref, shf_ref, scf_ref, wr_ref, br_ref, x1_ref, hf_ref, lg_ref):
    y = jnp.dot(o_bf16, wo_ref[...], preferred_element_type=F32)
    x1 = x_ref[...] + ga_ref[...] * y
    x1_ref[...] = x1
    hf = _rms(x1, gf_ref[...]) * (1.0 + scf_ref[...]) + shf_ref[...]
    hf_ref[...] = hf
    hi, lo = _split_bf16(hf)
    both = jnp.dot(hi, wr_ref[...], preferred_element_type=F32)
    lg_ref[...] = (both[:, :ROUTER_COLS] + jnp.dot(lo, wr_ref[:, :ROUTER_COLS], preferred_element_type=F32)
                   + both[:, ROUTER_COLS:] + br_ref[...])


def _out_kernel(o_ref, *rest):
    _out_tail(o_ref[...], *rest)


def _out_proj(att, wo_bf16, x, gate_a, gain_f, shift_f, scale_f, w_router, b_router, seq):
    t, d = x.shape
    tm = min(TOK_TILE, seq)
    per_b = seq // tm
    row = lambda i: (i, 0)
    fixed = lambda i: (0, 0)
    per_batch = lambda i: (i // per_b, 0, 0)
    return pl.pallas_call(
        _out_kernel,
        grid=(t // tm,),
        in_specs=[pl.BlockSpec((tm, att.shape[1]), row),
                  pl.BlockSpec(wo_bf16.shape, fixed),
                  pl.BlockSpec((tm, d), row),
                  pl.BlockSpec((None, 1, d), per_batch),
                  pl.BlockSpec((1, d), fixed),
                  pl.BlockSpec((None, 1, d), per_batch),
                  pl.BlockSpec((None, 1, d), per_batch),
                  pl.BlockSpec((d, 2 * ROUTER_COLS), fixed),
                  pl.BlockSpec((1, ROUTER_COLS), fixed)],
        out_specs=[pl.BlockSpec((tm, d), row), pl.BlockSpec((tm, d), row),
                   pl.BlockSpec((tm, ROUTER_COLS), row)],
        out_shape=[jax.ShapeDtypeStruct((t, d), F32), jax.ShapeDtypeStruct((t, d), F32),
                   jax.ShapeDtypeStruct((t, ROUTER_COLS), F32)],
        compiler_params=_cparams(("parallel",)),
        name="out_proj",
    )(att, wo_bf16, x, gate_a, gain_f.reshape(1, d), shift_f, scale_f,
      jnp.concatenate(_split_bf16(w_router), axis=1), b_router)


def _route_kernel(lg_ref, tri_ref, eid_ref, gate_ref, seg_ref, cnt_ref, cnt_sc):
    lt = lg_ref[...].T
    best, g_idx = lt[0:1], jnp.zeros((1, lt.shape[1]), jnp.int32)
    for g in range(1, MOE_GROUPS):
        upd = lt[g:g + 1] > best
        best = jnp.where(upd, lt[g:g + 1], best)
        g_idx = jnp.where(upd, g, g_idx)
    denom = jnp.zeros_like(best)
    for g in range(MOE_GROUPS):
        denom = denom + jnp.exp(lt[g:g + 1] - best)
    g_w = 1.0 / denom
    e_sel = lt[EXPERT_COL0:EXPERT_COL0 + MOE_EPG]
    for g in range(1, MOE_GROUPS):
        r0 = EXPERT_COL0 + g * MOE_EPG
        e_sel = jnp.where(g_idx == g, lt[r0:r0 + MOE_EPG], e_sel)

    def first_max(vals):
        v, i = vals[0:1], jnp.zeros((1, vals.shape[1]), jnp.int32)
        for e in range(1, MOE_EPG):
            upd = vals[e:e + 1] > v
            v = jnp.where(upd, vals[e:e + 1], v)
            i = jnp.where(upd, e, i)
        return v, i

    v1, i1 = first_max(e_sel)
    row = lax.broadcasted_iota(jnp.int32, e_sel.shape, 0)
    v2, i2 = first_max(jnp.where(row == i1, NEG_INF, e_sel))
    e2 = jnp.exp(v2 - v1)
    w1 = 1.0 / (1.0 + e2)
    w2 = e2 / (1.0 + e2)
    e1 = g_idx * MOE_EPG + i1
    e2 = g_idx * MOE_EPG + i2

    @pl.when(pl.program_id(0) == 0)
    def _():
        cnt_sc[...] = jnp.zeros(cnt_sc.shape, F32)

    tm = lt.shape[1]
    erows = lax.broadcasted_iota(jnp.int32, (MOE_EXPERTS, tm), 0)
    ohs, pres = [], []
    for e_k in (e1, e2):
        oh = (erows == e_k).astype(F32)
        ohs.append(oh)
        pres.append(jnp.dot(oh.astype(BF16), tri_ref[...], preferred_element_type=F32))
    n0 = pres[0][:, tm - 1:]
    n_tot = n0 + pres[1][:, tm - 1:]
    chunks = jnp.floor((n_tot + (SEG_ALIGN - 1.0)) * (1.0 / SEG_ALIGN))
    er = lax.broadcasted_iota(jnp.int32, (MOE_EXPERTS, MOE_EXPERTS), 0)
    ec = lax.broadcasted_iota(jnp.int32, (MOE_EXPERTS, MOE_EXPERTS), 1)
    before = jnp.dot((ec < er).astype(BF16), jnp.broadcast_to(chunks, (MOE_EXPERTS, LANES)).astype(BF16),
                     preferred_element_type=F32)
    seg_start = before[:, 0:1] * SEG_ALIGN
    slot0 = jnp.sum(ohs[0] * (seg_start + pres[0] - 1.0), axis=0, keepdims=True)
    slot1 = jnp.sum(ohs[1] * (seg_start + n0 + pres[1] - 1.0), axis=0, keepdims=True)
    seg_off = cnt_sc[...]
    cnt_sc[...] = seg_off + chunks * SEG_ALIGN
    cnt_ref[...] = jnp.broadcast_to(cnt_sc[...], cnt_ref.shape).astype(jnp.int32)
    scol = lax.broadcasted_iota(jnp.int32, seg_ref.shape, 1)
    seg_ref[...] = jnp.where(scol == 0, chunks, jnp.where(scol == 1, seg_start, jnp.where(
        scol == 2, seg_off, 0.0))).astype(jnp.int32)

    erow = lax.broadcasted_iota(jnp.int32, eid_ref.shape, 0)
    eid_ref[...] = jnp.where(erow == 0, e1, jnp.where(erow == 1, e2, jnp.where(
        erow == 2, slot0.astype(jnp.int32), jnp.where(erow == 3, slot1.astype(jnp.int32), 0))))
    grow = lax.broadcasted_iota(jnp.int32, lt.shape, 0)
    gates = jnp.where(grow == 0, g_w * w1, jnp.where(grow == 1, g_w * w2, 0.0))
    gate_ref[...] = gates.T


def _route(logits):
    t = logits.shape[0]
    tm = min(MOE_TILE, t)
    n_tiles = t // tm
    r = np.arange(tm)
    tri = jnp.asarray(r[:, None] <= r[None, :], BF16)
    return pl.pallas_call(
        _route_kernel,
        grid=(n_tiles,),
        in_specs=[pl.BlockSpec((tm, ROUTER_COLS), lambda i: (i, 0)),
                  pl.BlockSpec((tm, tm), lambda i: (0, 0))],
        out_specs=[pl.BlockSpec((8, tm), lambda i: (0, i)), pl.BlockSpec((tm, ROUTER_COLS), lambda i: (i, 0)),
                   pl.BlockSpec((MOE_EXPERTS, LANES), lambda i: (i, 0)),
                   pl.BlockSpec((MOE_EXPERTS, LANES), lambda i: (0, 0))],
        out_shape=[jax.ShapeDtypeStruct((8, t), jnp.int32), jax.ShapeDtypeStruct((t, ROUTER_COLS), F32),
                   jax.ShapeDtypeStruct((n_tiles * MOE_EXPERTS, LANES), jnp.int32),
                   jax.ShapeDtypeStruct((MOE_EXPERTS, LANES), jnp.int32)],
        scratch_shapes=[pltpu.VMEM((MOE_EXPERTS, 1), F32)],
        compiler_params=_cparams(("arbitrary",)),
        name="moe_route",
    )(logits, tri)


def _moe_rows(t):
    n_tiles = t // min(MOE_TILE, t)
    rows = 2 * t + n_tiles * MOE_EXPERTS * (SEG_ALIGN - 1) + MOE_EXPERTS * (MOE_BLK - 1)
    return -(-rows // MOE_BLK) * MOE_BLK


def _chunk_rows(first, count, row0, n_slots):
    c = jnp.arange(n_slots, dtype=jnp.int32)
    run = jnp.sum((first + count)[..., None, :] <= c[:, None], axis=-1)
    hit = run[..., None] == jnp.arange(first.shape[-1], dtype=jnp.int32)
    base = jnp.sum(jnp.where(hit, (row0 - SEG_CHUNK * first)[..., None, :], 0), axis=-1)
    return jnp.where(run < first.shape[-1], base + SEG_CHUNK * c, 0)


def _dispatch_plan(route, seg, totals):
    n_tiles = seg.shape[0] // MOE_EXPERTS
    tm = route.shape[1] // n_tiles
    n_slots = _sorted_rows(tm) // SEG_CHUNK
    padded = (totals + MOE_BLK - 1) // MOE_BLK * MOE_BLK
    pad_end = jnp.cumsum(padded)
    pad_start = pad_end - padded
    n_used = pad_end[-1] // MOE_BLK
    seg = seg.reshape(n_tiles, MOE_EXPERTS, LANES)
    chunks, first, seg_off = seg[:, :, 0], seg[:, :, 1] // SEG_CHUNK, seg[:, :, 2]
    dst = _chunk_rows(first, chunks, pad_start[None, :] + seg_off, n_slots)
    fill = (padded - totals) // SEG_CHUNK
    fill_first = jnp.cumsum(fill) - fill
    fill_dst = _chunk_rows(fill_first, fill, pad_start + totals, MOE_EXPERTS * (MOE_BLK // SEG_CHUNK))
    tail = jnp.stack([jnp.broadcast_to(n_used, (n_tiles,)), jnp.broadcast_to(jnp.sum(fill), (n_tiles,)),
                      jnp.sum(chunks, axis=1)], axis=1)
    table = jnp.concatenate([dst, jnp.zeros((n_tiles, TABLE_W - 3 - n_slots), jnp.int32), tail], axis=1)
    slots = route[2:4].reshape(2, n_tiles, tm).transpose(1, 0, 2).reshape(n_tiles, 1, 2 * tm)
    return (table.astype(jnp.int32).reshape(n_tiles, 1, TABLE_W), slots, fill_dst.astype(jnp.int32).reshape(1, 1, -1),
            pad_start.astype(jnp.int32), (padded // MOE_BLK).astype(jnp.int32))


def _chunk_copy(hbm_ref, hbm_row, sorted_ref, chunk, sem, to_hbm):
    local = sorted_ref.at[pl.ds(pl.multiple_of(chunk * SEG_CHUNK, SEG_CHUNK), SEG_CHUNK), :]
    remote = hbm_ref.at[pl.ds(pl.multiple_of(hbm_row, SEG_CHUNK), SEG_CHUNK), :]
    return pltpu.make_async_copy(local, remote, sem) if to_hbm else pltpu.make_async_copy(remote, local, sem)


def _start_chunks(table_ref, hbm_ref, sorted_ref, sem, to_hbm):
    def chunk(c, carry):
        _chunk_copy(hbm_ref, table_ref[0, 0, c], sorted_ref, c, sem, to_hbm).start()
        return carry

    lax.fori_loop(0, table_ref[0, 0, TABLE_W - 1], chunk, 0)


def _wait_chunks(count, hbm_ref, sorted_ref, sem, to_hbm):
    def chunk(c, carry):
        _chunk_copy(hbm_ref, 0, sorted_ref, 0, sem, to_hbm).wait()
        return carry

    lax.fori_loop(0, count, chunk, 0)


def _zero_fill(table_ref, fill_ref, xs_ref, zero_sc, sem, n_blocks, wait):
    def finish(copy):
        if wait:
            copy.wait()
        else:
            copy.start()

    def chunk(c, carry):
        row = pl.multiple_of(fill_ref[0, 0, c], SEG_CHUNK)
        finish(pltpu.make_async_copy(zero_sc.at[pl.ds(0, SEG_CHUNK), :], xs_ref.at[pl.ds(row, SEG_CHUNK), :], sem))
        return carry

    def block(b, carry):
        row = pl.multiple_of(b * MOE_BLK, MOE_BLK)
        finish(pltpu.make_async_copy(zero_sc, xs_ref.at[pl.ds(row, MOE_BLK), :], sem))
        return carry

    lax.fori_loop(0, table_ref[0, 0, TABLE_W - 2], chunk, 0)
    lax.fori_loop(table_ref[0, 0, TABLE_W - 3], n_blocks, block, 0)


def _dispatch_kernel(table_ref, slot_ref, fill_ref, h_ref, xs_ref, sorted_sc, zero_sc, pending_sc, sem, zero_sem,
                     *, tm, n_blocks):
    step = pl.program_id(0)
    first_step = step == 0
    parity = step % 2
    mine, other = sorted_sc.at[parity], sorted_sc.at[1 - parity]

    @pl.when(first_step)
    def _():
        sorted_sc[...] = jnp.zeros(sorted_sc.shape, F32)
        zero_sc[...] = jnp.zeros(zero_sc.shape, F32)
        pending_sc[0] = 0
        _zero_fill(table_ref, fill_ref, xs_ref, zero_sc, zero_sem, n_blocks, wait=False)

    def place(i, carry):
        for k in range(2):
            mine[pl.ds(slot_ref[0, 0, k * tm + i], 1), :] = h_ref[pl.ds(i, 1), :]
        return carry

    lax.fori_loop(0, tm, place, 0, unroll=8)
    _start_chunks(table_ref, xs_ref, mine, sem.at[parity], to_hbm=True)
    _wait_chunks(pending_sc[0], xs_ref, other, sem.at[1 - parity], to_hbm=True)
    pending_sc[0] = table_ref[0, 0, TABLE_W - 1]

    @pl.when(step == pl.num_programs(0) - 1)
    def _():
        _wait_chunks(pending_sc[0], xs_ref, mine, sem.at[parity], to_hbm=True)

    @pl.when(first_step)
    def _():
        _zero_fill(table_ref, fill_ref, xs_ref, zero_sc, zero_sem, n_blocks, wait=True)


def _sorted_rows(tm):
    return -(-(2 * tm + MOE_EXPERTS * (SEG_ALIGN - 1)) // SEG_CHUNK) * SEG_CHUNK


def _dispatch(h, table, slots, fill, rows):
    t, d = h.shape
    tm = min(MOE_TILE, t)
    kern = functools.partial(_dispatch_kernel, tm=tm, n_blocks=rows // MOE_BLK)
    return pl.pallas_call(
        kern,
        grid=(t // tm,),
        in_specs=[pl.BlockSpec((1, 1, TABLE_W), lambda i: (i, 0, 0), memory_space=pltpu.SMEM),
                  pl.BlockSpec((1, 1, 2 * tm), lambda i: (i, 0, 0), memory_space=pltpu.SMEM),
                  pl.BlockSpec(fill.shape, lambda i: (0, 0, 0), memory_space=pltpu.SMEM),
                  pl.BlockSpec((tm, d), lambda i: (i, 0))],
        out_specs=pl.BlockSpec(memory_space=pl.ANY),
        out_shape=jax.ShapeDtypeStruct((rows, d), F32),
        scratch_shapes=[pltpu.VMEM((2, _sorted_rows(tm), d), F32), pltpu.VMEM((MOE_BLK, d), F32),
                        pltpu.SMEM((1,), jnp.int32), pltpu.SemaphoreType.DMA((2,)), pltpu.SemaphoreType.DMA(())],
        compiler_params=_cparams(("arbitrary",)),
        name="moe_dispatch",
    )(table, slots, fill, h)


def _expert_kernel(start_ref, nblk_ref, xs_ref, wg_ref, wu_ref, wd_ref, ys_ref,
                   wg_sc, wu_sc, wd_sc, x_buf, y_buf, in_sem, out_sem, *, n_blocks):
    e = pl.program_id(0)
    n = nblk_ref[e]
    row0 = start_ref[e]

    def rows_of(b):
        return pl.ds(pl.multiple_of(row0 + b * MOE_BLK, MOE_BLK), MOE_BLK)

    def x_copy(b, slot):
        return pltpu.make_async_copy(xs_ref.at[rows_of(b), :], x_buf.at[slot], in_sem.at[slot])

    def y_copy(b, slot):
        return pltpu.make_async_copy(y_buf.at[slot], ys_ref.at[rows_of(b), :], out_sem.at[slot])

    @pl.when(n > 0)
    def _():
        x_copy(0, 0).start()
        wg_sc[...] = wg_ref[...].astype(BF16)
        wu_sc[...] = wu_ref[...].astype(BF16)
        wd_sc[...] = wd_ref[...].astype(BF16)

        def block(b, carry):
            slot = b % 2
            x_copy(b, slot).wait()

            @pl.when(b + 1 < n)
            def _():
                x_copy(b + 1, 1 - slot).start()

            @pl.when(b >= 2)
            def _():
                y_copy(b - 2, slot).wait()

            x = x_buf[slot].astype(BF16)
            g = jnp.dot(x, wg_sc[...], preferred_element_type=F32)
            u = jnp.dot(x, wu_sc[...], preferred_element_type=F32)
            a = (g * (1.0 / (1.0 + jnp.exp(-g))) * u).astype(BF16)
            y_buf[slot] = jnp.dot(a, wd_sc[...], preferred_element_type=F32)
            y_copy(b, slot).start()
            return carry

        lax.fori_loop(0, n, block, 0)

        @pl.when(n >= 2)
        def _():
            y_copy(n - 2, n % 2).wait()

        y_copy(n - 1, (n - 1) % 2).wait()

    @pl.when(e == pl.num_programs(0) - 1)
    def _():
        y_buf[0] = jnp.zeros(y_buf.shape[1:], F32)
        first_dead = (row0 + n * MOE_BLK) // MOE_BLK

        def fill(b, carry):
            dst = ys_ref.at[pl.ds(pl.multiple_of(b * MOE_BLK, MOE_BLK), MOE_BLK), :]
            pltpu.make_async_copy(y_buf.at[0], dst, out_sem.at[0]).start()
            return carry

        def drain(b, carry):
            dst = ys_ref.at[pl.ds(pl.multiple_of(b * MOE_BLK, MOE_BLK), MOE_BLK), :]
            pltpu.make_async_copy(y_buf.at[0], dst, out_sem.at[0]).wait()
            return carry

        lax.fori_loop(first_dead, n_blocks, fill, 0)
        lax.fori_loop(first_dead, n_blocks, drain, 0)


def _experts(xs, expert_start, expert_blocks, w_gate, w_up, w_down, layer):
    rows, d = xs.shape
    hid = w_gate.shape[-1]
    expert = lambda e, start, nblk: (layer, e, 0, 0)
    grid_spec = pltpu.PrefetchScalarGridSpec(
        num_scalar_prefetch=2,
        grid=(MOE_EXPERTS,),
        in_specs=[pl.BlockSpec(memory_space=pl.ANY),
                  pl.BlockSpec((None, None, d, hid), expert),
                  pl.BlockSpec((None, None, d, hid), expert),
                  pl.BlockSpec((None, None, hid, d), expert)],
        out_specs=pl.BlockSpec(memory_space=pl.ANY),
        scratch_shapes=[pltpu.VMEM((d, hid), BF16), pltpu.VMEM((d, hid), BF16), pltpu.VMEM((hid, d), BF16),
                        pltpu.VMEM((2, MOE_BLK, d), F32), pltpu.VMEM((2, MOE_BLK, d), F32),
                        pltpu.SemaphoreType.DMA((2,)), pltpu.SemaphoreType.DMA((2,))],
    )
    return pl.pallas_call(
        functools.partial(_expert_kernel, n_blocks=rows // MOE_BLK),
        grid_spec=grid_spec,
        out_shape=jax.ShapeDtypeStruct((rows, d), F32),
        compiler_params=_cparams(("arbitrary",)),
        name="moe_experts",
    )(expert_start, expert_blocks, xs, w_gate, w_up, w_down)


def _combine_kernel(table_ref, next_table_ref, slot_ref, ys_ref, x_ref, gf_ref, gate_ref, o_ref,
                    sorted_sc, buf, sem, *, tm):
    step = pl.program_id(0)
    parity = step % 2
    mine, other = sorted_sc.at[parity], sorted_sc.at[1 - parity]

    @pl.when(step == 0)
    def _():
        _start_chunks(table_ref, ys_ref, mine, sem.at[parity], to_hbm=False)

    @pl.when(step + 1 < pl.num_programs(0))
    def _():
        _start_chunks(next_table_ref, ys_ref, other, sem.at[1 - parity], to_hbm=False)

    _wait_chunks(table_ref[0, 0, TABLE_W - 1], ys_ref, mine, sem.at[parity], to_hbm=False)

    def pick(i, carry):
        for k in range(2):
            buf[k, pl.ds(i, 1), :] = mine[pl.ds(slot_ref[0, 0, k * tm + i], 1), :]
        return carry

    lax.fori_loop(0, tm, pick, 0, unroll=8)
    gates = gate_ref[...]
    y = gates[:, 0:1] * buf[0] + gates[:, 1:2] * buf[1]
    o_ref[...] = x_ref[...] + gf_ref[...] * y


def _combine(ys, table, slots, x, gate_f, gates, seq):
    t, d = x.shape
    tm = min(MOE_TILE, t)
    per_b = seq // tm if seq >= tm else 1
    kern = functools.partial(_combine_kernel, tm=tm)
    last = t // tm - 1
    return pl.pallas_call(
        kern,
        grid=(t // tm,),
        in_specs=[pl.BlockSpec((1, 1, TABLE_W), lambda i: (i, 0, 0), memory_space=pltpu.SMEM),
                  pl.BlockSpec((1, 1, TABLE_W), lambda i: (jnp.minimum(i + 1, last), 0, 0), memory_space=pltpu.SMEM),
                  pl.BlockSpec((1, 1, 2 * tm), lambda i: (i, 0, 0), memory_space=pltpu.SMEM),
                  pl.BlockSpec(memory_space=pl.ANY),
                  pl.BlockSpec((tm, d), lambda i: (i, 0)),
                  pl.BlockSpec((None, 1, d), lambda i: (i // per_b, 0, 0)),
                  pl.BlockSpec((tm, ROUTER_COLS), lambda i: (i, 0))],
        out_specs=pl.BlockSpec((tm, d), lambda i: (i, 0)),
        out_shape=jax.ShapeDtypeStruct((t, d), F32),
        scratch_shapes=[pltpu.VMEM((2, _sorted_rows(tm), d), F32), pltpu.VMEM((2, tm, d), F32),
                        pltpu.SemaphoreType.DMA((2,))],
        compiler_params=_cparams(("arbitrary",)),
        name="moe_combine",
    )(table, table, slots, ys, x, gate_f, gates)


def _moe(h, logits, x1, gate_f, w_gate, w_up, w_down, layer, seq):
    rows = _moe_rows(h.shape[0])
    route, gates, seg, totals = _route(logits)
    table, slots, fill, expert_start, expert_blocks = _dispatch_plan(route, seg, totals[:, 0])
    xs = _dispatch(h, table, slots, fill, rows)
    ys = _experts(xs, expert_start, expert_blocks, w_gate, w_up, w_down, layer)
    return _combine(ys, table, slots, x1, gate_f, gates, seq)


def _router_weights(w_group, b_group, w_expert, b_expert):
    d = w_group.shape[0]
    w = jnp.zeros((d, ROUTER_COLS), F32)
    w = w.at[:, 0:MOE_GROUPS].set(w_group.astype(F32))
    w = w.at[:, EXPERT_COL0:EXPERT_COL0 + MOE_EXPERTS].set(w_expert.astype(F32))
    b = jnp.zeros((1, ROUTER_COLS), F32)
    b = b.at[0, 0:MOE_GROUPS].set(b_group.astype(F32))
    b = b.at[0, EXPERT_COL0:EXPERT_COL0 + MOE_EXPERTS].set(b_expert.astype(F32))
    return w, b


def _lambda_init(layer):
    return 0.8 - 0.6 * math.exp(-0.3 * layer)


def kernel(x, c, rel_bias, ada_w, ada_b, norm_attn, norm_ffn, a_w_qkv, a_q_norm, a_k_norm, a_lambda, a_subln, a_w_o, kv_norm, kv_ada_w, kv_ada_b, kv_w, kv_k_norm, b_w_q, b_q_norm, b_w_o, moe_w_group, moe_b_group, moe_w_expert, moe_b_expert, moe_w_gate, moe_w_up, moe_w_down):
    batch, seq, d = x.shape
    t = batch * seq
    scale = HEAD_DIM ** -0.5
    n_bw = N_GROUPS * B_WIDTH

    c_pad = jnp.zeros((8, d), F32).at[:batch].set(c.astype(F32))
    mod = _modulation(c_pad, ada_w, ada_b)[:, :batch]
    kv_mod = _modulation(c_pad, kv_ada_w[None], kv_ada_b[None])[0, :batch]

    def part(m, i):
        return m[:, i * d:(i + 1) * d].reshape(batch, 1, d)

    a_bias, nd = _attn_a_bias_diags(rel_bias, seq, min(ATT_TILE, seq))
    b_bias = [_attn_b_bias_tiles(rel_bias, window // dil, dil) for window, dil in B_GROUPS]
    xf = x.reshape(t, d).astype(F32)
    kv = None
    for layer in range(DEPTH):
        m = mod[layer]
        sh_a, sc_a, g_a, sh_f, sc_f, g_f = (part(m, i) for i in range(6))
        w_r, b_r = _router_weights(moe_w_group[layer], moe_b_group[layer], moe_w_expert[layer], moe_b_expert[layer])
        if layer < N_A_LAYERS:
            qk_gain = jnp.concatenate([jnp.tile(a_q_norm[layer].astype(F32) * (scale * LOG2_E), 2 * N_HEADS),
                                       jnp.tile(a_k_norm[layer].astype(F32), 2 * N_HEADS),
                                       jnp.ones((N_HEADS * LANES,), F32)])
            qkv = _norm_proj(xf, norm_attn[layer], sh_a, sc_a, a_w_qkv[layer].astype(BF16), qk_gain,
                             2 * N_HEADS * LANES, seq, BF16)
            att = _attn_a(qkv, a_bias, nd, a_lambda[layer].astype(F32), a_subln[layer].astype(F32),
                          _lambda_init(layer), batch, seq)
            wo = a_w_o[layer].astype(BF16)
        else:
            j = layer - N_A_LAYERS
            if kv is None:
                k_gain = jnp.concatenate([jnp.tile(kv_k_norm.astype(F32), (1, N_HEADS)).reshape(-1),
                                          jnp.ones((n_bw,), F32)])
                kv = _norm_proj(xf, kv_norm, part(kv_mod, 0), part(kv_mod, 1), kv_w.astype(BF16), k_gain, n_bw, seq, F32)
            q_gain = jnp.tile(b_q_norm[j].astype(F32) * (scale * LOG2_E), (1, N_HEADS)).reshape(-1)
            q = _norm_proj(xf, norm_attn[layer], sh_a, sc_a, b_w_q[j].astype(BF16), q_gain, n_bw, seq, F32)
            att = _attn_b(q, kv, b_bias, batch, seq)
            wo = b_w_o[j].astype(BF16)
        x1, hf, logits = _out_proj(att, wo, xf, g_a, norm_ffn[layer], sh_f, sc_f, w_r, b_r, seq)
        xf = _moe(hf, logits, x1, g_f, moe_w_gate, moe_w_up, moe_w_down, layer, seq)
    return xf.reshape(batch, seq, d).astype(x.dtype)
```

```python
import functools
import math

import jax
import jax.numpy as jnp
import numpy as np
from jax import lax
from jax.experimental import pallas as pl
from jax.experimental.pallas import tpu as pltpu

F32 = jnp.float32
BF16 = jnp.bfloat16

D_MODEL = 1024
DEPTH = 4
N_A_LAYERS = 2
N_HEADS = 8
HEAD_DIM = 64
B_GROUPS = ((128, 1), (512, 4), (2048, 16))
N_GROUPS = len(B_GROUPS)
B_WIDTH = N_HEADS * HEAD_DIM
REL_BUCKETS = 32
REL_MAX_EXACT = 16
REL_MAX_DIST = 2048
MOE_GROUPS = 4
MOE_EPG = 8
MOE_EXPERTS = MOE_GROUPS * MOE_EPG
MOE_HIDDEN = 512
EPS = 1e-6
NEG_INF = float("-inf")
LOG2_E = math.log2(math.e)

LANES = 128
MXU_DIM = 256
ROUTER_COLS = 128
EXPERT_COL0 = 8

TOK_TILE = 512
ATT_TILE = 512
B_TOKENS = 2048
BQ_SUB = 128
MOE_BLK = 256
MOE_TILE = 512
SEG_ALIGN = 8
SEG_CHUNK = 8
TABLE_W = 256
VMEM_LIMIT = 56 * 1024 * 1024


def _cparams(sem):
    return pltpu.CompilerParams(dimension_semantics=sem, vmem_limit_bytes=VMEM_LIMIT)


def _bucket_table(max_dist):
    n = np.arange(max_dist + 1)
    nf = np.maximum(n, 1).astype(np.float64)
    large = REL_MAX_EXACT + (np.log(nf / REL_MAX_EXACT) / math.log(REL_MAX_DIST / REL_MAX_EXACT)
                             * (REL_BUCKETS - REL_MAX_EXACT)).astype(np.int64)
    large = np.minimum(large, REL_BUCKETS - 1)
    return np.where(n < REL_MAX_EXACT, n, large).astype(np.int32)


def _toeplitz(v, n, m):
    length = v.shape[-1]
    assert length >= n + m - 1 and length - 1 >= m
    lead = v.shape[:-1]
    flat = jnp.tile(v, (1,) * len(lead) + (n,))[..., :n * (length - 1)]
    return flat.reshape(lead + (n, length - 1))[..., :m]


def _diag_values(rel_bias, dist, ok):
    table = _bucket_table(int(dist.max()))
    vals = rel_bias.astype(F32)[table[np.clip(dist, 0, None)]]
    vals = jnp.where(ok[..., None], vals, NEG_INF)
    return jnp.moveaxis(vals, -1, 0)


def _attn_a_bias_diags(rel_bias, seq, tile):
    table = _bucket_table(seq)
    last_start = int(np.argmax(table == REL_BUCKETS - 1))
    nd = 0
    while nd * tile - (tile - 1) < last_start and nd * tile < seq:
        nd += 1
    nd += 1
    length = 2 * tile
    u = np.arange(length)
    u = np.where(u < tile, u, u - length)
    dist = np.arange(nd)[:, None] * tile - u[None, :]
    return _diag_values(rel_bias, dist, dist >= 0) * LOG2_E, nd


def _attn_b_bias_tiles(rel_bias, band, dil):
    length = 3 * BQ_SUB
    u = np.arange(length)
    u = np.where(u < 2 * BQ_SUB, u, u - length)
    rel = BQ_SUB - u
    ok = (rel >= 0) & (rel <= band)
    general = _toeplitz(_diag_values(rel_bias, dil * rel, ok) * LOG2_E, BQ_SUB, 2 * BQ_SUB)
    no_prev = np.arange(2 * BQ_SUB)[None, None, :] >= BQ_SUB
    return jnp.stack([jnp.where(no_prev, general, NEG_INF), general])


def _group_sum_matrix():
    r = np.arange(MXU_DIM)
    return jnp.asarray((r[:, None] // HEAD_DIM) == (r[None, :] // HEAD_DIM), BF16)


def _mod_kernel(c_ref, w_ref, b_ref, o_ref):
    c = c_ref[...]
    c_act = c * (1.0 / (1.0 + jnp.exp(-c)))
    o_ref[...] = jnp.dot(c_act, w_ref[...], preferred_element_type=F32,
                         precision=lax.Precision.HIGHEST) + b_ref[...]


def _modulation(c_pad, w, b, tn=1024):
    nl, d, n = w.shape
    return pl.pallas_call(
        _mod_kernel,
        grid=(nl, n // tn),
        in_specs=[pl.BlockSpec((8, d), lambda l, j: (0, 0)),
                  pl.BlockSpec((None, d, tn), lambda l, j: (l, 0, j)),
                  pl.BlockSpec((None, 1, tn), lambda l, j: (l, 0, j))],
        out_specs=pl.BlockSpec((None, 8, tn), lambda l, j: (l, 0, j)),
        out_shape=jax.ShapeDtypeStruct((nl, 8, n), F32),
        compiler_params=_cparams(("parallel", "parallel")),
        name="adaln_mod",
    )(c_pad, w, b.reshape(nl, 1, n))


def _rms(x, gain):
    return x * lax.rsqrt(jnp.mean(x * x, axis=-1, keepdims=True) + EPS) * gain


def _proj_kernel(x_ref, g_ref, sh_ref, sc_ref, w_ref, hg_ref, bd_ref, o_ref, *, n_out, n_norm):
    h = _rms(x_ref[...], g_ref[...]) * (1.0 + sc_ref[...]) + sh_ref[...]
    hb = h.astype(BF16)
    wide = 2 * MXU_DIM
    for c0 in range(0, n_out, wide):
        a = jnp.dot(hb, w_ref[:, c0:c0 + wide], preferred_element_type=F32)
        if c0 < n_norm:
            sq = (a * a).astype(BF16)
            ms = jnp.concatenate([jnp.dot(sq[:, :MXU_DIM], bd_ref[...], preferred_element_type=F32),
                                  jnp.dot(sq[:, MXU_DIM:], bd_ref[...], preferred_element_type=F32)], axis=1)
            a = a * lax.rsqrt(ms * (1.0 / HEAD_DIM) + EPS) * hg_ref[:, c0:c0 + wide]
        if o_ref.dtype == jnp.uint32:
            o_ref[:, c0 // 2:(c0 + wide) // 2] = _pack_pair(a[:, :MXU_DIM], a[:, MXU_DIM:])
        else:
            o_ref[:, c0:c0 + wide] = a.astype(o_ref.dtype)


def _norm_proj(x, gain, shift, scale, w_bf16, head_gain, n_norm, seq, out_dtype):
    t, d = x.shape
    n_out = w_bf16.shape[1]
    tm = min(TOK_TILE, seq)
    per_b = seq // tm
    kern = functools.partial(_proj_kernel, n_out=n_out, n_norm=n_norm)
    out_cols = n_out // 2 if out_dtype == jnp.uint32 else n_out
    return pl.pallas_call(
        kern,
        grid=(t // tm,),
        in_specs=[pl.BlockSpec((tm, d), lambda i: (i, 0)),
                  pl.BlockSpec((1, d), lambda i: (0, 0)),
                  pl.BlockSpec((None, 1, d), lambda i: (i // per_b, 0, 0)),
                  pl.BlockSpec((None, 1, d), lambda i: (i // per_b, 0, 0)),
                  pl.BlockSpec((d, n_out), lambda i: (0, 0)),
                  pl.BlockSpec((1, n_out), lambda i: (0, 0)),
                  pl.BlockSpec((MXU_DIM, MXU_DIM), lambda i: (0, 0))],
        out_specs=pl.BlockSpec((tm, out_cols), lambda i: (i, 0)),
        out_shape=jax.ShapeDtypeStruct((t, out_cols), out_dtype),
        compiler_params=_cparams(("parallel",)),
        name="norm_proj",
    )(x, gain.reshape(1, d), shift, scale, w_bf16, head_gain.reshape(1, n_out), _group_sum_matrix())


def _lane_tile(a, reps):
    return jnp.concatenate([a] * reps, axis=1)


def _attn_a_kernel(q_ref, k_ref, v_ref, diag_ref, lam_ref, g_ref, o_ref, bias_ref, m_sc, acc_sc,
                   *, tile, nd, lam_init):
    qi = pl.program_id(2)

    @pl.when((pl.program_id(1) == 0) & (qi == 0))
    def _():
        for d in range(nd):
            row = jnp.broadcast_to(diag_ref[d:d + 1, :], (tile, 2 * tile))
            bias_ref[d] = pltpu.roll(row, 0, 1, stride=1, stride_axis=0)[:, :tile]

    q = q_ref[...]
    lane = lax.broadcasted_iota(jnp.int32, q.shape, 1)
    zero = jnp.zeros_like(q)
    qq = jnp.concatenate([jnp.where(lane < HEAD_DIM, q, zero),
                          jnp.where(lane >= HEAD_DIM, q, zero)], axis=0)
    m_sc[...] = jnp.full(m_sc.shape, NEG_INF, F32)
    acc_sc[...] = jnp.zeros(acc_sc.shape, F32)
    ones = jnp.ones((tile, LANES), BF16)

    def kv_tile(j):
        start = pl.multiple_of(j * tile, tile)
        k = k_ref[pl.ds(start, tile), :]
        v_ones = jnp.concatenate([v_ref[pl.ds(start, tile), :], ones], axis=1)
        s = lax.dot_general(qq, k, (((1,), (1,)), ((), ())), preferred_element_type=F32)
        bias = bias_ref[jnp.minimum(qi - j, nd - 1)]
        for c in range(2):
            sc = s[c * tile:(c + 1) * tile] + bias
            m_prev = m_sc[c]
            m_new = jnp.maximum(m_prev, jnp.max(sc, axis=-1, keepdims=True))
            alpha = jnp.exp2(m_prev - m_new)
            p = jnp.exp2(sc - _lane_tile(m_new, tile // LANES))
            acc_sc[c] = (_lane_tile(alpha, 2) * acc_sc[c]
                         + jnp.dot(p.astype(BF16), v_ones, preferred_element_type=F32))
            m_sc[c] = m_new

    def kv_pair(i, carry):
        kv_tile(2 * i)
        kv_tile(2 * i + 1)
        return carry

    n_kv = qi + 1
    lax.fori_loop(0, n_kv // 2, kv_pair, 0)

    @pl.when(n_kv % 2 == 1)
    def _():
        kv_tile(qi)

    lp = lam_ref[...]
    lam = (jnp.exp(jnp.sum(lp[0:1] * lp[1:2], axis=-1, keepdims=True))
           - jnp.exp(jnp.sum(lp[2:3] * lp[3:4], axis=-1, keepdims=True)) + lam_init)
    a0, a1 = acc_sc[0], acc_sc[1]
    o = a0[:, :LANES] / a0[:, LANES:] - lam * (a1[:, :LANES] / a1[:, LANES:])
    o_ref[...] = (_rms(o, g_ref[...]) * (1.0 - lam_init)).astype(BF16)


def _attn_a(qkv, bias_diags, nd, lam_p, subln, lam_init, batch, seq):
    tile = min(ATT_TILE, seq)
    nq = seq // tile
    qkv3 = qkv.reshape(batch, seq, 3 * N_HEADS * LANES)
    kern = functools.partial(_attn_a_kernel, tile=tile, nd=nd, lam_init=lam_init)
    out = pl.pallas_call(
        kern,
        grid=(N_HEADS, batch, nq),
        in_specs=[pl.BlockSpec((None, tile, LANES), lambda h, b, i: (b, i, h)),
                  pl.BlockSpec((None, seq, LANES), lambda h, b, i: (b, 0, N_HEADS + h)),
                  pl.BlockSpec((None, seq, LANES), lambda h, b, i: (b, 0, 2 * N_HEADS + h)),
                  pl.BlockSpec((None, nd, 2 * tile), lambda h, b, i: (h, 0, 0)),
                  pl.BlockSpec((4, HEAD_DIM), lambda h, b, i: (0, 0)),
                  pl.BlockSpec((1, LANES), lambda h, b, i: (0, 0))],
        out_specs=pl.BlockSpec((None, tile, LANES), lambda h, b, i: (b, i, h)),
        out_shape=jax.ShapeDtypeStruct((batch, seq, N_HEADS * LANES), BF16),
        scratch_shapes=[pltpu.VMEM((nd, tile, tile), F32), pltpu.VMEM((2, tile, LANES), F32),
                        pltpu.VMEM((2, tile, 2 * LANES), F32)],
        compiler_params=_cparams(("arbitrary", "arbitrary", "arbitrary")),
        name="attn_a",
    )(qkv3, qkv3, qkv3, bias_diags, lam_p, subln.reshape(1, LANES))
    return out.reshape(batch * seq, N_HEADS * LANES)


def _attn_b_kernel(*refs, tile):
    ins, o_ref, scratch = refs[:6 * N_GROUPS], refs[6 * N_GROUPS], refs[6 * N_GROUPS + 1:]
    first_tile = pl.program_id(1) == 0
    high_shift = jnp.where(pl.program_id(2) < 2, 0, 16).astype(jnp.uint32)

    def unpack(words):
        return lax.bitcast_convert_type((words >> high_shift) << 16, F32).astype(BF16)

    lane = lax.broadcasted_iota(jnp.int32, (BQ_SUB, LANES), 1)
    low = lane < HEAD_DIM
    ones = jnp.ones((2 * BQ_SUB, LANES), BF16)

    for g, (_, dil) in enumerate(B_GROUPS):
        q_ref, k_ref, v_ref, kp_ref, vp_ref, bias_ref = ins[6 * g:6 * g + 6]
        og_ref, lg_ref = scratch[2 * g:2 * g + 2]
        span = BQ_SUB * dil

        def residue(base, dil=dil):
            return pl.ds(base, BQ_SUB) if dil == 1 else pl.ds(base, BQ_SUB, stride=dil)

        for sb in range(tile // span):
            for r in range(dil):
                base = sb * span + r
                rows = residue(base)
                qp = unpack(q_ref[rows, :])
                if sb == 0:
                    k_prev, v_prev = kp_ref[residue(r), :], vp_ref[residue(r), :]
                    variant = jnp.where(first_tile, 0, 1)
                else:
                    k_prev, v_prev = k_ref[residue(base - span), :], v_ref[residue(base - span), :]
                    variant = 1
                kk = unpack(jnp.concatenate([k_prev, k_ref[rows, :]], axis=0))
                vv = unpack(jnp.concatenate([v_prev, v_ref[rows, :]], axis=0))
                v_ones = jnp.concatenate([vv, ones], axis=1)
                zero = jnp.zeros_like(qp)
                qq = jnp.concatenate([jnp.where(low, qp, zero), jnp.where(low, zero, qp)], axis=0)
                s = lax.dot_general(qq, kk, (((1,), (1,)), ((), ())), preferred_element_type=F32)
                outs, lses = [], []
                for c in range(2):
                    sc = s[c * BQ_SUB:(c + 1) * BQ_SUB] + bias_ref[variant, c]
                    m = jnp.max(sc, axis=-1, keepdims=True)
                    p = jnp.exp2(sc - m)
                    acc = jnp.dot(p.astype(BF16), v_ones, preferred_element_type=F32)
                    outs.append(acc[:, :LANES] / acc[:, LANES:])
                    lses.append(m + jnp.log2(acc[:, LANES:]))
                og_ref[rows, :] = jnp.where(low, outs[0], outs[1])
                lg_ref[rows, :] = jnp.where(low, lses[0], lses[1])

    l0, l1, l2 = scratch[1][...], scratch[3][...], scratch[5][...]
    m = jnp.maximum(jnp.maximum(l0, l1), l2)
    e0, e1, e2 = jnp.exp2(l0 - m), jnp.exp2(l1 - m), jnp.exp2(l2 - m)
    z = e0 + e1 + e2
    o_ref[...] = ((e0 / z) * scratch[0][...] + (e1 / z) * scratch[2][...] + (e2 / z) * scratch[4][...]).astype(BF16)


def _attn_b(q, kv, bias_tiles, batch, seq):
    tile = min(B_TOKENS, seq)
    pairs = N_HEADS // 2
    q3 = q.reshape(batch, seq, N_GROUPS * B_WIDTH // 2)
    kv3 = kv.reshape(batch, seq, N_GROUPS * B_WIDTH)
    operands, in_specs = [], []
    for g, (_, dil) in enumerate(B_GROUPS):
        span = BQ_SUB * dil
        assert tile % span == 0
        k_col = lambda hp, g=g: g * 2 + hp % 2
        v_col = lambda hp, g=g: (N_GROUPS + g) * 2 + hp % 2
        prev = lambda n, per=tile // span: jnp.maximum(n * per - 1, 0)
        operands += [q3, kv3, kv3, kv3, kv3, bias_tiles[g]]
        in_specs += [pl.BlockSpec((None, tile, LANES), lambda b, n, hp, c=k_col: (b, n, c(hp))),
                     pl.BlockSpec((None, tile, LANES), lambda b, n, hp, c=k_col: (b, n, c(hp))),
                     pl.BlockSpec((None, tile, LANES), lambda b, n, hp, c=v_col: (b, n, c(hp))),
                     pl.BlockSpec((None, span, LANES), lambda b, n, hp, c=k_col, p=prev: (b, p(n), c(hp))),
                     pl.BlockSpec((None, span, LANES), lambda b, n, hp, c=v_col, p=prev: (b, p(n), c(hp))),
                     pl.BlockSpec((2, 2, BQ_SUB, 2 * BQ_SUB), lambda b, n, hp: (0, hp, 0, 0))]
    out = pl.pallas_call(
        functools.partial(_attn_b_kernel, tile=tile),
        grid=(batch, seq // tile, pairs),
        in_specs=in_specs,
        out_specs=pl.BlockSpec((None, tile, LANES), lambda b, n, hp: (b, n, hp)),
        out_shape=jax.ShapeDtypeStruct((batch, seq, B_WIDTH), BF16),
        scratch_shapes=[pltpu.VMEM((tile, LANES), F32)] * (2 * N_GROUPS),
        compiler_params=_cparams(("parallel", "parallel", "parallel")),
        name="attn_b",
    )(*operands)
    return out.reshape(batch * seq, B_WIDTH)


def _split_bf16(a):
    hi = a.astype(BF16)
    return hi, (a - hi.astype(F32)).astype(BF16)


def _out_tail(o_bf16, wo_ref, x_ref, ga_ref, gf_ref, shf_ref, scf_ref, wr_ref, br_ref, x1_ref, hf_ref, lg_ref):
    y = jnp.dot(o_bf16, wo_ref[...], preferred_element_type=F32)
    x1 = x_ref[...] + ga_ref[...] * y
    x1_ref[...] = x1
    hf = _rms(x1, gf_ref[...]) * (1.0 + scf_ref[...]) + shf_ref[...]
    half = hf.shape[1] // 2
    hf_ref[...] = _pack_pair(hf[:, :half], hf[:, half:])
    hi, lo = _split_bf16(hf)
    both = jnp.dot(hi, wr_ref[...], preferred_element_type=F32)
    lg_ref[...] = (both[:, :ROUTER_COLS] + jnp.dot(lo, wr_ref[:, :ROUTER_COLS], preferred_element_type=F32)
                   + both[:, ROUTER_COLS:] + br_ref[...])


def _out_kernel(o_ref, *rest):
    _out_tail(o_ref[...], *rest)


def _out_proj(att, wo_bf16, x, gate_a, gain_f, shift_f, scale_f, w_router, b_router, seq):
    t, d = x.shape
    tm = min(TOK_TILE, seq)
    per_b = seq // tm
    row = lambda i: (i, 0)
    fixed = lambda i: (0, 0)
    per_batch = lambda i: (i // per_b, 0, 0)
    return pl.pallas_call(
        _out_kernel,
        grid=(t // tm,),
        in_specs=[pl.BlockSpec((tm, att.shape[1]), row),
                  pl.BlockSpec(wo_bf16.shape, fixed),
                  pl.BlockSpec((tm, d), row),
                  pl.BlockSpec((None, 1, d), per_batch),
                  pl.BlockSpec((1, d), fixed),
                  pl.BlockSpec((None, 1, d), per_batch),
                  pl.BlockSpec((None, 1, d), per_batch),
                  pl.BlockSpec((d, 2 * ROUTER_COLS), fixed),
                  pl.BlockSpec((1, ROUTER_COLS), fixed)],
        out_specs=[pl.BlockSpec((tm, d), row), pl.BlockSpec((tm, d // 2), row),
                   pl.BlockSpec((tm, ROUTER_COLS), row)],
        out_shape=[jax.ShapeDtypeStruct((t, d), F32), jax.ShapeDtypeStruct((t, d // 2), jnp.uint32),
                   jax.ShapeDtypeStruct((t, ROUTER_COLS), F32)],
        compiler_params=_cparams(("parallel",)),
        name="out_proj",
    )(att, wo_bf16, x, gate_a, gain_f.reshape(1, d), shift_f, scale_f,
      jnp.concatenate(_split_bf16(w_router), axis=1), b_router)


def _route_kernel(lg_ref, tri_ref, eid_ref, gate_ref, seg_ref, cnt_ref, cnt_sc):
    lt = lg_ref[...].T
    best, g_idx = lt[0:1], jnp.zeros((1, lt.shape[1]), jnp.int32)
    for g in range(1, MOE_GROUPS):
        upd = lt[g:g + 1] > best
        best = jnp.where(upd, lt[g:g + 1], best)
        g_idx = jnp.where(upd, g, g_idx)
    denom = jnp.zeros_like(best)
    for g in range(MOE_GROUPS):
        denom = denom + jnp.exp(lt[g:g + 1] - best)
    g_w = 1.0 / denom
    e_sel = lt[EXPERT_COL0:EXPERT_COL0 + MOE_EPG]
    for g in range(1, MOE_GROUPS):
        r0 = EXPERT_COL0 + g * MOE_EPG
        e_sel = jnp.where(g_idx == g, lt[r0:r0 + MOE_EPG], e_sel)

    def first_max(vals):
        v, i = vals[0:1], jnp.zeros((1, vals.shape[1]), jnp.int32)
        for e in range(1, MOE_EPG):
            upd = vals[e:e + 1] > v
            v = jnp.where(upd, vals[e:e + 1], v)
            i = jnp.where(upd, e, i)
        return v, i

    v1, i1 = first_max(e_sel)
    row = lax.broadcasted_iota(jnp.int32, e_sel.shape, 0)
    v2, i2 = first_max(jnp.where(row == i1, NEG_INF, e_sel))
    e2 = jnp.exp(v2 - v1)
    w1 = 1.0 / (1.0 + e2)
    w2 = e2 / (1.0 + e2)
    e1 = g_idx * MOE_EPG + i1
    e2 = g_idx * MOE_EPG + i2

    @pl.when(pl.program_id(0) == 0)
    def _():
        cnt_sc[...] = jnp.zeros(cnt_sc.shape, F32)

    tm = lt.shape[1]
    erows = lax.broadcasted_iota(jnp.int32, (MOE_EXPERTS, tm), 0)
    ohs, pres = [], []
    for e_k in (e1, e2):
        oh = (erows == e_k).astype(F32)
        ohs.append(oh)
        pres.append(jnp.dot(oh.astype(BF16), tri_ref[...], preferred_element_type=F32))
    n0 = pres[0][:, tm - 1:]
    n_tot = n0 + pres[1][:, tm - 1:]
    chunks = jnp.floor((n_tot + (SEG_ALIGN - 1.0)) * (1.0 / SEG_ALIGN))
    er = lax.broadcasted_iota(jnp.int32, (MOE_EXPERTS, MOE_EXPERTS), 0)
    ec = lax.broadcasted_iota(jnp.int32, (MOE_EXPERTS, MOE_EXPERTS), 1)
    before = jnp.dot((ec < er).astype(BF16), jnp.broadcast_to(chunks, (MOE_EXPERTS, LANES)).astype(BF16),
                     preferred_element_type=F32)
    seg_start = before[:, 0:1] * SEG_ALIGN
    slot0 = jnp.sum(ohs[0] * (seg_start + pres[0] - 1.0), axis=0, keepdims=True)
    slot1 = jnp.sum(ohs[1] * (seg_start + n0 + pres[1] - 1.0), axis=0, keepdims=True)
    seg_off = cnt_sc[...]
    cnt_sc[...] = seg_off + chunks * SEG_ALIGN
    cnt_ref[...] = jnp.broadcast_to(cnt_sc[...], cnt_ref.shape).astype(jnp.int32)
    scol = lax.broadcasted_iota(jnp.int32, seg_ref.shape, 1)
    seg_ref[...] = jnp.where(scol == 0, chunks, jnp.where(scol == 1, seg_start, jnp.where(
        scol == 2, seg_off, 0.0))).astype(jnp.int32)

    erow = lax.broadcasted_iota(jnp.int32, eid_ref.shape, 0)
    eid_ref[...] = jnp.where(erow == 0, e1, jnp.where(erow == 1, e2, jnp.where(
        erow == 2, slot0.astype(jnp.int32), jnp.where(erow == 3, slot1.astype(jnp.int32), 0))))
    grow = lax.broadcasted_iota(jnp.int32, lt.shape, 0)
    gates = jnp.where(grow == 0, g_w * w1, jnp.where(grow == 1, g_w * w2, 0.0))
    gate_ref[...] = gates.T


def _route(logits):
    t = logits.shape[0]
    tm = min(MOE_TILE, t)
    n_tiles = t // tm
    r = np.arange(tm)
    tri = jnp.asarray(r[:, None] <= r[None, :], BF16)
    return pl.pallas_call(
        _route_kernel,
        grid=(n_tiles,),
        in_specs=[pl.BlockSpec((tm, ROUTER_COLS), lambda i: (i, 0)),
                  pl.BlockSpec((tm, tm), lambda i: (0, 0))],
        out_specs=[pl.BlockSpec((8, tm), lambda i: (0, i)), pl.BlockSpec((tm, ROUTER_COLS), lambda i: (i, 0)),
                   pl.BlockSpec((MOE_EXPERTS, LANES), lambda i: (i, 0)),
                   pl.BlockSpec((MOE_EXPERTS, LANES), lambda i: (0, 0))],
        out_shape=[jax.ShapeDtypeStruct((8, t), jnp.int32), jax.ShapeDtypeStruct((t, ROUTER_COLS), F32),
                   jax.ShapeDtypeStruct((n_tiles * MOE_EXPERTS, LANES), jnp.int32),
                   jax.ShapeDtypeStruct((MOE_EXPERTS, LANES), jnp.int32)],
        scratch_shapes=[pltpu.VMEM((MOE_EXPERTS, 1), F32)],
        compiler_params=_cparams(("arbitrary",)),
        name="moe_route",
    )(logits, tri)


def _moe_rows(t):
    n_tiles = t // min(MOE_TILE, t)
    rows = 2 * t + n_tiles * MOE_EXPERTS * (SEG_ALIGN - 1) + MOE_EXPERTS * (MOE_BLK - 1)
    return -(-rows // MOE_BLK) * MOE_BLK


def _chunk_rows(first, count, row0, n_slots):
    c = jnp.arange(n_slots, dtype=jnp.int32)
    run = jnp.sum((first + count)[..., None, :] <= c[:, None], axis=-1)
    hit = run[..., None] == jnp.arange(first.shape[-1], dtype=jnp.int32)
    base = jnp.sum(jnp.where(hit, (row0 - SEG_CHUNK * first)[..., None, :], 0), axis=-1)
    return jnp.where(run < first.shape[-1], base + SEG_CHUNK * c, 0)


def _dispatch_plan(route, seg, totals, n_blocks):
    n_tiles = seg.shape[0] // MOE_EXPERTS
    tm = route.shape[1] // n_tiles
    n_slots = _sorted_rows(tm) // SEG_CHUNK
    padded = (totals + MOE_BLK - 1) // MOE_BLK * MOE_BLK
    pad_end = jnp.cumsum(padded)
    pad_start = pad_end - padded
    n_used = pad_end[-1] // MOE_BLK
    seg = seg.reshape(n_tiles, MOE_EXPERTS, LANES)
    chunks, first, seg_off = seg[:, :, 0], seg[:, :, 1] // SEG_CHUNK, seg[:, :, 2]
    dst = _chunk_rows(first, chunks, pad_start[None, :] + seg_off, n_slots)
    fill = (padded - totals) // SEG_CHUNK
    fill_first = jnp.cumsum(fill) - fill
    fill_dst = _chunk_rows(fill_first, fill, pad_start + totals, MOE_EXPERTS * (MOE_BLK // SEG_CHUNK))
    tail = jnp.stack([jnp.broadcast_to(n_used, (n_tiles,)), jnp.broadcast_to(jnp.sum(fill), (n_tiles,)),
                      jnp.sum(chunks, axis=1)], axis=1)
    table = jnp.concatenate([dst, jnp.zeros((n_tiles, TABLE_W - 3 - n_slots), jnp.int32), tail], axis=1)
    slots = route[2:4].reshape(2, n_tiles, tm).transpose(1, 0, 2).reshape(n_tiles, 1, 2 * tm)
    block_start = jnp.arange(n_blocks, dtype=jnp.int32) * MOE_BLK
    block_e = jnp.minimum(jnp.sum(pad_end[None, :] <= block_start[:, None], axis=1), MOE_EXPERTS - 1).astype(jnp.int32)
    return (table.astype(jnp.int32).reshape(n_tiles, 1, TABLE_W), slots, fill_dst.astype(jnp.int32).reshape(1, 1, -1),
            block_e, n_used.astype(jnp.int32).reshape(1))


def _chunk_copy(hbm_ref, hbm_row, sorted_ref, chunk, sem, to_hbm):
    local = sorted_ref.at[pl.ds(pl.multiple_of(chunk * SEG_CHUNK, SEG_CHUNK), SEG_CHUNK), :]
    remote = hbm_ref.at[pl.ds(pl.multiple_of(hbm_row, SEG_CHUNK), SEG_CHUNK), :]
    return pltpu.make_async_copy(local, remote, sem) if to_hbm else pltpu.make_async_copy(remote, local, sem)


def _start_chunks(table_ref, hbm_ref, sorted_ref, sem, to_hbm):
    def chunk(c, carry):
        _chunk_copy(hbm_ref, table_ref[0, 0, c], sorted_ref, c, sem, to_hbm).start()
        return carry

    lax.fori_loop(0, table_ref[0, 0, TABLE_W - 1], chunk, 0)


def _wait_chunks(count, hbm_ref, sorted_ref, sem, to_hbm):
    def chunk(c, carry):
        _chunk_copy(hbm_ref, 0, sorted_ref, 0, sem, to_hbm).wait()
        return carry

    lax.fori_loop(0, count, chunk, 0)


def _zero_fill(table_ref, fill_ref, xs_ref, zero_sc, sem, n_blocks, wait):
    def finish(copy):
        if wait:
            copy.wait()
        else:
            copy.start()

    def chunk(c, carry):
        row = pl.multiple_of(fill_ref[0, 0, c], SEG_CHUNK)
        finish(pltpu.make_async_copy(zero_sc.at[pl.ds(0, SEG_CHUNK), :], xs_ref.at[pl.ds(row, SEG_CHUNK), :], sem))
        return carry

    def block(b, carry):
        row = pl.multiple_of(b * MOE_BLK, MOE_BLK)
        finish(pltpu.make_async_copy(zero_sc, xs_ref.at[pl.ds(row, MOE_BLK), :], sem))
        return carry

    lax.fori_loop(0, table_ref[0, 0, TABLE_W - 2], chunk, 0)
    lax.fori_loop(table_ref[0, 0, TABLE_W - 3], n_blocks, block, 0)


def _dispatch_kernel(table_ref, slot_ref, fill_ref, h_ref, xs_ref, sorted_sc, zero_sc, pending_sc, sem, zero_sem,
                     *, tm, n_blocks):
    step = pl.program_id(0)
    first_step = step == 0
    parity = step % 2
    mine, other = sorted_sc.at[parity], sorted_sc.at[1 - parity]

    @pl.when(first_step)
    def _():
        sorted_sc[...] = jnp.zeros(sorted_sc.shape, sorted_sc.dtype)
        zero_sc[...] = jnp.zeros(zero_sc.shape, zero_sc.dtype)
        pending_sc[0] = 0
        _zero_fill(table_ref, fill_ref, xs_ref, zero_sc, zero_sem, n_blocks, wait=False)

    def place(i, carry):
        for k in range(2):
            mine[pl.ds(slot_ref[0, 0, k * tm + i], 1), :] = h_ref[pl.ds(i, 1), :]
        return carry

    lax.fori_loop(0, tm, place, 0, unroll=8)
    _start_chunks(table_ref, xs_ref, mine, sem.at[parity], to_hbm=True)
    _wait_chunks(pending_sc[0], xs_ref, other, sem.at[1 - parity], to_hbm=True)
    pending_sc[0] = table_ref[0, 0, TABLE_W - 1]

    @pl.when(step == pl.num_programs(0) - 1)
    def _():
        _wait_chunks(pending_sc[0], xs_ref, mine, sem.at[parity], to_hbm=True)

    @pl.when(first_step)
    def _():
        _zero_fill(table_ref, fill_ref, xs_ref, zero_sc, zero_sem, n_blocks, wait=True)


def _sorted_rows(tm):
    return -(-(2 * tm + MOE_EXPERTS * (SEG_ALIGN - 1)) // SEG_CHUNK) * SEG_CHUNK


def _dispatch(h, table, slots, fill, rows):
    t, d = h.shape
    tm = min(MOE_TILE, t)
    kern = functools.partial(_dispatch_kernel, tm=tm, n_blocks=rows // MOE_BLK)
    return pl.pallas_call(
        kern,
        grid=(t // tm,),
        in_specs=[pl.BlockSpec((1, 1, TABLE_W), lambda i: (i, 0, 0), memory_space=pltpu.SMEM),
                  pl.BlockSpec((1, 1, 2 * tm), lambda i: (i, 0, 0), memory_space=pltpu.SMEM),
                  pl.BlockSpec(fill.shape, lambda i: (0, 0, 0), memory_space=pltpu.SMEM),
                  pl.BlockSpec((tm, d), lambda i: (i, 0))],
        out_specs=pl.BlockSpec(memory_space=pl.ANY),
        out_shape=jax.ShapeDtypeStruct((rows, d), h.dtype),
        scratch_shapes=[pltpu.VMEM((2, _sorted_rows(tm), d), h.dtype), pltpu.VMEM((MOE_BLK, d), h.dtype),
                        pltpu.SMEM((1,), jnp.int32), pltpu.SemaphoreType.DMA((2,)), pltpu.SemaphoreType.DMA(())],
        compiler_params=_cparams(("arbitrary",)),
        name="moe_dispatch",
    )(table, slots, fill, h)


def _pack_pair(lo, hi):
    lo_bits = lax.bitcast_convert_type(lo.astype(BF16).astype(F32), jnp.uint32) >> 16
    hi_bits = lax.bitcast_convert_type(hi.astype(BF16).astype(F32), jnp.uint32) & jnp.uint32(0xFFFF0000)
    return hi_bits | lo_bits


def _unpack_pair(words):
    lo = lax.bitcast_convert_type(words << 16, F32).astype(BF16)
    hi = lax.bitcast_convert_type(words & jnp.uint32(0xFFFF0000), F32).astype(BF16)
    return lo, hi


def _expert_kernel(be_ref, nu_ref, xs_ref, wg_ref, wu_ref, wd_ref, ys_ref, wg_sc, wu_sc, wd_sc):
    i = pl.program_id(0)
    live = i < nu_ref[0]

    @pl.when(jnp.logical_not(live))
    def _():
        ys_ref[...] = jnp.zeros(ys_ref.shape, F32)

    @pl.when(live)
    def _():
        @pl.when((i == 0) | (be_ref[i] != be_ref[jnp.maximum(i - 1, 0)]))
        def _():
            wg_sc[...] = wg_ref[...].astype(BF16)
            wu_sc[...] = wu_ref[...].astype(BF16)
            wd_sc[...] = wd_ref[...].astype(BF16)

        x = jnp.concatenate(_unpack_pair(xs_ref[...]), axis=1)
        g = jnp.dot(x, wg_sc[...], preferred_element_type=F32)
        u = jnp.dot(x, wu_sc[...], preferred_element_type=F32)
        a = (g * (1.0 / (1.0 + jnp.exp(-g))) * u).astype(BF16)
        ys_ref[...] = jnp.dot(a, wd_sc[...], preferred_element_type=F32)


def _experts(xs, block_e, n_used, w_gate, w_up, w_down, layer):
    rows = xs.shape[0]
    d, hid = w_gate.shape[-2:]
    n_blocks = rows // MOE_BLK
    used = lambda i, be, nu: (jnp.minimum(i, nu[0] - 1), 0)
    expert = lambda i, be, nu: (layer, be[jnp.minimum(i, nu[0] - 1)], 0, 0)
    grid_spec = pltpu.PrefetchScalarGridSpec(
        num_scalar_prefetch=2,
        grid=(n_blocks,),
        in_specs=[pl.BlockSpec((MOE_BLK, d // 2), used),
                  pl.BlockSpec((None, None, d, hid), expert),
                  pl.BlockSpec((None, None, d, hid), expert),
                  pl.BlockSpec((None, None, hid, d), expert)],
        out_specs=pl.BlockSpec((MOE_BLK, d), lambda i, be, nu: (i, 0)),
        scratch_shapes=[pltpu.VMEM((d, hid), BF16), pltpu.VMEM((d, hid), BF16), pltpu.VMEM((hid, d), BF16)],
    )
    return pl.pallas_call(
        _expert_kernel,
        grid_spec=grid_spec,
        out_shape=jax.ShapeDtypeStruct((rows, d), F32),
        compiler_params=_cparams(("arbitrary",)),
        name="moe_experts",
    )(block_e, n_used, xs, w_gate, w_up, w_down)


def _combine_kernel(table_ref, next_table_ref, slot_ref, ys_ref, x_ref, gf_ref, gate_ref, o_ref,
                    sorted_sc, buf, sem, *, tm):
    step = pl.program_id(0)
    parity = step % 2
    mine, other = sorted_sc.at[parity], sorted_sc.at[1 - parity]

    @pl.when(step == 0)
    def _():
        _start_chunks(table_ref, ys_ref, mine, sem.at[parity], to_hbm=False)

    @pl.when(step + 1 < pl.num_programs(0))
    def _():
        _start_chunks(next_table_ref, ys_ref, other, sem.at[1 - parity], to_hbm=False)

    _wait_chunks(table_ref[0, 0, TABLE_W - 1], ys_ref, mine, sem.at[parity], to_hbm=False)

    def pick(i, carry):
        for k in range(2):
            buf[k, pl.ds(i, 1), :] = mine[pl.ds(slot_ref[0, 0, k * tm + i], 1), :]
        return carry

    lax.fori_loop(0, tm, pick, 0, unroll=8)
    gates = gate_ref[...]
    y = gates[:, 0:1] * buf[0] + gates[:, 1:2] * buf[1]
    o_ref[...] = x_ref[...] + gf_ref[...] * y


def _combine(ys, table, slots, x, gate_f, gates, seq):
    t, d = x.shape
    tm = min(MOE_TILE, t)
    per_b = seq // tm if seq >= tm else 1
    kern = functools.partial(_combine_kernel, tm=tm)
    last = t // tm - 1
    return pl.pallas_call(
        kern,
        grid=(t // tm,),
        in_specs=[pl.BlockSpec((1, 1, TABLE_W), lambda i: (i, 0, 0), memory_space=pltpu.SMEM),
                  pl.BlockSpec((1, 1, TABLE_W), lambda i: (jnp.minimum(i + 1, last), 0, 0), memory_space=pltpu.SMEM),
                  pl.BlockSpec((1, 1, 2 * tm), lambda i: (i, 0, 0), memory_space=pltpu.SMEM),
                  pl.BlockSpec(memory_space=pl.ANY),
                  pl.BlockSpec((tm, d), lambda i: (i, 0)),
                  pl.BlockSpec((None, 1, d), lambda i: (i // per_b, 0, 0)),
                  pl.BlockSpec((tm, ROUTER_COLS), lambda i: (i, 0))],
        out_specs=pl.BlockSpec((tm, d), lambda i: (i, 0)),
        out_shape=jax.ShapeDtypeStruct((t, d), F32),
        scratch_shapes=[pltpu.VMEM((2, _sorted_rows(tm), d), F32), pltpu.VMEM((2, tm, d), F32),
                        pltpu.SemaphoreType.DMA((2,))],
        compiler_params=_cparams(("arbitrary",)),
        name="moe_combine",
    )(table, table, slots, ys, x, gate_f, gates)


def _moe(h, logits, x1, gate_f, w_gate, w_up, w_down, layer, seq):
    rows = _moe_rows(h.shape[0])
    route, gates, seg, totals = _route(logits)
    table, slots, fill, block_e, n_used = _dispatch_plan(route, seg, totals[:, 0], rows // MOE_BLK)
    xs = _dispatch(h, table, slots, fill, rows)
    ys = _experts(xs, block_e, n_used, w_gate, w_up, w_down, layer)
    return _combine(ys, table, slots, x1, gate_f, gates, seq)


def _router_weights(w_group, b_group, w_expert, b_expert):
    d = w_group.shape[0]
    w = jnp.zeros((d, ROUTER_COLS), F32)
    w = w.at[:, 0:MOE_GROUPS].set(w_group.astype(F32))
    w = w.at[:, EXPERT_COL0:EXPERT_COL0 + MOE_EXPERTS].set(w_expert.astype(F32))
    b = jnp.zeros((1, ROUTER_COLS), F32)
    b = b.at[0, 0:MOE_GROUPS].set(b_group.astype(F32))
    b = b.at[0, EXPERT_COL0:EXPERT_COL0 + MOE_EXPERTS].set(b_expert.astype(F32))
    return w, b


def _lambda_init(layer):
    return 0.8 - 0.6 * math.exp(-0.3 * layer)


def kernel(x, c, rel_bias, ada_w, ada_b, norm_attn, norm_ffn, a_w_qkv, a_q_norm, a_k_norm, a_lambda, a_subln, a_w_o, kv_norm, kv_ada_w, kv_ada_b, kv_w, kv_k_norm, b_w_q, b_q_norm, b_w_o, moe_w_group, moe_b_group, moe_w_expert, moe_b_expert, moe_w_gate, moe_w_up, moe_w_down):
    batch, seq, d = x.shape
    t = batch * seq
    scale = HEAD_DIM ** -0.5
    n_bw = N_GROUPS * B_WIDTH

    c_pad = jnp.zeros((8, d), F32).at[:batch].set(c.astype(F32))
    mod = _modulation(c_pad, ada_w, ada_b)[:, :batch]
    kv_mod = _modulation(c_pad, kv_ada_w[None], kv_ada_b[None])[0, :batch]

    def part(m, i):
        return m[:, i * d:(i + 1) * d].reshape(batch, 1, d)

    a_bias, nd = _attn_a_bias_diags(rel_bias, seq, min(ATT_TILE, seq))
    b_bias = [_attn_b_bias_tiles(rel_bias, window // dil, dil) for window, dil in B_GROUPS]
    xf = x.reshape(t, d).astype(F32)
    kv = None
    for layer in range(DEPTH):
        m = mod[layer]
        sh_a, sc_a, g_a, sh_f, sc_f, g_f = (part(m, i) for i in range(6))
        w_r, b_r = _router_weights(moe_w_group[layer], moe_b_group[layer], moe_w_expert[layer], moe_b_expert[layer])
        if layer < N_A_LAYERS:
            qk_gain = jnp.concatenate([jnp.tile(a_q_norm[layer].astype(F32) * (scale * LOG2_E), 2 * N_HEADS),
                                       jnp.tile(a_k_norm[layer].astype(F32), 2 * N_HEADS),
                                       jnp.ones((N_HEADS * LANES,), F32)])
            qkv = _norm_proj(xf, norm_attn[layer], sh_a, sc_a, a_w_qkv[layer].astype(BF16), qk_gain,
                             2 * N_HEADS * LANES, seq, BF16)
            att = _attn_a(qkv, a_bias, nd, a_lambda[layer].astype(F32), a_subln[layer].astype(F32),
                          _lambda_init(layer), batch, seq)
            wo = a_w_o[layer].astype(BF16)
        else:
            j = layer - N_A_LAYERS
            if kv is None:
                k_gain = jnp.concatenate([jnp.tile(kv_k_norm.astype(F32), (1, N_HEADS)).reshape(-1),
                                          jnp.ones((n_bw,), F32)])
                kv = _norm_proj(xf, kv_norm, part(kv_mod, 0), part(kv_mod, 1), kv_w.astype(BF16), k_gain, n_bw, seq,
                                jnp.uint32)
            q_gain = jnp.tile(b_q_norm[j].astype(F32) * (scale * LOG2_E), (1, N_HEADS)).reshape(-1)
            q = _norm_proj(xf, norm_attn[layer], sh_a, sc_a, b_w_q[j].astype(BF16), q_gain, n_bw, seq, jnp.uint32)
            att = _attn_b(q, kv, b_bias, batch, seq)
            wo = b_w_o[j].astype(BF16)
        x1, hf, logits = _out_proj(att, wo, xf, g_a, norm_ffn[layer], sh_f, sc_f, w_r, b_r, seq)
        xf = _moe(hf, logits, x1, g_f, moe_w_gate, moe_w_up, moe_w_down, layer, seq)
    return xf.reshape(batch, seq, d).astype(x.dtype)
```

```python
import functools
import math

import jax
import jax.numpy as jnp
import numpy as np
from jax import lax
from jax.experimental import pallas as pl
from jax.experimental.pallas import tpu as pltpu

F32 = jnp.float32
BF16 = jnp.bfloat16

D_MODEL = 1024
DEPTH = 4
N_A_LAYERS = 2
N_HEADS = 8
HEAD_DIM = 64
B_GROUPS = ((128, 1), (512, 4), (2048, 16))
N_GROUPS = len(B_GROUPS)
B_WIDTH = N_HEADS * HEAD_DIM
REL_BUCKETS = 32
REL_MAX_EXACT = 16
REL_MAX_DIST = 2048
MOE_GROUPS = 4
MOE_EPG = 8
MOE_EXPERTS = MOE_GROUPS * MOE_EPG
MOE_HIDDEN = 512
EPS = 1e-6
NEG_INF = float("-inf")
LOG2_E = math.log2(math.e)

LANES = 128
MXU_DIM = 256
ROUTER_COLS = 128
EXPERT_COL0 = 8

TOK_TILE = 512
ATT_TILE = 512
B_TOKENS = 2048
BQ_SUB = 128
MOE_BLK = 256
MOE_TILE = 512
SEG_ALIGN = 8
SEG_CHUNK = 8
TABLE_W = 256
VMEM_LIMIT = 56 * 1024 * 1024


def _cparams(sem):
    return pltpu.CompilerParams(dimension_semantics=sem, vmem_limit_bytes=VMEM_LIMIT)


def _bucket_table(max_dist):
    n = np.arange(max_dist + 1)
    nf = np.maximum(n, 1).astype(np.float64)
    large = REL_MAX_EXACT + (np.log(nf / REL_MAX_EXACT) / math.log(REL_MAX_DIST / REL_MAX_EXACT)
                             * (REL_BUCKETS - REL_MAX_EXACT)).astype(np.int64)
    large = np.minimum(large, REL_BUCKETS - 1)
    return np.where(n < REL_MAX_EXACT, n, large).astype(np.int32)


def _toeplitz(v, n, m):
    length = v.shape[-1]
    assert length >= n + m - 1 and length - 1 >= m
    lead = v.shape[:-1]
    flat = jnp.tile(v, (1,) * len(lead) + (n,))[..., :n * (length - 1)]
    return flat.reshape(lead + (n, length - 1))[..., :m]


def _diag_values(rel_bias, dist, ok):
    table = _bucket_table(int(dist.max()))
    vals = rel_bias.astype(F32)[table[np.clip(dist, 0, None)]]
    vals = jnp.where(ok[..., None], vals, NEG_INF)
    return jnp.moveaxis(vals, -1, 0)


def _attn_a_bias_diags(rel_bias, seq, tile):
    table = _bucket_table(seq)
    last_start = int(np.argmax(table == REL_BUCKETS - 1))
    nd = 0
    while nd * tile - (tile - 1) < last_start and nd * tile < seq:
        nd += 1
    nd += 1
    length = 2 * tile
    u = np.arange(length)
    u = np.where(u < tile, u, u - length)
    dist = np.arange(nd)[:, None] * tile - u[None, :]
    return _diag_values(rel_bias, dist, dist >= 0) * LOG2_E, nd


def _attn_b_bias_tiles(rel_bias, band, dil):
    length = 3 * BQ_SUB
    u = np.arange(length)
    u = np.where(u < 2 * BQ_SUB, u, u - length)
    rel = BQ_SUB - u
    ok = (rel >= 0) & (rel <= band)
    general = _toeplitz(_diag_values(rel_bias, dil * rel, ok) * LOG2_E, BQ_SUB, 2 * BQ_SUB)
    no_prev = np.arange(2 * BQ_SUB)[None, None, :] >= BQ_SUB
    return jnp.stack([jnp.where(no_prev, general, NEG_INF), general])


def _group_sum_matrix():
    r = np.arange(MXU_DIM)
    return jnp.asarray((r[:, None] // HEAD_DIM) == (r[None, :] // HEAD_DIM), BF16)


def _mod_kernel(c_ref, w_ref, b_ref, o_ref):
    c = c_ref[...]
    c_act = c * (1.0 / (1.0 + jnp.exp(-c)))
    o_ref[...] = jnp.dot(c_act, w_ref[...], preferred_element_type=F32,
                         precision=lax.Precision.HIGHEST) + b_ref[...]


def _modulation(c_pad, w, b, tn=1024):
    nl, d, n = w.shape
    return pl.pallas_call(
        _mod_kernel,
        grid=(nl, n // tn),
        in_specs=[pl.BlockSpec((8, d), lambda l, j: (0, 0)),
                  pl.BlockSpec((None, d, tn), lambda l, j: (l, 0, j)),
                  pl.BlockSpec((None, 1, tn), lambda l, j: (l, 0, j))],
        out_specs=pl.BlockSpec((None, 8, tn), lambda l, j: (l, 0, j)),
        out_shape=jax.ShapeDtypeStruct((nl, 8, n), F32),
        compiler_params=_cparams(("parallel", "parallel")),
        name="adaln_mod",
    )(c_pad, w, b.reshape(nl, 1, n))


def _rms(x, gain):
    return x * lax.rsqrt(jnp.mean(x * x, axis=-1, keepdims=True) + EPS) * gain


def _proj_kernel(x_ref, g_ref, sh_ref, sc_ref, w_ref, hg_ref, bd_ref, o_ref, *, n_out, n_norm):
    h = _rms(x_ref[...], g_ref[...]) * (1.0 + sc_ref[...]) + sh_ref[...]
    hb = h.astype(BF16)
    wide = 2 * MXU_DIM
    for c0 in range(0, n_out, wide):
        a = jnp.dot(hb, w_ref[:, c0:c0 + wide], preferred_element_type=F32)
        if c0 < n_norm:
            sq = (a * a).astype(BF16)
            ms = jnp.concatenate([jnp.dot(sq[:, :MXU_DIM], bd_ref[...], preferred_element_type=F32),
                                  jnp.dot(sq[:, MXU_DIM:], bd_ref[...], preferred_element_type=F32)], axis=1)
            a = a * lax.rsqrt(ms * (1.0 / HEAD_DIM) + EPS) * hg_ref[:, c0:c0 + wide]
        o_ref[:, c0:c0 + wide] = a.astype(o_ref.dtype)


def _norm_proj(x, gain, shift, scale, w_bf16, head_gain, n_norm, seq, out_dtype):
    t, d = x.shape
    n_out = w_bf16.shape[1]
    tm = min(TOK_TILE, seq)
    per_b = seq // tm
    kern = functools.partial(_proj_kernel, n_out=n_out, n_norm=n_norm)
    return pl.pallas_call(
        kern,
        grid=(t // tm,),
        in_specs=[pl.BlockSpec((tm, d), lambda i: (i, 0)),
                  pl.BlockSpec((1, d), lambda i: (0, 0)),
                  pl.BlockSpec((None, 1, d), lambda i: (i // per_b, 0, 0)),
                  pl.BlockSpec((None, 1, d), lambda i: (i // per_b, 0, 0)),
                  pl.BlockSpec((d, n_out), lambda i: (0, 0)),
                  pl.BlockSpec((1, n_out), lambda i: (0, 0)),
                  pl.BlockSpec((MXU_DIM, MXU_DIM), lambda i: (0, 0))],
        out_specs=pl.BlockSpec((tm, n_out), lambda i: (i, 0)),
        out_shape=jax.ShapeDtypeStruct((t, n_out), out_dtype),
        compiler_params=_cparams(("parallel",)),
        name="norm_proj",
    )(x, gain.reshape(1, d), shift, scale, w_bf16, head_gain.reshape(1, n_out), _group_sum_matrix())


def _lane_tile(a, reps):
    return jnp.concatenate([a] * reps, axis=1)


def _attn_a_kernel(q_ref, k_ref, v_ref, diag_ref, lam_ref, g_ref, o_ref, bias_ref, m_sc, acc_sc,
                   *, tile, nd, lam_init):
    qi = pl.program_id(2)

    @pl.when((pl.program_id(1) == 0) & (qi == 0))
    def _():
        for d in range(nd):
            row = jnp.broadcast_to(diag_ref[d:d + 1, :], (tile, 2 * tile))
            bias_ref[d] = pltpu.roll(row, 0, 1, stride=1, stride_axis=0)[:, :tile]

    q = q_ref[...]
    lane = lax.broadcasted_iota(jnp.int32, q.shape, 1)
    zero = jnp.zeros_like(q)
    qq = jnp.concatenate([jnp.where(lane < HEAD_DIM, q, zero),
                          jnp.where(lane >= HEAD_DIM, q, zero)], axis=0)
    m_sc[...] = jnp.full(m_sc.shape, NEG_INF, F32)
    acc_sc[...] = jnp.zeros(acc_sc.shape, F32)
    ones = jnp.ones((tile, LANES), BF16)

    def kv_tile(j):
        start = pl.multiple_of(j * tile, tile)
        k = k_ref[pl.ds(start, tile), :]
        v_ones = jnp.concatenate([v_ref[pl.ds(start, tile), :], ones], axis=1)
        s = lax.dot_general(qq, k, (((1,), (1,)), ((), ())), preferred_element_type=F32)
        bias = bias_ref[jnp.minimum(qi - j, nd - 1)]
        for c in range(2):
            sc = s[c * tile:(c + 1) * tile] + bias
            m_prev = m_sc[c]
            m_new = jnp.maximum(m_prev, jnp.max(sc, axis=-1, keepdims=True))
            alpha = jnp.exp2(m_prev - m_new)
            p = jnp.exp2(sc - _lane_tile(m_new, tile // LANES))
            acc_sc[c] = (_lane_tile(alpha, 2) * acc_sc[c]
                         + jnp.dot(p.astype(BF16), v_ones, preferred_element_type=F32))
            m_sc[c] = m_new

    def kv_quad(i, carry):
        for u in range(4):
            kv_tile(4 * i + u)
        return carry

    n_kv = qi + 1
    n_quad = n_kv // 4
    lax.fori_loop(0, n_quad, kv_quad, 0)

    @pl.when(n_kv % 4 >= 2)
    def _():
        kv_tile(4 * n_quad)
        kv_tile(4 * n_quad + 1)

    @pl.when(n_kv % 2 == 1)
    def _():
        kv_tile(qi)

    lp = lam_ref[...]
    lam = (jnp.exp(jnp.sum(lp[0:1] * lp[1:2], axis=-1, keepdims=True))
           - jnp.exp(jnp.sum(lp[2:3] * lp[3:4], axis=-1, keepdims=True)) + lam_init)
    a0, a1 = acc_sc[0], acc_sc[1]
    o = a0[:, :LANES] / a0[:, LANES:] - lam * (a1[:, :LANES] / a1[:, LANES:])
    o_ref[...] = (_rms(o, g_ref[...]) * (1.0 - lam_init)).astype(BF16)


def _attn_a(qkv, bias_diags, nd, lam_p, subln, lam_init, batch, seq):
    tile = min(ATT_TILE, seq)
    nq = seq // tile
    qkv3 = qkv.reshape(batch, seq, 3 * N_HEADS * LANES)
    kern = functools.partial(_attn_a_kernel, tile=tile, nd=nd, lam_init=lam_init)
    out = pl.pallas_call(
        kern,
        grid=(N_HEADS, batch, nq),
        in_specs=[pl.BlockSpec((None, tile, LANES), lambda h, b, i: (b, i, h)),
                  pl.BlockSpec((None, seq, LANES), lambda h, b, i: (b, 0, N_HEADS + h)),
                  pl.BlockSpec((None, seq, LANES), lambda h, b, i: (b, 0, 2 * N_HEADS + h)),
                  pl.BlockSpec((None, nd, 2 * tile), lambda h, b, i: (h, 0, 0)),
                  pl.BlockSpec((4, HEAD_DIM), lambda h, b, i: (0, 0)),
                  pl.BlockSpec((1, LANES), lambda h, b, i: (0, 0))],
        out_specs=pl.BlockSpec((None, tile, LANES), lambda h, b, i: (b, i, h)),
        out_shape=jax.ShapeDtypeStruct((batch, seq, N_HEADS * LANES), BF16),
        scratch_shapes=[pltpu.VMEM((nd, tile, tile), F32), pltpu.VMEM((2, tile, LANES), F32),
                        pltpu.VMEM((2, tile, 2 * LANES), F32)],
        compiler_params=_cparams(("arbitrary", "arbitrary", "arbitrary")),
        name="attn_a",
    )(qkv3, qkv3, qkv3, bias_diags, lam_p, subln.reshape(1, LANES))
    return out.reshape(batch * seq, N_HEADS * LANES)


def _attn_b_kernel(*refs, tile):
    ins, o_ref, scratch = refs[:6 * N_GROUPS], refs[6 * N_GROUPS], refs[6 * N_GROUPS + 1:]
    first_tile = pl.program_id(1) == 0
    lane = lax.broadcasted_iota(jnp.int32, (BQ_SUB, LANES), 1)
    low = lane < HEAD_DIM
    ones = jnp.ones((2 * BQ_SUB, LANES), BF16)

    for g, (_, dil) in enumerate(B_GROUPS):
        q_ref, k_ref, v_ref, kp_ref, vp_ref, bias_ref = ins[6 * g:6 * g + 6]
        og_ref, lg_ref = scratch[2 * g:2 * g + 2]
        span = BQ_SUB * dil

        def residue(base, dil=dil):
            return pl.ds(base, BQ_SUB) if dil == 1 else pl.ds(base, BQ_SUB, stride=dil)

        band = {}
        for sb in range(tile // span):
            for r in range(dil):
                base = sb * span + r
                rows = residue(base)
                qp = q_ref[rows, :].astype(BF16)
                band[sb, r] = (k_ref[rows, :].astype(BF16), v_ref[rows, :].astype(BF16))
                if sb == 0:
                    k_prev, v_prev = kp_ref[residue(r), :].astype(BF16), vp_ref[residue(r), :].astype(BF16)
                    variant = jnp.where(first_tile, 0, 1)
                else:
                    k_prev, v_prev = band.pop((sb - 1, r))
                    variant = 1
                kk = jnp.concatenate([k_prev, band[sb, r][0]], axis=0)
                vv = jnp.concatenate([v_prev, band[sb, r][1]], axis=0)
                v_ones = jnp.concatenate([vv, ones], axis=1)
                zero = jnp.zeros_like(qp)
                qq = jnp.concatenate([jnp.where(low, qp, zero), jnp.where(low, zero, qp)], axis=0)
                s = lax.dot_general(qq, kk, (((1,), (1,)), ((), ())), preferred_element_type=F32)
                outs, lses = [], []
                for c in range(2):
                    sc = s[c * BQ_SUB:(c + 1) * BQ_SUB] + bias_ref[variant, c]
                    m = jnp.max(sc, axis=-1, keepdims=True)
                    p = jnp.exp2(sc - m)
                    acc = jnp.dot(p.astype(BF16), v_ones, preferred_element_type=F32)
                    outs.append(acc[:, :LANES] / acc[:, LANES:])
                    lses.append(m + jnp.log2(acc[:, LANES:]))
                og_ref[rows, :] = jnp.where(low, outs[0], outs[1])
                lg_ref[rows, :] = jnp.where(low, lses[0], lses[1])

    l0, l1, l2 = scratch[1][...], scratch[3][...], scratch[5][...]
    m = jnp.maximum(jnp.maximum(l0, l1), l2)
    e0, e1, e2 = jnp.exp2(l0 - m), jnp.exp2(l1 - m), jnp.exp2(l2 - m)
    z = e0 + e1 + e2
    o_ref[...] = ((e0 / z) * scratch[0][...] + (e1 / z) * scratch[2][...] + (e2 / z) * scratch[4][...]).astype(BF16)


def _attn_b(q, kv, bias_tiles, batch, seq):
    tile = min(B_TOKENS, seq)
    pairs = N_HEADS // 2
    q3 = q.reshape(batch, seq, N_GROUPS * B_WIDTH)
    kv3 = kv.reshape(batch, seq, 2 * N_GROUPS * B_WIDTH)
    operands, in_specs = [], []
    for g, (_, dil) in enumerate(B_GROUPS):
        span = BQ_SUB * dil
        assert tile % span == 0
        k_col = lambda hp, g=g: g * pairs + hp
        v_col = lambda hp, g=g: (N_GROUPS + g) * pairs + hp
        prev = lambda n, per=tile // span: jnp.maximum(n * per - 1, 0)
        operands += [q3, kv3, kv3, kv3, kv3, bias_tiles[g]]
        in_specs += [pl.BlockSpec((None, tile, LANES), lambda b, n, hp, c=k_col: (b, n, c(hp))),
                     pl.BlockSpec((None, tile, LANES), lambda b, n, hp, c=k_col: (b, n, c(hp))),
                     pl.BlockSpec((None, tile, LANES), lambda b, n, hp, c=v_col: (b, n, c(hp))),
                     pl.BlockSpec((None, span, LANES), lambda b, n, hp, c=k_col, p=prev: (b, p(n), c(hp))),
                     pl.BlockSpec((None, span, LANES), lambda b, n, hp, c=v_col, p=prev: (b, p(n), c(hp))),
                     pl.BlockSpec((2, 2, BQ_SUB, 2 * BQ_SUB), lambda b, n, hp: (0, hp, 0, 0))]
    out = pl.pallas_call(
        functools.partial(_attn_b_kernel, tile=tile),
        grid=(batch, seq // tile, pairs),
        in_specs=in_specs,
        out_specs=pl.BlockSpec((None, tile, LANES), lambda b, n, hp: (b, n, hp)),
        out_shape=jax.ShapeDtypeStruct((batch, seq, B_WIDTH), BF16),
        scratch_shapes=[pltpu.VMEM((tile, LANES), F32)] * (2 * N_GROUPS),
        compiler_params=_cparams(("parallel", "parallel", "parallel")),
        name="attn_b",
    )(*operands)
    return out.reshape(batch * seq, B_WIDTH)


def _split_bf16(a):
    hi = a.astype(BF16)
    return hi, (a - hi.astype(F32)).astype(BF16)


def _out_tail(o_bf16, wo_ref, x_ref, ga_ref, gf_ref, shf_ref, scf_ref, wr_ref, br_ref, x1_ref, hf_ref, lg_ref):
    y = jnp.dot(o_bf16, wo_ref[...], preferred_element_type=F32)
    x1 = x_ref[...] + ga_ref[...] * y
    x1_ref[...] = x1
    hf = _rms(x1, gf_ref[...]) * (1.0 + scf_ref[...]) + shf_ref[...]
    half = hf.shape[1] // 2
    hf_ref[...] = _pack_pair(hf[:, :half], hf[:, half:])
    hi, lo = _split_bf16(hf)
    both = jnp.dot(hi, wr_ref[...], preferred_element_type=F32)
    lg_ref[...] = (both[:, :ROUTER_COLS] + jnp.dot(lo, wr_ref[:, :ROUTER_COLS], preferred_element_type=F32)
                   + both[:, ROUTER_COLS:] + br_ref[...])


def _out_kernel(o_ref, *rest):
    _out_tail(o_ref[...], *rest)


def _out_proj(att, wo_bf16, x, gate_a, gain_f, shift_f, scale_f, w_router, b_router, seq):
    t, d = x.shape
    tm = min(TOK_TILE, seq)
    per_b = seq // tm
    row = lambda i: (i, 0)
    fixed = lambda i: (0, 0)
    per_batch = lambda i: (i // per_b, 0, 0)
    return pl.pallas_call(
        _out_kernel,
        grid=(t // tm,),
        in_specs=[pl.BlockSpec((tm, att.shape[1]), row),
                  pl.BlockSpec(wo_bf16.shape, fixed),
                  pl.BlockSpec((tm, d), row),
                  pl.BlockSpec((None, 1, d), per_batch),
                  pl.BlockSpec((1, d), fixed),
                  pl.BlockSpec((None, 1, d), per_batch),
                  pl.BlockSpec((None, 1, d), per_batch),
                  pl.BlockSpec((d, 2 * ROUTER_COLS), fixed),
                  pl.BlockSpec((1, ROUTER_COLS), fixed)],
        out_specs=[pl.BlockSpec((tm, d), row), pl.BlockSpec((tm, d // 2), row),
                   pl.BlockSpec((tm, ROUTER_COLS), row)],
        out_shape=[jax.ShapeDtypeStruct((t, d), F32), jax.ShapeDtypeStruct((t, d // 2), jnp.uint32),
                   jax.ShapeDtypeStruct((t, ROUTER_COLS), F32)],
        compiler_params=_cparams(("parallel",)),
        name="out_proj",
    )(att, wo_bf16, x, gate_a, gain_f.reshape(1, d), shift_f, scale_f,
      jnp.concatenate(_split_bf16(w_router), axis=1), b_router)


def _route_kernel(lg_ref, tri_ref, eid_ref, gate_ref, seg_ref, cnt_ref, cnt_sc):
    lt = lg_ref[...].T
    best, g_idx = lt[0:1], jnp.zeros((1, lt.shape[1]), jnp.int32)
    for g in range(1, MOE_GROUPS):
        upd = lt[g:g + 1] > best
        best = jnp.where(upd, lt[g:g + 1], best)
        g_idx = jnp.where(upd, g, g_idx)
    denom = jnp.zeros_like(best)
    for g in range(MOE_GROUPS):
        denom = denom + jnp.exp(lt[g:g + 1] - best)
    g_w = 1.0 / denom
    e_sel = lt[EXPERT_COL0:EXPERT_COL0 + MOE_EPG]
    for g in range(1, MOE_GROUPS):
        r0 = EXPERT_COL0 + g * MOE_EPG
        e_sel = jnp.where(g_idx == g, lt[r0:r0 + MOE_EPG], e_sel)

    def first_max(vals):
        v, i = vals[0:1], jnp.zeros((1, vals.shape[1]), jnp.int32)
        for e in range(1, MOE_EPG):
            upd = vals[e:e + 1] > v
            v = jnp.where(upd, vals[e:e + 1], v)
            i = jnp.where(upd, e, i)
        return v, i

    v1, i1 = first_max(e_sel)
    row = lax.broadcasted_iota(jnp.int32, e_sel.shape, 0)
    v2, i2 = first_max(jnp.where(row == i1, NEG_INF, e_sel))
    e2 = jnp.exp(v2 - v1)
    w1 = 1.0 / (1.0 + e2)
    w2 = e2 / (1.0 + e2)
    e1 = g_idx * MOE_EPG + i1
    e2 = g_idx * MOE_EPG + i2

    @pl.when(pl.program_id(0) == 0)
    def _():
        cnt_sc[...] = jnp.zeros(cnt_sc.shape, F32)

    tm = lt.shape[1]
    erows = lax.broadcasted_iota(jnp.int32, (MOE_EXPERTS, tm), 0)
    ohs, pres = [], []
    for e_k in (e1, e2):
        oh = (erows == e_k).astype(F32)
        ohs.append(oh)
        pres.append(jnp.dot(oh.astype(BF16), tri_ref[...], preferred_element_type=F32))
    n0 = pres[0][:, tm - 1:]
    n_tot = n0 + pres[1][:, tm - 1:]
    chunks = jnp.floor((n_tot + (SEG_ALIGN - 1.0)) * (1.0 / SEG_ALIGN))
    er = lax.broadcasted_iota(jnp.int32, (MOE_EXPERTS, MOE_EXPERTS), 0)
    ec = lax.broadcasted_iota(jnp.int32, (MOE_EXPERTS, MOE_EXPERTS), 1)
    before = jnp.dot((ec < er).astype(BF16), jnp.broadcast_to(chunks, (MOE_EXPERTS, LANES)).astype(BF16),
                     preferred_element_type=F32)
    seg_start = before[:, 0:1] * SEG_ALIGN
    slot0 = jnp.sum(ohs[0] * (seg_start + pres[0] - 1.0), axis=0, keepdims=True)
    slot1 = jnp.sum(ohs[1] * (seg_start + n0 + pres[1] - 1.0), axis=0, keepdims=True)
    seg_off = cnt_sc[...]
    cnt_sc[...] = seg_off + chunks * SEG_ALIGN
    cnt_ref[...] = jnp.broadcast_to(cnt_sc[...], cnt_ref.shape).astype(jnp.int32)
    scol = lax.broadcasted_iota(jnp.int32, seg_ref.shape, 1)
    seg_ref[...] = jnp.where(scol == 0, chunks, jnp.where(scol == 1, seg_start, jnp.where(
        scol == 2, seg_off, 0.0))).astype(jnp.int32)

    erow = lax.broadcasted_iota(jnp.int32, eid_ref.shape, 0)
    eid_ref[...] = jnp.where(erow == 0, e1, jnp.where(erow == 1, e2, jnp.where(
        erow == 2, slot0.astype(jnp.int32), jnp.where(erow == 3, slot1.astype(jnp.int32), 0))))
    grow = lax.broadcasted_iota(jnp.int32, lt.shape, 0)
    gates = jnp.where(grow == 0, g_w * w1, jnp.where(grow == 1, g_w * w2, 0.0))
    gate_ref[...] = gates.T


def _route(logits):
    t = logits.shape[0]
    tm = min(MOE_TILE, t)
    n_tiles = t // tm
    r = np.arange(tm)
    tri = jnp.asarray(r[:, None] <= r[None, :], BF16)
    return pl.pallas_call(
        _route_kernel,
        grid=(n_tiles,),
        in_specs=[pl.BlockSpec((tm, ROUTER_COLS), lambda i: (i, 0)),
                  pl.BlockSpec((tm, tm), lambda i: (0, 0))],
        out_specs=[pl.BlockSpec((8, tm), lambda i: (0, i)), pl.BlockSpec((tm, ROUTER_COLS), lambda i: (i, 0)),
                   pl.BlockSpec((MOE_EXPERTS, LANES), lambda i: (i, 0)),
                   pl.BlockSpec((MOE_EXPERTS, LANES), lambda i: (0, 0))],
        out_shape=[jax.ShapeDtypeStruct((8, t), jnp.int32), jax.ShapeDtypeStruct((t, ROUTER_COLS), F32),
                   jax.ShapeDtypeStruct((n_tiles * MOE_EXPERTS, LANES), jnp.int32),
                   jax.ShapeDtypeStruct((MOE_EXPERTS, LANES), jnp.int32)],
        scratch_shapes=[pltpu.VMEM((MOE_EXPERTS, 1), F32)],
        compiler_params=_cparams(("arbitrary",)),
        name="moe_route",
    )(logits, tri)


def _moe_rows(t):
    n_tiles = t // min(MOE_TILE, t)
    rows = 2 * t + n_tiles * MOE_EXPERTS * (SEG_ALIGN - 1) + MOE_EXPERTS * (MOE_BLK - 1)
    return -(-rows // MOE_BLK) * MOE_BLK


def _chunk_rows(first, count, row0, n_slots):
    c = jnp.arange(n_slots, dtype=jnp.int32)
    run = jnp.sum((first + count)[..., None, :] <= c[:, None], axis=-1)
    hit = run[..., None] == jnp.arange(first.shape[-1], dtype=jnp.int32)
    base = jnp.sum(jnp.where(hit, (row0 - SEG_CHUNK * first)[..., None, :], 0), axis=-1)
    return jnp.where(run < first.shape[-1], base + SEG_CHUNK * c, 0)


def _dispatch_plan(route, seg, totals, n_blocks):
    n_tiles = seg.shape[0] // MOE_EXPERTS
    tm = route.shape[1] // n_tiles
    n_slots = _sorted_rows(tm) // SEG_CHUNK
    padded = (totals + MOE_BLK - 1) // MOE_BLK * MOE_BLK
    pad_end = jnp.cumsum(padded)
    pad_start = pad_end - padded
    n_used = pad_end[-1] // MOE_BLK
    seg = seg.reshape(n_tiles, MOE_EXPERTS, LANES)
    chunks, first, seg_off = seg[:, :, 0], seg[:, :, 1] // SEG_CHUNK, seg[:, :, 2]
    dst = _chunk_rows(first, chunks, pad_start[None, :] + seg_off, n_slots)
    fill = (padded - totals) // SEG_CHUNK
    fill_first = jnp.cumsum(fill) - fill
    fill_dst = _chunk_rows(fill_first, fill, pad_start + totals, MOE_EXPERTS * (MOE_BLK // SEG_CHUNK))
    tail = jnp.stack([jnp.broadcast_to(n_used, (n_tiles,)), jnp.broadcast_to(jnp.sum(fill), (n_tiles,)),
                      jnp.sum(chunks, axis=1)], axis=1)
    table = jnp.concatenate([dst, jnp.zeros((n_tiles, TABLE_W - 3 - n_slots), jnp.int32), tail], axis=1)
    slots = route[2:4].reshape(2, n_tiles, tm).transpose(1, 0, 2).reshape(n_tiles, 1, 2 * tm)
    block_start = jnp.arange(n_blocks, dtype=jnp.int32) * MOE_BLK
    block_e = jnp.minimum(jnp.sum(pad_end[None, :] <= block_start[:, None], axis=1), MOE_EXPERTS - 1).astype(jnp.int32)
    return (table.astype(jnp.int32).reshape(n_tiles, 1, TABLE_W), slots, fill_dst.astype(jnp.int32).reshape(1, 1, -1),
            block_e, n_used.astype(jnp.int32).reshape(1))


def _chunk_copy(hbm_ref, hbm_row, sorted_ref, chunk, sem, to_hbm):
    local = sorted_ref.at[pl.ds(pl.multiple_of(chunk * SEG_CHUNK, SEG_CHUNK), SEG_CHUNK), :]
    remote = hbm_ref.at[pl.ds(pl.multiple_of(hbm_row, SEG_CHUNK), SEG_CHUNK), :]
    return pltpu.make_async_copy(local, remote, sem) if to_hbm else pltpu.make_async_copy(remote, local, sem)


def _start_chunks(table_ref, hbm_ref, sorted_ref, sem, to_hbm):
    def chunk(c, carry):
        _chunk_copy(hbm_ref, table_ref[0, 0, c], sorted_ref, c, sem, to_hbm).start()
        return carry

    lax.fori_loop(0, table_ref[0, 0, TABLE_W - 1], chunk, 0)


def _wait_chunks(count, hbm_ref, sorted_ref, sem, to_hbm):
    def chunk(c, carry):
        _chunk_copy(hbm_ref, 0, sorted_ref, 0, sem, to_hbm).wait()
        return carry

    lax.fori_loop(0, count, chunk, 0)


def _zero_fill(table_ref, fill_ref, xs_ref, zero_sc, sem, n_blocks, wait):
    def finish(copy):
        if wait:
            copy.wait()
        else:
            copy.start()

    def chunk(c, carry):
        row = pl.multiple_of(fill_ref[0, 0, c], SEG_CHUNK)
        finish(pltpu.make_async_copy(zero_sc.at[pl.ds(0, SEG_CHUNK), :], xs_ref.at[pl.ds(row, SEG_CHUNK), :], sem))
        return carry

    def block(b, carry):
        row = pl.multiple_of(b * MOE_BLK, MOE_BLK)
        finish(pltpu.make_async_copy(zero_sc, xs_ref.at[pl.ds(row, MOE_BLK), :], sem))
        return carry

    lax.fori_loop(0, table_ref[0, 0, TABLE_W - 2], chunk, 0)
    lax.fori_loop(table_ref[0, 0, TABLE_W - 3], n_blocks, block, 0)


def _dispatch_kernel(table_ref, slot_ref, fill_ref, h_ref, xs_ref, sorted_sc, zero_sc, pending_sc, sem, zero_sem,
                     *, tm, n_blocks):
    step = pl.program_id(0)
    first_step = step == 0
    parity = step % 2
    mine, other = sorted_sc.at[parity], sorted_sc.at[1 - parity]

    @pl.when(first_step)
    def _():
        sorted_sc[...] = jnp.zeros(sorted_sc.shape, sorted_sc.dtype)
        zero_sc[...] = jnp.zeros(zero_sc.shape, zero_sc.dtype)
        pending_sc[0] = 0
        _zero_fill(table_ref, fill_ref, xs_ref, zero_sc, zero_sem, n_blocks, wait=False)

    def place(i, carry):
        for k in range(2):
            mine[pl.ds(slot_ref[0, 0, k * tm + i], 1), :] = h_ref[pl.ds(i, 1), :]
        return carry

    lax.fori_loop(0, tm, place, 0, unroll=8)
    _start_chunks(table_ref, xs_ref, mine, sem.at[parity], to_hbm=True)
    _wait_chunks(pending_sc[0], xs_ref, other, sem.at[1 - parity], to_hbm=True)
    pending_sc[0] = table_ref[0, 0, TABLE_W - 1]

    @pl.when(step == pl.num_programs(0) - 1)
    def _():
        _wait_chunks(pending_sc[0], xs_ref, mine, sem.at[parity], to_hbm=True)

    @pl.when(first_step)
    def _():
        _zero_fill(table_ref, fill_ref, xs_ref, zero_sc, zero_sem, n_blocks, wait=True)


def _sorted_rows(tm):
    return -(-(2 * tm + MOE_EXPERTS * (SEG_ALIGN - 1)) // SEG_CHUNK) * SEG_CHUNK


def _dispatch(h, table, slots, fill, rows):
    t, d = h.shape
    tm = min(MOE_TILE, t)
    kern = functools.partial(_dispatch_kernel, tm=tm, n_blocks=rows // MOE_BLK)
    return pl.pallas_call(
        kern,
        grid=(t // tm,),
        in_specs=[pl.BlockSpec((1, 1, TABLE_W), lambda i: (i, 0, 0), memory_space=pltpu.SMEM),
                  pl.BlockSpec((1, 1, 2 * tm), lambda i: (i, 0, 0), memory_space=pltpu.SMEM),
                  pl.BlockSpec(fill.shape, lambda i: (0, 0, 0), memory_space=pltpu.SMEM),
                  pl.BlockSpec((tm, d), lambda i: (i, 0))],
        out_specs=pl.BlockSpec(memory_space=pl.ANY),
        out_shape=jax.ShapeDtypeStruct((rows, d), h.dtype),
        scratch_shapes=[pltpu.VMEM((2, _sorted_rows(tm), d), h.dtype), pltpu.VMEM((MOE_BLK, d), h.dtype),
                        pltpu.SMEM((1,), jnp.int32), pltpu.SemaphoreType.DMA((2,)), pltpu.SemaphoreType.DMA(())],
        compiler_params=_cparams(("arbitrary",)),
        name="moe_dispatch",
    )(table, slots, fill, h)


def _pack_pair(lo, hi):
    lo_bits = lax.bitcast_convert_type(lo.astype(BF16).astype(F32), jnp.uint32) >> 16
    hi_bits = lax.bitcast_convert_type(hi.astype(BF16).astype(F32), jnp.uint32) & jnp.uint32(0xFFFF0000)
    return hi_bits | lo_bits


def _unpack_pair(words):
    lo = lax.bitcast_convert_type(words << 16, F32).astype(BF16)
    hi = lax.bitcast_convert_type(words & jnp.uint32(0xFFFF0000), F32).astype(BF16)
    return lo, hi


def _expert_kernel(be_ref, nu_ref, xs_ref, wg_ref, wu_ref, wd_ref, ys_ref, wg_sc, wu_sc, wd_sc):
    i = pl.program_id(0)
    live = i < nu_ref[0]

    @pl.when(jnp.logical_not(live))
    def _():
        ys_ref[...] = jnp.zeros(ys_ref.shape, F32)

    @pl.when(live)
    def _():
        @pl.when((i == 0) | (be_ref[i] != be_ref[jnp.maximum(i - 1, 0)]))
        def _():
            wg_sc[...] = wg_ref[...].astype(BF16)
            wu_sc[...] = wu_ref[...].astype(BF16)
            wd_sc[...] = wd_ref[...].astype(BF16)

        x = jnp.concatenate(_unpack_pair(xs_ref[...]), axis=1)
        g = jnp.dot(x, wg_sc[...], preferred_element_type=F32)
        u = jnp.dot(x, wu_sc[...], preferred_element_type=F32)
        a = (g * (1.0 / (1.0 + jnp.exp(-g))) * u).astype(BF16)
        ys_ref[...] = jnp.dot(a, wd_sc[...], preferred_element_type=F32)


def _experts(xs, block_e, n_used, w_gate, w_up, w_down, layer):
    rows = xs.shape[0]
    d, hid = w_gate.shape[-2:]
    n_blocks = rows // MOE_BLK
    used = lambda i, be, nu: (jnp.minimum(i, nu[0] - 1), 0)
    expert = lambda i, be, nu: (layer, be[jnp.minimum(i, nu[0] - 1)], 0, 0)
    grid_spec = pltpu.PrefetchScalarGridSpec(
        num_scalar_prefetch=2,
        grid=(n_blocks,),
        in_specs=[pl.BlockSpec((MOE_BLK, d // 2), used),
                  pl.BlockSpec((None, None, d, hid), expert),
                  pl.BlockSpec((None, None, d, hid), expert),
                  pl.BlockSpec((None, None, hid, d), expert)],
        out_specs=pl.BlockSpec((MOE_BLK, d), lambda i, be, nu: (i, 0)),
        scratch_shapes=[pltpu.VMEM((d, hid), BF16), pltpu.VMEM((d, hid), BF16), pltpu.VMEM((hid, d), BF16)],
    )
    return pl.pallas_call(
        _expert_kernel,
        grid_spec=grid_spec,
        out_shape=jax.ShapeDtypeStruct((rows, d), F32),
        compiler_params=_cparams(("arbitrary",)),
        name="moe_experts",
    )(block_e, n_used, xs, w_gate, w_up, w_down)


def _combine_kernel(table_ref, next_table_ref, slot_ref, ys_ref, x_ref, gf_ref, gate_ref, o_ref,
                    sorted_sc, buf, sem, *, tm):
    step = pl.program_id(0)
    parity = step % 2
    mine, other = sorted_sc.at[parity], sorted_sc.at[1 - parity]

    @pl.when(step == 0)
    def _():
        _start_chunks(table_ref, ys_ref, mine, sem.at[parity], to_hbm=False)

    @pl.when(step + 1 < pl.num_programs(0))
    def _():
        _start_chunks(next_table_ref, ys_ref, other, sem.at[1 - parity], to_hbm=False)

    _wait_chunks(table_ref[0, 0, TABLE_W - 1], ys_ref, mine, sem.at[parity], to_hbm=False)

    def pick(i, carry):
        for k in range(2):
            buf[k, pl.ds(i, 1), :] = mine[pl.ds(slot_ref[0, 0, k * tm + i], 1), :]
        return carry

    lax.fori_loop(0, tm, pick, 0, unroll=8)
    gates = gate_ref[...]
    y = gates[:, 0:1] * buf[0] + gates[:, 1:2] * buf[1]
    o_ref[...] = x_ref[...] + gf_ref[...] * y


def _combine(ys, table, slots, x, gate_f, gates, seq):
    t, d = x.shape
    tm = min(MOE_TILE, t)
    per_b = seq // tm if seq >= tm else 1
    kern = functools.partial(_combine_kernel, tm=tm)
    last = t // tm - 1
    return pl.pallas_call(
        kern,
        grid=(t // tm,),
        in_specs=[pl.BlockSpec((1, 1, TABLE_W), lambda i: (i, 0, 0), memory_space=pltpu.SMEM),
                  pl.BlockSpec((1, 1, TABLE_W), lambda i: (jnp.minimum(i + 1, last), 0, 0), memory_space=pltpu.SMEM),
                  pl.BlockSpec((1, 1, 2 * tm), lambda i: (i, 0, 0), memory_space=pltpu.SMEM),
                  pl.BlockSpec(memory_space=pl.ANY),
                  pl.BlockSpec((tm, d), lambda i: (i, 0)),
                  pl.BlockSpec((None, 1, d), lambda i: (i // per_b, 0, 0)),
                  pl.BlockSpec((tm, ROUTER_COLS), lambda i: (i, 0))],
        out_specs=pl.BlockSpec((tm, d), lambda i: (i, 0)),
        out_shape=jax.ShapeDtypeStruct((t, d), F32),
        scratch_shapes=[pltpu.VMEM((2, _sorted_rows(tm), d), F32), pltpu.VMEM((2, tm, d), F32),
                        pltpu.SemaphoreType.DMA((2,))],
        compiler_params=_cparams(("arbitrary",)),
        name="moe_combine",
    )(table, table, slots, ys, x, gate_f, gates)


def _moe(h, logits, x1, gate_f, w_gate, w_up, w_down, layer, seq):
    rows = _moe_rows(h.shape[0])
    route, gates, seg, totals = _route(logits)
    table, slots, fill, block_e, n_used = _dispatch_plan(route, seg, totals[:, 0], rows // MOE_BLK)
    xs = _dispatch(h, table, slots, fill, rows)
    ys = _experts(xs, block_e, n_used, w_gate, w_up, w_down, layer)
    return _combine(ys, table, slots, x1, gate_f, gates, seq)


def _router_weights(w_group, b_group, w_expert, b_expert):
    d = w_group.shape[0]
    w = jnp.zeros((d, ROUTER_COLS), F32)
    w = w.at[:, 0:MOE_GROUPS].set(w_group.astype(F32))
    w = w.at[:, EXPERT_COL0:EXPERT_COL0 + MOE_EXPERTS].set(w_expert.astype(F32))
    b = jnp.zeros((1, ROUTER_COLS), F32)
    b = b.at[0, 0:MOE_GROUPS].set(b_group.astype(F32))
    b = b.at[0, EXPERT_COL0:EXPERT_COL0 + MOE_EXPERTS].set(b_expert.astype(F32))
    return w, b


def _lambda_init(layer):
    return 0.8 - 0.6 * math.exp(-0.3 * layer)


def kernel(x, c, rel_bias, ada_w, ada_b, norm_attn, norm_ffn, a_w_qkv, a_q_norm, a_k_norm, a_lambda, a_subln, a_w_o, kv_norm, kv_ada_w, kv_ada_b, kv_w, kv_k_norm, b_w_q, b_q_norm, b_w_o, moe_w_group, moe_b_group, moe_w_expert, moe_b_expert, moe_w_gate, moe_w_up, moe_w_down):
    batch, seq, d = x.shape
    t = batch * seq
    scale = HEAD_DIM ** -0.5
    n_bw = N_GROUPS * B_WIDTH

    c_pad = jnp.zeros((8, d), F32).at[:batch].set(c.astype(F32))
    mod = _modulation(c_pad, ada_w, ada_b)[:, :batch]
    kv_mod = _modulation(c_pad, kv_ada_w[None], kv_ada_b[None])[0, :batch]

    def part(m, i):
        return m[:, i * d:(i + 1) * d].reshape(batch, 1, d)

    a_bias, nd = _attn_a_bias_diags(rel_bias, seq, min(ATT_TILE, seq))
    b_bias = [_attn_b_bias_tiles(rel_bias, window // dil, dil) for window, dil in B_GROUPS]
    xf = x.reshape(t, d).astype(F32)
    kv = None
    for layer in range(DEPTH):
        m = mod[layer]
        sh_a, sc_a, g_a, sh_f, sc_f, g_f = (part(m, i) for i in range(6))
        w_r, b_r = _router_weights(moe_w_group[layer], moe_b_group[layer], moe_w_expert[layer], moe_b_expert[layer])
        if layer < N_A_LAYERS:
            qk_gain = jnp.concatenate([jnp.tile(a_q_norm[layer].astype(F32) * (scale * LOG2_E), 2 * N_HEADS),
                                       jnp.tile(a_k_norm[layer].astype(F32), 2 * N_HEADS),
                                       jnp.ones((N_HEADS * LANES,), F32)])
            qkv = _norm_proj(xf, norm_attn[layer], sh_a, sc_a, a_w_qkv[layer].astype(BF16), qk_gain,
                             2 * N_HEADS * LANES, seq, BF16)
            att = _attn_a(qkv, a_bias, nd, a_lambda[layer].astype(F32), a_subln[layer].astype(F32),
                          _lambda_init(layer), batch, seq)
            wo = a_w_o[layer].astype(BF16)
        else:
            j = layer - N_A_LAYERS
            if kv is None:
                k_gain = jnp.concatenate([jnp.tile(kv_k_norm.astype(F32), (1, N_HEADS)).reshape(-1),
                                          jnp.ones((n_bw,), F32)])
                kv = _norm_proj(xf, kv_norm, part(kv_mod, 0), part(kv_mod, 1), kv_w.astype(BF16), k_gain, n_bw, seq, F32)
            q_gain = jnp.tile(b_q_norm[j].astype(F32) * (scale * LOG2_E), (1, N_HEADS)).reshape(-1)
            q = _norm_proj(xf, norm_attn[layer], sh_a, sc_a, b_w_q[j].astype(BF16), q_gain, n_bw, seq, F32)
            att = _attn_b(q, kv, b_bias, batch, seq)
            wo = b_w_o[j].astype(BF16)
        x1, hf, logits = _out_proj(att, wo, xf, g_a, norm_ffn[layer], sh_f, sc_f, w_r, b_r, seq)
        xf = _moe(hf, logits, x1, g_f, moe_w_gate, moe_w_up, moe_w_down, layer, seq)
    return xf.reshape(batch, seq, d).astype(x.dtype)
```

```python
import functools
import math

import jax
import jax.numpy as jnp
import numpy as np
from jax import lax
from jax.experimental import pallas as pl
from jax.experimental.pallas import tpu as pltpu

F32 = jnp.float32
BF16 = jnp.bfloat16

D_MODEL = 1024
DEPTH = 4
N_A_LAYERS = 2
N_HEADS = 8
HEAD_DIM = 64
B_GROUPS = ((128, 1), (512, 4), (2048, 16))
N_GROUPS = len(B_GROUPS)
B_WIDTH = N_HEADS * HEAD_DIM
REL_BUCKETS = 32
REL_MAX_EXACT = 16
REL_MAX_DIST = 2048
MOE_GROUPS = 4
MOE_EPG = 8
MOE_EXPERTS = MOE_GROUPS * MOE_EPG
MOE_HIDDEN = 512
EPS = 1e-6
NEG_INF = float("-inf")
LOG2_E = math.log2(math.e)

LANES = 128
MXU_DIM = 256
ROUTER_COLS = 128
EXPERT_COL0 = 8

TOK_TILE = 512
ATT_TILE = 512
B_TOKENS = 2048
BQ_SUB = 128
MOE_BLK = 512
MOE_TILE = 512
SEG_ALIGN = 8
SEG_CHUNK = 8
TABLE_W = 256
VMEM_LIMIT = 56 * 1024 * 1024


def _cparams(sem):
    return pltpu.CompilerParams(dimension_semantics=sem, vmem_limit_bytes=VMEM_LIMIT)


def _bucket_table(max_dist):
    n = np.arange(max_dist + 1)
    nf = np.maximum(n, 1).astype(np.float64)
    large = REL_MAX_EXACT + (np.log(nf / REL_MAX_EXACT) / math.log(REL_MAX_DIST / REL_MAX_EXACT)
                             * (REL_BUCKETS - REL_MAX_EXACT)).astype(np.int64)
    large = np.minimum(large, REL_BUCKETS - 1)
    return np.where(n < REL_MAX_EXACT, n, large).astype(np.int32)


def _toeplitz(v, n, m):
    length = v.shape[-1]
    assert length >= n + m - 1 and length - 1 >= m
    lead = v.shape[:-1]
    flat = jnp.tile(v, (1,) * len(lead) + (n,))[..., :n * (length - 1)]
    return flat.reshape(lead + (n, length - 1))[..., :m]


def _diag_values(rel_bias, dist, ok):
    table = _bucket_table(int(dist.max()))
    vals = rel_bias.astype(F32)[table[np.clip(dist, 0, None)]]
    vals = jnp.where(ok[..., None], vals, NEG_INF)
    return jnp.moveaxis(vals, -1, 0)


def _attn_a_bias_diags(rel_bias, seq, tile):
    table = _bucket_table(seq)
    last_start = int(np.argmax(table == REL_BUCKETS - 1))
    nd = 0
    while nd * tile - (tile - 1) < last_start and nd * tile < seq:
        nd += 1
    nd += 1
    length = 2 * tile
    u = np.arange(length)
    u = np.where(u < tile, u, u - length)
    dist = np.arange(nd)[:, None] * tile - u[None, :]
    return _diag_values(rel_bias, dist, dist >= 0) * LOG2_E, nd


def _attn_b_bias_tiles(rel_bias, band, dil):
    length = 3 * BQ_SUB
    u = np.arange(length)
    u = np.where(u < 2 * BQ_SUB, u, u - length)
    rel = BQ_SUB - u
    ok = (rel >= 0) & (rel <= band)
    general = _toeplitz(_diag_values(rel_bias, dil * rel, ok) * LOG2_E, BQ_SUB, 2 * BQ_SUB)
    no_prev = np.arange(2 * BQ_SUB)[None, None, :] >= BQ_SUB
    return jnp.stack([jnp.where(no_prev, general, NEG_INF), general])


def _group_sum_matrix():
    r = np.arange(MXU_DIM)
    return jnp.asarray((r[:, None] // HEAD_DIM) == (r[None, :] // HEAD_DIM), BF16)


def _mod_kernel(c_ref, w_ref, b_ref, o_ref):
    c = c_ref[...]
    c_act = c * (1.0 / (1.0 + jnp.exp(-c)))
    o_ref[...] = jnp.dot(c_act, w_ref[...], preferred_element_type=F32,
                         precision=lax.Precision.HIGHEST) + b_ref[...]


def _modulation(c_pad, w, b, tn=1024):
    nl, d, n = w.shape
    return pl.pallas_call(
        _mod_kernel,
        grid=(nl, n // tn),
        in_specs=[pl.BlockSpec((8, d), lambda l, j: (0, 0)),
                  pl.BlockSpec((None, d, tn), lambda l, j: (l, 0, j)),
                  pl.BlockSpec((None, 1, tn), lambda l, j: (l, 0, j))],
        out_specs=pl.BlockSpec((None, 8, tn), lambda l, j: (l, 0, j)),
        out_shape=jax.ShapeDtypeStruct((nl, 8, n), F32),
        compiler_params=_cparams(("parallel", "parallel")),
        name="adaln_mod",
    )(c_pad, w, b.reshape(nl, 1, n))


def _rms(x, gain):
    return x * lax.rsqrt(jnp.mean(x * x, axis=-1, keepdims=True) + EPS) * gain


def _proj_kernel(x_ref, g_ref, sh_ref, sc_ref, w_ref, hg_ref, bd_ref, o_ref, *, n_out, n_norm):
    h = _rms(x_ref[...], g_ref[...]) * (1.0 + sc_ref[...]) + sh_ref[...]
    hb = h.astype(BF16)
    wide = 2 * MXU_DIM
    for c0 in range(0, n_out, wide):
        a = jnp.dot(hb, w_ref[:, c0:c0 + wide], preferred_element_type=F32)
        if c0 < n_norm:
            sq = (a * a).astype(BF16)
            ms = jnp.concatenate([jnp.dot(sq[:, :MXU_DIM], bd_ref[...], preferred_element_type=F32),
                                  jnp.dot(sq[:, MXU_DIM:], bd_ref[...], preferred_element_type=F32)], axis=1)
            a = a * lax.rsqrt(ms * (1.0 / HEAD_DIM) + EPS) * hg_ref[:, c0:c0 + wide]
        o_ref[:, c0:c0 + wide] = a.astype(o_ref.dtype)


def _norm_proj(x, gain, shift, scale, w_bf16, head_gain, n_norm, seq, out_dtype):
    t, d = x.shape
    n_out = w_bf16.shape[1]
    tm = min(TOK_TILE, seq)
    per_b = seq // tm
    kern = functools.partial(_proj_kernel, n_out=n_out, n_norm=n_norm)
    return pl.pallas_call(
        kern,
        grid=(t // tm,),
        in_specs=[pl.BlockSpec((tm, d), lambda i: (i, 0)),
                  pl.BlockSpec((1, d), lambda i: (0, 0)),
                  pl.BlockSpec((None, 1, d), lambda i: (i // per_b, 0, 0)),
                  pl.BlockSpec((None, 1, d), lambda i: (i // per_b, 0, 0)),
                  pl.BlockSpec((d, n_out), lambda i: (0, 0)),
                  pl.BlockSpec((1, n_out), lambda i: (0, 0)),
                  pl.BlockSpec((MXU_DIM, MXU_DIM), lambda i: (0, 0))],
        out_specs=pl.BlockSpec((tm, n_out), lambda i: (i, 0)),
        out_shape=jax.ShapeDtypeStruct((t, n_out), out_dtype),
        compiler_params=_cparams(("parallel",)),
        name="norm_proj",
    )(x, gain.reshape(1, d), shift, scale, w_bf16, head_gain.reshape(1, n_out), _group_sum_matrix())


def _lane_tile(a, reps):
    return jnp.concatenate([a] * reps, axis=1)


def _attn_a_kernel(q_ref, k_ref, v_ref, diag_ref, lam_ref, g_ref, o_ref, bias_ref, m_sc, acc_sc,
                   *, tile, nd, lam_init):
    qi = pl.program_id(2)

    @pl.when((pl.program_id(1) == 0) & (qi == 0))
    def _():
        for d in range(nd):
            row = jnp.broadcast_to(diag_ref[d:d + 1, :], (tile, 2 * tile))
            bias_ref[d] = pltpu.roll(row, 0, 1, stride=1, stride_axis=0)[:, :tile]

    q = q_ref[...]
    lane = lax.broadcasted_iota(jnp.int32, q.shape, 1)
    zero = jnp.zeros_like(q)
    qq = jnp.concatenate([jnp.where(lane < HEAD_DIM, q, zero),
                          jnp.where(lane >= HEAD_DIM, q, zero)], axis=0)
    m_sc[...] = jnp.full(m_sc.shape, NEG_INF, F32)
    acc_sc[...] = jnp.zeros(acc_sc.shape, F32)
    ones = jnp.ones((tile, LANES), BF16)

    def kv_tile(j):
        start = pl.multiple_of(j * tile, tile)
        k = k_ref[pl.ds(start, tile), :]
        v_ones = jnp.concatenate([v_ref[pl.ds(start, tile), :], ones], axis=1)
        s = lax.dot_general(qq, k, (((1,), (1,)), ((), ())), preferred_element_type=F32)
        bias = bias_ref[jnp.minimum(qi - j, nd - 1)]
        for c in range(2):
            sc = s[c * tile:(c + 1) * tile] + bias
            m_prev = m_sc[c]
            m_new = jnp.maximum(m_prev, jnp.max(sc, axis=-1, keepdims=True))
            alpha = jnp.exp2(m_prev - m_new)
            p = jnp.exp2(sc - _lane_tile(m_new, tile // LANES))
            acc_sc[c] = (_lane_tile(alpha, 2) * acc_sc[c]
                         + jnp.dot(p.astype(BF16), v_ones, preferred_element_type=F32))
            m_sc[c] = m_new

    def kv_quad(i, carry):
        for u in range(4):
            kv_tile(4 * i + u)
        return carry

    n_kv = qi + 1
    n_quad = n_kv // 4
    lax.fori_loop(0, n_quad, kv_quad, 0)

    @pl.when(n_kv % 4 >= 2)
    def _():
        kv_tile(4 * n_quad)
        kv_tile(4 * n_quad + 1)

    @pl.when(n_kv % 2 == 1)
    def _():
        kv_tile(qi)

    lp = lam_ref[...]
    lam = (jnp.exp(jnp.sum(lp[0:1] * lp[1:2], axis=-1, keepdims=True))
           - jnp.exp(jnp.sum(lp[2:3] * lp[3:4], axis=-1, keepdims=True)) + lam_init)
    a0, a1 = acc_sc[0], acc_sc[1]
    o = a0[:, :LANES] / a0[:, LANES:] - lam * (a1[:, :LANES] / a1[:, LANES:])
    o_ref[...] = (_rms(o, g_ref[...]) * (1.0 - lam_init)).astype(BF16)


def _attn_a(qkv, bias_diags, nd, lam_p, subln, lam_init, batch, seq):
    tile = min(ATT_TILE, seq)
    nq = seq // tile
    qkv3 = qkv.reshape(batch, seq, 3 * N_HEADS * LANES)
    kern = functools.partial(_attn_a_kernel, tile=tile, nd=nd, lam_init=lam_init)
    out = pl.pallas_call(
        kern,
        grid=(N_HEADS, batch, nq),
        in_specs=[pl.BlockSpec((None, tile, LANES), lambda h, b, i: (b, i, h)),
                  pl.BlockSpec((None, seq, LANES), lambda h, b, i: (b, 0, N_HEADS + h)),
                  pl.BlockSpec((None, seq, LANES), lambda h, b, i: (b, 0, 2 * N_HEADS + h)),
                  pl.BlockSpec((None, nd, 2 * tile), lambda h, b, i: (h, 0, 0)),
                  pl.BlockSpec((4, HEAD_DIM), lambda h, b, i: (0, 0)),
                  pl.BlockSpec((1, LANES), lambda h, b, i: (0, 0))],
        out_specs=pl.BlockSpec((None, tile, LANES), lambda h, b, i: (b, i, h)),
        out_shape=jax.ShapeDtypeStruct((batch, seq, N_HEADS * LANES), BF16),
        scratch_shapes=[pltpu.VMEM((nd, tile, tile), F32), pltpu.VMEM((2, tile, LANES), F32),
                        pltpu.VMEM((2, tile, 2 * LANES), F32)],
        compiler_params=_cparams(("arbitrary", "arbitrary", "arbitrary")),
        name="attn_a",
    )(qkv3, qkv3, qkv3, bias_diags, lam_p, subln.reshape(1, LANES))
    return out.reshape(batch * seq, N_HEADS * LANES)


def _attn_b_kernel(*refs, tile):
    ins, o_ref, scratch = refs[:6 * N_GROUPS], refs[6 * N_GROUPS], refs[6 * N_GROUPS + 1:]
    first_tile = pl.program_id(1) == 0
    lane = lax.broadcasted_iota(jnp.int32, (BQ_SUB, LANES), 1)
    low = lane < HEAD_DIM
    ones = jnp.ones((2 * BQ_SUB, LANES), BF16)

    for g, (_, dil) in enumerate(B_GROUPS):
        q_ref, k_ref, v_ref, kp_ref, vp_ref, bias_ref = ins[6 * g:6 * g + 6]
        og_ref, lg_ref = scratch[2 * g:2 * g + 2]
        span = BQ_SUB * dil

        def residue(base, dil=dil):
            return pl.ds(base, BQ_SUB) if dil == 1 else pl.ds(base, BQ_SUB, stride=dil)

        band = {}
        for sb in range(tile // span):
            for r in range(dil):
                base = sb * span + r
                rows = residue(base)
                qp = q_ref[rows, :].astype(BF16)
                band[sb, r] = (k_ref[rows, :].astype(BF16), v_ref[rows, :].astype(BF16))
                if sb == 0:
                    k_prev, v_prev = kp_ref[residue(r), :].astype(BF16), vp_ref[residue(r), :].astype(BF16)
                    variant = jnp.where(first_tile, 0, 1)
                else:
                    k_prev, v_prev = band.pop((sb - 1, r))
                    variant = 1
                kk = jnp.concatenate([k_prev, band[sb, r][0]], axis=0)
                vv = jnp.concatenate([v_prev, band[sb, r][1]], axis=0)
                v_ones = jnp.concatenate([vv, ones], axis=1)
                zero = jnp.zeros_like(qp)
                qq = jnp.concatenate([jnp.where(low, qp, zero), jnp.where(low, zero, qp)], axis=0)
                s = lax.dot_general(qq, kk, (((1,), (1,)), ((), ())), preferred_element_type=F32)
                outs, lses = [], []
                for c in range(2):
                    sc = s[c * BQ_SUB:(c + 1) * BQ_SUB] + bias_ref[variant, c]
                    m = jnp.max(sc, axis=-1, keepdims=True)
                    p = jnp.exp2(sc - m)
                    acc = jnp.dot(p.astype(BF16), v_ones, preferred_element_type=F32)
                    outs.append(acc[:, :LANES] / acc[:, LANES:])
                    lses.append(m + jnp.log2(acc[:, LANES:]))
                og_ref[rows, :] = jnp.where(low, outs[0], outs[1])
                lg_ref[rows, :] = jnp.where(low, lses[0], lses[1])

    l0, l1, l2 = scratch[1][...], scratch[3][...], scratch[5][...]
    m = jnp.maximum(jnp.maximum(l0, l1), l2)
    e0, e1, e2 = jnp.exp2(l0 - m), jnp.exp2(l1 - m), jnp.exp2(l2 - m)
    z = e0 + e1 + e2
    o_ref[...] = ((e0 / z) * scratch[0][...] + (e1 / z) * scratch[2][...] + (e2 / z) * scratch[4][...]).astype(BF16)


def _attn_b(q, kv, bias_tiles, batch, seq):
    tile = min(B_TOKENS, seq)
    pairs = N_HEADS // 2
    q3 = q.reshape(batch, seq, N_GROUPS * B_WIDTH)
    kv3 = kv.reshape(batch, seq, 2 * N_GROUPS * B_WIDTH)
    operands, in_specs = [], []
    for g, (_, dil) in enumerate(B_GROUPS):
        span = BQ_SUB * dil
        assert tile % span == 0
        k_col = lambda hp, g=g: g * pairs + hp
        v_col = lambda hp, g=g: (N_GROUPS + g) * pairs + hp
        prev = lambda n, per=tile // span: jnp.maximum(n * per - 1, 0)
        operands += [q3, kv3, kv3, kv3, kv3, bias_tiles[g]]
        in_specs += [pl.BlockSpec((None, tile, LANES), lambda b, n, hp, c=k_col: (b, n, c(hp))),
                     pl.BlockSpec((None, tile, LANES), lambda b, n, hp, c=k_col: (b, n, c(hp))),
                     pl.BlockSpec((None, tile, LANES), lambda b, n, hp, c=v_col: (b, n, c(hp))),
                     pl.BlockSpec((None, span, LANES), lambda b, n, hp, c=k_col, p=prev: (b, p(n), c(hp))),
                     pl.BlockSpec((None, span, LANES), lambda b, n, hp, c=v_col, p=prev: (b, p(n), c(hp))),
                     pl.BlockSpec((2, 2, BQ_SUB, 2 * BQ_SUB), lambda b, n, hp: (0, hp, 0, 0))]
    out = pl.pallas_call(
        functools.partial(_attn_b_kernel, tile=tile),
        grid=(batch, seq // tile, pairs),
        in_specs=in_specs,
        out_specs=pl.BlockSpec((None, tile, LANES), lambda b, n, hp: (b, n, hp)),
        out_shape=jax.ShapeDtypeStruct((batch, seq, B_WIDTH), BF16),
        scratch_shapes=[pltpu.VMEM((tile, LANES), F32)] * (2 * N_GROUPS),
        compiler_params=_cparams(("parallel", "parallel", "parallel")),
        name="attn_b",
    )(*operands)
    return out.reshape(batch * seq, B_WIDTH)


def _split_bf16(a):
    hi = a.astype(BF16)
    return hi, (a - hi.astype(F32)).astype(BF16)


def _out_tail(o_bf16, wo_ref, x_ref, ga_ref, gf_ref, shf_ref, scf_ref, wr_ref, br_ref, x1_ref, hf_ref, lg_ref):
    y = jnp.dot(o_bf16, wo_ref[...], preferred_element_type=F32)
    x1 = x_ref[...] + ga_ref[...] * y
    x1_ref[...] = x1
    hf = _rms(x1, gf_ref[...]) * (1.0 + scf_ref[...]) + shf_ref[...]
    half = hf.shape[1] // 2
    hf_ref[...] = _pack_pair(hf[:, :half], hf[:, half:])
    hi, lo = _split_bf16(hf)
    both = jnp.dot(hi, wr_ref[...], preferred_element_type=F32)
    lg_ref[...] = (both[:, :ROUTER_COLS] + jnp.dot(lo, wr_ref[:, :ROUTER_COLS], preferred_element_type=F32)
                   + both[:, ROUTER_COLS:] + br_ref[...])


def _out_kernel(o_ref, *rest):
    _out_tail(o_ref[...], *rest)


def _out_proj(att, wo_bf16, x, gate_a, gain_f, shift_f, scale_f, w_router, b_router, seq):
    t, d = x.shape
    tm = min(TOK_TILE, seq)
    per_b = seq // tm
    row = lambda i: (i, 0)
    fixed = lambda i: (0, 0)
    per_batch = lambda i: (i // per_b, 0, 0)
    return pl.pallas_call(
        _out_kernel,
        grid=(t // tm,),
        in_specs=[pl.BlockSpec((tm, att.shape[1]), row),
                  pl.BlockSpec(wo_bf16.shape, fixed),
                  pl.BlockSpec((tm, d), row),
                  pl.BlockSpec((None, 1, d), per_batch),
                  pl.BlockSpec((1, d), fixed),
                  pl.BlockSpec((None, 1, d), per_batch),
                  pl.BlockSpec((None, 1, d), per_batch),
                  pl.BlockSpec((d, 2 * ROUTER_COLS), fixed),
                  pl.BlockSpec((1, ROUTER_COLS), fixed)],
        out_specs=[pl.BlockSpec((tm, d), row), pl.BlockSpec((tm, d // 2), row),
                   pl.BlockSpec((tm, ROUTER_COLS), row)],
        out_shape=[jax.ShapeDtypeStruct((t, d), F32), jax.ShapeDtypeStruct((t, d // 2), jnp.uint32),
                   jax.ShapeDtypeStruct((t, ROUTER_COLS), F32)],
        compiler_params=_cparams(("parallel",)),
        name="out_proj",
    )(att, wo_bf16, x, gate_a, gain_f.reshape(1, d), shift_f, scale_f,
      jnp.concatenate(_split_bf16(w_router), axis=1), b_router)


def _route_kernel(lg_ref, tri_ref, eid_ref, gate_ref, seg_ref, cnt_ref, cnt_sc):
    lt = lg_ref[...].T
    best, g_idx = lt[0:1], jnp.zeros((1, lt.shape[1]), jnp.int32)
    for g in range(1, MOE_GROUPS):
        upd = lt[g:g + 1] > best
        best = jnp.where(upd, lt[g:g + 1], best)
        g_idx = jnp.where(upd, g, g_idx)
    denom = jnp.zeros_like(best)
    for g in range(MOE_GROUPS):
        denom = denom + jnp.exp(lt[g:g + 1] - best)
    g_w = 1.0 / denom
    e_sel = lt[EXPERT_COL0:EXPERT_COL0 + MOE_EPG]
    for g in range(1, MOE_GROUPS):
        r0 = EXPERT_COL0 + g * MOE_EPG
        e_sel = jnp.where(g_idx == g, lt[r0:r0 + MOE_EPG], e_sel)

    def first_max(vals):
        v, i = vals[0:1], jnp.zeros((1, vals.shape[1]), jnp.int32)
        for e in range(1, MOE_EPG):
            upd = vals[e:e + 1] > v
            v = jnp.where(upd, vals[e:e + 1], v)
            i = jnp.where(upd, e, i)
        return v, i

    v1, i1 = first_max(e_sel)
    row = lax.broadcasted_iota(jnp.int32, e_sel.shape, 0)
    v2, i2 = first_max(jnp.where(row == i1, NEG_INF, e_sel))
    e2 = jnp.exp(v2 - v1)
    w1 = 1.0 / (1.0 + e2)
    w2 = e2 / (1.0 + e2)
    e1 = g_idx * MOE_EPG + i1
    e2 = g_idx * MOE_EPG + i2

    @pl.when(pl.program_id(0) == 0)
    def _():
        cnt_sc[...] = jnp.zeros(cnt_sc.shape, F32)

    tm = lt.shape[1]
    erows = lax.broadcasted_iota(jnp.int32, (MOE_EXPERTS, tm), 0)
    ohs, pres = [], []
    for e_k in (e1, e2):
        oh = (erows == e_k).astype(F32)
        ohs.append(oh)
        pres.append(jnp.dot(oh.astype(BF16), tri_ref[...], preferred_element_type=F32))
    n0 = pres[0][:, tm - 1:]
    n_tot = n0 + pres[1][:, tm - 1:]
    chunks = jnp.floor((n_tot + (SEG_ALIGN - 1.0)) * (1.0 / SEG_ALIGN))
    er = lax.broadcasted_iota(jnp.int32, (MOE_EXPERTS, MOE_EXPERTS), 0)
    ec = lax.broadcasted_iota(jnp.int32, (MOE_EXPERTS, MOE_EXPERTS), 1)
    before = jnp.dot((ec < er).astype(BF16), jnp.broadcast_to(chunks, (MOE_EXPERTS, LANES)).astype(BF16),
                     preferred_element_type=F32)
    seg_start = before[:, 0:1] * SEG_ALIGN
    slot0 = jnp.sum(ohs[0] * (seg_start + pres[0] - 1.0), axis=0, keepdims=True)
    slot1 = jnp.sum(ohs[1] * (seg_start + n0 + pres[1] - 1.0), axis=0, keepdims=True)
    seg_off = cnt_sc[...]
    cnt_sc[...] = seg_off + chunks * SEG_ALIGN
    cnt_ref[...] = jnp.broadcast_to(cnt_sc[...], cnt_ref.shape).astype(jnp.int32)
    scol = lax.broadcasted_iota(jnp.int32, seg_ref.shape, 1)
    seg_ref[...] = jnp.where(scol == 0, chunks, jnp.where(scol == 1, seg_start, jnp.where(
        scol == 2, seg_off, 0.0))).astype(jnp.int32)

    erow = lax.broadcasted_iota(jnp.int32, eid_ref.shape, 0)
    eid_ref[...] = jnp.where(erow == 0, e1, jnp.where(erow == 1, e2, jnp.where(
        erow == 2, slot0.astype(jnp.int32), jnp.where(erow == 3, slot1.astype(jnp.int32), 0))))
    grow = lax.broadcasted_iota(jnp.int32, lt.shape, 0)
    gates = jnp.where(grow == 0, g_w * w1, jnp.where(grow == 1, g_w * w2, 0.0))
    gate_ref[...] = gates.T


def _route(logits):
    t = logits.shape[0]
    tm = min(MOE_TILE, t)
    n_tiles = t // tm
    r = np.arange(tm)
    tri = jnp.asarray(r[:, None] <= r[None, :], BF16)
    return pl.pallas_call(
        _route_kernel,
        grid=(n_tiles,),
        in_specs=[pl.BlockSpec((tm, ROUTER_COLS), lambda i: (i, 0)),
                  pl.BlockSpec((tm, tm), lambda i: (0, 0))],
        out_specs=[pl.BlockSpec((8, tm), lambda i: (0, i)), pl.BlockSpec((tm, ROUTER_COLS), lambda i: (i, 0)),
                   pl.BlockSpec((MOE_EXPERTS, LANES), lambda i: (i, 0)),
                   pl.BlockSpec((MOE_EXPERTS, LANES), lambda i: (0, 0))],
        out_shape=[jax.ShapeDtypeStruct((8, t), jnp.int32), jax.ShapeDtypeStruct((t, ROUTER_COLS), F32),
                   jax.ShapeDtypeStruct((n_tiles * MOE_EXPERTS, LANES), jnp.int32),
                   jax.ShapeDtypeStruct((MOE_EXPERTS, LANES), jnp.int32)],
        scratch_shapes=[pltpu.VMEM((MOE_EXPERTS, 1), F32)],
        compiler_params=_cparams(("arbitrary",)),
        name="moe_route",
    )(logits, tri)


def _moe_rows(t):
    n_tiles = t // min(MOE_TILE, t)
    rows = 2 * t + n_tiles * MOE_EXPERTS * (SEG_ALIGN - 1) + MOE_EXPERTS * (MOE_BLK - 1)
    return -(-rows // MOE_BLK) * MOE_BLK


def _chunk_rows(first, count, row0, n_slots):
    c = jnp.arange(n_slots, dtype=jnp.int32)
    run = jnp.sum((first + count)[..., None, :] <= c[:, None], axis=-1)
    hit = run[..., None] == jnp.arange(first.shape[-1], dtype=jnp.int32)
    base = jnp.sum(jnp.where(hit, (row0 - SEG_CHUNK * first)[..., None, :], 0), axis=-1)
    return jnp.where(run < first.shape[-1], base + SEG_CHUNK * c, 0)


def _dispatch_plan(route, seg, totals, n_blocks):
    n_tiles = seg.shape[0] // MOE_EXPERTS
    tm = route.shape[1] // n_tiles
    n_slots = _sorted_rows(tm) // SEG_CHUNK
    padded = (totals + MOE_BLK - 1) // MOE_BLK * MOE_BLK
    pad_end = jnp.cumsum(padded)
    pad_start = pad_end - padded
    n_used = pad_end[-1] // MOE_BLK
    seg = seg.reshape(n_tiles, MOE_EXPERTS, LANES)
    chunks, first, seg_off = seg[:, :, 0], seg[:, :, 1] // SEG_CHUNK, seg[:, :, 2]
    dst = _chunk_rows(first, chunks, pad_start[None, :] + seg_off, n_slots)
    fill = (padded - totals) // SEG_CHUNK
    fill_first = jnp.cumsum(fill) - fill
    fill_dst = _chunk_rows(fill_first, fill, pad_start + totals, MOE_EXPERTS * (MOE_BLK // SEG_CHUNK))
    tail = jnp.stack([jnp.broadcast_to(n_used, (n_tiles,)), jnp.broadcast_to(jnp.sum(fill), (n_tiles,)),
                      jnp.sum(chunks, axis=1)], axis=1)
    table = jnp.concatenate([dst, jnp.zeros((n_tiles, TABLE_W - 3 - n_slots), jnp.int32), tail], axis=1)
    slots = route[2:4].reshape(2, n_tiles, tm).transpose(1, 0, 2).reshape(n_tiles, 1, 2 * tm)
    block_start = jnp.arange(n_blocks, dtype=jnp.int32) * MOE_BLK
    block_e = jnp.minimum(jnp.sum(pad_end[None, :] <= block_start[:, None], axis=1), MOE_EXPERTS - 1).astype(jnp.int32)
    return (table.astype(jnp.int32).reshape(n_tiles, 1, TABLE_W), slots, fill_dst.astype(jnp.int32).reshape(1, 1, -1),
            block_e, n_used.astype(jnp.int32).reshape(1))


def _chunk_copy(hbm_ref, hbm_row, sorted_ref, chunk, sem, to_hbm):
    local = sorted_ref.at[pl.ds(pl.multiple_of(chunk * SEG_CHUNK, SEG_CHUNK), SEG_CHUNK), :]
    remote = hbm_ref.at[pl.ds(pl.multiple_of(hbm_row, SEG_CHUNK), SEG_CHUNK), :]
    return pltpu.make_async_copy(local, remote, sem) if to_hbm else pltpu.make_async_copy(remote, local, sem)


def _start_chunks(table_ref, hbm_ref, sorted_ref, sem, to_hbm):
    def chunk(c, carry):
        _chunk_copy(hbm_ref, table_ref[0, 0, c], sorted_ref, c, sem, to_hbm).start()
        return carry

    lax.fori_loop(0, table_ref[0, 0, TABLE_W - 1], chunk, 0)


def _wait_chunks(count, hbm_ref, sorted_ref, sem, to_hbm):
    def chunk(c, carry):
        _chunk_copy(hbm_ref, 0, sorted_ref, 0, sem, to_hbm).wait()
        return carry

    lax.fori_loop(0, count, chunk, 0)


def _zero_fill(table_ref, fill_ref, xs_ref, zero_sc, sem, n_blocks, wait):
    def finish(copy):
        if wait:
            copy.wait()
        else:
            copy.start()

    def chunk(c, carry):
        row = pl.multiple_of(fill_ref[0, 0, c], SEG_CHUNK)
        finish(pltpu.make_async_copy(zero_sc.at[pl.ds(0, SEG_CHUNK), :], xs_ref.at[pl.ds(row, SEG_CHUNK), :], sem))
        return carry

    def block(b, carry):
        row = pl.multiple_of(b * MOE_BLK, MOE_BLK)
        finish(pltpu.make_async_copy(zero_sc, xs_ref.at[pl.ds(row, MOE_BLK), :], sem))
        return carry

    lax.fori_loop(0, table_ref[0, 0, TABLE_W - 2], chunk, 0)
    lax.fori_loop(table_ref[0, 0, TABLE_W - 3], n_blocks, block, 0)


def _dispatch_kernel(table_ref, slot_ref, fill_ref, h_ref, xs_ref, sorted_sc, zero_sc, pending_sc, sem, zero_sem,
                     *, tm, n_blocks):
    step = pl.program_id(0)
    first_step = step == 0
    parity = step % 2
    mine, other = sorted_sc.at[parity], sorted_sc.at[1 - parity]

    @pl.when(first_step)
    def _():
        sorted_sc[...] = jnp.zeros(sorted_sc.shape, sorted_sc.dtype)
        zero_sc[...] = jnp.zeros(zero_sc.shape, zero_sc.dtype)
        pending_sc[0] = 0
        _zero_fill(table_ref, fill_ref, xs_ref, zero_sc, zero_sem, n_blocks, wait=False)

    def place(i, carry):
        for k in range(2):
            mine[pl.ds(slot_ref[0, 0, k * tm + i], 1), :] = h_ref[pl.ds(i, 1), :]
        return carry

    lax.fori_loop(0, tm, place, 0, unroll=8)
    _start_chunks(table_ref, xs_ref, mine, sem.at[parity], to_hbm=True)
    _wait_chunks(pending_sc[0], xs_ref, other, sem.at[1 - parity], to_hbm=True)
    pending_sc[0] = table_ref[0, 0, TABLE_W - 1]

    @pl.when(step == pl.num_programs(0) - 1)
    def _():
        _wait_chunks(pending_sc[0], xs_ref, mine, sem.at[parity], to_hbm=True)

    @pl.when(first_step)
    def _():
        _zero_fill(table_ref, fill_ref, xs_ref, zero_sc, zero_sem, n_blocks, wait=True)


def _sorted_rows(tm):
    return -(-(2 * tm + MOE_EXPERTS * (SEG_ALIGN - 1)) // SEG_CHUNK) * SEG_CHUNK


def _dispatch(h, table, slots, fill, rows):
    t, d = h.shape
    tm = min(MOE_TILE, t)
    kern = functools.partial(_dispatch_kernel, tm=tm, n_blocks=rows // MOE_BLK)
    return pl.pallas_call(
        kern,
        grid=(t // tm,),
        in_specs=[pl.BlockSpec((1, 1, TABLE_W), lambda i: (i, 0, 0), memory_space=pltpu.SMEM),
                  pl.BlockSpec((1, 1, 2 * tm), lambda i: (i, 0, 0), memory_space=pltpu.SMEM),
                  pl.BlockSpec(fill.shape, lambda i: (0, 0, 0), memory_space=pltpu.SMEM),
                  pl.BlockSpec((tm, d), lambda i: (i, 0))],
        out_specs=pl.BlockSpec(memory_space=pl.ANY),
        out_shape=jax.ShapeDtypeStruct((rows, d), h.dtype),
        scratch_shapes=[pltpu.VMEM((2, _sorted_rows(tm), d), h.dtype), pltpu.VMEM((MOE_BLK, d), h.dtype),
                        pltpu.SMEM((1,), jnp.int32), pltpu.SemaphoreType.DMA((2,)), pltpu.SemaphoreType.DMA(())],
        compiler_params=_cparams(("arbitrary",)),
        name="moe_dispatch",
    )(table, slots, fill, h)


def _pack_pair(lo, hi):
    lo_bits = lax.bitcast_convert_type(lo.astype(BF16).astype(F32), jnp.uint32) >> 16
    hi_bits = lax.bitcast_convert_type(hi.astype(BF16).astype(F32), jnp.uint32) & jnp.uint32(0xFFFF0000)
    return hi_bits | lo_bits


def _unpack_pair(words):
    lo = lax.bitcast_convert_type(words << 16, F32).astype(BF16)
    hi = lax.bitcast_convert_type(words & jnp.uint32(0xFFFF0000), F32).astype(BF16)
    return lo, hi


def _expert_kernel(be_ref, nu_ref, xs_ref, wg_ref, wu_ref, wd_ref, ys_ref, wg_sc, wu_sc, wd_sc):
    i = pl.program_id(0)
    live = i < nu_ref[0]

    @pl.when(jnp.logical_not(live))
    def _():
        ys_ref[...] = jnp.zeros(ys_ref.shape, F32)

    @pl.when(live)
    def _():
        @pl.when((i == 0) | (be_ref[i] != be_ref[jnp.maximum(i - 1, 0)]))
        def _():
            wg_sc[...] = wg_ref[...].astype(BF16)
            wu_sc[...] = wu_ref[...].astype(BF16)
            wd_sc[...] = wd_ref[...].astype(BF16)

        x = jnp.concatenate(_unpack_pair(xs_ref[...]), axis=1)
        g = jnp.dot(x, wg_sc[...], preferred_element_type=F32)
        u = jnp.dot(x, wu_sc[...], preferred_element_type=F32)
        a = (g * (1.0 / (1.0 + jnp.exp(-g))) * u).astype(BF16)
        ys_ref[...] = jnp.dot(a, wd_sc[...], preferred_element_type=F32)


def _experts(xs, block_e, n_used, w_gate, w_up, w_down, layer):
    rows = xs.shape[0]
    d, hid = w_gate.shape[-2:]
    n_blocks = rows // MOE_BLK
    used = lambda i, be, nu: (jnp.minimum(i, nu[0] - 1), 0)
    expert = lambda i, be, nu: (layer, be[jnp.minimum(i, nu[0] - 1)], 0, 0)
    grid_spec = pltpu.PrefetchScalarGridSpec(
        num_scalar_prefetch=2,
        grid=(n_blocks,),
        in_specs=[pl.BlockSpec((MOE_BLK, d // 2), used),
                  pl.BlockSpec((None, None, d, hid), expert),
                  pl.BlockSpec((None, None, d, hid), expert),
                  pl.BlockSpec((None, None, hid, d), expert)],
        out_specs=pl.BlockSpec((MOE_BLK, d), lambda i, be, nu: (i, 0)),
        scratch_shapes=[pltpu.VMEM((d, hid), BF16), pltpu.VMEM((d, hid), BF16), pltpu.VMEM((hid, d), BF16)],
    )
    return pl.pallas_call(
        _expert_kernel,
        grid_spec=grid_spec,
        out_shape=jax.ShapeDtypeStruct((rows, d), F32),
        compiler_params=_cparams(("arbitrary",)),
        name="moe_experts",
    )(block_e, n_used, xs, w_gate, w_up, w_down)


def _combine_kernel(table_ref, next_table_ref, slot_ref, ys_ref, x_ref, gf_ref, gate_ref, o_ref,
                    sorted_sc, buf, sem, *, tm):
    step = pl.program_id(0)
    parity = step % 2
    mine, other = sorted_sc.at[parity], sorted_sc.at[1 - parity]

    @pl.when(step == 0)
    def _():
        _start_chunks(table_ref, ys_ref, mine, sem.at[parity], to_hbm=False)

    @pl.when(step + 1 < pl.num_programs(0))
    def _():
        _start_chunks(next_table_ref, ys_ref, other, sem.at[1 - parity], to_hbm=False)

    _wait_chunks(table_ref[0, 0, TABLE_W - 1], ys_ref, mine, sem.at[parity], to_hbm=False)

    def pick(i, carry):
        for k in range(2):
            buf[k, pl.ds(i, 1), :] = mine[pl.ds(slot_ref[0, 0, k * tm + i], 1), :]
        return carry

    lax.fori_loop(0, tm, pick, 0, unroll=8)
    gates = gate_ref[...]
    y = gates[:, 0:1] * buf[0] + gates[:, 1:2] * buf[1]
    o_ref[...] = x_ref[...] + gf_ref[...] * y


def _combine(ys, table, slots, x, gate_f, gates, seq):
    t, d = x.shape
    tm = min(MOE_TILE, t)
    per_b = seq // tm if seq >= tm else 1
    kern = functools.partial(_combine_kernel, tm=tm)
    last = t // tm - 1
    return pl.pallas_call(
        kern,
        grid=(t // tm,),
        in_specs=[pl.BlockSpec((1, 1, TABLE_W), lambda i: (i, 0, 0), memory_space=pltpu.SMEM),
                  pl.BlockSpec((1, 1, TABLE_W), lambda i: (jnp.minimum(i + 1, last), 0, 0), memory_space=pltpu.SMEM),
                  pl.BlockSpec((1, 1, 2 * tm), lambda i: (i, 0, 0), memory_space=pltpu.SMEM),
                  pl.BlockSpec(memory_space=pl.ANY),
                  pl.BlockSpec((tm, d), lambda i: (i, 0)),
                  pl.BlockSpec((None, 1, d), lambda i: (i // per_b, 0, 0)),
                  pl.BlockSpec((tm, ROUTER_COLS), lambda i: (i, 0))],
        out_specs=pl.BlockSpec((tm, d), lambda i: (i, 0)),
        out_shape=jax.ShapeDtypeStruct((t, d), F32),
        scratch_shapes=[pltpu.VMEM((2, _sorted_rows(tm), d), F32), pltpu.VMEM((2, tm, d), F32),
                        pltpu.SemaphoreType.DMA((2,))],
        compiler_params=_cparams(("arbitrary",)),
        name="moe_combine",
    )(table, table, slots, ys, x, gate_f, gates)


def _moe(h, logits, x1, gate_f, w_gate, w_up, w_down, layer, seq):
    rows = _moe_rows(h.shape[0])
    route, gates, seg, totals = _route(logits)
    table, slots, fill, block_e, n_used = _dispatch_plan(route, seg, totals[:, 0], rows // MOE_BLK)
    xs = _dispatch(h, table, slots, fill, rows)
    ys = _experts(xs, block_e, n_used, w_gate, w_up, w_down, layer)
    return _combine(ys, table, slots, x1, gate_f, gates, seq)


def _router_weights(w_group, b_group, w_expert, b_expert):
    d = w_group.shape[0]
    w = jnp.zeros((d, ROUTER_COLS), F32)
    w = w.at[:, 0:MOE_GROUPS].set(w_group.astype(F32))
    w = w.at[:, EXPERT_COL0:EXPERT_COL0 + MOE_EXPERTS].set(w_expert.astype(F32))
    b = jnp.zeros((1, ROUTER_COLS), F32)
    b = b.at[0, 0:MOE_GROUPS].set(b_group.astype(F32))
    b = b.at[0, EXPERT_COL0:EXPERT_COL0 + MOE_EXPERTS].set(b_expert.astype(F32))
    return w, b


def _lambda_init(layer):
    return 0.8 - 0.6 * math.exp(-0.3 * layer)


def kernel(x, c, rel_bias, ada_w, ada_b, norm_attn, norm_ffn, a_w_qkv, a_q_norm, a_k_norm, a_lambda, a_subln, a_w_o, kv_norm, kv_ada_w, kv_ada_b, kv_w, kv_k_norm, b_w_q, b_q_norm, b_w_o, moe_w_group, moe_b_group, moe_w_expert, moe_b_expert, moe_w_gate, moe_w_up, moe_w_down):
    batch, seq, d = x.shape
    t = batch * seq
    scale = HEAD_DIM ** -0.5
    n_bw = N_GROUPS * B_WIDTH

    c_pad = jnp.zeros((8, d), F32).at[:batch].set(c.astype(F32))
    mod = _modulation(c_pad, ada_w, ada_b)[:, :batch]
    kv_mod = _modulation(c_pad, kv_ada_w[None], kv_ada_b[None])[0, :batch]

    def part(m, i):
        return m[:, i * d:(i + 1) * d].reshape(batch, 1, d)

    a_bias, nd = _attn_a_bias_diags(rel_bias, seq, min(ATT_TILE, seq))
    b_bias = [_attn_b_bias_tiles(rel_bias, window // dil, dil) for window, dil in B_GROUPS]
    xf = x.reshape(t, d).astype(F32)
    kv = None
    for layer in range(DEPTH):
        m = mod[layer]
        sh_a, sc_a, g_a, sh_f, sc_f, g_f = (part(m, i) for i in range(6))
        w_r, b_r = _router_weights(moe_w_group[layer], moe_b_group[layer], moe_w_expert[layer], moe_b_expert[layer])
        if layer < N_A_LAYERS:
            qk_gain = jnp.concatenate([jnp.tile(a_q_norm[layer].astype(F32) * (scale * LOG2_E), 2 * N_HEADS),
                                       jnp.tile(a_k_norm[layer].astype(F32), 2 * N_HEADS),
                                       jnp.ones((N_HEADS * LANES,), F32)])
            qkv = _norm_proj(xf, norm_attn[layer], sh_a, sc_a, a_w_qkv[layer].astype(BF16), qk_gain,
                             2 * N_HEADS * LANES, seq, BF16)
            att = _attn_a(qkv, a_bias, nd, a_lambda[layer].astype(F32), a_subln[layer].astype(F32),
                          _lambda_init(layer), batch, seq)
            wo = a_w_o[layer].astype(BF16)
        else:
            j = layer - N_A_LAYERS
            if kv is None:
                k_gain = jnp.concatenate([jnp.tile(kv_k_norm.astype(F32), (1, N_HEADS)).reshape(-1),
                                          jnp.ones((n_bw,), F32)])
                kv = _norm_proj(xf, kv_norm, part(kv_mod, 0), part(kv_mod, 1), kv_w.astype(BF16), k_gain, n_bw, seq, F32)
            q_gain = jnp.tile(b_q_norm[j].astype(F32) * (scale * LOG2_E), (1, N_HEADS)).reshape(-1)
            q = _norm_proj(xf, norm_attn[layer], sh_a, sc_a, b_w_q[j].astype(BF16), q_gain, n_bw, seq, F32)
            att = _attn_b(q, kv, b_bias, batch, seq)
            wo = b_w_o[j].astype(BF16)
        x1, hf, logits = _out_proj(att, wo, xf, g_a, norm_ffn[layer], sh_f, sc_f, w_r, b_r, seq)
        xf = _moe(hf, logits, x1, g_f, moe_w_gate, moe_w_up, moe_w_down, layer, seq)
    return xf.reshape(batch, seq, d).astype(x.dtype)
```

```python
import functools
import math

import jax
import jax.numpy as jnp
import numpy as np
from jax import lax
from jax.experimental import pallas as pl
from jax.experimental.pallas import tpu as pltpu

F32 = jnp.float32
BF16 = jnp.bfloat16

D_MODEL = 1024
DEPTH = 4
N_A_LAYERS = 2
N_HEADS = 8
HEAD_DIM = 64
B_GROUPS = ((128, 1), (512, 4), (2048, 16))
N_GROUPS = len(B_GROUPS)
B_WIDTH = N_HEADS * HEAD_DIM
REL_BUCKETS = 32
REL_MAX_EXACT = 16
REL_MAX_DIST = 2048
MOE_GROUPS = 4
MOE_EPG = 8
MOE_EXPERTS = MOE_GROUPS * MOE_EPG
MOE_HIDDEN = 512
EPS = 1e-6
NEG_INF = float("-inf")
LOG2_E = math.log2(math.e)

LANES = 128
MXU_DIM = 256
ROUTER_COLS = 128
EXPERT_COL0 = 8

TOK_TILE = 512
ATT_TILE = 512
B_TOKENS = 2048
BQ_SUB = 128
MOE_BLK = 512
MOE_TILE = 1024
SEG_ALIGN = 8
SEG_CHUNK = 8
TABLE_W = 384
VMEM_LIMIT = 56 * 1024 * 1024


def _cparams(sem):
    return pltpu.CompilerParams(dimension_semantics=sem, vmem_limit_bytes=VMEM_LIMIT)


def _bucket_table(max_dist):
    n = np.arange(max_dist + 1)
    nf = np.maximum(n, 1).astype(np.float64)
    large = REL_MAX_EXACT + (np.log(nf / REL_MAX_EXACT) / math.log(REL_MAX_DIST / REL_MAX_EXACT)
                             * (REL_BUCKETS - REL_MAX_EXACT)).astype(np.int64)
    large = np.minimum(large, REL_BUCKETS - 1)
    return np.where(n < REL_MAX_EXACT, n, large).astype(np.int32)


def _toeplitz(v, n, m):
    length = v.shape[-1]
    assert length >= n + m - 1 and length - 1 >= m
    lead = v.shape[:-1]
    flat = jnp.tile(v, (1,) * len(lead) + (n,))[..., :n * (length - 1)]
    return flat.reshape(lead + (n, length - 1))[..., :m]


def _diag_values(rel_bias, dist, ok):
    table = _bucket_table(int(dist.max()))
    vals = rel_bias.astype(F32)[table[np.clip(dist, 0, None)]]
    vals = jnp.where(ok[..., None], vals, NEG_INF)
    return jnp.moveaxis(vals, -1, 0)


def _attn_a_bias_diags(rel_bias, seq, tile):
    table = _bucket_table(seq)
    last_start = int(np.argmax(table == REL_BUCKETS - 1))
    nd = 0
    while nd * tile - (tile - 1) < last_start and nd * tile < seq:
        nd += 1
    nd += 1
    length = 2 * tile
    u = np.arange(length)
    u = np.where(u < tile, u, u - length)
    dist = np.arange(nd)[:, None] * tile - u[None, :]
    return _diag_values(rel_bias, dist, dist >= 0) * LOG2_E, nd


def _attn_b_bias_tiles(rel_bias, band, dil):
    length = 3 * BQ_SUB
    u = np.arange(length)
    u = np.where(u < 2 * BQ_SUB, u, u - length)
    rel = BQ_SUB - u
    ok = (rel >= 0) & (rel <= band)
    general = _toeplitz(_diag_values(rel_bias, dil * rel, ok) * LOG2_E, BQ_SUB, 2 * BQ_SUB)
    no_prev = np.arange(2 * BQ_SUB)[None, None, :] >= BQ_SUB
    return jnp.stack([jnp.where(no_prev, general, NEG_INF), general])


def _group_sum_matrix():
    r = np.arange(MXU_DIM)
    return jnp.asarray((r[:, None] // HEAD_DIM) == (r[None, :] // HEAD_DIM), BF16)


def _mod_kernel(c_ref, w_ref, b_ref, o_ref):
    c = c_ref[...]
    c_act = c * (1.0 / (1.0 + jnp.exp(-c)))
    o_ref[...] = jnp.dot(c_act, w_ref[...], preferred_element_type=F32,
                         precision=lax.Precision.HIGHEST) + b_ref[...]


def _modulation(c_pad, w, b, tn=1024):
    nl, d, n = w.shape
    return pl.pallas_call(
        _mod_kernel,
        grid=(nl, n // tn),
        in_specs=[pl.BlockSpec((8, d), lambda l, j: (0, 0)),
                  pl.BlockSpec((None, d, tn), lambda l, j: (l, 0, j)),
                  pl.BlockSpec((None, 1, tn), lambda l, j: (l, 0, j))],
        out_specs=pl.BlockSpec((None, 8, tn), lambda l, j: (l, 0, j)),
        out_shape=jax.ShapeDtypeStruct((nl, 8, n), F32),
        compiler_params=_cparams(("parallel", "parallel")),
        name="adaln_mod",
    )(c_pad, w, b.reshape(nl, 1, n))


def _rms(x, gain):
    return x * lax.rsqrt(jnp.mean(x * x, axis=-1, keepdims=True) + EPS) * gain


def _proj_kernel(x_ref, g_ref, sh_ref, sc_ref, w_ref, hg_ref, bd_ref, o_ref, *, n_out, n_norm):
    h = _rms(x_ref[...], g_ref[...]) * (1.0 + sc_ref[...]) + sh_ref[...]
    hb = h.astype(BF16)
    wide = 2 * MXU_DIM
    for c0 in range(0, n_out, wide):
        a = jnp.dot(hb, w_ref[:, c0:c0 + wide], preferred_element_type=F32)
        if c0 < n_norm:
            sq = (a * a).astype(BF16)
            ms = jnp.concatenate([jnp.dot(sq[:, :MXU_DIM], bd_ref[...], preferred_element_type=F32),
                                  jnp.dot(sq[:, MXU_DIM:], bd_ref[...], preferred_element_type=F32)], axis=1)
            a = a * lax.rsqrt(ms * (1.0 / HEAD_DIM) + EPS) * hg_ref[:, c0:c0 + wide]
        o_ref[:, c0:c0 + wide] = a.astype(o_ref.dtype)


def _norm_proj(x, gain, shift, scale, w_bf16, head_gain, n_norm, seq, out_dtype):
    t, d = x.shape
    n_out = w_bf16.shape[1]
    tm = min(TOK_TILE, seq)
    per_b = seq // tm
    kern = functools.partial(_proj_kernel, n_out=n_out, n_norm=n_norm)
    return pl.pallas_call(
        kern,
        grid=(t // tm,),
        in_specs=[pl.BlockSpec((tm, d), lambda i: (i, 0)),
                  pl.BlockSpec((1, d), lambda i: (0, 0)),
                  pl.BlockSpec((None, 1, d), lambda i: (i // per_b, 0, 0)),
                  pl.BlockSpec((None, 1, d), lambda i: (i // per_b, 0, 0)),
                  pl.BlockSpec((d, n_out), lambda i: (0, 0)),
                  pl.BlockSpec((1, n_out), lambda i: (0, 0)),
                  pl.BlockSpec((MXU_DIM, MXU_DIM), lambda i: (0, 0))],
        out_specs=pl.BlockSpec((tm, n_out), lambda i: (i, 0)),
        out_shape=jax.ShapeDtypeStruct((t, n_out), out_dtype),
        compiler_params=_cparams(("parallel",)),
        name="norm_proj",
    )(x, gain.reshape(1, d), shift, scale, w_bf16, head_gain.reshape(1, n_out), _group_sum_matrix())


def _lane_tile(a, reps):
    return jnp.concatenate([a] * reps, axis=1)


def _attn_a_kernel(q_ref, k_ref, v_ref, diag_ref, lam_ref, g_ref, o_ref, bias_ref, m_sc, acc_sc,
                   *, tile, nd, lam_init):
    qi = pl.program_id(2)

    @pl.when((pl.program_id(1) == 0) & (qi == 0))
    def _():
        for d in range(nd):
            row = jnp.broadcast_to(diag_ref[d:d + 1, :], (tile, 2 * tile))
            bias_ref[d] = pltpu.roll(row, 0, 1, stride=1, stride_axis=0)[:, :tile]

    q = q_ref[...]
    lane = lax.broadcasted_iota(jnp.int32, q.shape, 1)
    zero = jnp.zeros_like(q)
    qq = jnp.concatenate([jnp.where(lane < HEAD_DIM, q, zero),
                          jnp.where(lane >= HEAD_DIM, q, zero)], axis=0)
    m_sc[...] = jnp.full(m_sc.shape, NEG_INF, F32)
    acc_sc[...] = jnp.zeros(acc_sc.shape, F32)
    ones = jnp.ones((tile, LANES), BF16)

    def kv_tile(j):
        start = pl.multiple_of(j * tile, tile)
        k = k_ref[pl.ds(start, tile), :]
        v_ones = jnp.concatenate([v_ref[pl.ds(start, tile), :], ones], axis=1)
        s = lax.dot_general(qq, k, (((1,), (1,)), ((), ())), preferred_element_type=F32)
        bias = bias_ref[jnp.minimum(qi - j, nd - 1)]
        for c in range(2):
            sc = s[c * tile:(c + 1) * tile] + bias
            m_prev = m_sc[c]
            m_new = jnp.maximum(m_prev, jnp.max(sc, axis=-1, keepdims=True))
            alpha = jnp.exp2(m_prev - m_new)
            p = jnp.exp2(sc - _lane_tile(m_new, tile // LANES))
            acc_sc[c] = (_lane_tile(alpha, 2) * acc_sc[c]
                         + jnp.dot(p.astype(BF16), v_ones, preferred_element_type=F32))
            m_sc[c] = m_new

    def kv_quad(i, carry):
        for u in range(4):
            kv_tile(4 * i + u)
        return carry

    n_kv = qi + 1
    n_quad = n_kv // 4
    lax.fori_loop(0, n_quad, kv_quad, 0)

    @pl.when(n_kv % 4 >= 2)
    def _():
        kv_tile(4 * n_quad)
        kv_tile(4 * n_quad + 1)

    @pl.when(n_kv % 2 == 1)
    def _():
        kv_tile(qi)

    lp = lam_ref[...]
    lam = (jnp.exp(jnp.sum(lp[0:1] * lp[1:2], axis=-1, keepdims=True))
           - jnp.exp(jnp.sum(lp[2:3] * lp[3:4], axis=-1, keepdims=True)) + lam_init)
    a0, a1 = acc_sc[0], acc_sc[1]
    o = a0[:, :LANES] / a0[:, LANES:] - lam * (a1[:, :LANES] / a1[:, LANES:])
    o_ref[...] = (_rms(o, g_ref[...]) * (1.0 - lam_init)).astype(BF16)


def _attn_a(qkv, bias_diags, nd, lam_p, subln, lam_init, batch, seq):
    tile = min(ATT_TILE, seq)
    nq = seq // tile
    qkv3 = qkv.reshape(batch, seq, 3 * N_HEADS * LANES)
    kern = functools.partial(_attn_a_kernel, tile=tile, nd=nd, lam_init=lam_init)
    out = pl.pallas_call(
        kern,
        grid=(N_HEADS, batch, nq),
        in_specs=[pl.BlockSpec((None, tile, LANES), lambda h, b, i: (b, i, h)),
                  pl.BlockSpec((None, seq, LANES), lambda h, b, i: (b, 0, N_HEADS + h)),
                  pl.BlockSpec((None, seq, LANES), lambda h, b, i: (b, 0, 2 * N_HEADS + h)),
                  pl.BlockSpec((None, nd, 2 * tile), lambda h, b, i: (h, 0, 0)),
                  pl.BlockSpec((4, HEAD_DIM), lambda h, b, i: (0, 0)),
                  pl.BlockSpec((1, LANES), lambda h, b, i: (0, 0))],
        out_specs=pl.BlockSpec((None, tile, LANES), lambda h, b, i: (b, i, h)),
        out_shape=jax.ShapeDtypeStruct((batch, seq, N_HEADS * LANES), BF16),
        scratch_shapes=[pltpu.VMEM((nd, tile, tile), F32), pltpu.VMEM((2, tile, LANES), F32),
                        pltpu.VMEM((2, tile, 2 * LANES), F32)],
        compiler_params=_cparams(("arbitrary", "arbitrary", "arbitrary")),
        name="attn_a",
    )(qkv3, qkv3, qkv3, bias_diags, lam_p, subln.reshape(1, LANES))
    return out.reshape(batch * seq, N_HEADS * LANES)


def _attn_b_kernel(*refs, tile):
    ins, o_ref, scratch = refs[:6 * N_GROUPS], refs[6 * N_GROUPS], refs[6 * N_GROUPS + 1:]
    first_tile = pl.program_id(1) == 0
    lane = lax.broadcasted_iota(jnp.int32, (BQ_SUB, LANES), 1)
    low = lane < HEAD_DIM
    ones = jnp.ones((2 * BQ_SUB, LANES), BF16)

    for g, (_, dil) in enumerate(B_GROUPS):
        q_ref, k_ref, v_ref, kp_ref, vp_ref, bias_ref = ins[6 * g:6 * g + 6]
        og_ref, lg_ref = scratch[2 * g:2 * g + 2]
        span = BQ_SUB * dil

        def residue(base, dil=dil):
            return pl.ds(base, BQ_SUB) if dil == 1 else pl.ds(base, BQ_SUB, stride=dil)

        band = {}
        for sb in range(tile // span):
            for r in range(dil):
                base = sb * span + r
                rows = residue(base)
                qp = q_ref[rows, :].astype(BF16)
                band[sb, r] = (k_ref[rows, :].astype(BF16), v_ref[rows, :].astype(BF16))
                if sb == 0:
                    k_prev, v_prev = kp_ref[residue(r), :].astype(BF16), vp_ref[residue(r), :].astype(BF16)
                    variant = jnp.where(first_tile, 0, 1)
                else:
                    k_prev, v_prev = band.pop((sb - 1, r))
                    variant = 1
                kk = jnp.concatenate([k_prev, band[sb, r][0]], axis=0)
                vv = jnp.concatenate([v_prev, band[sb, r][1]], axis=0)
                v_ones = jnp.concatenate([vv, ones], axis=1)
                zero = jnp.zeros_like(qp)
                qq = jnp.concatenate([jnp.where(low, qp, zero), jnp.where(low, zero, qp)], axis=0)
                s = lax.dot_general(qq, kk, (((1,), (1,)), ((), ())), preferred_element_type=F32)
                outs, lses = [], []
                for c in range(2):
                    sc = s[c * BQ_SUB:(c + 1) * BQ_SUB] + bias_ref[variant, c]
                    m = jnp.max(sc, axis=-1, keepdims=True)
                    p = jnp.exp2(sc - m)
                    acc = jnp.dot(p.astype(BF16), v_ones, preferred_element_type=F32)
                    outs.append(acc[:, :LANES] / acc[:, LANES:])
                    lses.append(m + jnp.log2(acc[:, LANES:]))
                og_ref[rows, :] = jnp.where(low, outs[0], outs[1])
                lg_ref[rows, :] = jnp.where(low, lses[0], lses[1])

    l0, l1, l2 = scratch[1][...], scratch[3][...], scratch[5][...]
    m = jnp.maximum(jnp.maximum(l0, l1), l2)
    e0, e1, e2 = jnp.exp2(l0 - m), jnp.exp2(l1 - m), jnp.exp2(l2 - m)
    z = e0 + e1 + e2
    o_ref[...] = ((e0 / z) * scratch[0][...] + (e1 / z) * scratch[2][...] + (e2 / z) * scratch[4][...]).astype(BF16)


def _attn_b(q, kv, bias_tiles, batch, seq):
    tile = min(B_TOKENS, seq)
    pairs = N_HEADS // 2
    q3 = q.reshape(batch, seq, N_GROUPS * B_WIDTH)
    kv3 = kv.reshape(batch, seq, 2 * N_GROUPS * B_WIDTH)
    operands, in_specs = [], []
    for g, (_, dil) in enumerate(B_GROUPS):
        span = BQ_SUB * dil
        assert tile % span == 0
        k_col = lambda hp, g=g: g * pairs + hp
        v_col = lambda hp, g=g: (N_GROUPS + g) * pairs + hp
        prev = lambda n, per=tile // span: jnp.maximum(n * per - 1, 0)
        operands += [q3, kv3, kv3, kv3, kv3, bias_tiles[g]]
        in_specs += [pl.BlockSpec((None, tile, LANES), lambda b, n, hp, c=k_col: (b, n, c(hp))),
                     pl.BlockSpec((None, tile, LANES), lambda b, n, hp, c=k_col: (b, n, c(hp))),
                     pl.BlockSpec((None, tile, LANES), lambda b, n, hp, c=v_col: (b, n, c(hp))),
                     pl.BlockSpec((None, span, LANES), lambda b, n, hp, c=k_col, p=prev: (b, p(n), c(hp))),
                     pl.BlockSpec((None, span, LANES), lambda b, n, hp, c=v_col, p=prev: (b, p(n), c(hp))),
                     pl.BlockSpec((2, 2, BQ_SUB, 2 * BQ_SUB), lambda b, n, hp: (0, hp, 0, 0))]
    out = pl.pallas_call(
        functools.partial(_attn_b_kernel, tile=tile),
        grid=(batch, seq // tile, pairs),
        in_specs=in_specs,
        out_specs=pl.BlockSpec((None, tile, LANES), lambda b, n, hp: (b, n, hp)),
        out_shape=jax.ShapeDtypeStruct((batch, seq, B_WIDTH), BF16),
        scratch_shapes=[pltpu.VMEM((tile, LANES), F32)] * (2 * N_GROUPS),
        compiler_params=_cparams(("parallel", "parallel", "parallel")),
        name="attn_b",
    )(*operands)
    return out.reshape(batch * seq, B_WIDTH)


def _split_bf16(a):
    hi = a.astype(BF16)
    return hi, (a - hi.astype(F32)).astype(BF16)


def _out_tail(o_bf16, wo_ref, x_ref, ga_ref, gf_ref, shf_ref, scf_ref, wr_ref, br_ref, x1_ref, hf_ref, lg_ref):
    y = jnp.dot(o_bf16, wo_ref[...], preferred_element_type=F32)
    x1 = x_ref[...] + ga_ref[...] * y
    x1_ref[...] = x1
    hf = _rms(x1, gf_ref[...]) * (1.0 + scf_ref[...]) + shf_ref[...]
    half = hf.shape[1] // 2
    hf_ref[...] = _pack_pair(hf[:, :half], hf[:, half:])
    hi, lo = _split_bf16(hf)
    both = jnp.dot(hi, wr_ref[...], preferred_element_type=F32)
    lg_ref[...] = (both[:, :ROUTER_COLS] + jnp.dot(lo, wr_ref[:, :ROUTER_COLS], preferred_element_type=F32)
                   + both[:, ROUTER_COLS:] + br_ref[...])


def _out_kernel(o_ref, *rest):
    _out_tail(o_ref[...], *rest)


def _out_proj(att, wo_bf16, x, gate_a, gain_f, shift_f, scale_f, w_router, b_router, seq):
    t, d = x.shape
    tm = min(TOK_TILE, seq)
    per_b = seq // tm
    row = lambda i: (i, 0)
    fixed = lambda i: (0, 0)
    per_batch = lambda i: (i // per_b, 0, 0)
    return pl.pallas_call(
        _out_kernel,
        grid=(t // tm,),
        in_specs=[pl.BlockSpec((tm, att.shape[1]), row),
                  pl.BlockSpec(wo_bf16.shape, fixed),
                  pl.BlockSpec((tm, d), row),
                  pl.BlockSpec((None, 1, d), per_batch),
                  pl.BlockSpec((1, d), fixed),
                  pl.BlockSpec((None, 1, d), per_batch),
                  pl.BlockSpec((None, 1, d), per_batch),
                  pl.BlockSpec((d, 2 * ROUTER_COLS), fixed),
                  pl.BlockSpec((1, ROUTER_COLS), fixed)],
        out_specs=[pl.BlockSpec((tm, d), row), pl.BlockSpec((tm, d // 2), row),
                   pl.BlockSpec((tm, ROUTER_COLS), row)],
        out_shape=[jax.ShapeDtypeStruct((t, d), F32), jax.ShapeDtypeStruct((t, d // 2), jnp.uint32),
                   jax.ShapeDtypeStruct((t, ROUTER_COLS), F32)],
        compiler_params=_cparams(("parallel",)),
        name="out_proj",
    )(att, wo_bf16, x, gate_a, gain_f.reshape(1, d), shift_f, scale_f,
      jnp.concatenate(_split_bf16(w_router), axis=1), b_router)


def _route_kernel(lg_ref, tri_ref, eid_ref, gate_ref, seg_ref, cnt_ref, cnt_sc):
    lt = lg_ref[...].T
    best, g_idx = lt[0:1], jnp.zeros((1, lt.shape[1]), jnp.int32)
    for g in range(1, MOE_GROUPS):
        upd = lt[g:g + 1] > best
        best = jnp.where(upd, lt[g:g + 1], best)
        g_idx = jnp.where(upd, g, g_idx)
    denom = jnp.zeros_like(best)
    for g in range(MOE_GROUPS):
        denom = denom + jnp.exp(lt[g:g + 1] - best)
    g_w = 1.0 / denom
    e_sel = lt[EXPERT_COL0:EXPERT_COL0 + MOE_EPG]
    for g in range(1, MOE_GROUPS):
        r0 = EXPERT_COL0 + g * MOE_EPG
        e_sel = jnp.where(g_idx == g, lt[r0:r0 + MOE_EPG], e_sel)

    def first_max(vals):
        v, i = vals[0:1], jnp.zeros((1, vals.shape[1]), jnp.int32)
        for e in range(1, MOE_EPG):
            upd = vals[e:e + 1] > v
            v = jnp.where(upd, vals[e:e + 1], v)
            i = jnp.where(upd, e, i)
        return v, i

    v1, i1 = first_max(e_sel)
    row = lax.broadcasted_iota(jnp.int32, e_sel.shape, 0)
    v2, i2 = first_max(jnp.where(row == i1, NEG_INF, e_sel))
    e2 = jnp.exp(v2 - v1)
    w1 = 1.0 / (1.0 + e2)
    w2 = e2 / (1.0 + e2)
    e1 = g_idx * MOE_EPG + i1
    e2 = g_idx * MOE_EPG + i2

    @pl.when(pl.program_id(0) == 0)
    def _():
        cnt_sc[...] = jnp.zeros(cnt_sc.shape, F32)

    tm = lt.shape[1]
    erows = lax.broadcasted_iota(jnp.int32, (MOE_EXPERTS, tm), 0)
    ohs, pres = [], []
    for e_k in (e1, e2):
        oh = (erows == e_k).astype(F32)
        ohs.append(oh)
        pres.append(jnp.dot(oh.astype(BF16), tri_ref[...], preferred_element_type=F32))
    n0 = pres[0][:, tm - 1:]
    n_tot = n0 + pres[1][:, tm - 1:]
    chunks = jnp.floor((n_tot + (SEG_ALIGN - 1.0)) * (1.0 / SEG_ALIGN))
    er = lax.broadcasted_iota(jnp.int32, (MOE_EXPERTS, MOE_EXPERTS), 0)
    ec = lax.broadcasted_iota(jnp.int32, (MOE_EXPERTS, MOE_EXPERTS), 1)
    before = jnp.dot((ec < er).astype(BF16), jnp.broadcast_to(chunks, (MOE_EXPERTS, LANES)).astype(BF16),
                     preferred_element_type=F32)
    seg_start = before[:, 0:1] * SEG_ALIGN
    slot0 = jnp.sum(ohs[0] * (seg_start + pres[0] - 1.0), axis=0, keepdims=True)
    slot1 = jnp.sum(ohs[1] * (seg_start + n0 + pres[1] - 1.0), axis=0, keepdims=True)
    seg_off = cnt_sc[...]
    cnt_sc[...] = seg_off + chunks * SEG_ALIGN
    cnt_ref[...] = jnp.broadcast_to(cnt_sc[...], cnt_ref.shape).astype(jnp.int32)
    scol = lax.broadcasted_iota(jnp.int32, seg_ref.shape, 1)
    seg_ref[...] = jnp.where(scol == 0, chunks, jnp.where(scol == 1, seg_start, jnp.where(
        scol == 2, seg_off, 0.0))).astype(jnp.int32)

    erow = lax.broadcasted_iota(jnp.int32, eid_ref.shape, 0)
    eid_ref[...] = jnp.where(erow == 0, e1, jnp.where(erow == 1, e2, jnp.where(
        erow == 2, slot0.astype(jnp.int32), jnp.where(erow == 3, slot1.astype(jnp.int32), 0))))
    grow = lax.broadcasted_iota(jnp.int32, lt.shape, 0)
    gates = jnp.where(grow == 0, g_w * w1, jnp.where(grow == 1, g_w * w2, 0.0))
    gate_ref[...] = gates.T


def _route(logits):
    t = logits.shape[0]
    tm = min(MOE_TILE, t)
    n_tiles = t // tm
    r = np.arange(tm)
    tri = jnp.asarray(r[:, None] <= r[None, :], BF16)
    return pl.pallas_call(
        _route_kernel,
        grid=(n_tiles,),
        in_specs=[pl.BlockSpec((tm, ROUTER_COLS), lambda i: (i, 0)),
                  pl.BlockSpec((tm, tm), lambda i: (0, 0))],
        out_specs=[pl.BlockSpec((8, tm), lambda i: (0, i)), pl.BlockSpec((tm, ROUTER_COLS), lambda i: (i, 0)),
                   pl.BlockSpec((MOE_EXPERTS, LANES), lambda i: (i, 0)),
                   pl.BlockSpec((MOE_EXPERTS, LANES), lambda i: (0, 0))],
        out_shape=[jax.ShapeDtypeStruct((8, t), jnp.int32), jax.ShapeDtypeStruct((t, ROUTER_COLS), F32),
                   jax.ShapeDtypeStruct((n_tiles * MOE_EXPERTS, LANES), jnp.int32),
                   jax.ShapeDtypeStruct((MOE_EXPERTS, LANES), jnp.int32)],
        scratch_shapes=[pltpu.VMEM((MOE_EXPERTS, 1), F32)],
        compiler_params=_cparams(("arbitrary",)),
        name="moe_route",
    )(logits, tri)


def _moe_rows(t):
    n_tiles = t // min(MOE_TILE, t)
    rows = 2 * t + n_tiles * MOE_EXPERTS * (SEG_ALIGN - 1) + MOE_EXPERTS * (MOE_BLK - 1)
    return -(-rows // MOE_BLK) * MOE_BLK


def _chunk_rows(first, count, row0, n_slots):
    c = jnp.arange(n_slots, dtype=jnp.int32)
    run = jnp.sum((first + count)[..., None, :] <= c[:, None], axis=-1)
    hit = run[..., None] == jnp.arange(first.shape[-1], dtype=jnp.int32)
    base = jnp.sum(jnp.where(hit, (row0 - SEG_CHUNK * first)[..., None, :], 0), axis=-1)
    return jnp.where(run < first.shape[-1], base + SEG_CHUNK * c, 0)


def _dispatch_plan(route, seg, totals, n_blocks):
    n_tiles = seg.shape[0] // MOE_EXPERTS
    tm = route.shape[1] // n_tiles
    n_slots = _sorted_rows(tm) // SEG_CHUNK
    assert n_slots <= TABLE_W - 3
    padded = (totals + MOE_BLK - 1) // MOE_BLK * MOE_BLK
    pad_end = jnp.cumsum(padded)
    pad_start = pad_end - padded
    n_used = pad_end[-1] // MOE_BLK
    seg = seg.reshape(n_tiles, MOE_EXPERTS, LANES)
    chunks, first, seg_off = seg[:, :, 0], seg[:, :, 1] // SEG_CHUNK, seg[:, :, 2]
    dst = _chunk_rows(first, chunks, pad_start[None, :] + seg_off, n_slots)
    fill = (padded - totals) // SEG_CHUNK
    fill_first = jnp.cumsum(fill) - fill
    fill_dst = _chunk_rows(fill_first, fill, pad_start + totals, MOE_EXPERTS * (MOE_BLK // SEG_CHUNK))
    tail = jnp.stack([jnp.broadcast_to(n_used, (n_tiles,)), jnp.broadcast_to(jnp.sum(fill), (n_tiles,)),
                      jnp.sum(chunks, axis=1)], axis=1)
    table = jnp.concatenate([dst, jnp.zeros((n_tiles, TABLE_W - 3 - n_slots), jnp.int32), tail], axis=1)
    slots = route[2:4].reshape(2, n_tiles, tm).transpose(1, 0, 2).reshape(n_tiles, 1, 2 * tm)
    block_start = jnp.arange(n_blocks, dtype=jnp.int32) * MOE_BLK
    block_e = jnp.minimum(jnp.sum(pad_end[None, :] <= block_start[:, None], axis=1), MOE_EXPERTS - 1).astype(jnp.int32)
    return (table.astype(jnp.int32).reshape(n_tiles, 1, TABLE_W), slots, fill_dst.astype(jnp.int32).reshape(1, 1, -1),
            block_e, n_used.astype(jnp.int32).reshape(1))


def _chunk_copy(hbm_ref, hbm_row, sorted_ref, chunk, sem, to_hbm):
    local = sorted_ref.at[pl.ds(pl.multiple_of(chunk * SEG_CHUNK, SEG_CHUNK), SEG_CHUNK), :]
    remote = hbm_ref.at[pl.ds(pl.multiple_of(hbm_row, SEG_CHUNK), SEG_CHUNK), :]
    return pltpu.make_async_copy(local, remote, sem) if to_hbm else pltpu.make_async_copy(remote, local, sem)


def _start_chunks(table_ref, hbm_ref, sorted_ref, sem, to_hbm):
    def chunk(c, carry):
        _chunk_copy(hbm_ref, table_ref[0, 0, c], sorted_ref, c, sem, to_hbm).start()
        return carry

    lax.fori_loop(0, table_ref[0, 0, TABLE_W - 1], chunk, 0)


def _wait_chunks(count, hbm_ref, sorted_ref, sem, to_hbm):
    def chunk(c, carry):
        _chunk_copy(hbm_ref, 0, sorted_ref, 0, sem, to_hbm).wait()
        return carry

    lax.fori_loop(0, count, chunk, 0)


def _zero_fill(table_ref, fill_ref, xs_ref, zero_sc, sem, n_blocks, wait):
    def finish(copy):
        if wait:
            copy.wait()
        else:
            copy.start()

    def chunk(c, carry):
        row = pl.multiple_of(fill_ref[0, 0, c], SEG_CHUNK)
        finish(pltpu.make_async_copy(zero_sc.at[pl.ds(0, SEG_CHUNK), :], xs_ref.at[pl.ds(row, SEG_CHUNK), :], sem))
        return carry

    def block(b, carry):
        row = pl.multiple_of(b * MOE_BLK, MOE_BLK)
        finish(pltpu.make_async_copy(zero_sc, xs_ref.at[pl.ds(row, MOE_BLK), :], sem))
        return carry

    lax.fori_loop(0, table_ref[0, 0, TABLE_W - 2], chunk, 0)
    lax.fori_loop(table_ref[0, 0, TABLE_W - 3], n_blocks, block, 0)


def _dispatch_kernel(table_ref, slot_ref, fill_ref, h_ref, xs_ref, sorted_sc, zero_sc, pending_sc, sem, zero_sem,
                     *, tm, n_blocks):
    step = pl.program_id(0)
    first_step = step == 0
    parity = step % 2
    mine, other = sorted_sc.at[parity], sorted_sc.at[1 - parity]

    @pl.when(first_step)
    def _():
        sorted_sc[...] = jnp.zeros(sorted_sc.shape, sorted_sc.dtype)
        zero_sc[...] = jnp.zeros(zero_sc.shape, zero_sc.dtype)
        pending_sc[0] = 0
        _zero_fill(table_ref, fill_ref, xs_ref, zero_sc, zero_sem, n_blocks, wait=False)

    def place(i, carry):
        for k in range(2):
            mine[pl.ds(slot_ref[0, 0, k * tm + i], 1), :] = h_ref[pl.ds(i, 1), :]
        return carry

    lax.fori_loop(0, tm, place, 0, unroll=8)
    _start_chunks(table_ref, xs_ref, mine, sem.at[parity], to_hbm=True)
    _wait_chunks(pending_sc[0], xs_ref, other, sem.at[1 - parity], to_hbm=True)
    pending_sc[0] = table_ref[0, 0, TABLE_W - 1]

    @pl.when(step == pl.num_programs(0) - 1)
    def _():
        _wait_chunks(pending_sc[0], xs_ref, mine, sem.at[parity], to_hbm=True)

    @pl.when(first_step)
    def _():
        _zero_fill(table_ref, fill_ref, xs_ref, zero_sc, zero_sem, n_blocks, wait=True)


def _sorted_rows(tm):
    return -(-(2 * tm + MOE_EXPERTS * (SEG_ALIGN - 1)) // SEG_CHUNK) * SEG_CHUNK


def _dispatch(h, table, slots, fill, rows):
    t, d = h.shape
    tm = min(MOE_TILE, t)
    kern = functools.partial(_dispatch_kernel, tm=tm, n_blocks=rows // MOE_BLK)
    return pl.pallas_call(
        kern,
        grid=(t // tm,),
        in_specs=[pl.BlockSpec((1, 1, TABLE_W), lambda i: (i, 0, 0), memory_space=pltpu.SMEM),
                  pl.BlockSpec((1, 1, 2 * tm), lambda i: (i, 0, 0), memory_space=pltpu.SMEM),
                  pl.BlockSpec(fill.shape, lambda i: (0, 0, 0), memory_space=pltpu.SMEM),
                  pl.BlockSpec((tm, d), lambda i: (i, 0))],
        out_specs=pl.BlockSpec(memory_space=pl.ANY),
        out_shape=jax.ShapeDtypeStruct((rows, d), h.dtype),
        scratch_shapes=[pltpu.VMEM((2, _sorted_rows(tm), d), h.dtype), pltpu.VMEM((MOE_BLK, d), h.dtype),
                        pltpu.SMEM((1,), jnp.int32), pltpu.SemaphoreType.DMA((2,)), pltpu.SemaphoreType.DMA(())],
        compiler_params=_cparams(("arbitrary",)),
        name="moe_dispatch",
    )(table, slots, fill, h)


def _pack_pair(lo, hi):
    lo_bits = lax.bitcast_convert_type(lo.astype(BF16).astype(F32), jnp.uint32) >> 16
    hi_bits = lax.bitcast_convert_type(hi.astype(BF16).astype(F32), jnp.uint32) & jnp.uint32(0xFFFF0000)
    return hi_bits | lo_bits


def _unpack_pair(words):
    lo = lax.bitcast_convert_type(words << 16, F32).astype(BF16)
    hi = lax.bitcast_convert_type(words & jnp.uint32(0xFFFF0000), F32).astype(BF16)
    return lo, hi


def _expert_kernel(be_ref, nu_ref, xs_ref, wg_ref, wu_ref, wd_ref, ys_ref, wg_sc, wu_sc, wd_sc):
    i = pl.program_id(0)
    live = i < nu_ref[0]

    @pl.when(jnp.logical_not(live))
    def _():
        ys_ref[...] = jnp.zeros(ys_ref.shape, F32)

    @pl.when(live)
    def _():
        @pl.when((i == 0) | (be_ref[i] != be_ref[jnp.maximum(i - 1, 0)]))
        def _():
            wg_sc[...] = wg_ref[...].astype(BF16)
            wu_sc[...] = wu_ref[...].astype(BF16)
            wd_sc[...] = wd_ref[...].astype(BF16)

        x = jnp.concatenate(_unpack_pair(xs_ref[...]), axis=1)
        g = jnp.dot(x, wg_sc[...], preferred_element_type=F32)
        u = jnp.dot(x, wu_sc[...], preferred_element_type=F32)
        a = (g * (1.0 / (1.0 + jnp.exp(-g))) * u).astype(BF16)
        ys_ref[...] = jnp.dot(a, wd_sc[...], preferred_element_type=F32)


def _experts(xs, block_e, n_used, w_gate, w_up, w_down, layer):
    rows = xs.shape[0]
    d, hid = w_gate.shape[-2:]
    n_blocks = rows // MOE_BLK
    used = lambda i, be, nu: (jnp.minimum(i, nu[0] - 1), 0)
    expert = lambda i, be, nu: (layer, be[jnp.minimum(i, nu[0] - 1)], 0, 0)
    grid_spec = pltpu.PrefetchScalarGridSpec(
        num_scalar_prefetch=2,
        grid=(n_blocks,),
        in_specs=[pl.BlockSpec((MOE_BLK, d // 2), used),
                  pl.BlockSpec((None, None, d, hid), expert),
                  pl.BlockSpec((None, None, d, hid), expert),
                  pl.BlockSpec((None, None, hid, d), expert)],
        out_specs=pl.BlockSpec((MOE_BLK, d), lambda i, be, nu: (i, 0)),
        scratch_shapes=[pltpu.VMEM((d, hid), BF16), pltpu.VMEM((d, hid), BF16), pltpu.VMEM((hid, d), BF16)],
    )
    return pl.pallas_call(
        _expert_kernel,
        grid_spec=grid_spec,
        out_shape=jax.ShapeDtypeStruct((rows, d), F32),
        compiler_params=_cparams(("arbitrary",)),
        name="moe_experts",
    )(block_e, n_used, xs, w_gate, w_up, w_down)


def _combine_kernel(table_ref, next_table_ref, slot_ref, ys_ref, x_ref, gf_ref, gate_ref, o_ref,
                    sorted_sc, buf, sem, *, tm):
    step = pl.program_id(0)
    parity = step % 2
    mine, other = sorted_sc.at[parity], sorted_sc.at[1 - parity]

    @pl.when(step == 0)
    def _():
        _start_chunks(table_ref, ys_ref, mine, sem.at[parity], to_hbm=False)

    @pl.when(step + 1 < pl.num_programs(0))
    def _():
        _start_chunks(next_table_ref, ys_ref, other, sem.at[1 - parity], to_hbm=False)

    _wait_chunks(table_ref[0, 0, TABLE_W - 1], ys_ref, mine, sem.at[parity], to_hbm=False)

    def pick(i, carry):
        for k in range(2):
            buf[k, pl.ds(i, 1), :] = mine[pl.ds(slot_ref[0, 0, k * tm + i], 1), :]
        return carry

    lax.fori_loop(0, tm, pick, 0, unroll=8)
    for r0 in range(0, tm, TOK_TILE):
        rows = pl.ds(r0, min(TOK_TILE, tm))
        y = gate_ref[rows, 0:1] * buf[0, rows, :] + gate_ref[rows, 1:2] * buf[1, rows, :]
        o_ref[rows, :] = x_ref[rows, :] + gf_ref[...] * y


def _combine(ys, table, slots, x, gate_f, gates, seq):
    t, d = x.shape
    tm = min(MOE_TILE, t)
    per_b = seq // tm if seq >= tm else 1
    kern = functools.partial(_combine_kernel, tm=tm)
    last = t // tm - 1
    return pl.pallas_call(
        kern,
        grid=(t // tm,),
        in_specs=[pl.BlockSpec((1, 1, TABLE_W), lambda i: (i, 0, 0), memory_space=pltpu.SMEM),
                  pl.BlockSpec((1, 1, TABLE_W), lambda i: (jnp.minimum(i + 1, last), 0, 0), memory_space=pltpu.SMEM),
                  pl.BlockSpec((1, 1, 2 * tm), lambda i: (i, 0, 0), memory_space=pltpu.SMEM),
                  pl.BlockSpec(memory_space=pl.ANY),
                  pl.BlockSpec((tm, d), lambda i: (i, 0)),
                  pl.BlockSpec((None, 1, d), lambda i: (i // per_b, 0, 0)),
                  pl.BlockSpec((tm, ROUTER_COLS), lambda i: (i, 0))],
        out_specs=pl.BlockSpec((tm, d), lambda i: (i, 0)),
        out_shape=jax.ShapeDtypeStruct((t, d), F32),
        scratch_shapes=[pltpu.VMEM((2, _sorted_rows(tm), d), F32), pltpu.VMEM((2, tm, d), F32),
                        pltpu.SemaphoreType.DMA((2,))],
        compiler_params=_cparams(("arbitrary",)),
        name="moe_combine",
    )(table, table, slots, ys, x, gate_f, gates)


def _moe(h, logits, x1, gate_f, w_gate, w_up, w_down, layer, seq):
    rows = _moe_rows(h.shape[0])
    route, gates, seg, totals = _route(logits)
    table, slots, fill, block_e, n_used = _dispatch_plan(route, seg, totals[:, 0], rows // MOE_BLK)
    xs = _dispatch(h, table, slots, fill, rows)
    ys = _experts(xs, block_e, n_used, w_gate, w_up, w_down, layer)
    return _combine(ys, table, slots, x1, gate_f, gates, seq)


def _router_weights(w_group, b_group, w_expert, b_expert):
    d = w_group.shape[0]
    w = jnp.zeros((d, ROUTER_COLS), F32)
    w = w.at[:, 0:MOE_GROUPS].set(w_group.astype(F32))
    w = w.at[:, EXPERT_COL0:EXPERT_COL0 + MOE_EXPERTS].set(w_expert.astype(F32))
    b = jnp.zeros((1, ROUTER_COLS), F32)
    b = b.at[0, 0:MOE_GROUPS].set(b_group.astype(F32))
    b = b.at[0, EXPERT_COL0:EXPERT_COL0 + MOE_EXPERTS].set(b_expert.astype(F32))
    return w, b


def _lambda_init(layer):
    return 0.8 - 0.6 * math.exp(-0.3 * layer)


def kernel(x, c, rel_bias, ada_w, ada_b, norm_attn, norm_ffn, a_w_qkv, a_q_norm, a_k_norm, a_lambda, a_subln, a_w_o, kv_norm, kv_ada_w, kv_ada_b, kv_w, kv_k_norm, b_w_q, b_q_norm, b_w_o, moe_w_group, moe_b_group, moe_w_expert, moe_b_expert, moe_w_gate, moe_w_up, moe_w_down):
    batch, seq, d = x.shape
    t = batch * seq
    scale = HEAD_DIM ** -0.5
    n_bw = N_GROUPS * B_WIDTH

    c_pad = jnp.zeros((8, d), F32).at[:batch].set(c.astype(F32))
    mod = _modulation(c_pad, ada_w, ada_b)[:, :batch]
    kv_mod = _modulation(c_pad, kv_ada_w[None], kv_ada_b[None])[0, :batch]

    def part(m, i):
        return m[:, i * d:(i + 1) * d].reshape(batch, 1, d)

    a_bias, nd = _attn_a_bias_diags(rel_bias, seq, min(ATT_TILE, seq))
    b_bias = [_attn_b_bias_tiles(rel_bias, window // dil, dil) for window, dil in B_GROUPS]
    xf = x.reshape(t, d).astype(F32)
    kv = None
    for layer in range(DEPTH):
        m = mod[layer]
        sh_a, sc_a, g_a, sh_f, sc_f, g_f = (part(m, i) for i in range(6))
        w_r, b_r = _router_weights(moe_w_group[layer], moe_b_group[layer], moe_w_expert[layer], moe_b_expert[layer])
        if layer < N_A_LAYERS:
            qk_gain = jnp.concatenate([jnp.tile(a_q_norm[layer].astype(F32) * (scale * LOG2_E), 2 * N_HEADS),
                                       jnp.tile(a_k_norm[layer].astype(F32), 2 * N_HEADS),
                                       jnp.ones((N_HEADS * LANES,), F32)])
            qkv = _norm_proj(xf, norm_attn[layer], sh_a, sc_a, a_w_qkv[layer].astype(BF16), qk_gain,
                             2 * N_HEADS * LANES, seq, BF16)
            att = _attn_a(qkv, a_bias, nd, a_lambda[layer].astype(F32), a_subln[layer].astype(F32),
                          _lambda_init(layer), batch, seq)
            wo = a_w_o[layer].astype(BF16)
        else:
            j = layer - N_A_LAYERS
            if kv is None:
                k_gain = jnp.concatenate([jnp.tile(kv_k_norm.astype(F32), (1, N_HEADS)).reshape(-1),
                                          jnp.ones((n_bw,), F32)])
                kv = _norm_proj(xf, kv_norm, part(kv_mod, 0), part(kv_mod, 1), kv_w.astype(BF16), k_gain, n_bw, seq, F32)
            q_gain = jnp.tile(b_q_norm[j].astype(F32) * (scale * LOG2_E), (1, N_HEADS)).reshape(-1)
            q = _norm_proj(xf, norm_attn[layer], sh_a, sc_a, b_w_q[j].astype(BF16), q_gain, n_bw, seq, F32)
            att = _attn_b(q, kv, b_bias, batch, seq)
            wo = b_w_o[j].astype(BF16)
        x1, hf, logits = _out_proj(att, wo, xf, g_a, norm_ffn[layer], sh_f, sc_f, w_r, b_r, seq)
        xf = _moe(hf, logits, x1, g_f, moe_w_gate, moe_w_up, moe_w_down, layer, seq)
    return xf.reshape(batch, seq, d).astype(x.dtype)
```

```python
import functools
import math

import jax
import jax.numpy as jnp
import numpy as np
from jax import lax
from jax.experimental import pallas as pl
from jax.experimental.pallas import tpu as pltpu

F32 = jnp.float32
BF16 = jnp.bfloat16

D_MODEL = 1024
DEPTH = 4
N_A_LAYERS = 2
N_HEADS = 8
HEAD_DIM = 64
B_GROUPS = ((128, 1), (512, 4), (2048, 16))
N_GROUPS = len(B_GROUPS)
B_WIDTH = N_HEADS * HEAD_DIM
REL_BUCKETS = 32
REL_MAX_EXACT = 16
REL_MAX_DIST = 2048
MOE_GROUPS = 4
MOE_EPG = 8
MOE_EXPERTS = MOE_GROUPS * MOE_EPG
MOE_HIDDEN = 512
EPS = 1e-6
NEG_INF = float("-inf")
LOG2_E = math.log2(math.e)

LANES = 128
MXU_DIM = 256
ROUTER_COLS = 128
EXPERT_COL0 = 8

TOK_TILE = 512
ATT_TILE = 512
B_TOKENS = 2048
BQ_SUB = 128
MOE_BLK = 512
MOE_TILE = 1024
SEG_ALIGN = 8
SEG_CHUNK = 8
TABLE_W = 384
VMEM_LIMIT = 56 * 1024 * 1024


def _cparams(sem):
    return pltpu.CompilerParams(dimension_semantics=sem, vmem_limit_bytes=VMEM_LIMIT)


def _bucket_table(max_dist):
    n = np.arange(max_dist + 1)
    nf = np.maximum(n, 1).astype(np.float64)
    large = REL_MAX_EXACT + (np.log(nf / REL_MAX_EXACT) / math.log(REL_MAX_DIST / REL_MAX_EXACT)
                             * (REL_BUCKETS - REL_MAX_EXACT)).astype(np.int64)
    large = np.minimum(large, REL_BUCKETS - 1)
    return np.where(n < REL_MAX_EXACT, n, large).astype(np.int32)


def _toeplitz(v, n, m):
    length = v.shape[-1]
    assert length >= n + m - 1 and length - 1 >= m
    lead = v.shape[:-1]
    flat = jnp.tile(v, (1,) * len(lead) + (n,))[..., :n * (length - 1)]
    return flat.reshape(lead + (n, length - 1))[..., :m]


def _diag_values(rel_bias, dist, ok):
    table = _bucket_table(int(dist.max()))
    vals = rel_bias.astype(F32)[table[np.clip(dist, 0, None)]]
    vals = jnp.where(ok[..., None], vals, NEG_INF)
    return jnp.moveaxis(vals, -1, 0)


def _attn_a_bias_diags(rel_bias, seq, tile):
    table = _bucket_table(seq)
    last_start = int(np.argmax(table == REL_BUCKETS - 1))
    nd = 0
    while nd * tile - (tile - 1) < last_start and nd * tile < seq:
        nd += 1
    nd += 1
    length = 2 * tile
    u = np.arange(length)
    u = np.where(u < tile, u, u - length)
    dist = np.arange(nd)[:, None] * tile - u[None, :]
    return _diag_values(rel_bias, dist, dist >= 0) * LOG2_E, nd


def _attn_b_bias_tiles(rel_bias, band, dil):
    length = 3 * BQ_SUB
    u = np.arange(length)
    u = np.where(u < 2 * BQ_SUB, u, u - length)
    rel = BQ_SUB - u
    ok = (rel >= 0) & (rel <= band)
    general = _toeplitz(_diag_values(rel_bias, dil * rel, ok) * LOG2_E, BQ_SUB, 2 * BQ_SUB)
    no_prev = np.arange(2 * BQ_SUB)[None, None, :] >= BQ_SUB
    return jnp.stack([jnp.where(no_prev, general, NEG_INF), general])


def _group_sum_matrix():
    r = np.arange(MXU_DIM)
    return jnp.asarray((r[:, None] // HEAD_DIM) == (r[None, :] // HEAD_DIM), BF16)


def _mod_kernel(c_ref, w_ref, b_ref, o_ref):
    c = c_ref[...]
    c_act = c * (1.0 / (1.0 + jnp.exp(-c)))
    o_ref[...] = jnp.dot(c_act, w_ref[...], preferred_element_type=F32,
                         precision=lax.Precision.HIGHEST) + b_ref[...]


def _modulation(c_pad, w, b, tn=1024):
    nl, d, n = w.shape
    return pl.pallas_call(
        _mod_kernel,
        grid=(nl, n // tn),
        in_specs=[pl.BlockSpec((8, d), lambda l, j: (0, 0)),
                  pl.BlockSpec((None, d, tn), lambda l, j: (l, 0, j)),
                  pl.BlockSpec((None, 1, tn), lambda l, j: (l, 0, j))],
        out_specs=pl.BlockSpec((None, 8, tn), lambda l, j: (l, 0, j)),
        out_shape=jax.ShapeDtypeStruct((nl, 8, n), F32),
        compiler_params=_cparams(("parallel", "parallel")),
        name="adaln_mod",
    )(c_pad, w, b.reshape(nl, 1, n))


def _rms(x, gain):
    return x * lax.rsqrt(jnp.mean(x * x, axis=-1, keepdims=True) + EPS) * gain


def _proj_kernel(x_ref, g_ref, sh_ref, sc_ref, w_ref, hg_ref, bd_ref, o_ref, *, n_out, n_norm):
    h = _rms(x_ref[...], g_ref[...]) * (1.0 + sc_ref[...]) + sh_ref[...]
    hb = h.astype(BF16)
    wide = 2 * MXU_DIM
    for c0 in range(0, n_out, wide):
        a = jnp.dot(hb, w_ref[:, c0:c0 + wide], preferred_element_type=F32)
        if c0 < n_norm:
            sq = (a * a).astype(BF16)
            ms = jnp.concatenate([jnp.dot(sq[:, :MXU_DIM], bd_ref[...], preferred_element_type=F32),
                                  jnp.dot(sq[:, MXU_DIM:], bd_ref[...], preferred_element_type=F32)], axis=1)
            a = a * lax.rsqrt(ms * (1.0 / HEAD_DIM) + EPS) * hg_ref[:, c0:c0 + wide]
        o_ref[:, c0:c0 + wide] = a.astype(o_ref.dtype)


def _norm_proj(x, gain, shift, scale, w_bf16, head_gain, n_norm, seq, out_dtype):
    t, d = x.shape
    n_out = w_bf16.shape[1]
    tm = min(TOK_TILE, seq)
    per_b = seq // tm
    kern = functools.partial(_proj_kernel, n_out=n_out, n_norm=n_norm)
    return pl.pallas_call(
        kern,
        grid=(t // tm,),
        in_specs=[pl.BlockSpec((tm, d), lambda i: (i, 0)),
                  pl.BlockSpec((1, d), lambda i: (0, 0)),
                  pl.BlockSpec((None, 1, d), lambda i: (i // per_b, 0, 0)),
                  pl.BlockSpec((None, 1, d), lambda i: (i // per_b, 0, 0)),
                  pl.BlockSpec((d, n_out), lambda i: (0, 0)),
                  pl.BlockSpec((1, n_out), lambda i: (0, 0)),
                  pl.BlockSpec((MXU_DIM, MXU_DIM), lambda i: (0, 0))],
        out_specs=pl.BlockSpec((tm, n_out), lambda i: (i, 0)),
        out_shape=jax.ShapeDtypeStruct((t, n_out), out_dtype),
        compiler_params=_cparams(("parallel",)),
        name="norm_proj",
    )(x, gain.reshape(1, d), shift, scale, w_bf16, head_gain.reshape(1, n_out), _group_sum_matrix())


def _lane_tile(a, reps):
    return jnp.concatenate([a] * reps, axis=1)


def _attn_a_kernel(q_ref, k_ref, v_ref, diag_ref, lam_ref, g_ref, o_ref, bias_ref, m_sc, acc_sc,
                   *, tile, nd, lam_init):
    qi = pl.program_id(2)

    @pl.when((pl.program_id(1) == 0) & (qi == 0))
    def _():
        for d in range(nd):
            row = jnp.broadcast_to(diag_ref[d:d + 1, :], (tile, 2 * tile))
            bias_ref[d] = pltpu.roll(row, 0, 1, stride=1, stride_axis=0)[:, :tile]

    q = q_ref[...]
    lane = lax.broadcasted_iota(jnp.int32, q.shape, 1)
    zero = jnp.zeros_like(q)
    qq = jnp.concatenate([jnp.where(lane < HEAD_DIM, q, zero),
                          jnp.where(lane >= HEAD_DIM, q, zero)], axis=0)
    m_sc[...] = jnp.full(m_sc.shape, NEG_INF, F32)
    acc_sc[...] = jnp.zeros(acc_sc.shape, F32)
    ones = jnp.ones((tile, LANES), BF16)

    def kv_piece(key_start, n_keys, bias, row0):
        n_rows = tile - row0
        k = k_ref[pl.ds(key_start, n_keys), :]
        v_ones = jnp.concatenate([v_ref[pl.ds(key_start, n_keys), :], ones[:n_keys]], axis=1)
        lhs = qq if row0 == 0 else jnp.concatenate([qq[row0:tile], qq[tile + row0:]], axis=0)
        s = lax.dot_general(lhs, k, (((1,), (1,)), ((), ())), preferred_element_type=F32)
        for c in range(2):
            sc = s[c * n_rows:(c + 1) * n_rows] + bias
            m_prev = m_sc[c, row0:, :]
            m_new = jnp.maximum(m_prev, jnp.max(sc, axis=-1, keepdims=True))
            alpha = jnp.exp2(m_prev - m_new)
            p = jnp.exp2(sc - _lane_tile(m_new, n_keys // LANES))
            acc_sc[c, row0:, :] = (_lane_tile(alpha, 2) * acc_sc[c, row0:, :]
                                   + jnp.dot(p.astype(BF16), v_ones, preferred_element_type=F32))
            m_sc[c, row0:, :] = m_new

    def kv_tile(j):
        kv_piece(pl.multiple_of(j * tile, tile), tile, bias_ref[jnp.minimum(qi - j, nd - 1)], 0)

    def kv_quad(i, carry):
        for u in range(4):
            kv_tile(4 * i + u)
        return carry

    n_quad = qi // 4
    lax.fori_loop(0, n_quad, kv_quad, 0)

    @pl.when(qi % 4 >= 2)
    def _():
        kv_tile(4 * n_quad)
        kv_tile(4 * n_quad + 1)

    @pl.when(qi % 2 == 1)
    def _():
        kv_tile(qi - 1)

    half = tile // 2
    diag = pl.multiple_of(qi * tile, tile)
    kv_piece(diag, half, bias_ref[0, :, :half], 0)
    kv_piece(pl.multiple_of(diag + half, half), half, bias_ref[0, half:, half:], half)

    lp = lam_ref[...]
    lam = (jnp.exp(jnp.sum(lp[0:1] * lp[1:2], axis=-1, keepdims=True))
           - jnp.exp(jnp.sum(lp[2:3] * lp[3:4], axis=-1, keepdims=True)) + lam_init)
    a0, a1 = acc_sc[0], acc_sc[1]
    o = a0[:, :LANES] / a0[:, LANES:] - lam * (a1[:, :LANES] / a1[:, LANES:])
    o_ref[...] = (_rms(o, g_ref[...]) * (1.0 - lam_init)).astype(BF16)


def _attn_a(qkv, bias_diags, nd, lam_p, subln, lam_init, batch, seq):
    tile = min(ATT_TILE, seq)
    nq = seq // tile
    qkv3 = qkv.reshape(batch, seq, 3 * N_HEADS * LANES)
    kern = functools.partial(_attn_a_kernel, tile=tile, nd=nd, lam_init=lam_init)
    out = pl.pallas_call(
        kern,
        grid=(N_HEADS, batch, nq),
        in_specs=[pl.BlockSpec((None, tile, LANES), lambda h, b, i: (b, i, h)),
                  pl.BlockSpec((None, seq, LANES), lambda h, b, i: (b, 0, N_HEADS + h)),
                  pl.BlockSpec((None, seq, LANES), lambda h, b, i: (b, 0, 2 * N_HEADS + h)),
                  pl.BlockSpec((None, nd, 2 * tile), lambda h, b, i: (h, 0, 0)),
                  pl.BlockSpec((4, HEAD_DIM), lambda h, b, i: (0, 0)),
                  pl.BlockSpec((1, LANES), lambda h, b, i: (0, 0))],
        out_specs=pl.BlockSpec((None, tile, LANES), lambda h, b, i: (b, i, h)),
        out_shape=jax.ShapeDtypeStruct((batch, seq, N_HEADS * LANES), BF16),
        scratch_shapes=[pltpu.VMEM((nd, tile, tile), F32), pltpu.VMEM((2, tile, LANES), F32),
                        pltpu.VMEM((2, tile, 2 * LANES), F32)],
        compiler_params=_cparams(("arbitrary", "arbitrary", "arbitrary")),
        name="attn_a",
    )(qkv3, qkv3, qkv3, bias_diags, lam_p, subln.reshape(1, LANES))
    return out.reshape(batch * seq, N_HEADS * LANES)


def _attn_b_kernel(*refs, tile):
    ins, o_ref, scratch = refs[:6 * N_GROUPS], refs[6 * N_GROUPS], refs[6 * N_GROUPS + 1:]
    first_tile = pl.program_id(1) == 0
    lane = lax.broadcasted_iota(jnp.int32, (BQ_SUB, LANES), 1)
    low = lane < HEAD_DIM
    ones = jnp.ones((2 * BQ_SUB, LANES), BF16)

    for g, (_, dil) in enumerate(B_GROUPS):
        q_ref, k_ref, v_ref, kp_ref, vp_ref, bias_ref = ins[6 * g:6 * g + 6]
        og_ref, lg_ref = scratch[2 * g:2 * g + 2]
        span = BQ_SUB * dil

        def residue(base, dil=dil):
            return pl.ds(base, BQ_SUB) if dil == 1 else pl.ds(base, BQ_SUB, stride=dil)

        band = {}
        for sb in range(tile // span):
            for r in range(dil):
                base = sb * span + r
                rows = residue(base)
                qp = q_ref[rows, :].astype(BF16)
                band[sb, r] = (k_ref[rows, :].astype(BF16), v_ref[rows, :].astype(BF16))
                if sb == 0:
                    k_prev, v_prev = kp_ref[residue(r), :].astype(BF16), vp_ref[residue(r), :].astype(BF16)
                    variant = jnp.where(first_tile, 0, 1)
                else:
                    k_prev, v_prev = band.pop((sb - 1, r))
                    variant = 1
                kk = jnp.concatenate([k_prev, band[sb, r][0]], axis=0)
                vv = jnp.concatenate([v_prev, band[sb, r][1]], axis=0)
                v_ones = jnp.concatenate([vv, ones], axis=1)
                zero = jnp.zeros_like(qp)
                qq = jnp.concatenate([jnp.where(low, qp, zero), jnp.where(low, zero, qp)], axis=0)
                s = lax.dot_general(qq, kk, (((1,), (1,)), ((), ())), preferred_element_type=F32)
                outs, lses = [], []
                for c in range(2):
                    sc = s[c * BQ_SUB:(c + 1) * BQ_SUB] + bias_ref[variant, c]
                    m = jnp.max(sc, axis=-1, keepdims=True)
                    p = jnp.exp2(sc - m)
                    acc = jnp.dot(p.astype(BF16), v_ones, preferred_element_type=F32)
                    outs.append(acc[:, :LANES] / acc[:, LANES:])
                    lses.append(m + jnp.log2(acc[:, LANES:]))
                og_ref[rows, :] = jnp.where(low, outs[0], outs[1])
                lg_ref[rows, :] = jnp.where(low, lses[0], lses[1])

    l0, l1, l2 = scratch[1][...], scratch[3][...], scratch[5][...]
    m = jnp.maximum(jnp.maximum(l0, l1), l2)
    e0, e1, e2 = jnp.exp2(l0 - m), jnp.exp2(l1 - m), jnp.exp2(l2 - m)
    z = e0 + e1 + e2
    o_ref[...] = ((e0 / z) * scratch[0][...] + (e1 / z) * scratch[2][...] + (e2 / z) * scratch[4][...]).astype(BF16)


def _attn_b(q, kv, bias_tiles, batch, seq):
    tile = min(B_TOKENS, seq)
    pairs = N_HEADS // 2
    q3 = q.reshape(batch, seq, N_GROUPS * B_WIDTH)
    kv3 = kv.reshape(batch, seq, 2 * N_GROUPS * B_WIDTH)
    operands, in_specs = [], []
    for g, (_, dil) in enumerate(B_GROUPS):
        span = BQ_SUB * dil
        assert tile % span == 0
        k_col = lambda hp, g=g: g * pairs + hp
        v_col = lambda hp, g=g: (N_GROUPS + g) * pairs + hp
        prev = lambda n, per=tile // span: jnp.maximum(n * per - 1, 0)
        operands += [q3, kv3, kv3, kv3, kv3, bias_tiles[g]]
        in_specs += [pl.BlockSpec((None, tile, LANES), lambda b, n, hp, c=k_col: (b, n, c(hp))),
                     pl.BlockSpec((None, tile, LANES), lambda b, n, hp, c=k_col: (b, n, c(hp))),
                     pl.BlockSpec((None, tile, LANES), lambda b, n, hp, c=v_col: (b, n, c(hp))),
                     pl.BlockSpec((None, span, LANES), lambda b, n, hp, c=k_col, p=prev: (b, p(n), c(hp))),
                     pl.BlockSpec((None, span, LANES), lambda b, n, hp, c=v_col, p=prev: (b, p(n), c(hp))),
                     pl.BlockSpec((2, 2, BQ_SUB, 2 * BQ_SUB), lambda b, n, hp: (0, hp, 0, 0))]
    out = pl.pallas_call(
        functools.partial(_attn_b_kernel, tile=tile),
        grid=(batch, seq // tile, pairs),
        in_specs=in_specs,
        out_specs=pl.BlockSpec((None, tile, LANES), lambda b, n, hp: (b, n, hp)),
        out_shape=jax.ShapeDtypeStruct((batch, seq, B_WIDTH), BF16),
        scratch_shapes=[pltpu.VMEM((tile, LANES), F32)] * (2 * N_GROUPS),
        compiler_params=_cparams(("parallel", "parallel", "parallel")),
        name="attn_b",
    )(*operands)
    return out.reshape(batch * seq, B_WIDTH)


def _split_bf16(a):
    hi = a.astype(BF16)
    return hi, (a - hi.astype(F32)).astype(BF16)


def _out_tail(o_bf16, wo_ref, x_ref, ga_ref, gf_ref, shf_ref, scf_ref, wr_ref, br_ref, x1_ref, hf_ref, lg_ref):
    y = jnp.dot(o_bf16, wo_ref[...], preferred_element_type=F32)
    x1 = x_ref[...] + ga_ref[...] * y
    x1_ref[...] = x1
    hf = _rms(x1, gf_ref[...]) * (1.0 + scf_ref[...]) + shf_ref[...]
    half = hf.shape[1] // 2
    hf_ref[...] = _pack_pair(hf[:, :half], hf[:, half:])
    hi, lo = _split_bf16(hf)
    both = jnp.dot(hi, wr_ref[...], preferred_element_type=F32)
    lg_ref[...] = (both[:, :ROUTER_COLS] + jnp.dot(lo, wr_ref[:, :ROUTER_COLS], preferred_element_type=F32)
                   + both[:, ROUTER_COLS:] + br_ref[...])


def _out_kernel(o_ref, *rest):
    _out_tail(o_ref[...], *rest)


def _out_proj(att, wo_bf16, x, gate_a, gain_f, shift_f, scale_f, w_router, b_router, seq):
    t, d = x.shape
    tm = min(TOK_TILE, seq)
    per_b = seq // tm
    row = lambda i: (i, 0)
    fixed = lambda i: (0, 0)
    per_batch = lambda i: (i // per_b, 0, 0)
    return pl.pallas_call(
        _out_kernel,
        grid=(t // tm,),
        in_specs=[pl.BlockSpec((tm, att.shape[1]), row),
                  pl.BlockSpec(wo_bf16.shape, fixed),
                  pl.BlockSpec((tm, d), row),
                  pl.BlockSpec((None, 1, d), per_batch),
                  pl.BlockSpec((1, d), fixed),
                  pl.BlockSpec((None, 1, d), per_batch),
                  pl.BlockSpec((None, 1, d), per_batch),
                  pl.BlockSpec((d, 2 * ROUTER_COLS), fixed),
                  pl.BlockSpec((1, ROUTER_COLS), fixed)],
        out_specs=[pl.BlockSpec((tm, d), row), pl.BlockSpec((tm, d // 2), row),
                   pl.BlockSpec((tm, ROUTER_COLS), row)],
        out_shape=[jax.ShapeDtypeStruct((t, d), F32), jax.ShapeDtypeStruct((t, d // 2), jnp.uint32),
                   jax.ShapeDtypeStruct((t, ROUTER_COLS), F32)],
        compiler_params=_cparams(("parallel",)),
        name="out_proj",
    )(att, wo_bf16, x, gate_a, gain_f.reshape(1, d), shift_f, scale_f,
      jnp.concatenate(_split_bf16(w_router), axis=1), b_router)


def _route_kernel(lg_ref, tri_ref, eid_ref, gate_ref, seg_ref, cnt_ref, cnt_sc):
    lt = lg_ref[...].T
    best, g_idx = lt[0:1], jnp.zeros((1, lt.shape[1]), jnp.int32)
    for g in range(1, MOE_GROUPS):
        upd = lt[g:g + 1] > best
        best = jnp.where(upd, lt[g:g + 1], best)
        g_idx = jnp.where(upd, g, g_idx)
    denom = jnp.zeros_like(best)
    for g in range(MOE_GROUPS):
        denom = denom + jnp.exp(lt[g:g + 1] - best)
    g_w = 1.0 / denom
    e_sel = lt[EXPERT_COL0:EXPERT_COL0 + MOE_EPG]
    for g in range(1, MOE_GROUPS):
        r0 = EXPERT_COL0 + g * MOE_EPG
        e_sel = jnp.where(g_idx == g, lt[r0:r0 + MOE_EPG], e_sel)

    def first_max(vals):
        v, i = vals[0:1], jnp.zeros((1, vals.shape[1]), jnp.int32)
        for e in range(1, MOE_EPG):
            upd = vals[e:e + 1] > v
            v = jnp.where(upd, vals[e:e + 1], v)
            i = jnp.where(upd, e, i)
        return v, i

    v1, i1 = first_max(e_sel)
    row = lax.broadcasted_iota(jnp.int32, e_sel.shape, 0)
    v2, i2 = first_max(jnp.where(row == i1, NEG_INF, e_sel))
    e2 = jnp.exp(v2 - v1)
    w1 = 1.0 / (1.0 + e2)
    w2 = e2 / (1.0 + e2)
    e1 = g_idx * MOE_EPG + i1
    e2 = g_idx * MOE_EPG + i2

    @pl.when(pl.program_id(0) == 0)
    def _():
        cnt_sc[...] = jnp.zeros(cnt_sc.shape, F32)

    tm = lt.shape[1]
    erows = lax.broadcasted_iota(jnp.int32, (MOE_EXPERTS, tm), 0)
    ohs, pres = [], []
    for e_k in (e1, e2):
        oh = (erows == e_k).astype(F32)
        ohs.append(oh)
        pres.append(jnp.dot(oh.astype(BF16), tri_ref[...], preferred_element_type=F32))
    n0 = pres[0][:, tm - 1:]
    n_tot = n0 + pres[1][:, tm - 1:]
    chunks = jnp.floor((n_tot + (SEG_ALIGN - 1.0)) * (1.0 / SEG_ALIGN))
    er = lax.broadcasted_iota(jnp.int32, (MOE_EXPERTS, MOE_EXPERTS), 0)
    ec = lax.broadcasted_iota(jnp.int32, (MOE_EXPERTS, MOE_EXPERTS), 1)
    before = jnp.dot((ec < er).astype(BF16), jnp.broadcast_to(chunks, (MOE_EXPERTS, LANES)).astype(BF16),
                     preferred_element_type=F32)
    seg_start = before[:, 0:1] * SEG_ALIGN
    slot0 = jnp.sum(ohs[0] * (seg_start + pres[0] - 1.0), axis=0, keepdims=True)
    slot1 = jnp.sum(ohs[1] * (seg_start + n0 + pres[1] - 1.0), axis=0, keepdims=True)
    seg_off = cnt_sc[...]
    cnt_sc[...] = seg_off + chunks * SEG_ALIGN
    cnt_ref[...] = jnp.broadcast_to(cnt_sc[...], cnt_ref.shape).astype(jnp.int32)
    scol = lax.broadcasted_iota(jnp.int32, seg_ref.shape, 1)
    seg_ref[...] = jnp.where(scol == 0, chunks, jnp.where(scol == 1, seg_start, jnp.where(
        scol == 2, seg_off, 0.0))).astype(jnp.int32)

    erow = lax.broadcasted_iota(jnp.int32, eid_ref.shape, 0)
    eid_ref[...] = jnp.where(erow == 0, e1, jnp.where(erow == 1, e2, jnp.where(
        erow == 2, slot0.astype(jnp.int32), jnp.where(erow == 3, slot1.astype(jnp.int32), 0))))
    grow = lax.broadcasted_iota(jnp.int32, lt.shape, 0)
    gates = jnp.where(grow == 0, g_w * w1, jnp.where(grow == 1, g_w * w2, 0.0))
    gate_ref[...] = gates.T


def _route(logits):
    t = logits.shape[0]
    tm = min(MOE_TILE, t)
    n_tiles = t // tm
    r = np.arange(tm)
    tri = jnp.asarray(r[:, None] <= r[None, :], BF16)
    return pl.pallas_call(
        _route_kernel,
        grid=(n_tiles,),
        in_specs=[pl.BlockSpec((tm, ROUTER_COLS), lambda i: (i, 0)),
                  pl.BlockSpec((tm, tm), lambda i: (0, 0))],
        out_specs=[pl.BlockSpec((8, tm), lambda i: (0, i)), pl.BlockSpec((tm, ROUTER_COLS), lambda i: (i, 0)),
                   pl.BlockSpec((MOE_EXPERTS, LANES), lambda i: (i, 0)),
                   pl.BlockSpec((MOE_EXPERTS, LANES), lambda i: (0, 0))],
        out_shape=[jax.ShapeDtypeStruct((8, t), jnp.int32), jax.ShapeDtypeStruct((t, ROUTER_COLS), F32),
                   jax.ShapeDtypeStruct((n_tiles * MOE_EXPERTS, LANES), jnp.int32),
                   jax.ShapeDtypeStruct((MOE_EXPERTS, LANES), jnp.int32)],
        scratch_shapes=[pltpu.VMEM((MOE_EXPERTS, 1), F32)],
        compiler_params=_cparams(("arbitrary",)),
        name="moe_route",
    )(logits, tri)


def _moe_rows(t):
    n_tiles = t // min(MOE_TILE, t)
    rows = 2 * t + n_tiles * MOE_EXPERTS * (SEG_ALIGN - 1) + MOE_EXPERTS * (MOE_BLK - 1)
    return -(-rows // MOE_BLK) * MOE_BLK


def _chunk_rows(first, count, row0, n_slots):
    c = jnp.arange(n_slots, dtype=jnp.int32)
    run = jnp.sum((first + count)[..., None, :] <= c[:, None], axis=-1)
    hit = run[..., None] == jnp.arange(first.shape[-1], dtype=jnp.int32)
    base = jnp.sum(jnp.where(hit, (row0 - SEG_CHUNK * first)[..., None, :], 0), axis=-1)
    return jnp.where(run < first.shape[-1], base + SEG_CHUNK * c, 0)


def _dispatch_plan(route, seg, totals, n_blocks):
    n_tiles = seg.shape[0] // MOE_EXPERTS
    tm = route.shape[1] // n_tiles
    n_slots = _sorted_rows(tm) // SEG_CHUNK
    assert n_slots <= TABLE_W - 3
    padded = (totals + MOE_BLK - 1) // MOE_BLK * MOE_BLK
    pad_end = jnp.cumsum(padded)
    pad_start = pad_end - padded
    n_used = pad_end[-1] // MOE_BLK
    seg = seg.reshape(n_tiles, MOE_EXPERTS, LANES)
    chunks, first, seg_off = seg[:, :, 0], seg[:, :, 1] // SEG_CHUNK, seg[:, :, 2]
    dst = _chunk_rows(first, chunks, pad_start[None, :] + seg_off, n_slots)
    fill = (padded - totals) // SEG_CHUNK
    fill_first = jnp.cumsum(fill) - fill
    fill_dst = _chunk_rows(fill_first, fill, pad_start + totals, MOE_EXPERTS * (MOE_BLK // SEG_CHUNK))
    tail = jnp.stack([jnp.broadcast_to(n_used, (n_tiles,)), jnp.broadcast_to(jnp.sum(fill), (n_tiles,)),
                      jnp.sum(chunks, axis=1)], axis=1)
    table = jnp.concatenate([dst, jnp.zeros((n_tiles, TABLE_W - 3 - n_slots), jnp.int32), tail], axis=1)
    slots = route[2:4].reshape(2, n_tiles, tm).transpose(1, 0, 2).reshape(n_tiles, 1, 2 * tm)
    block_start = jnp.arange(n_blocks, dtype=jnp.int32) * MOE_BLK
    block_e = jnp.minimum(jnp.sum(pad_end[None, :] <= block_start[:, None], axis=1), MOE_EXPERTS - 1).astype(jnp.int32)
    return (table.astype(jnp.int32).reshape(n_tiles, 1, TABLE_W), slots, fill_dst.astype(jnp.int32).reshape(1, 1, -1),
            block_e, n_used.astype(jnp.int32).reshape(1))


def _chunk_copy(hbm_ref, hbm_row, sorted_ref, chunk, sem, to_hbm):
    local = sorted_ref.at[pl.ds(pl.multiple_of(chunk * SEG_CHUNK, SEG_CHUNK), SEG_CHUNK), :]
    remote = hbm_ref.at[pl.ds(pl.multiple_of(hbm_row, SEG_CHUNK), SEG_CHUNK), :]
    return pltpu.make_async_copy(local, remote, sem) if to_hbm else pltpu.make_async_copy(remote, local, sem)


def _start_chunks(table_ref, hbm_ref, sorted_ref, sem, to_hbm):
    def chunk(c, carry):
        _chunk_copy(hbm_ref, table_ref[0, 0, c], sorted_ref, c, sem, to_hbm).start()
        return carry

    lax.fori_loop(0, table_ref[0, 0, TABLE_W - 1], chunk, 0)


def _wait_chunks(count, hbm_ref, sorted_ref, sem, to_hbm):
    def chunk(c, carry):
        _chunk_copy(hbm_ref, 0, sorted_ref, 0, sem, to_hbm).wait()
        return carry

    lax.fori_loop(0, count, chunk, 0)


def _zero_fill(table_ref, fill_ref, xs_ref, zero_sc, sem, n_blocks, wait):
    def finish(copy):
        if wait:
            copy.wait()
        else:
            copy.start()

    def chunk(c, carry):
        row = pl.multiple_of(fill_ref[0, 0, c], SEG_CHUNK)
        finish(pltpu.make_async_copy(zero_sc.at[pl.ds(0, SEG_CHUNK), :], xs_ref.at[pl.ds(row, SEG_CHUNK), :], sem))
        return carry

    def block(b, carry):
        row = pl.multiple_of(b * MOE_BLK, MOE_BLK)
        finish(pltpu.make_async_copy(zero_sc, xs_ref.at[pl.ds(row, MOE_BLK), :], sem))
        return carry

    lax.fori_loop(0, table_ref[0, 0, TABLE_W - 2], chunk, 0)
    lax.fori_loop(table_ref[0, 0, TABLE_W - 3], n_blocks, block, 0)


def _dispatch_kernel(table_ref, slot_ref, fill_ref, h_ref, xs_ref, sorted_sc, zero_sc, pending_sc, sem, zero_sem,
                     *, tm, n_blocks):
    step = pl.program_id(0)
    first_step = step == 0
    parity = step % 2
    mine, other = sorted_sc.at[parity], sorted_sc.at[1 - parity]

    @pl.when(first_step)
    def _():
        sorted_sc[...] = jnp.zeros(sorted_sc.shape, sorted_sc.dtype)
        zero_sc[...] = jnp.zeros(zero_sc.shape, zero_sc.dtype)
        pending_sc[0] = 0
        _zero_fill(table_ref, fill_ref, xs_ref, zero_sc, zero_sem, n_blocks, wait=False)

    def place(i, carry):
        for k in range(2):
            mine[pl.ds(slot_ref[0, 0, k * tm + i], 1), :] = h_ref[pl.ds(i, 1), :]
        return carry

    lax.fori_loop(0, tm, place, 0, unroll=8)
    _start_chunks(table_ref, xs_ref, mine, sem.at[parity], to_hbm=True)
    _wait_chunks(pending_sc[0], xs_ref, other, sem.at[1 - parity], to_hbm=True)
    pending_sc[0] = table_ref[0, 0, TABLE_W - 1]

    @pl.when(step == pl.num_programs(0) - 1)
    def _():
        _wait_chunks(pending_sc[0], xs_ref, mine, sem.at[parity], to_hbm=True)

    @pl.when(first_step)
    def _():
        _zero_fill(table_ref, fill_ref, xs_ref, zero_sc, zero_sem, n_blocks, wait=True)


def _sorted_rows(tm):
    return -(-(2 * tm + MOE_EXPERTS * (SEG_ALIGN - 1)) // SEG_CHUNK) * SEG_CHUNK


def _dispatch(h, table, slots, fill, rows):
    t, d = h.shape
    tm = min(MOE_TILE, t)
    kern = functools.partial(_dispatch_kernel, tm=tm, n_blocks=rows // MOE_BLK)
    return pl.pallas_call(
        kern,
        grid=(t // tm,),
        in_specs=[pl.BlockSpec((1, 1, TABLE_W), lambda i: (i, 0, 0), memory_space=pltpu.SMEM),
                  pl.BlockSpec((1, 1, 2 * tm), lambda i: (i, 0, 0), memory_space=pltpu.SMEM),
                  pl.BlockSpec(fill.shape, lambda i: (0, 0, 0), memory_space=pltpu.SMEM),
                  pl.BlockSpec((tm, d), lambda i: (i, 0))],
        out_specs=pl.BlockSpec(memory_space=pl.ANY),
        out_shape=jax.ShapeDtypeStruct((rows, d), h.dtype),
        scratch_shapes=[pltpu.VMEM((2, _sorted_rows(tm), d), h.dtype), pltpu.VMEM((MOE_BLK, d), h.dtype),
                        pltpu.SMEM((1,), jnp.int32), pltpu.SemaphoreType.DMA((2,)), pltpu.SemaphoreType.DMA(())],
        compiler_params=_cparams(("arbitrary",)),
        name="moe_dispatch",
    )(table, slots, fill, h)


def _pack_pair(lo, hi):
    lo_bits = lax.bitcast_convert_type(lo.astype(BF16).astype(F32), jnp.uint32) >> 16
    hi_bits = lax.bitcast_convert_type(hi.astype(BF16).astype(F32), jnp.uint32) & jnp.uint32(0xFFFF0000)
    return hi_bits | lo_bits


def _unpack_pair(words):
    lo = lax.bitcast_convert_type(words << 16, F32).astype(BF16)
    hi = lax.bitcast_convert_type(words & jnp.uint32(0xFFFF0000), F32).astype(BF16)
    return lo, hi


def _expert_kernel(be_ref, nu_ref, xs_ref, wg_ref, wu_ref, wd_ref, ys_ref, wg_sc, wu_sc, wd_sc):
    i = pl.program_id(0)
    live = i < nu_ref[0]

    @pl.when(jnp.logical_not(live))
    def _():
        ys_ref[...] = jnp.zeros(ys_ref.shape, F32)

    @pl.when(live)
    def _():
        @pl.when((i == 0) | (be_ref[i] != be_ref[jnp.maximum(i - 1, 0)]))
        def _():
            wg_sc[...] = wg_ref[...].astype(BF16)
            wu_sc[...] = wu_ref[...].astype(BF16)
            wd_sc[...] = wd_ref[...].astype(BF16)

        x = jnp.concatenate(_unpack_pair(xs_ref[...]), axis=1)
        g = jnp.dot(x, wg_sc[...], preferred_element_type=F32)
        u = jnp.dot(x, wu_sc[...], preferred_element_type=F32)
        a = (g * (1.0 / (1.0 + jnp.exp(-g))) * u).astype(BF16)
        ys_ref[...] = jnp.dot(a, wd_sc[...], preferred_element_type=F32)


def _experts(xs, block_e, n_used, w_gate, w_up, w_down, layer):
    rows = xs.shape[0]
    d, hid = w_gate.shape[-2:]
    n_blocks = rows // MOE_BLK
    used = lambda i, be, nu: (jnp.minimum(i, nu[0] - 1), 0)
    expert = lambda i, be, nu: (layer, be[jnp.minimum(i, nu[0] - 1)], 0, 0)
    grid_spec = pltpu.PrefetchScalarGridSpec(
        num_scalar_prefetch=2,
        grid=(n_blocks,),
        in_specs=[pl.BlockSpec((MOE_BLK, d // 2), used),
                  pl.BlockSpec((None, None, d, hid), expert),
                  pl.BlockSpec((None, None, d, hid), expert),
                  pl.BlockSpec((None, None, hid, d), expert)],
        out_specs=pl.BlockSpec((MOE_BLK, d), lambda i, be, nu: (i, 0)),
        scratch_shapes=[pltpu.VMEM((d, hid), BF16), pltpu.VMEM((d, hid), BF16), pltpu.VMEM((hid, d), BF16)],
    )
    return pl.pallas_call(
        _expert_kernel,
        grid_spec=grid_spec,
        out_shape=jax.ShapeDtypeStruct((rows, d), F32),
        compiler_params=_cparams(("arbitrary",)),
        name="moe_experts",
    )(block_e, n_used, xs, w_gate, w_up, w_down)


def _combine_kernel(table_ref, next_table_ref, slot_ref, ys_ref, x_ref, gf_ref, gate_ref, o_ref,
                    sorted_sc, buf, sem, *, tm):
    step = pl.program_id(0)
    parity = step % 2
    mine, other = sorted_sc.at[parity], sorted_sc.at[1 - parity]

    @pl.when(step == 0)
    def _():
        _start_chunks(table_ref, ys_ref, mine, sem.at[parity], to_hbm=False)

    @pl.when(step + 1 < pl.num_programs(0))
    def _():
        _start_chunks(next_table_ref, ys_ref, other, sem.at[1 - parity], to_hbm=False)

    _wait_chunks(table_ref[0, 0, TABLE_W - 1], ys_ref, mine, sem.at[parity], to_hbm=False)

    def pick(i, carry):
        for k in range(2):
            buf[k, pl.ds(i, 1), :] = mine[pl.ds(slot_ref[0, 0, k * tm + i], 1), :]
        return carry

    lax.fori_loop(0, tm, pick, 0, unroll=8)
    for r0 in range(0, tm, TOK_TILE):
        rows = pl.ds(r0, min(TOK_TILE, tm))
        y = gate_ref[rows, 0:1] * buf[0, rows, :] + gate_ref[rows, 1:2] * buf[1, rows, :]
        o_ref[rows, :] = x_ref[rows, :] + gf_ref[...] * y


def _combine(ys, table, slots, x, gate_f, gates, seq):
    t, d = x.shape
    tm = min(MOE_TILE, t)
    per_b = seq // tm if seq >= tm else 1
    kern = functools.partial(_combine_kernel, tm=tm)
    last = t // tm - 1
    return pl.pallas_call(
        kern,
        grid=(t // tm,),
        in_specs=[pl.BlockSpec((1, 1, TABLE_W), lambda i: (i, 0, 0), memory_space=pltpu.SMEM),
                  pl.BlockSpec((1, 1, TABLE_W), lambda i: (jnp.minimum(i + 1, last), 0, 0), memory_space=pltpu.SMEM),
                  pl.BlockSpec((1, 1, 2 * tm), lambda i: (i, 0, 0), memory_space=pltpu.SMEM),
                  pl.BlockSpec(memory_space=pl.ANY),
                  pl.BlockSpec((tm, d), lambda i: (i, 0)),
                  pl.BlockSpec((None, 1, d), lambda i: (i // per_b, 0, 0)),
                  pl.BlockSpec((tm, ROUTER_COLS), lambda i: (i, 0))],
        out_specs=pl.BlockSpec((tm, d), lambda i: (i, 0)),
        out_shape=jax.ShapeDtypeStruct((t, d), F32),
        scratch_shapes=[pltpu.VMEM((2, _sorted_rows(tm), d), F32), pltpu.VMEM((2, tm, d), F32),
                        pltpu.SemaphoreType.DMA((2,))],
        compiler_params=_cparams(("arbitrary",)),
        name="moe_combine",
    )(table, table, slots, ys, x, gate_f, gates)


def _moe(h, logits, x1, gate_f, w_gate, w_up, w_down, layer, seq):
    rows = _moe_rows(h.shape[0])
    route, gates, seg, totals = _route(logits)
    table, slots, fill, block_e, n_used = _dispatch_plan(route, seg, totals[:, 0], rows // MOE_BLK)
    xs = _dispatch(h, table, slots, fill, rows)
    ys = _experts(xs, block_e, n_used, w_gate, w_up, w_down, layer)
    return _combine(ys, table, slots, x1, gate_f, gates, seq)


def _router_weights(w_group, b_group, w_expert, b_expert):
    d = w_group.shape[0]
    w = jnp.zeros((d, ROUTER_COLS), F32)
    w = w.at[:, 0:MOE_GROUPS].set(w_group.astype(F32))
    w = w.at[:, EXPERT_COL0:EXPERT_COL0 + MOE_EXPERTS].set(w_expert.astype(F32))
    b = jnp.zeros((1, ROUTER_COLS), F32)
    b = b.at[0, 0:MOE_GROUPS].set(b_group.astype(F32))
    b = b.at[0, EXPERT_COL0:EXPERT_COL0 + MOE_EXPERTS].set(b_expert.astype(F32))
    return w, b


def _lambda_init(layer):
    return 0.8 - 0.6 * math.exp(-0.3 * layer)


def kernel(x, c, rel_bias, ada_w, ada_b, norm_attn, norm_ffn, a_w_qkv, a_q_norm, a_k_norm, a_lambda, a_subln, a_w_o, kv_norm, kv_ada_w, kv_ada_b, kv_w, kv_k_norm, b_w_q, b_q_norm, b_w_o, moe_w_group, moe_b_group, moe_w_expert, moe_b_expert, moe_w_gate, moe_w_up, moe_w_down):
    batch, seq, d = x.shape
    t = batch * seq
    scale = HEAD_DIM ** -0.5
    n_bw = N_GROUPS * B_WIDTH

    c_pad = jnp.zeros((8, d), F32).at[:batch].set(c.astype(F32))
    mod = _modulation(c_pad, ada_w, ada_b)[:, :batch]
    kv_mod = _modulation(c_pad, kv_ada_w[None], kv_ada_b[None])[0, :batch]

    def part(m, i):
        return m[:, i * d:(i + 1) * d].reshape(batch, 1, d)

    a_bias, nd = _attn_a_bias_diags(rel_bias, seq, min(ATT_TILE, seq))
    b_bias = [_attn_b_bias_tiles(rel_bias, window // dil, dil) for window, dil in B_GROUPS]
    xf = x.reshape(t, d).astype(F32)
    kv = None
    for layer in range(DEPTH):
        m = mod[layer]
        sh_a, sc_a, g_a, sh_f, sc_f, g_f = (part(m, i) for i in range(6))
        w_r, b_r = _router_weights(moe_w_group[layer], moe_b_group[layer], moe_w_expert[layer], moe_b_expert[layer])
        if layer < N_A_LAYERS:
            qk_gain = jnp.concatenate([jnp.tile(a_q_norm[layer].astype(F32) * (scale * LOG2_E), 2 * N_HEADS),
                                       jnp.tile(a_k_norm[layer].astype(F32), 2 * N_HEADS),
                                       jnp.ones((N_HEADS * LANES,), F32)])
            qkv = _norm_proj(xf, norm_attn[layer], sh_a, sc_a, a_w_qkv[layer].astype(BF16), qk_gain,
                             2 * N_HEADS * LANES, seq, BF16)
            att = _attn_a(qkv, a_bias, nd, a_lambda[layer].astype(F32), a_subln[layer].astype(F32),
                          _lambda_init(layer), batch, seq)
            wo = a_w_o[layer].astype(BF16)
        else:
            j = layer - N_A_LAYERS
            if kv is None:
                k_gain = jnp.concatenate([jnp.tile(kv_k_norm.astype(F32), (1, N_HEADS)).reshape(-1),
                                          jnp.ones((n_bw,), F32)])
                kv = _norm_proj(xf, kv_norm, part(kv_mod, 0), part(kv_mod, 1), kv_w.astype(BF16), k_gain, n_bw, seq, F32)
            q_gain = jnp.tile(b_q_norm[j].astype(F32) * (scale * LOG2_E), (1, N_HEADS)).reshape(-1)
            q = _norm_proj(xf, norm_attn[layer], sh_a, sc_a, b_w_q[j].astype(BF16), q_gain, n_bw, seq, F32)
            att = _attn_b(q, kv, b_bias, batch, seq)
            wo = b_w_o[j].astype(BF16)
        x1, hf, logits = _out_proj(att, wo, xf, g_a, norm_ffn[layer], sh_f, sc_f, w_r, b_r, seq)
        xf = _moe(hf, logits, x1, g_f, moe_w_gate, moe_w_up, moe_w_down, layer, seq)
    return xf.reshape(batch, seq, d).astype(x.dtype)
```

```python
import functools
import math

import jax
import jax.numpy as jnp
import numpy as np
from jax import lax
from jax.experimental import pallas as pl
from jax.experimental.pallas import tpu as pltpu

F32 = jnp.float32
BF16 = jnp.bfloat16

DEPTH = 4
N_A_LAYERS = 2
N_HEADS = 8
HEAD_DIM = 64
B_GROUPS = ((128, 1), (512, 4), (2048, 16))
N_GROUPS = len(B_GROUPS)
B_WIDTH = N_HEADS * HEAD_DIM
REL_BUCKETS = 32
REL_MAX_EXACT = 16
REL_MAX_DIST = 2048
MOE_GROUPS = 4
MOE_EPG = 8
MOE_EXPERTS = MOE_GROUPS * MOE_EPG
EPS = 1e-6
NEG_INF = float("-inf")
LOG2_E = math.log2(math.e)

LANES = 128
MXU_DIM = 256
ROUTER_COLS = 128
EXPERT_COL0 = 8

TOK_TILE = 512
ATT_TILE = 512
B_TOKENS = 2048
BQ_SUB = 128
MOE_BLK = 512
MOE_TILE = 1024
SEG_ALIGN = 8
SEG_CHUNK = 8
TABLE_W = 384
VMEM_LIMIT = 56 * 1024 * 1024


def _cparams(sem):
    return pltpu.CompilerParams(dimension_semantics=sem, vmem_limit_bytes=VMEM_LIMIT)


def _bucket_table(max_dist):
    n = np.arange(max_dist + 1)
    nf = np.maximum(n, 1).astype(np.float64)
    large = REL_MAX_EXACT + (np.log(nf / REL_MAX_EXACT) / math.log(REL_MAX_DIST / REL_MAX_EXACT)
                             * (REL_BUCKETS - REL_MAX_EXACT)).astype(np.int64)
    large = np.minimum(large, REL_BUCKETS - 1)
    return np.where(n < REL_MAX_EXACT, n, large).astype(np.int32)


def _toeplitz(v, n, m):
    length = v.shape[-1]
    assert length >= n + m - 1 and length - 1 >= m
    lead = v.shape[:-1]
    flat = jnp.tile(v, (1,) * len(lead) + (n,))[..., :n * (length - 1)]
    return flat.reshape(lead + (n, length - 1))[..., :m]


def _diag_values(rel_bias, dist, ok):
    table = _bucket_table(int(dist.max()))
    vals = rel_bias.astype(F32)[table[np.clip(dist, 0, None)]]
    vals = jnp.where(ok[..., None], vals, NEG_INF)
    return jnp.moveaxis(vals, -1, 0)


def _attn_a_bias_diags(rel_bias, seq, tile):
    table = _bucket_table(seq)
    last_start = int(np.argmax(table == REL_BUCKETS - 1))
    nd = 0
    while nd * tile - (tile - 1) < last_start and nd * tile < seq:
        nd += 1
    nd += 1
    length = 2 * tile
    u = np.arange(length)
    u = np.where(u < tile, u, u - length)
    dist = np.arange(nd)[:, None] * tile - u[None, :]
    return _diag_values(rel_bias, dist, dist >= 0) * LOG2_E, nd


def _attn_b_bias_tiles(rel_bias, band, dil):
    length = 3 * BQ_SUB
    u = np.arange(length)
    u = np.where(u < 2 * BQ_SUB, u, u - length)
    rel = BQ_SUB - u
    ok = (rel >= 0) & (rel <= band)
    general = _toeplitz(_diag_values(rel_bias, dil * rel, ok) * LOG2_E, BQ_SUB, 2 * BQ_SUB)
    no_prev = np.arange(2 * BQ_SUB)[None, None, :] >= BQ_SUB
    return jnp.stack([jnp.where(no_prev, general, NEG_INF), general])


def _group_sum_matrix():
    r = np.arange(MXU_DIM)
    return jnp.asarray((r[:, None] // HEAD_DIM) == (r[None, :] // HEAD_DIM), BF16)


def _mod_kernel(c_ref, w_ref, b_ref, o_ref):
    c = c_ref[...]
    c_act = c * (1.0 / (1.0 + jnp.exp(-c)))
    o_ref[...] = jnp.dot(c_act, w_ref[...], preferred_element_type=F32,
                         precision=lax.Precision.HIGHEST) + b_ref[...]


def _modulation(c_pad, w, b, tn=1024):
    nl, d, n = w.shape
    return pl.pallas_call(
        _mod_kernel,
        grid=(nl, n // tn),
        in_specs=[pl.BlockSpec((8, d), lambda l, j: (0, 0)),
                  pl.BlockSpec((None, d, tn), lambda l, j: (l, 0, j)),
                  pl.BlockSpec((None, 1, tn), lambda l, j: (l, 0, j))],
        out_specs=pl.BlockSpec((None, 8, tn), lambda l, j: (l, 0, j)),
        out_shape=jax.ShapeDtypeStruct((nl, 8, n), F32),
        compiler_params=_cparams(("parallel", "parallel")),
        name="adaln_mod",
    )(c_pad, w, b.reshape(nl, 1, n))


def _rms(x, gain):
    return x * lax.rsqrt(jnp.mean(x * x, axis=-1, keepdims=True) + EPS) * gain


def _proj_kernel(x_ref, g_ref, sh_ref, sc_ref, w_ref, hg_ref, bd_ref, o_ref, *, n_out, n_norm):
    h = _rms(x_ref[...], g_ref[...]) * (1.0 + sc_ref[...]) + sh_ref[...]
    hb = h.astype(BF16)
    wide = 2 * MXU_DIM
    for c0 in range(0, n_out, wide):
        a = jnp.dot(hb, w_ref[:, c0:c0 + wide], preferred_element_type=F32)
        if c0 < n_norm:
            sq = (a * a).astype(BF16)
            ms = jnp.concatenate([jnp.dot(sq[:, :MXU_DIM], bd_ref[...], preferred_element_type=F32),
                                  jnp.dot(sq[:, MXU_DIM:], bd_ref[...], preferred_element_type=F32)], axis=1)
            a = a * lax.rsqrt(ms * (1.0 / HEAD_DIM) + EPS) * hg_ref[:, c0:c0 + wide]
        o_ref[:, c0:c0 + wide] = a.astype(o_ref.dtype)


def _norm_proj(x, gain, shift, scale, w_bf16, head_gain, n_norm, seq, out_dtype):
    t, d = x.shape
    n_out = w_bf16.shape[1]
    tm = min(TOK_TILE, seq)
    per_b = seq // tm
    kern = functools.partial(_proj_kernel, n_out=n_out, n_norm=n_norm)
    return pl.pallas_call(
        kern,
        grid=(t // tm,),
        in_specs=[pl.BlockSpec((tm, d), lambda i: (i, 0)),
                  pl.BlockSpec((1, d), lambda i: (0, 0)),
                  pl.BlockSpec((None, 1, d), lambda i: (i // per_b, 0, 0)),
                  pl.BlockSpec((None, 1, d), lambda i: (i // per_b, 0, 0)),
                  pl.BlockSpec((d, n_out), lambda i: (0, 0)),
                  pl.BlockSpec((1, n_out), lambda i: (0, 0)),
                  pl.BlockSpec((MXU_DIM, MXU_DIM), lambda i: (0, 0))],
        out_specs=pl.BlockSpec((tm, n_out), lambda i: (i, 0)),
        out_shape=jax.ShapeDtypeStruct((t, n_out), out_dtype),
        compiler_params=_cparams(("parallel",)),
        name="norm_proj",
    )(x, gain.reshape(1, d), shift, scale, w_bf16, head_gain.reshape(1, n_out), _group_sum_matrix())


def _lane_tile(a, reps):
    return jnp.concatenate([a] * reps, axis=1)


def _attn_a_kernel(q_ref, k_ref, v_ref, diag_ref, lam_ref, g_ref, o_ref, bias_ref, m_sc, acc_sc,
                   *, tile, nd, lam_init):
    qi = pl.program_id(2)

    @pl.when((pl.program_id(1) == 0) & (qi == 0))
    def _():
        for d in range(nd):
            row = jnp.broadcast_to(diag_ref[d:d + 1, :], (tile, 2 * tile))
            bias_ref[d] = pltpu.roll(row, 0, 1, stride=1, stride_axis=0)[:, :tile]

    q = q_ref[...]
    lane = lax.broadcasted_iota(jnp.int32, q.shape, 1)
    zero = jnp.zeros_like(q)
    qq = jnp.concatenate([jnp.where(lane < HEAD_DIM, q, zero),
                          jnp.where(lane >= HEAD_DIM, q, zero)], axis=0)
    m_sc[...] = jnp.full(m_sc.shape, NEG_INF, F32)
    acc_sc[...] = jnp.zeros(acc_sc.shape, F32)
    ones = jnp.ones((tile, LANES), BF16)

    def kv_piece(key_start, n_keys, bias, row0):
        n_rows = tile - row0
        k = k_ref[pl.ds(key_start, n_keys), :]
        v_ones = jnp.concatenate([v_ref[pl.ds(key_start, n_keys), :], ones[:n_keys]], axis=1)
        lhs = qq if row0 == 0 else jnp.concatenate([qq[row0:tile], qq[tile + row0:]], axis=0)
        s = lax.dot_general(lhs, k, (((1,), (1,)), ((), ())), preferred_element_type=F32)
        for c in range(2):
            sc = s[c * n_rows:(c + 1) * n_rows] + bias
            m_prev = m_sc[c, row0:, :]
            m_new = jnp.maximum(m_prev, jnp.max(sc, axis=-1, keepdims=True))
            alpha = jnp.exp2(m_prev - m_new)
            p = jnp.exp2(sc - _lane_tile(m_new, n_keys // LANES))
            acc_sc[c, row0:, :] = (_lane_tile(alpha, 2) * acc_sc[c, row0:, :]
                                   + jnp.dot(p.astype(BF16), v_ones, preferred_element_type=F32))
            m_sc[c, row0:, :] = m_new

    def kv_tile(j):
        kv_piece(pl.multiple_of(j * tile, tile), tile, bias_ref[jnp.minimum(qi - j, nd - 1)], 0)

    def kv_quad(i, carry):
        for u in range(4):
            kv_tile(4 * i + u)
        return carry

    n_quad = qi // 4
    lax.fori_loop(0, n_quad, kv_quad, 0)

    @pl.when(qi % 4 >= 2)
    def _():
        kv_tile(4 * n_quad)
        kv_tile(4 * n_quad + 1)

    @pl.when(qi % 2 == 1)
    def _():
        kv_tile(qi - 1)

    half = tile // 2
    diag = pl.multiple_of(qi * tile, tile)
    kv_piece(diag, half, bias_ref[0, :, :half], 0)
    kv_piece(pl.multiple_of(diag + half, half), half, bias_ref[0, half:, half:], half)

    lp = lam_ref[...]
    lam = (jnp.exp(jnp.sum(lp[0:1] * lp[1:2], axis=-1, keepdims=True))
           - jnp.exp(jnp.sum(lp[2:3] * lp[3:4], axis=-1, keepdims=True)) + lam_init)
    a0, a1 = acc_sc[0], acc_sc[1]
    o = a0[:, :LANES] / a0[:, LANES:] - lam * (a1[:, :LANES] / a1[:, LANES:])
    o_ref[...] = (_rms(o, g_ref[...]) * (1.0 - lam_init)).astype(BF16)


def _attn_a(qkv, bias_diags, nd, lam_p, subln, lam_init, batch, seq):
    tile = min(ATT_TILE, seq)
    nq = seq // tile
    qkv3 = qkv.reshape(batch, seq, 3 * N_HEADS * LANES)
    kern = functools.partial(_attn_a_kernel, tile=tile, nd=nd, lam_init=lam_init)
    out = pl.pallas_call(
        kern,
        grid=(N_HEADS, batch, nq),
        in_specs=[pl.BlockSpec((None, tile, LANES), lambda h, b, i: (b, i, h)),
                  pl.BlockSpec((None, seq, LANES), lambda h, b, i: (b, 0, N_HEADS + h)),
                  pl.BlockSpec((None, seq, LANES), lambda h, b, i: (b, 0, 2 * N_HEADS + h)),
                  pl.BlockSpec((None, nd, 2 * tile), lambda h, b, i: (h, 0, 0)),
                  pl.BlockSpec((4, HEAD_DIM), lambda h, b, i: (0, 0)),
                  pl.BlockSpec((1, LANES), lambda h, b, i: (0, 0))],
        out_specs=pl.BlockSpec((None, tile, LANES), lambda h, b, i: (b, i, h)),
        out_shape=jax.ShapeDtypeStruct((batch, seq, N_HEADS * LANES), BF16),
        scratch_shapes=[pltpu.VMEM((nd, tile, tile), F32), pltpu.VMEM((2, tile, LANES), F32),
                        pltpu.VMEM((2, tile, 2 * LANES), F32)],
        compiler_params=_cparams(("arbitrary", "arbitrary", "arbitrary")),
        name="attn_a",
    )(qkv3, qkv3, qkv3, bias_diags, lam_p, subln.reshape(1, LANES))
    return out.reshape(batch * seq, N_HEADS * LANES)


def _attn_b_kernel(*refs, tile):
    ins, o_ref, scratch = refs[:6 * N_GROUPS], refs[6 * N_GROUPS], refs[6 * N_GROUPS + 1:]
    first_tile = pl.program_id(1) == 0
    lane = lax.broadcasted_iota(jnp.int32, (BQ_SUB, LANES), 1)
    low = lane < HEAD_DIM
    ones = jnp.ones((2 * BQ_SUB, LANES), BF16)

    for g, (_, dil) in enumerate(B_GROUPS):
        q_ref, k_ref, v_ref, kp_ref, vp_ref, bias_ref = ins[6 * g:6 * g + 6]
        og_ref, lg_ref = scratch[2 * g:2 * g + 2]
        span = BQ_SUB * dil

        def residue(base, dil=dil):
            return pl.ds(base, BQ_SUB) if dil == 1 else pl.ds(base, BQ_SUB, stride=dil)

        band = {}
        for sb in range(tile // span):
            for r in range(dil):
                base = sb * span + r
                rows = residue(base)
                qp = q_ref[rows, :].astype(BF16)
                band[sb, r] = (k_ref[rows, :].astype(BF16), v_ref[rows, :].astype(BF16))
                if sb == 0:
                    k_prev, v_prev = kp_ref[residue(r), :].astype(BF16), vp_ref[residue(r), :].astype(BF16)
                    variant = jnp.where(first_tile, 0, 1)
                else:
                    k_prev, v_prev = band.pop((sb - 1, r))
                    variant = 1
                kk = jnp.concatenate([k_prev, band[sb, r][0]], axis=0)
                vv = jnp.concatenate([v_prev, band[sb, r][1]], axis=0)
                v_ones = jnp.concatenate([vv, ones], axis=1)
                zero = jnp.zeros_like(qp)
                qq = jnp.concatenate([jnp.where(low, qp, zero), jnp.where(low, zero, qp)], axis=0)
                s = lax.dot_general(qq, kk, (((1,), (1,)), ((), ())), preferred_element_type=F32)
                outs, lses = [], []
                for c in range(2):
                    sc = s[c * BQ_SUB:(c + 1) * BQ_SUB] + bias_ref[variant, c]
                    m = jnp.max(sc, axis=-1, keepdims=True)
                    p = jnp.exp2(sc - m)
                    acc = jnp.dot(p.astype(BF16), v_ones, preferred_element_type=F32)
                    outs.append(acc[:, :LANES] / acc[:, LANES:])
                    lses.append(m + jnp.log2(acc[:, LANES:]))
                og_ref[rows, :] = jnp.where(low, outs[0], outs[1])
                lg_ref[rows, :] = jnp.where(low, lses[0], lses[1])

    l0, l1, l2 = scratch[1][...], scratch[3][...], scratch[5][...]
    m = jnp.maximum(jnp.maximum(l0, l1), l2)
    e0, e1, e2 = jnp.exp2(l0 - m), jnp.exp2(l1 - m), jnp.exp2(l2 - m)
    z = e0 + e1 + e2
    o_ref[...] = ((e0 / z) * scratch[0][...] + (e1 / z) * scratch[2][...] + (e2 / z) * scratch[4][...]).astype(BF16)


def _attn_b(q, kv, bias_tiles, batch, seq):
    tile = min(B_TOKENS, seq)
    pairs = N_HEADS // 2
    q3 = q.reshape(batch, seq, N_GROUPS * B_WIDTH)
    kv3 = kv.reshape(batch, seq, 2 * N_GROUPS * B_WIDTH)
    operands, in_specs = [], []
    for g, (_, dil) in enumerate(B_GROUPS):
        span = BQ_SUB * dil
        assert tile % span == 0
        k_col = lambda hp, g=g: g * pairs + hp
        v_col = lambda hp, g=g: (N_GROUPS + g) * pairs + hp
        prev = lambda n, per=tile // span: jnp.maximum(n * per - 1, 0)
        operands += [q3, kv3, kv3, kv3, kv3, bias_tiles[g]]
        in_specs += [pl.BlockSpec((None, tile, LANES), lambda b, n, hp, c=k_col: (b, n, c(hp))),
                     pl.BlockSpec((None, tile, LANES), lambda b, n, hp, c=k_col: (b, n, c(hp))),
                     pl.BlockSpec((None, tile, LANES), lambda b, n, hp, c=v_col: (b, n, c(hp))),
                     pl.BlockSpec((None, span, LANES), lambda b, n, hp, c=k_col, p=prev: (b, p(n), c(hp))),
                     pl.BlockSpec((None, span, LANES), lambda b, n, hp, c=v_col, p=prev: (b, p(n), c(hp))),
                     pl.BlockSpec((2, 2, BQ_SUB, 2 * BQ_SUB), lambda b, n, hp: (0, hp, 0, 0))]
    out = pl.pallas_call(
        functools.partial(_attn_b_kernel, tile=tile),
        grid=(batch, seq // tile, pairs),
        in_specs=in_specs,
        out_specs=pl.BlockSpec((None, tile, LANES), lambda b, n, hp: (b, n, hp)),
        out_shape=jax.ShapeDtypeStruct((batch, seq, B_WIDTH), BF16),
        scratch_shapes=[pltpu.VMEM((tile, LANES), F32)] * (2 * N_GROUPS),
        compiler_params=_cparams(("parallel", "parallel", "parallel")),
        name="attn_b",
    )(*operands)
    return out.reshape(batch * seq, B_WIDTH)


def _split_bf16(a):
    hi = a.astype(BF16)
    return hi, (a - hi.astype(F32)).astype(BF16)


def _out_tail(o_bf16, wo_ref, x_ref, ga_ref, gf_ref, shf_ref, scf_ref, wr_ref, br_ref, x1_ref, hf_ref, lg_ref):
    y = jnp.dot(o_bf16, wo_ref[...], preferred_element_type=F32)
    x1 = x_ref[...] + ga_ref[...] * y
    x1_ref[...] = x1
    hf = _rms(x1, gf_ref[...]) * (1.0 + scf_ref[...]) + shf_ref[...]
    half = hf.shape[1] // 2
    hf_ref[...] = _pack_pair(hf[:, :half], hf[:, half:])
    hi, lo = _split_bf16(hf)
    both = jnp.dot(hi, wr_ref[...], preferred_element_type=F32)
    lg_ref[...] = (both[:, :ROUTER_COLS] + jnp.dot(lo, wr_ref[:, :ROUTER_COLS], preferred_element_type=F32)
                   + both[:, ROUTER_COLS:] + br_ref[...])


def _out_kernel(o_ref, *rest):
    _out_tail(o_ref[...], *rest)


def _out_proj(att, wo_bf16, x, gate_a, gain_f, shift_f, scale_f, w_router, b_router, seq):
    t, d = x.shape
    tm = min(TOK_TILE, seq)
    per_b = seq // tm
    row = lambda i: (i, 0)
    fixed = lambda i: (0, 0)
    per_batch = lambda i: (i // per_b, 0, 0)
    return pl.pallas_call(
        _out_kernel,
        grid=(t // tm,),
        in_specs=[pl.BlockSpec((tm, att.shape[1]), row),
                  pl.BlockSpec(wo_bf16.shape, fixed),
                  pl.BlockSpec((tm, d), row),
                  pl.BlockSpec((None, 1, d), per_batch),
                  pl.BlockSpec((1, d), fixed),
                  pl.BlockSpec((None, 1, d), per_batch),
                  pl.BlockSpec((None, 1, d), per_batch),
                  pl.BlockSpec((d, 2 * ROUTER_COLS), fixed),
                  pl.BlockSpec((1, ROUTER_COLS), fixed)],
        out_specs=[pl.BlockSpec((tm, d), row), pl.BlockSpec((tm, d // 2), row),
                   pl.BlockSpec((tm, ROUTER_COLS), row)],
        out_shape=[jax.ShapeDtypeStruct((t, d), F32), jax.ShapeDtypeStruct((t, d // 2), jnp.uint32),
                   jax.ShapeDtypeStruct((t, ROUTER_COLS), F32)],
        compiler_params=_cparams(("parallel",)),
        name="out_proj",
    )(att, wo_bf16, x, gate_a, gain_f.reshape(1, d), shift_f, scale_f,
      jnp.concatenate(_split_bf16(w_router), axis=1), b_router)


def _route_kernel(lg_ref, tri_ref, eid_ref, gate_ref, seg_ref, cnt_ref, cnt_sc):
    lt = lg_ref[...].T
    best, g_idx = lt[0:1], jnp.zeros((1, lt.shape[1]), jnp.int32)
    for g in range(1, MOE_GROUPS):
        upd = lt[g:g + 1] > best
        best = jnp.where(upd, lt[g:g + 1], best)
        g_idx = jnp.where(upd, g, g_idx)
    denom = jnp.zeros_like(best)
    for g in range(MOE_GROUPS):
        denom = denom + jnp.exp(lt[g:g + 1] - best)
    g_w = 1.0 / denom
    e_sel = lt[EXPERT_COL0:EXPERT_COL0 + MOE_EPG]
    for g in range(1, MOE_GROUPS):
        r0 = EXPERT_COL0 + g * MOE_EPG
        e_sel = jnp.where(g_idx == g, lt[r0:r0 + MOE_EPG], e_sel)

    def first_max(vals):
        v, i = vals[0:1], jnp.zeros((1, vals.shape[1]), jnp.int32)
        for e in range(1, MOE_EPG):
            upd = vals[e:e + 1] > v
            v = jnp.where(upd, vals[e:e + 1], v)
            i = jnp.where(upd, e, i)
        return v, i

    v1, i1 = first_max(e_sel)
    row = lax.broadcasted_iota(jnp.int32, e_sel.shape, 0)
    v2, i2 = first_max(jnp.where(row == i1, NEG_INF, e_sel))
    e2 = jnp.exp(v2 - v1)
    w1 = 1.0 / (1.0 + e2)
    w2 = e2 / (1.0 + e2)
    e1 = g_idx * MOE_EPG + i1
    e2 = g_idx * MOE_EPG + i2

    @pl.when(pl.program_id(0) == 0)
    def _():
        cnt_sc[...] = jnp.zeros(cnt_sc.shape, F32)

    tm = lt.shape[1]
    erows = lax.broadcasted_iota(jnp.int32, (MOE_EXPERTS, tm), 0)
    ohs, pres = [], []
    for e_k in (e1, e2):
        oh = (erows == e_k).astype(F32)
        ohs.append(oh)
        pres.append(jnp.dot(oh.astype(BF16), tri_ref[...], preferred_element_type=F32))
    n0 = pres[0][:, tm - 1:]
    n_tot = n0 + pres[1][:, tm - 1:]
    chunks = jnp.floor((n_tot + (SEG_ALIGN - 1.0)) * (1.0 / SEG_ALIGN))
    er = lax.broadcasted_iota(jnp.int32, (MOE_EXPERTS, MOE_EXPERTS), 0)
    ec = lax.broadcasted_iota(jnp.int32, (MOE_EXPERTS, MOE_EXPERTS), 1)
    before = jnp.dot((ec < er).astype(BF16), jnp.broadcast_to(chunks, (MOE_EXPERTS, LANES)).astype(BF16),
                     preferred_element_type=F32)
    seg_start = before[:, 0:1] * SEG_ALIGN
    slot0 = jnp.sum(ohs[0] * (seg_start + pres[0] - 1.0), axis=0, keepdims=True)
    slot1 = jnp.sum(ohs[1] * (seg_start + n0 + pres[1] - 1.0), axis=0, keepdims=True)
    seg_off = cnt_sc[...]
    cnt_sc[...] = seg_off + chunks * SEG_ALIGN
    cnt_ref[...] = jnp.broadcast_to(cnt_sc[...], cnt_ref.shape).astype(jnp.int32)
    scol = lax.broadcasted_iota(jnp.int32, seg_ref.shape, 1)
    seg_ref[...] = jnp.where(scol == 0, chunks, jnp.where(scol == 1, seg_start, jnp.where(
        scol == 2, seg_off, 0.0))).astype(jnp.int32)

    erow = lax.broadcasted_iota(jnp.int32, eid_ref.shape, 0)
    eid_ref[...] = jnp.where(erow == 0, e1, jnp.where(erow == 1, e2, jnp.where(
        erow == 2, slot0.astype(jnp.int32), jnp.where(erow == 3, slot1.astype(jnp.int32), 0))))
    grow = lax.broadcasted_iota(jnp.int32, lt.shape, 0)
    gates = jnp.where(grow == 0, g_w * w1, jnp.where(grow == 1, g_w * w2, 0.0))
    gate_ref[...] = gates.T


def _route(logits):
    t = logits.shape[0]
    tm = min(MOE_TILE, t)
    n_tiles = t // tm
    r = np.arange(tm)
    tri = jnp.asarray(r[:, None] <= r[None, :], BF16)
    return pl.pallas_call(
        _route_kernel,
        grid=(n_tiles,),
        in_specs=[pl.BlockSpec((tm, ROUTER_COLS), lambda i: (i, 0)),
                  pl.BlockSpec((tm, tm), lambda i: (0, 0))],
        out_specs=[pl.BlockSpec((8, tm), lambda i: (0, i)), pl.BlockSpec((tm, ROUTER_COLS), lambda i: (i, 0)),
                   pl.BlockSpec((MOE_EXPERTS, LANES), lambda i: (i, 0)),
                   pl.BlockSpec((MOE_EXPERTS, LANES), lambda i: (0, 0))],
        out_shape=[jax.ShapeDtypeStruct((8, t), jnp.int32), jax.ShapeDtypeStruct((t, ROUTER_COLS), F32),
                   jax.ShapeDtypeStruct((n_tiles * MOE_EXPERTS, LANES), jnp.int32),
                   jax.ShapeDtypeStruct((MOE_EXPERTS, LANES), jnp.int32)],
        scratch_shapes=[pltpu.VMEM((MOE_EXPERTS, 1), F32)],
        compiler_params=_cparams(("arbitrary",)),
        name="moe_route",
    )(logits, tri)


def _moe_rows(t):
    n_tiles = t // min(MOE_TILE, t)
    rows = 2 * t + n_tiles * MOE_EXPERTS * (SEG_ALIGN - 1) + MOE_EXPERTS * (MOE_BLK - 1)
    return -(-rows // MOE_BLK) * MOE_BLK


def _chunk_rows(first, count, row0, n_slots):
    c = jnp.arange(n_slots, dtype=jnp.int32)
    run = jnp.sum((first + count)[..., None, :] <= c[:, None], axis=-1)
    hit = run[..., None] == jnp.arange(first.shape[-1], dtype=jnp.int32)
    base = jnp.sum(jnp.where(hit, (row0 - SEG_CHUNK * first)[..., None, :], 0), axis=-1)
    return jnp.where(run < first.shape[-1], base + SEG_CHUNK * c, 0)


def _dispatch_plan(route, seg, totals, n_blocks):
    n_tiles = seg.shape[0] // MOE_EXPERTS
    tm = route.shape[1] // n_tiles
    n_slots = _sorted_rows(tm) // SEG_CHUNK
    assert n_slots <= TABLE_W - 3
    padded = (totals + MOE_BLK - 1) // MOE_BLK * MOE_BLK
    pad_end = jnp.cumsum(padded)
    pad_start = pad_end - padded
    n_used = pad_end[-1] // MOE_BLK
    seg = seg.reshape(n_tiles, MOE_EXPERTS, LANES)
    chunks, first, seg_off = seg[:, :, 0], seg[:, :, 1] // SEG_CHUNK, seg[:, :, 2]
    dst = _chunk_rows(first, chunks, pad_start[None, :] + seg_off, n_slots)
    fill = (padded - totals) // SEG_CHUNK
    fill_first = jnp.cumsum(fill) - fill
    fill_dst = _chunk_rows(fill_first, fill, pad_start + totals, MOE_EXPERTS * (MOE_BLK // SEG_CHUNK))
    tail = jnp.stack([jnp.broadcast_to(n_used, (n_tiles,)), jnp.broadcast_to(jnp.sum(fill), (n_tiles,)),
                      jnp.sum(chunks, axis=1)], axis=1)
    table = jnp.concatenate([dst, jnp.zeros((n_tiles, TABLE_W - 3 - n_slots), jnp.int32), tail], axis=1)
    slots = route[2:4].reshape(2, n_tiles, tm).transpose(1, 0, 2).reshape(n_tiles, 1, 2 * tm)
    block_start = jnp.arange(n_blocks, dtype=jnp.int32) * MOE_BLK
    block_e = jnp.minimum(jnp.sum(pad_end[None, :] <= block_start[:, None], axis=1), MOE_EXPERTS - 1).astype(jnp.int32)
    return (table.astype(jnp.int32).reshape(n_tiles, 1, TABLE_W), slots, fill_dst.astype(jnp.int32).reshape(1, 1, -1),
            block_e, n_used.astype(jnp.int32).reshape(1))


def _chunk_copy(hbm_ref, hbm_row, sorted_ref, chunk, sem, to_hbm):
    local = sorted_ref.at[pl.ds(pl.multiple_of(chunk * SEG_CHUNK, SEG_CHUNK), SEG_CHUNK), :]
    remote = hbm_ref.at[pl.ds(pl.multiple_of(hbm_row, SEG_CHUNK), SEG_CHUNK), :]
    return pltpu.make_async_copy(local, remote, sem) if to_hbm else pltpu.make_async_copy(remote, local, sem)


def _start_chunks(table_ref, hbm_ref, sorted_ref, sem, to_hbm):
    def chunk(c, carry):
        _chunk_copy(hbm_ref, table_ref[0, 0, c], sorted_ref, c, sem, to_hbm).start()
        return carry

    lax.fori_loop(0, table_ref[0, 0, TABLE_W - 1], chunk, 0)


def _wait_chunks(count, hbm_ref, sorted_ref, sem, to_hbm):
    def chunk(c, carry):
        _chunk_copy(hbm_ref, 0, sorted_ref, 0, sem, to_hbm).wait()
        return carry

    lax.fori_loop(0, count, chunk, 0)


def _zero_fill(table_ref, fill_ref, xs_ref, zero_sc, sem, n_blocks, wait):
    def finish(copy):
        if wait:
            copy.wait()
        else:
            copy.start()

    def chunk(c, carry):
        row = pl.multiple_of(fill_ref[0, 0, c], SEG_CHUNK)
        finish(pltpu.make_async_copy(zero_sc.at[pl.ds(0, SEG_CHUNK), :], xs_ref.at[pl.ds(row, SEG_CHUNK), :], sem))
        return carry

    def block(b, carry):
        row = pl.multiple_of(b * MOE_BLK, MOE_BLK)
        finish(pltpu.make_async_copy(zero_sc, xs_ref.at[pl.ds(row, MOE_BLK), :], sem))
        return carry

    lax.fori_loop(0, table_ref[0, 0, TABLE_W - 2], chunk, 0)
    lax.fori_loop(table_ref[0, 0, TABLE_W - 3], n_blocks, block, 0)


def _dispatch_kernel(table_ref, slot_ref, fill_ref, h_ref, xs_ref, sorted_sc, zero_sc, pending_sc, sem, zero_sem,
                     *, tm, n_blocks):
    step = pl.program_id(0)
    first_step = step == 0
    parity = step % 2
    mine, other = sorted_sc.at[parity], sorted_sc.at[1 - parity]

    @pl.when(first_step)
    def _():
        sorted_sc[...] = jnp.zeros(sorted_sc.shape, sorted_sc.dtype)
        zero_sc[...] = jnp.zeros(zero_sc.shape, zero_sc.dtype)
        pending_sc[0] = 0
        _zero_fill(table_ref, fill_ref, xs_ref, zero_sc, zero_sem, n_blocks, wait=False)

    def place(i, carry):
        for k in range(2):
            mine[pl.ds(slot_ref[0, 0, k * tm + i], 1), :] = h_ref[pl.ds(i, 1), :]
        return carry

    lax.fori_loop(0, tm, place, 0, unroll=8)
    _start_chunks(table_ref, xs_ref, mine, sem.at[parity], to_hbm=True)
    _wait_chunks(pending_sc[0], xs_ref, other, sem.at[1 - parity], to_hbm=True)
    pending_sc[0] = table_ref[0, 0, TABLE_W - 1]

    @pl.when(step == pl.num_programs(0) - 1)
    def _():
        _wait_chunks(pending_sc[0], xs_ref, mine, sem.at[parity], to_hbm=True)

    @pl.when(first_step)
    def _():
        _zero_fill(table_ref, fill_ref, xs_ref, zero_sc, zero_sem, n_blocks, wait=True)


def _sorted_rows(tm):
    return -(-(2 * tm + MOE_EXPERTS * (SEG_ALIGN - 1)) // SEG_CHUNK) * SEG_CHUNK


def _dispatch(h, table, slots, fill, rows):
    t, d = h.shape
    tm = min(MOE_TILE, t)
    kern = functools.partial(_dispatch_kernel, tm=tm, n_blocks=rows // MOE_BLK)
    return pl.pallas_call(
        kern,
        grid=(t // tm,),
        in_specs=[pl.BlockSpec((1, 1, TABLE_W), lambda i: (i, 0, 0), memory_space=pltpu.SMEM),
                  pl.BlockSpec((1, 1, 2 * tm), lambda i: (i, 0, 0), memory_space=pltpu.SMEM),
                  pl.BlockSpec(fill.shape, lambda i: (0, 0, 0), memory_space=pltpu.SMEM),
                  pl.BlockSpec((tm, d), lambda i: (i, 0))],
        out_specs=pl.BlockSpec(memory_space=pl.ANY),
        out_shape=jax.ShapeDtypeStruct((rows, d), h.dtype),
        scratch_shapes=[pltpu.VMEM((2, _sorted_rows(tm), d), h.dtype), pltpu.VMEM((MOE_BLK, d), h.dtype),
                        pltpu.SMEM((1,), jnp.int32), pltpu.SemaphoreType.DMA((2,)), pltpu.SemaphoreType.DMA(())],
        compiler_params=_cparams(("arbitrary",)),
        name="moe_dispatch",
    )(table, slots, fill, h)


def _pack_pair(lo, hi):
    lo_bits = lax.bitcast_convert_type(lo.astype(BF16).astype(F32), jnp.uint32) >> 16
    hi_bits = lax.bitcast_convert_type(hi.astype(BF16).astype(F32), jnp.uint32) & jnp.uint32(0xFFFF0000)
    return hi_bits | lo_bits


def _unpack_pair(words):
    lo = lax.bitcast_convert_type(words << 16, F32).astype(BF16)
    hi = lax.bitcast_convert_type(words & jnp.uint32(0xFFFF0000), F32).astype(BF16)
    return lo, hi


def _expert_kernel(be_ref, nu_ref, xs_ref, wg_ref, wu_ref, wd_ref, ys_ref, wg_sc, wu_sc, wd_sc):
    i = pl.program_id(0)
    live = i < nu_ref[0]

    @pl.when(jnp.logical_not(live))
    def _():
        ys_ref[...] = jnp.zeros(ys_ref.shape, F32)

    @pl.when(live)
    def _():
        @pl.when((i == 0) | (be_ref[i] != be_ref[jnp.maximum(i - 1, 0)]))
        def _():
            wg_sc[...] = wg_ref[...].astype(BF16)
            wu_sc[...] = wu_ref[...].astype(BF16)
            wd_sc[...] = wd_ref[...].astype(BF16)

        x = jnp.concatenate(_unpack_pair(xs_ref[...]), axis=1)
        g = jnp.dot(x, wg_sc[...], preferred_element_type=F32)
        u = jnp.dot(x, wu_sc[...], preferred_element_type=F32)
        a = (g * (1.0 / (1.0 + jnp.exp(-g))) * u).astype(BF16)
        ys_ref[...] = jnp.dot(a, wd_sc[...], preferred_element_type=F32)


def _experts(xs, block_e, n_used, w_gate, w_up, w_down, layer):
    rows = xs.shape[0]
    d, hid = w_gate.shape[-2:]
    n_blocks = rows // MOE_BLK
    used = lambda i, be, nu: (jnp.minimum(i, nu[0] - 1), 0)
    expert = lambda i, be, nu: (layer, be[jnp.minimum(i, nu[0] - 1)], 0, 0)
    grid_spec = pltpu.PrefetchScalarGridSpec(
        num_scalar_prefetch=2,
        grid=(n_blocks,),
        in_specs=[pl.BlockSpec((MOE_BLK, d // 2), used),
                  pl.BlockSpec((None, None, d, hid), expert),
                  pl.BlockSpec((None, None, d, hid), expert),
                  pl.BlockSpec((None, None, hid, d), expert)],
        out_specs=pl.BlockSpec((MOE_BLK, d), lambda i, be, nu: (i, 0)),
        scratch_shapes=[pltpu.VMEM((d, hid), BF16), pltpu.VMEM((d, hid), BF16), pltpu.VMEM((hid, d), BF16)],
    )
    return pl.pallas_call(
        _expert_kernel,
        grid_spec=grid_spec,
        out_shape=jax.ShapeDtypeStruct((rows, d), F32),
        compiler_params=_cparams(("arbitrary",)),
        name="moe_experts",
    )(block_e, n_used, xs, w_gate, w_up, w_down)


def _combine_kernel(table_ref, next_table_ref, slot_ref, ys_ref, x_ref, gf_ref, gate_ref, o_ref,
                    sorted_sc, buf, sem, *, tm):
    step = pl.program_id(0)
    parity = step % 2
    mine, other = sorted_sc.at[parity], sorted_sc.at[1 - parity]

    @pl.when(step == 0)
    def _():
        _start_chunks(table_ref, ys_ref, mine, sem.at[parity], to_hbm=False)

    @pl.when(step + 1 < pl.num_programs(0))
    def _():
        _start_chunks(next_table_ref, ys_ref, other, sem.at[1 - parity], to_hbm=False)

    _wait_chunks(table_ref[0, 0, TABLE_W - 1], ys_ref, mine, sem.at[parity], to_hbm=False)

    def pick(i, carry):
        for k in range(2):
            buf[k, pl.ds(i, 1), :] = mine[pl.ds(slot_ref[0, 0, k * tm + i], 1), :]
        return carry

    lax.fori_loop(0, tm, pick, 0, unroll=8)
    for r0 in range(0, tm, TOK_TILE):
        rows = pl.ds(r0, min(TOK_TILE, tm))
        y = gate_ref[rows, 0:1] * buf[0, rows, :] + gate_ref[rows, 1:2] * buf[1, rows, :]
        o_ref[rows, :] = x_ref[rows, :] + gf_ref[...] * y


def _combine(ys, table, slots, x, gate_f, gates, seq):
    t, d = x.shape
    tm = min(MOE_TILE, t)
    per_b = seq // tm if seq >= tm else 1
    kern = functools.partial(_combine_kernel, tm=tm)
    last = t // tm - 1
    return pl.pallas_call(
        kern,
        grid=(t // tm,),
        in_specs=[pl.BlockSpec((1, 1, TABLE_W), lambda i: (i, 0, 0), memory_space=pltpu.SMEM),
                  pl.BlockSpec((1, 1, TABLE_W), lambda i: (jnp.minimum(i + 1, last), 0, 0), memory_space=pltpu.SMEM),
                  pl.BlockSpec((1, 1, 2 * tm), lambda i: (i, 0, 0), memory_space=pltpu.SMEM),
                  pl.BlockSpec(memory_space=pl.ANY),
                  pl.BlockSpec((tm, d), lambda i: (i, 0)),
                  pl.BlockSpec((None, 1, d), lambda i: (i // per_b, 0, 0)),
                  pl.BlockSpec((tm, ROUTER_COLS), lambda i: (i, 0))],
        out_specs=pl.BlockSpec((tm, d), lambda i: (i, 0)),
        out_shape=jax.ShapeDtypeStruct((t, d), F32),
        scratch_shapes=[pltpu.VMEM((2, _sorted_rows(tm), d), F32), pltpu.VMEM((2, tm, d), F32),
                        pltpu.SemaphoreType.DMA((2,))],
        compiler_params=_cparams(("arbitrary",)),
        name="moe_combine",
    )(table, table, slots, ys, x, gate_f, gates)


def _moe(h, logits, x1, gate_f, w_gate, w_up, w_down, layer, seq):
    rows = _moe_rows(h.shape[0])
    route, gates, seg, totals = _route(logits)
    table, slots, fill, block_e, n_used = _dispatch_plan(route, seg, totals[:, 0], rows // MOE_BLK)
    xs = _dispatch(h, table, slots, fill, rows)
    ys = _experts(xs, block_e, n_used, w_gate, w_up, w_down, layer)
    return _combine(ys, table, slots, x1, gate_f, gates, seq)


def _router_weights(w_group, b_group, w_expert, b_expert):
    d = w_group.shape[0]
    w = jnp.zeros((d, ROUTER_COLS), F32)
    w = w.at[:, 0:MOE_GROUPS].set(w_group.astype(F32))
    w = w.at[:, EXPERT_COL0:EXPERT_COL0 + MOE_EXPERTS].set(w_expert.astype(F32))
    b = jnp.zeros((1, ROUTER_COLS), F32)
    b = b.at[0, 0:MOE_GROUPS].set(b_group.astype(F32))
    b = b.at[0, EXPERT_COL0:EXPERT_COL0 + MOE_EXPERTS].set(b_expert.astype(F32))
    return w, b


def _lambda_init(layer):
    return 0.8 - 0.6 * math.exp(-0.3 * layer)


def kernel(x, c, rel_bias, ada_w, ada_b, norm_attn, norm_ffn, a_w_qkv, a_q_norm, a_k_norm, a_lambda, a_subln, a_w_o, kv_norm, kv_ada_w, kv_ada_b, kv_w, kv_k_norm, b_w_q, b_q_norm, b_w_o, moe_w_group, moe_b_group, moe_w_expert, moe_b_expert, moe_w_gate, moe_w_up, moe_w_down):
    batch, seq, d = x.shape
    t = batch * seq
    scale = HEAD_DIM ** -0.5
    n_bw = N_GROUPS * B_WIDTH

    c_pad = jnp.zeros((8, d), F32).at[:batch].set(c.astype(F32))
    mod = _modulation(c_pad, ada_w, ada_b)[:, :batch]
    kv_mod = _modulation(c_pad, kv_ada_w[None], kv_ada_b[None])[0, :batch]

    def part(m, i):
        return m[:, i * d:(i + 1) * d].reshape(batch, 1, d)

    a_bias, nd = _attn_a_bias_diags(rel_bias, seq, min(ATT_TILE, seq))
    b_bias = [_attn_b_bias_tiles(rel_bias, window // dil, dil) for window, dil in B_GROUPS]
    xf = x.reshape(t, d).astype(F32)
    kv = None
    for layer in range(DEPTH):
        m = mod[layer]
        sh_a, sc_a, g_a, sh_f, sc_f, g_f = (part(m, i) for i in range(6))
        w_r, b_r = _router_weights(moe_w_group[layer], moe_b_group[layer], moe_w_expert[layer], moe_b_expert[layer])
        if layer < N_A_LAYERS:
            qk_gain = jnp.concatenate([jnp.tile(a_q_norm[layer].astype(F32) * (scale * LOG2_E), 2 * N_HEADS),
                                       jnp.tile(a_k_norm[layer].astype(F32), 2 * N_HEADS),
                                       jnp.ones((N_HEADS * LANES,), F32)])
            qkv = _norm_proj(xf, norm_attn[layer], sh_a, sc_a, a_w_qkv[layer].astype(BF16), qk_gain,
                             2 * N_HEADS * LANES, seq, BF16)
            att = _attn_a(qkv, a_bias, nd, a_lambda[layer].astype(F32), a_subln[layer].astype(F32),
                          _lambda_init(layer), batch, seq)
            wo = a_w_o[layer].astype(BF16)
        else:
            j = layer - N_A_LAYERS
            if kv is None:
                k_gain = jnp.concatenate([jnp.tile(kv_k_norm.astype(F32), (1, N_HEADS)).reshape(-1),
                                          jnp.ones((n_bw,), F32)])
                kv = _norm_proj(xf, kv_norm, part(kv_mod, 0), part(kv_mod, 1), kv_w.astype(BF16), k_gain, n_bw, seq, F32)
            q_gain = jnp.tile(b_q_norm[j].astype(F32) * (scale * LOG2_E), (1, N_HEADS)).reshape(-1)
            q = _norm_proj(xf, norm_attn[layer], sh_a, sc_a, b_w_q[j].astype(BF16), q_gain, n_bw, seq, F32)
            att = _attn_b(q, kv, b_bias, batch, seq)
            wo = b_w_o[j].astype(BF16)
        x1, hf, logits = _out_proj(att, wo, xf, g_a, norm_ffn[layer], sh_f, sc_f, w_r, b_r, seq)
        xf = _moe(hf, logits, x1, g_f, moe_w_gate, moe_w_up, moe_w_down, layer, seq)
    return xf.reshape(batch, seq, d).astype(x.dtype)
```
